```python
import math
import jax, jax.numpy as jnp
from jax import lax
import numpy as np

D_MODEL = 2048
BATCH = 4
SEQ = 2048
DEPTH = 1

PLE_DIM = 256
EPS = 1e-6
NEG = -1e30
POOL_WIDTH = D_MODEL // 2
POOL_WINDOWS = (2, 4, 8, 16)
POOL_GROUPS = len(POOL_WINDOWS)
POOL_GROUP_DIM = POOL_WIDTH // POOL_GROUPS
HEAD_DIM = 128
N_HEADS = (D_MODEL // 2) // HEAD_DIM
N_KV_GROUPS = 2
HEADS_PER_GROUP = N_HEADS // N_KV_GROUPS
NSA_WIDTH = N_HEADS * HEAD_DIM
KV_WIDTH = N_KV_GROUPS * HEAD_DIM
N_BRANCH = 3
IN_WIDTH = POOL_WIDTH + NSA_WIDTH + 6 * KV_WIDTH + N_HEADS * N_BRANCH
ATTN_SCALE = HEAD_DIM ** -0.5
CMP_BLOCK = 32
CMP_STRIDE = 16
CMP_HIDDEN = 128
SEL_BLOCK = 64
N_SELECT = 16
SEL_QCHUNK = 32
WINDOW = 512
WIN_QBLOCK = 128
ROPE_THETA = 500000.0
ROPE_DIM = HEAD_DIM // 4
D_FF = -(-8 * D_MODEL // (3 * 256)) * 256

kernel_name = "hymba_pool_nsa_swiglu_ple"


def rmsnorm(x, g):
    xf = x.astype(jnp.float32)
    y = xf * lax.rsqrt(jnp.mean(xf * xf, axis=-1, keepdims=True) + EPS)
    return (y * g.astype(jnp.float32)).astype(x.dtype)


def rope_tables(T):
    pos = jnp.arange(T, dtype=jnp.float32)
    inv_freq = ROPE_THETA ** (-jnp.arange(0, ROPE_DIM, 2, dtype=jnp.float32) / ROPE_DIM)
    ang = pos[:, None] * inv_freq[None, :]
    return jnp.cos(ang)[:, None, :], jnp.sin(ang)[:, None, :]


def partial_rope(x, cos, sin):
    half = ROPE_DIM // 2
    cos = cos.astype(x.dtype)
    sin = sin.astype(x.dtype)
    x1 = x[..., :half]
    x2 = x[..., half:ROPE_DIM]
    return jnp.concatenate([x1 * cos - x2 * sin, x2 * cos + x1 * sin, x[..., ROPE_DIM:]], axis=-1)


def pool_mixer(u, pool_w, pool_scale):
    B, T, _ = u.shape
    uf = u.astype(jnp.float32).reshape(B, T, POOL_GROUPS, POOL_GROUP_DIM)
    cs = jnp.concatenate([jnp.zeros_like(uf[:, :1]), jnp.cumsum(uf, axis=1)], axis=1)
    t = jnp.arange(T)
    outs = []
    for gi, w in enumerate(POOL_WINDOWS):
        start = jnp.maximum(t + 1 - w, 0)
        win_sum = cs[:, t + 1, gi] - cs[:, start, gi]
        count = (t + 1 - start).astype(jnp.float32)[None, :, None]
        outs.append(win_sum / count - uf[:, :, gi])
    pooled = jnp.stack(outs, axis=2).astype(u.dtype)
    y = jnp.einsum('btgc,gcd->btgd', pooled, pool_w).reshape(B, T, POOL_WIDTH)
    return y * pool_scale


def compress_blocks(k, pe, w1, w2):
    B, T, G, D = k.shape
    n_cmp = (T - CMP_BLOCK) // CMP_STRIDE + 1
    idx = CMP_STRIDE * jnp.arange(n_cmp)[:, None] + jnp.arange(CMP_BLOCK)[None, :]
    blocks = k[:, idx] + pe[None, None, :, None, :]
    flat = blocks.transpose(0, 1, 3, 2, 4).reshape(B, n_cmp, G, CMP_BLOCK * D)
    return jax.nn.gelu(flat @ w1) @ w2


def nsa_mixer(q, kc, vc, ks, vs, kw, vw, gates,
              cmp_k_pe, cmp_k_w1, cmp_k_w2, cmp_v_pe, cmp_v_w1, cmp_v_w2):
    B, T, H, D = q.shape
    G, R = N_KV_GROUPS, HEADS_PER_GROUP
    dt = q.dtype
    t = jnp.arange(T)
    qg = q.reshape(B, T, G, R, D)

    kcmp = compress_blocks(kc, cmp_k_pe, cmp_k_w1, cmp_k_w2)
    vcmp = compress_blocks(vc, cmp_v_pe, cmp_v_w1, cmp_v_w2)
    n_cmp = kcmp.shape[1]
    s = jnp.einsum('btgrd,bngd->bgrtn', qg, kcmp).astype(jnp.float32) * ATTN_SCALE
    blk_end = CMP_STRIDE * jnp.arange(n_cmp) + CMP_BLOCK - 1
    valid = blk_end[None, :] <= t[:, None]
    p_cmp = jax.nn.softmax(jnp.where(valid, s, NEG), axis=-1) * valid
    o_cmp = jnp.einsum('bgrtn,bngd->btgrd', p_cmp.astype(dt), vcmp)

    n_sel = T // SEL_BLOCK
    cstart = CMP_STRIDE * jnp.arange(n_cmp)
    sstart = SEL_BLOCK * jnp.arange(n_sel)
    overlap = ((cstart[:, None] < sstart[None, :] + SEL_BLOCK) &
               (cstart[:, None] + CMP_BLOCK > sstart[None, :])).astype(jnp.float32)
    imp = jnp.einsum('bgtn,ns->bgts', p_cmp.sum(axis=2), overlap)
    j = jnp.arange(n_sel)[None, :]
    cur = (t // SEL_BLOCK)[:, None]
    forced = (j == 0) | (j == cur) | (j == cur - 1)
    imp = jnp.where(j > cur, -jnp.inf, jnp.where(forced, jnp.inf, imp))
    k_top = min(N_SELECT, n_sel)
    _, sel_idx = lax.top_k(imp, k_top)

    ksb = ks.transpose(0, 2, 1, 3).reshape(B, G, n_sel, SEL_BLOCK, D)
    vsb = vs.transpose(0, 2, 1, 3).reshape(B, G, n_sel, SEL_BLOCK, D)
    n_chunk = T // SEL_QCHUNK
    q_ch = qg.transpose(0, 2, 3, 1, 4).reshape(B, G, R, n_chunk, SEL_QCHUNK, D).transpose(3, 0, 1, 2, 4, 5)
    idx_ch = sel_idx.reshape(B, G, n_chunk, SEL_QCHUNK, k_top).transpose(2, 0, 1, 3, 4)
    t_ch = t.reshape(n_chunk, SEL_QCHUNK)
    bi = jnp.arange(B)[:, None, None, None]
    gi = jnp.arange(G)[None, :, None, None]
    in_blk = jnp.arange(SEL_BLOCK)

    def sel_chunk(args):
        qc, ic, tc = args
        kb = ksb[bi, gi, ic]
        vb = vsb[bi, gi, ic]
        sc = jnp.einsum('bgrqd,bgqksd->bgrqks', qc, kb).astype(jnp.float32) * ATTN_SCALE
        kpos = ic[..., None] * SEL_BLOCK + in_blk
        mask = kpos <= tc[None, None, :, None, None]
        sc = jnp.where(mask[:, :, None], sc, NEG)
        pr = jax.nn.softmax(sc.reshape(B, G, R, SEL_QCHUNK, -1), axis=-1).reshape(sc.shape)
        return jnp.einsum('bgrqks,bgqksd->bgrqd', pr.astype(dt), vb)

    o_sel = lax.map(sel_chunk, (q_ch, idx_ch, t_ch))
    o_sel = o_sel.transpose(1, 0, 4, 2, 3, 5).reshape(B, T, G, R, D)

    nb = T // WIN_QBLOCK
    nw = WINDOW // WIN_QBLOCK
    band = jnp.arange(nb)[:, None] + jnp.arange(nw + 1)[None, :]
    pad = ((0, 0), (0, 0), (WINDOW, 0), (0, 0))
    kwp = jnp.pad(kw.transpose(0, 2, 1, 3), pad).reshape(B, G, nb + nw, WIN_QBLOCK, D)
    vwp = jnp.pad(vw.transpose(0, 2, 1, 3), pad).reshape(B, G, nb + nw, WIN_QBLOCK, D)
    kband = kwp[:, :, band].reshape(B, G, nb, (nw + 1) * WIN_QBLOCK, D)
    vband = vwp[:, :, band].reshape(B, G, nb, (nw + 1) * WIN_QBLOCK, D)
    qb = qg.transpose(0, 2, 3, 1, 4).reshape(B, G, R, nb, WIN_QBLOCK, D)
    sw = jnp.einsum('bgrnqd,bgnkd->bgrnqk', qb, kband).astype(jnp.float32) * ATTN_SCALE
    qpos = jnp.arange(nb)[:, None] * WIN_QBLOCK + jnp.arange(WIN_QBLOCK)[None, :]
    kpos = jnp.arange(nb)[:, None] * WIN_QBLOCK - WINDOW + jnp.arange((nw + 1) * WIN_QBLOCK)[None, :]
    diff = qpos[:, :, None] - kpos[:, None, :]
    wmask = (diff >= 0) & (diff < WINDOW) & (kpos[:, None, :] >= 0)
    pw = jax.nn.softmax(jnp.where(wmask, sw, NEG), axis=-1)
    o_win = jnp.einsum('bgrnqk,bgnkd->bgrnqd', pw.astype(dt), vband)
    o_win = o_win.transpose(0, 3, 4, 1, 2, 5).reshape(B, T, G, R, D)

    g = jax.nn.sigmoid(gates.astype(jnp.float32)).astype(dt).reshape(B, T, G, R, N_BRANCH, 1)
    o = g[..., 0, :] * o_cmp + g[..., 1, :] * o_sel + g[..., 2, :] * o_win
    return o.reshape(B, T, NSA_WIDTH)


def hybrid_mixer(a, w_in, pool_w, pool_scale, cmp_k_pe, cmp_k_w1, cmp_k_w2,
                 cmp_v_pe, cmp_v_w1, cmp_v_w2, w_out, cos, sin):
    B, T, _ = a.shape
    proj = a @ w_in
    offs = [int(o) for o in np.cumsum([POOL_WIDTH, NSA_WIDTH] + [KV_WIDTH] * 6)]
    u, q, kc, vc, ks, vs, kw, vw, gts = jnp.split(proj, offs, axis=-1)
    kvshape = (B, T, N_KV_GROUPS, HEAD_DIM)
    q = partial_rope(q.reshape(B, T, N_HEADS, HEAD_DIM), cos, sin)
    kc = partial_rope(kc.reshape(kvshape), cos, sin)
    ks = partial_rope(ks.reshape(kvshape), cos, sin)
    kw = partial_rope(kw.reshape(kvshape), cos, sin)
    y_nsa = nsa_mixer(q, kc, vc.reshape(kvshape), ks, vs.reshape(kvshape), kw, vw.reshape(kvshape),
                      gts.reshape(B, T, N_HEADS, N_BRANCH),
                      cmp_k_pe, cmp_k_w1, cmp_k_w2, cmp_v_pe, cmp_v_w1, cmp_v_w2)
    y_pool = pool_mixer(u, pool_w, pool_scale)
    return jnp.concatenate([y_pool, y_nsa], axis=-1) @ w_out


def swiglu(x, w_gate, w_up, w_down):
    return (jax.nn.silu(x @ w_gate) * (x @ w_up)) @ w_down


def setup_inputs(seed: int = 0) -> dict:
    key = jax.random.key(seed)
    ks = jax.random.split(key, 24)
    f32 = jnp.float32
    L = DEPTH

    def nrm(k, shape, scale):
        return jax.random.normal(k, shape, f32) * scale

    def gain(k, shape):
        return 1.0 + 0.02 * jax.random.normal(k, shape, f32)

    return {
        'x': jax.random.normal(ks[0], (BATCH, SEQ, D_MODEL), f32),
        'p': jax.random.normal(ks[1], (DEPTH, BATCH, SEQ, PLE_DIM), f32),
        'in_norm_g': gain(ks[2], (L, D_MODEL)),
        'w_in': nrm(ks[3], (L, D_MODEL, IN_WIDTH), D_MODEL ** -0.5),
        'pool_w': nrm(ks[4], (L, POOL_GROUPS, POOL_GROUP_DIM, POOL_GROUP_DIM), POOL_GROUP_DIM ** -0.5),
        'pool_scale': gain(ks[5], (L, POOL_WIDTH)),
        'cmp_k_pe': nrm(ks[6], (L, CMP_BLOCK, HEAD_DIM), 0.02),
        'cmp_k_w1': nrm(ks[7], (L, CMP_BLOCK * HEAD_DIM, CMP_HIDDEN), (CMP_BLOCK * HEAD_DIM) ** -0.5),
        'cmp_k_w2': nrm(ks[8], (L, CMP_HIDDEN, HEAD_DIM), CMP_HIDDEN ** -0.5),
        'cmp_v_pe': nrm(ks[9], (L, CMP_BLOCK, HEAD_DIM), 0.02),
        'cmp_v_w1': nrm(ks[10], (L, CMP_BLOCK * HEAD_DIM, CMP_HIDDEN), (CMP_BLOCK * HEAD_DIM) ** -0.5),
        'cmp_v_w2': nrm(ks[11], (L, CMP_HIDDEN, HEAD_DIM), CMP_HIDDEN ** -0.5),
        'w_out': nrm(ks[12], (L, D_MODEL, D_MODEL), D_MODEL ** -0.5),
        'ffn_norm_g': gain(ks[13], (L, D_MODEL)),
        'w_gate': nrm(ks[14], (L, D_MODEL, D_FF), D_MODEL ** -0.5),
        'w_up': nrm(ks[15], (L, D_MODEL, D_FF), D_MODEL ** -0.5),
        'w_down': nrm(ks[16], (L, D_FF, D_MODEL), D_FF ** -0.5),
        'ple_norm_g': gain(ks[17], (L, D_MODEL)),
        'w_ple_gate': nrm(ks[18], (L, D_MODEL, D_MODEL), D_MODEL ** -0.5),
        'w_ple_proj': nrm(ks[19], (L, PLE_DIM, D_MODEL), PLE_DIM ** -0.5),
        'final_norm_g': gain(ks[20], (D_MODEL,)),
    }


def reference(x, p, in_norm_g, w_in, pool_w, pool_scale, cmp_k_pe, cmp_k_w1, cmp_k_w2,
              cmp_v_pe, cmp_v_w1, cmp_v_w2, w_out, ffn_norm_g, w_gate, w_up, w_down,
              ple_norm_g, w_ple_gate, w_ple_proj, final_norm_g):
    T = x.shape[1]
    cos, sin = rope_tables(T)
    h = x
    for i in range(DEPTH):
        a = rmsnorm(h, in_norm_g[i])
        h = h + hybrid_mixer(a, w_in[i], pool_w[i], pool_scale[i], cmp_k_pe[i], cmp_k_w1[i], cmp_k_w2[i],
                             cmp_v_pe[i], cmp_v_w1[i], cmp_v_w2[i], w_out[i], cos, sin)
        h = h + swiglu(rmsnorm(h, ffn_norm_g[i]), w_gate[i], w_up[i], w_down[i])
        gate = jax.nn.sigmoid((rmsnorm(h, ple_norm_g[i]) @ w_ple_gate[i]).astype(jnp.float32)).astype(h.dtype)
        h = h + (p[i] @ w_ple_proj[i]) * gate
    return rmsnorm(h, final_norm_g)
```

```python
import functools

import jax
import jax.numpy as jnp
from jax import lax
from jax.experimental import pallas as pl
from jax.experimental.pallas import tpu as pltpu

EPS = 1e-6
NEG = -1e30
LANES = 128
POOL_WINDOWS = (2, 4, 8, 16)
POOL_HALO = 16
HEAD_DIM = 128
N_KV_GROUPS = 2
HEADS_PER_GROUP = 4
N_BRANCH = 3
CMP_BLOCK = 32
CMP_STRIDE = 16
SEL_BLOCK = 64
N_SELECT = 16
WINDOW = 512
ROPE_THETA = 500000.0
ROPE_DIM = HEAD_DIM // 4
ATTN_SCALE = HEAD_DIM ** -0.5
VMEM_LIMIT = 56 * 1024 * 1024

f32 = jnp.float32
bf16 = jnp.bfloat16


def _dot(a, b):
    return jnp.dot(a, b, preferred_element_type=f32)


def _dot_nt(a, b):
    return lax.dot_general(a, b, (((1,), (1,)), ((), ())), preferred_element_type=f32)


def _rms(x, g):
    return x * lax.rsqrt(jnp.mean(x * x, axis=-1, keepdims=True) + EPS) * g


def _rope(h, cosf, sinf, lane):
    half = ROPE_DIM // 2
    partner = jnp.where(lane < half, pltpu.roll(h, LANES - half, axis=1), pltpu.roll(h, half, axis=1))
    return h * cosf + partner * sinf


def _in_proj_kernel(x_ref, g_ref, w_ref, cos_ref, sin_ref, u_ref, q_ref, kcvc_ref, kv_ref, gt_ref):
    a = _rms(x_ref[...], g_ref[...]).astype(bf16)
    cosf = cos_ref[...]
    sinf = sin_ref[...]
    lane = lax.broadcasted_iota(jnp.int32, cosf.shape, 1)
    nu = u_ref.shape[1]
    nq = q_ref.shape[1]
    off = 0
    for c in range(0, nu, 512):
        u_ref[:, c:c + 512] = _dot(a, w_ref[:, off + c:off + c + 512]).astype(bf16)
    off += nu
    for c in range(0, nq, 512):
        acc = _dot(a, w_ref[:, off + c:off + c + 512])
        for j in range(4):
            h = _rope(acc[:, j * 128:(j + 1) * 128], cosf, sinf, lane) * ATTN_SCALE
            q_ref[:, c + j * 128:c + (j + 1) * 128] = h.astype(bf16)
    off += nq
    acc = _dot(a, w_ref[:, off:off + 512])
    for j in range(4):
        h = acc[:, j * 128:(j + 1) * 128]
        if j < 2:
            h = _rope(h, cosf, sinf, lane)
        kcvc_ref[:, j * 128:(j + 1) * 128] = h
    off += 512
    acc = _dot(a, w_ref[:, off:off + 512])
    for j in range(4):
        h = acc[:, j * 128:(j + 1) * 128]
        if j < 2:
            h = _rope(h, cosf, sinf, lane)
        kv_ref[:, j * 128:(j + 1) * 128] = h.astype(bf16)
    off += 512
    acc = _dot(a, w_ref[:, off:off + 512])
    for j in range(4):
        h = acc[:, j * 128:(j + 1) * 128]
        if j < 2:
            h = _rope(h, cosf, sinf, lane)
        kv_ref[:, 512 + j * 128:512 + (j + 1) * 128] = h.astype(bf16)
    off += 512
    gt_ref[...] = _dot(a, w_ref[:, off:off + 256])


def _in_proj(x2, g, w_p, cosf, sinf, *, T, tm):
    n_rows, d = x2.shape
    n_w = w_p.shape[1]
    tiles_per_seq = T // tm
    row = lambda i: (i, 0)
    const = lambda i: (0, 0)
    tab = lambda i: (i % tiles_per_seq, 0)
    return pl.pallas_call(
        _in_proj_kernel,
        grid=(n_rows // tm,),
        in_specs=[
            pl.BlockSpec((tm, d), row),
            pl.BlockSpec((1, d), const),
            pl.BlockSpec((d, n_w), const, pipeline_mode=pl.Buffered(1)),
            pl.BlockSpec((tm, LANES), tab),
            pl.BlockSpec((tm, LANES), tab),
        ],
        out_specs=[
            pl.BlockSpec((tm, 1024), row),
            pl.BlockSpec((tm, 1024), row),
            pl.BlockSpec((tm, 512), row),
            pl.BlockSpec((tm, 1024), row),
            pl.BlockSpec((tm, 256), row),
        ],
        out_shape=[
            jax.ShapeDtypeStruct((n_rows, 1024), bf16),
            jax.ShapeDtypeStruct((n_rows, 1024), bf16),
            jax.ShapeDtypeStruct((n_rows, 512), f32),
            jax.ShapeDtypeStruct((n_rows, 1024), bf16),
            jax.ShapeDtypeStruct((n_rows, 256), f32),
        ],
        compiler_params=pltpu.CompilerParams(
            dimension_semantics=("arbitrary",), vmem_limit_bytes=VMEM_LIMIT),
        name="in_proj",
    )(x2, g, w_p, cosf, sinf)


def _gelu_tanh(x):
    return 0.5 * x * (1.0 + jnp.tanh(0.7978845608028654 * (x + 0.044715 * (x * x * x))))


def _compress_kernel(r_ref, pek_ref, w1k_ref, w2k_ref, pev_ref, w1v_ref, w2v_ref, kc_ref, vc_ref):
    r = r_ref[0]
    n_rows = r.shape[0]
    half_k = (CMP_BLOCK // 2) * HEAD_DIM
    rowi = lax.broadcasted_iota(jnp.int32, (n_rows, HEAD_DIM), 0)
    for sec0, pe_ref, w1_ref, w2_ref, o_ref in ((0, pek_ref, w1k_ref, w2k_ref, kc_ref),
                                                (2, pev_ref, w1v_ref, w2v_ref, vc_ref)):
        pe_a = pe_ref[:, :half_k]
        pe_b = pe_ref[:, half_k:]
        for g in range(N_KV_GROUPS):
            base = (sec0 + g) * HEAD_DIM
            x = jnp.concatenate(
                [r[:, l * 512 + base:l * 512 + base + HEAD_DIM] for l in range(CMP_STRIDE)], axis=1)
            za = _dot((x + pe_a).astype(bf16), w1_ref[:half_k, :])
            zb = _dot((x + pe_b).astype(bf16), w1_ref[half_k:, :])
            h = za + pltpu.roll(zb, n_rows - 1, axis=0)
            o = _dot(_gelu_tanh(h).astype(bf16), w2_ref[...])
            o_ref[0, g] = jnp.where(rowi < n_rows - 1, o, 0.0).astype(bf16)


def _compress(r3, pek, w1k, w2k, pev, w1v, w2v):
    B, n_rows, width = r3.shape
    const = lambda b: (0, 0)
    out = jax.ShapeDtypeStruct((B, N_KV_GROUPS, n_rows, HEAD_DIM), bf16)
    ospec = pl.BlockSpec((1, N_KV_GROUPS, n_rows, HEAD_DIM), lambda b: (b, 0, 0, 0))
    return pl.pallas_call(
        _compress_kernel,
        grid=(B,),
        in_specs=[
            pl.BlockSpec((1, n_rows, width), lambda b: (b, 0, 0)),
            pl.BlockSpec(pek.shape, const), pl.BlockSpec(w1k.shape, const), pl.BlockSpec(w2k.shape, const),
            pl.BlockSpec(pev.shape, const), pl.BlockSpec(w1v.shape, const), pl.BlockSpec(w2v.shape, const),
        ],
        out_specs=[ospec, ospec],
        out_shape=[out, out],
        compiler_params=pltpu.CompilerParams(
            dimension_semantics=("arbitrary",), vmem_limit_bytes=VMEM_LIMIT),
        name="compress",
    )(r3, pek, w1k, w2k, pev, w1v, w2v)


def _nsa_kernel(q_ref, kc_ref, vc_ref, ks_ref, vs_ref, kw_ref, vw_ref, eneg_ref, gt_ref, o_ref,
                m_sc, l_sc, acc_sc, *, tq, tk, T):
    R = HEADS_PER_GROUP
    M = R * tq
    i = pl.program_id(2)
    t0 = i * tq
    n_cmp = kc_ref.shape[2]
    n_sel = T // SEL_BLOCK

    q = q_ref[...]
    qs = jnp.concatenate([q[:, r * HEAD_DIM:(r + 1) * HEAD_DIM] for r in range(R)], axis=0)

    kc = kc_ref[0, 0]
    vc = vc_ref[0, 0]
    s = _dot_nt(qs, kc)
    trow = t0 + (lax.broadcasted_iota(jnp.int32, (M, n_cmp), 0) & (tq - 1))
    ncol = lax.broadcasted_iota(jnp.int32, (M, n_cmp), 1)
    valid = (CMP_STRIDE * ncol + (CMP_BLOCK - 1)) <= trow
    s = jnp.where(valid, s, NEG)
    e = jnp.exp(s - jnp.max(s, axis=1, keepdims=True))
    p = jnp.where(valid, e / jnp.sum(e, axis=1, keepdims=True), 0.0)
    o_cmp = _dot(p.astype(bf16), vc)

    psum = p[0:tq]
    for r in range(1, R):
        psum = psum + p[r * tq:(r + 1) * tq]
    p_hi = psum.astype(bf16)
    p_lo = (psum - p_hi.astype(f32)).astype(bf16)
    sb = lax.broadcasted_iota(jnp.int32, (n_sel, n_cmp), 0) * SEL_BLOCK
    cb = lax.broadcasted_iota(jnp.int32, (n_sel, n_cmp), 1) * CMP_STRIDE
    ov_t = jnp.where((cb < sb + SEL_BLOCK) & (cb + CMP_BLOCK > sb), 1.0, 0.0).astype(bf16)
    imp_t = _dot_nt(ov_t, p_hi) + _dot_nt(ov_t, p_lo)

    jj = lax.broadcasted_iota(jnp.int32, (n_sel, tq), 0)
    cur = jnp.right_shift(t0 + lax.broadcasted_iota(jnp.int32, (n_sel, tq), 1), SEL_BLOCK.bit_length() - 1)
    forced = (jj == 0) | (jj == cur) | (jj == cur - 1)
    v = jnp.where(jj > cur, -jnp.inf, jnp.where(forced, jnp.inf, imp_t))
    rank = jnp.zeros((n_sel, tq), f32)
    for c in range(n_sel):
        vc_row = v[c:c + 1, :]
        ahead = (vc_row > v) | ((vc_row == v) & (jj > c))
        rank = rank + jnp.where(ahead, 1.0, 0.0)
    unsel_t = jnp.where(rank < float(min(N_SELECT, n_sel)), 0.0, 1.0)
    unsel_t = jnp.concatenate([unsel_t, jnp.zeros((LANES - n_sel, tq), f32)], axis=0)
    unsel = unsel_t.T.astype(bf16)
    q_aug = jnp.concatenate([qs, jnp.concatenate([unsel] * R, axis=0)], axis=1)

    m_sc[...] = jnp.full(m_sc.shape, NEG, f32)
    l_sc[...] = jnp.zeros(l_sc.shape, f32)
    acc_sc[...] = jnp.zeros(acc_sc.shape, f32)
    dpos = (lax.broadcasted_iota(jnp.int32, (M, tk), 0) & (tq - 1)) - lax.broadcasted_iota(jnp.int32, (M, tk), 1)

    def sel_step(kt, causal):
        k0 = pl.multiple_of(kt * tk, tk)
        k_aug = jnp.concatenate([ks_ref[pl.ds(k0, tk), :], eneg_ref[pl.ds(k0, tk), :]], axis=1)
        sc = _dot_nt(q_aug, k_aug)
        if causal:
            sc = jnp.where(dpos + (t0 - k0) >= 0, sc, NEG)
        m_prev = m_sc[...]
        m_new = jnp.maximum(m_prev, jnp.max(sc, axis=1, keepdims=True))
        alpha = jnp.exp(m_prev - m_new)
        pr = jnp.exp(sc - m_new)
        l_sc[...] = alpha * l_sc[...] + jnp.sum(pr, axis=1, keepdims=True)
        acc_sc[...] = alpha * acc_sc[...] + _dot(pr.astype(bf16), vs_ref[pl.ds(k0, tk), :])
        m_sc[...] = m_new

    n_full = t0 // tk

    def body(kt, carry):
        sel_step(kt, False)
        return carry

    lax.fori_loop(0, n_full, body, 0)
    sel_step(n_full, True)
    o_sel = acc_sc[...] / l_sc[...]

    wk = WINDOW + tq
    w0 = pl.multiple_of(jnp.maximum(t0 - WINDOW, 0), tq)
    sw = _dot_nt(qs, kw_ref[pl.ds(w0, wk), :])
    diff = (t0 - w0) + (lax.broadcasted_iota(jnp.int32, (M, wk), 0) & (tq - 1)) \
        - lax.broadcasted_iota(jnp.int32, (M, wk), 1)
    sw = jnp.where((diff >= 0) & (diff < WINDOW), sw, NEG)
    ew = jnp.exp(sw - jnp.max(sw, axis=1, keepdims=True))
    pw = ew / jnp.sum(ew, axis=1, keepdims=True)
    o_win = _dot(pw.astype(bf16), vw_ref[pl.ds(w0, wk), :])

    gt = jax.nn.sigmoid(gt_ref[...])
    for r in range(R):
        rows = slice(r * tq, (r + 1) * tq)
        c = r * N_BRANCH
        o = (gt[:, c:c + 1] * o_cmp[rows] + gt[:, c + 1:c + 2] * o_sel[rows]
             + gt[:, c + 2:c + 3] * o_win[rows])
        o_ref[:, r * HEAD_DIM:(r + 1) * HEAD_DIM] = o.astype(bf16)


def _nsa(q, kcmp, vcmp, kv, eneg, gates, *, B, T, tq, tk):
    G, R = N_KV_GROUPS, HEADS_PER_GROUP
    nt = T // tq
    n_cmp = kcmp.shape[2]
    assert tq == LANES and T % tk == 0 and tk % tq == 0 and T >= WINDOW + tq
    rowblk = lambda b, g, i: (b * nt + i, g)
    cmp_spec = pl.BlockSpec((1, 1, n_cmp, HEAD_DIM), lambda b, g, i: (b, g, 0, 0))
    kvspec = lambda sec: pl.BlockSpec((T, HEAD_DIM), lambda b, g, i: (b, sec * G + g))
    return pl.pallas_call(
        functools.partial(_nsa_kernel, tq=tq, tk=tk, T=T),
        grid=(B, G, nt),
        in_specs=[
            pl.BlockSpec((tq, R * HEAD_DIM), rowblk),
            cmp_spec, cmp_spec,
            kvspec(0), kvspec(1), kvspec(2), kvspec(3),
            pl.BlockSpec((T, LANES), lambda b, g, i: (0, 0)),
            pl.BlockSpec((tq, LANES), rowblk),
        ],
        out_specs=pl.BlockSpec((tq, R * HEAD_DIM), rowblk),
        out_shape=jax.ShapeDtypeStruct((B * T, G * R * HEAD_DIM), bf16),
        scratch_shapes=[
            pltpu.VMEM((R * tq, 1), f32),
            pltpu.VMEM((R * tq, 1), f32),
            pltpu.VMEM((R * tq, HEAD_DIM), f32),
        ],
        compiler_params=pltpu.CompilerParams(
            dimension_semantics=("arbitrary", "arbitrary", "arbitrary"), vmem_limit_bytes=VMEM_LIMIT),
        name="nsa",
    )(q, kcmp, vcmp, kv, kv, kv, kv, eneg, gates)


def _out_proj_kernel(x_ref, u_ref, uh_ref, y_ref, pw_ref, ps_ref, wo_ref, o_ref, *, tm, T):
    i = pl.program_id(0)
    tiles_per_seq = T // tm
    keep_halo = jnp.where((i % tiles_per_seq) == 0, 0.0, 1.0)
    u = u_ref[...]
    halo = uh_ref[...]
    tt = lax.broadcasted_iota(jnp.int32, (tm, tm), 0)
    ss = lax.broadcasted_iota(jnp.int32, (tm, tm), 1)
    th = lax.broadcasted_iota(jnp.int32, (tm, POOL_HALO), 0)
    rh = lax.broadcasted_iota(jnp.int32, (tm, POOL_HALO), 1)
    tseq = (i % tiles_per_seq) * tm + lax.broadcasted_iota(jnp.int32, (tm, 1), 0)
    n_pool = u.shape[1]
    gd = n_pool // len(POOL_WINDOWS)
    acc = x_ref[...] + _dot(y_ref[...], wo_ref[n_pool:, :])
    for gi, w in enumerate(POOL_WINDOWS):
        cols = slice(gi * gd, (gi + 1) * gd)
        band = jnp.where((tt - ss >= 0) & (tt - ss < w), 1.0, 0.0).astype(bf16)
        band_h = jnp.where(th + POOL_HALO - rh < w, 1.0, 0.0).astype(bf16)
        ug = u[:, cols]
        win_sum = _dot(band, ug) + keep_halo * _dot(band_h, halo[:, cols])
        count = jnp.minimum(tseq + 1, w).astype(f32)
        pooled = win_sum / count - ug.astype(f32)
        yg = _dot(pooled.astype(bf16), pw_ref[gi]) * ps_ref[:, cols]
        acc = acc + _dot(yg.astype(bf16), wo_ref[cols, :])
    o_ref[...] = acc


def _out_proj(x2, u, y_nsa, pool_w, pool_scale, w_out, *, T, tm):
    n_rows, d = x2.shape
    n_pool = u.shape[1]
    hb = tm // POOL_HALO
    row = lambda i: (i, 0)
    const2 = lambda i: (0, 0)
    return pl.pallas_call(
        functools.partial(_out_proj_kernel, tm=tm, T=T),
        grid=(n_rows // tm,),
        in_specs=[
            pl.BlockSpec((tm, d), row),
            pl.BlockSpec((tm, n_pool), row),
            pl.BlockSpec((POOL_HALO, n_pool), lambda i: (jnp.maximum(i * hb - 1, 0), 0)),
            pl.BlockSpec((tm, y_nsa.shape[1]), row),
            pl.BlockSpec(pool_w.shape, lambda i: (0, 0, 0)),
            pl.BlockSpec((1, n_pool), const2),
            pl.BlockSpec(w_out.shape, const2, pipeline_mode=pl.Buffered(1)),
        ],
        out_specs=pl.BlockSpec((tm, d), row),
        out_shape=jax.ShapeDtypeStruct((n_rows, d), f32),
        compiler_params=pltpu.CompilerParams(
            dimension_semantics=("arbitrary",), vmem_limit_bytes=VMEM_LIMIT),
        name="out_proj",
    )(x2, u, u, y_nsa, pool_w, pool_scale, w_out)


def _ffn_kernel(h_ref, g_ref, wg_ref, wu_ref, wd_ref, o_ref, n_sc, acc_sc):
    f = pl.program_id(1)

    @pl.when(f == 0)
    def _():
        n_sc[...] = _rms(h_ref[...], g_ref[...]).astype(bf16)
        acc_sc[...] = jnp.zeros(acc_sc.shape, f32)

    n = n_sc[...]
    gate = _dot(n, wg_ref[...])
    up = _dot(n, wu_ref[...])
    a = gate * jax.nn.sigmoid(gate) * up
    acc_sc[...] += _dot(a.astype(bf16), wd_ref[...])

    @pl.when(f == pl.num_programs(1) - 1)
    def _():
        o_ref[...] = h_ref[...] + acc_sc[...]


def _ffn(h, g, w_gate, w_up, w_down, *, tm, tf):
    n_rows, d = h.shape
    d_ff = w_gate.shape[1]
    return pl.pallas_call(
        _ffn_kernel,
        grid=(n_rows // tm, d_ff // tf),
        in_specs=[
            pl.BlockSpec((tm, d), lambda i, f: (i, 0)),
            pl.BlockSpec((1, d), lambda i, f: (0, 0)),
            pl.BlockSpec((d, tf), lambda i, f: (0, f)),
            pl.BlockSpec((d, tf), lambda i, f: (0, f)),
            pl.BlockSpec((tf, d), lambda i, f: (f, 0)),
        ],
        out_specs=pl.BlockSpec((tm, d), lambda i, f: (i, 0)),
        out_shape=jax.ShapeDtypeStruct((n_rows, d), f32),
        scratch_shapes=[pltpu.VMEM((tm, d), bf16), pltpu.VMEM((tm, d), f32)],
        compiler_params=pltpu.CompilerParams(
            dimension_semantics=("arbitrary", "arbitrary"), vmem_limit_bytes=VMEM_LIMIT),
        name="ffn",
    )(h, g, w_gate, w_up, w_down)


def _ple_kernel(h_ref, p_ref, g_ref, wg_ref, wp_ref, fg_ref, o_ref):
    h = h_ref[...]
    n = _rms(h, g_ref[...]).astype(bf16)
    gate = jax.nn.sigmoid(_dot(n, wg_ref[...]))
    h = h + _dot(p_ref[...].astype(bf16), wp_ref[...]) * gate
    o_ref[...] = _rms(h, fg_ref[...])


def _ple(h, p2, g, w_gate, w_proj, fg, *, tm):
    n_rows, d = h.shape
    row = lambda i: (i, 0)
    const = lambda i: (0, 0)
    return pl.pallas_call(
        _ple_kernel,
        grid=(n_rows // tm,),
        in_specs=[
            pl.BlockSpec((tm, d), row),
            pl.BlockSpec((tm, p2.shape[1]), row),
            pl.BlockSpec((1, d), const),
            pl.BlockSpec(w_gate.shape, const, pipeline_mode=pl.Buffered(1)),
            pl.BlockSpec(w_proj.shape, const),
            pl.BlockSpec((1, d), const),
        ],
        out_specs=pl.BlockSpec((tm, d), row),
        out_shape=jax.ShapeDtypeStruct((n_rows, d), f32),
        compiler_params=pltpu.CompilerParams(
            dimension_semantics=("arbitrary",), vmem_limit_bytes=VMEM_LIMIT),
        name="ple",
    )(h, p2, g, w_gate, w_proj, fg)


def _rope_tables(T):
    pos = jnp.arange(T, dtype=f32)
    inv_freq = ROPE_THETA ** (-jnp.arange(0, ROPE_DIM, 2, dtype=f32) / ROPE_DIM)
    ang = pos[:, None] * inv_freq[None, :]
    cos, sin = jnp.cos(ang), jnp.sin(ang)
    rest = HEAD_DIM - ROPE_DIM
    cosf = jnp.concatenate([cos, cos, jnp.ones((T, rest), f32)], axis=1)
    sinf = jnp.concatenate([-sin, sin, jnp.zeros((T, rest), f32)], axis=1)
    return cosf, sinf


def _mixer_ffn(h2, in_norm_g, w_in, pool_w, pool_scale, cmp_k_pe, cmp_k_w1, cmp_k_w2,
               cmp_v_pe, cmp_v_w1, cmp_v_w2, w_out, ffn_norm_g, w_gate, w_up, w_down, *, B, T):
    n_pool = pool_scale.shape[0]
    n_heads = N_KV_GROUPS * HEADS_PER_GROUP
    nsa_w = n_heads * HEAD_DIM
    kv_w = N_KV_GROUPS * HEAD_DIM
    assert w_in.shape[1] == n_pool + nsa_w + 6 * kv_w + n_heads * N_BRANCH and n_pool == 1024 and nsa_w == 1024

    o0 = n_pool + nsa_w + 6 * kv_w
    w_gt = w_in[:, o0:]
    per_g = HEADS_PER_GROUP * N_BRANCH
    w_gt = jnp.concatenate(
        [jnp.pad(w_gt[:, g * per_g:(g + 1) * per_g], ((0, 0), (0, LANES - per_g))) for g in range(N_KV_GROUPS)],
        axis=1)
    w_p = jnp.concatenate([w_in[:, :o0], w_gt], axis=1).astype(bf16)

    cosf, sinf = _rope_tables(T)
    u, q, kcvc, kv, gates = _in_proj(h2, in_norm_g[None, :], w_p, cosf, sinf, T=T, tm=512)

    r3 = kcvc.reshape(B, T // CMP_STRIDE, CMP_STRIDE * kcvc.shape[1])
    kcmp, vcmp = _compress(
        r3,
        cmp_k_pe.reshape(1, -1), cmp_k_w1.astype(bf16), cmp_k_w2.astype(bf16),
        cmp_v_pe.reshape(1, -1), cmp_v_w1.astype(bf16), cmp_v_w2.astype(bf16))

    key_blk = jnp.arange(T, dtype=jnp.int32)[:, None] // SEL_BLOCK
    eneg = jnp.where(key_blk == jnp.arange(LANES, dtype=jnp.int32)[None, :], NEG, 0.0).astype(bf16)
    y_nsa = _nsa(q, kcmp, vcmp, kv, eneg, gates, B=B, T=T, tq=128, tk=256)

    h2 = _out_proj(h2, u, y_nsa, pool_w.astype(bf16), pool_scale[None, :], w_out.astype(bf16), T=T, tm=512)
    return _ffn(h2, ffn_norm_g[None, :], w_gate.astype(bf16), w_up.astype(bf16), w_down.astype(bf16),
                tm=512, tf=512)


def kernel(x, p, in_norm_g, w_in, pool_w, pool_scale, cmp_k_pe, cmp_k_w1, cmp_k_w2, cmp_v_pe, cmp_v_w1,
           cmp_v_w2, w_out, ffn_norm_g, w_gate, w_up, w_down, ple_norm_g, w_ple_gate, w_ple_proj, final_norm_g):
    B, T, d = x.shape
    depth = w_in.shape[0]
    assert depth == 1, "the final rmsnorm is fused into the last layer's per-layer-embedding kernel"
    h2 = x.reshape(B * T, d)
    i = 0
    h2 = _mixer_ffn(h2, in_norm_g[i], w_in[i], pool_w[i], pool_scale[i], cmp_k_pe[i], cmp_k_w1[i],
                    cmp_k_w2[i], cmp_v_pe[i], cmp_v_w1[i], cmp_v_w2[i], w_out[i], ffn_norm_g[i], w_gate[i],
                    w_up[i], w_down[i], B=B, T=T)
    out = _ple(h2, p[i].reshape(B * T, -1), ple_norm_g[i][None, :], w_ple_gate[i].astype(bf16),
               w_ple_proj[i].astype(bf16), final_norm_g[None, :], tm=512)
    return out.reshape(B, T, d)
```

```python
import functools

import jax
import jax.numpy as jnp
from jax import lax
from jax.experimental import pallas as pl
from jax.experimental.pallas import tpu as pltpu

EPS = 1e-6
NEG = -1e30
LANES = 128
POOL_WINDOWS = (2, 4, 8, 16)
POOL_HALO = 16
HEAD_DIM = 128
N_KV_GROUPS = 2
HEADS_PER_GROUP = 4
N_BRANCH = 3
CMP_BLOCK = 32
CMP_STRIDE = 16
SEL_BLOCK = 64
N_SELECT = 16
WINDOW = 512
ROPE_THETA = 500000.0
ROPE_DIM = HEAD_DIM // 4
ATTN_SCALE = HEAD_DIM ** -0.5
Q_SCALE = ATTN_SCALE * 1.4426950408889634
VMEM_LIMIT = 56 * 1024 * 1024

f32 = jnp.float32
bf16 = jnp.bfloat16


def _dot(a, b):
    return jnp.dot(a, b, preferred_element_type=f32)


def _dot_nt(a, b):
    return lax.dot_general(a, b, (((1,), (1,)), ((), ())), preferred_element_type=f32)


def _rms(x, g):
    return x * lax.rsqrt(jnp.mean(x * x, axis=-1, keepdims=True) + EPS) * g


def _rope(h, cosf, sinf, lane):
    half = ROPE_DIM // 2
    partner = jnp.where(lane < half, pltpu.roll(h, LANES - half, axis=1), pltpu.roll(h, half, axis=1))
    return h * cosf + partner * sinf


def _in_proj_kernel(x_ref, g_ref, w_ref, cos_ref, sin_ref, u_ref, q_ref, kcvc_ref, kv_ref, gt_ref):
    a = _rms(x_ref[...], g_ref[...]).astype(bf16)
    cosf = cos_ref[...]
    sinf = sin_ref[...]
    lane = lax.broadcasted_iota(jnp.int32, cosf.shape, 1)
    nu = u_ref.shape[1]
    nq = q_ref.shape[1]
    off = 0
    for c in range(0, nu, 512):
        u_ref[:, c:c + 512] = _dot(a, w_ref[:, off + c:off + c + 512]).astype(bf16)
    off += nu
    for c in range(0, nq, 512):
        acc = _dot(a, w_ref[:, off + c:off + c + 512])
        for j in range(4):
            h = _rope(acc[:, j * 128:(j + 1) * 128], cosf, sinf, lane) * Q_SCALE
            q_ref[:, c + j * 128:c + (j + 1) * 128] = h.astype(bf16)
    off += nq
    acc = _dot(a, w_ref[:, off:off + 512])
    for j in range(4):
        h = acc[:, j * 128:(j + 1) * 128]
        if j < 2:
            h = _rope(h, cosf, sinf, lane)
        kcvc_ref[:, j * 128:(j + 1) * 128] = h
    off += 512
    acc = _dot(a, w_ref[:, off:off + 512])
    for j in range(4):
        h = acc[:, j * 128:(j + 1) * 128]
        if j < 2:
            h = _rope(h, cosf, sinf, lane)
        kv_ref[:, j * 128:(j + 1) * 128] = h.astype(bf16)
    off += 512
    acc = _dot(a, w_ref[:, off:off + 512])
    for j in range(4):
        h = acc[:, j * 128:(j + 1) * 128]
        if j < 2:
            h = _rope(h, cosf, sinf, lane)
        kv_ref[:, 512 + j * 128:512 + (j + 1) * 128] = h.astype(bf16)
    off += 512
    gt_ref[...] = _dot(a, w_ref[:, off:off + 256])


def _in_proj(x2, g, w_p, cosf, sinf, *, T, tm):
    n_rows, d = x2.shape
    n_w = w_p.shape[1]
    tiles_per_seq = T // tm
    row = lambda i: (i, 0)
    const = lambda i: (0, 0)
    tab = lambda i: (i % tiles_per_seq, 0)
    return pl.pallas_call(
        _in_proj_kernel,
        grid=(n_rows // tm,),
        in_specs=[
            pl.BlockSpec((tm, d), row),
            pl.BlockSpec((1, d), const),
            pl.BlockSpec((d, n_w), const, pipeline_mode=pl.Buffered(1)),
            pl.BlockSpec((tm, LANES), tab),
            pl.BlockSpec((tm, LANES), tab),
        ],
        out_specs=[
            pl.BlockSpec((tm, 1024), row),
            pl.BlockSpec((tm, 1024), row),
            pl.BlockSpec((tm, 512), row),
            pl.BlockSpec((tm, 1024), row),
            pl.BlockSpec((tm, 256), row),
        ],
        out_shape=[
            jax.ShapeDtypeStruct((n_rows, 1024), bf16),
            jax.ShapeDtypeStruct((n_rows, 1024), bf16),
            jax.ShapeDtypeStruct((n_rows, 512), f32),
            jax.ShapeDtypeStruct((n_rows, 1024), bf16),
            jax.ShapeDtypeStruct((n_rows, 256), f32),
        ],
        compiler_params=pltpu.CompilerParams(
            dimension_semantics=("arbitrary",), vmem_limit_bytes=VMEM_LIMIT),
        name="in_proj",
    )(x2, g, w_p, cosf, sinf)


def _gelu_tanh(x):
    return 0.5 * x * (1.0 + jnp.tanh(0.7978845608028654 * (x + 0.044715 * (x * x * x))))


def _compress_kernel(r_ref, pek_ref, w1k_ref, w2k_ref, pev_ref, w1v_ref, w2v_ref, kc_ref, vc_ref):
    r = r_ref[0]
    n_rows = r.shape[0]
    half_k = (CMP_BLOCK // 2) * HEAD_DIM
    rowi = lax.broadcasted_iota(jnp.int32, (n_rows, HEAD_DIM), 0)
    for sec0, pe_ref, w1_ref, w2_ref, o_ref in ((0, pek_ref, w1k_ref, w2k_ref, kc_ref),
                                                (2, pev_ref, w1v_ref, w2v_ref, vc_ref)):
        pe_a = pe_ref[:, :half_k]
        pe_b = pe_ref[:, half_k:]
        for g in range(N_KV_GROUPS):
            base = (sec0 + g) * HEAD_DIM
            x = jnp.concatenate(
                [r[:, l * 512 + base:l * 512 + base + HEAD_DIM] for l in range(CMP_STRIDE)], axis=1)
            za = _dot((x + pe_a).astype(bf16), w1_ref[:half_k, :])
            zb = _dot((x + pe_b).astype(bf16), w1_ref[half_k:, :])
            h = za + pltpu.roll(zb, n_rows - 1, axis=0)
            o = _dot(_gelu_tanh(h).astype(bf16), w2_ref[...])
            o_ref[0, g] = jnp.where(rowi < n_rows - 1, o, 0.0).astype(bf16)


def _compress(r3, pek, w1k, w2k, pev, w1v, w2v):
    B, n_rows, width = r3.shape
    const = lambda b: (0, 0)
    out = jax.ShapeDtypeStruct((B, N_KV_GROUPS, n_rows, HEAD_DIM), bf16)
    ospec = pl.BlockSpec((1, N_KV_GROUPS, n_rows, HEAD_DIM), lambda b: (b, 0, 0, 0))
    return pl.pallas_call(
        _compress_kernel,
        grid=(B,),
        in_specs=[
            pl.BlockSpec((1, n_rows, width), lambda b: (b, 0, 0)),
            pl.BlockSpec(pek.shape, const), pl.BlockSpec(w1k.shape, const), pl.BlockSpec(w2k.shape, const),
            pl.BlockSpec(pev.shape, const), pl.BlockSpec(w1v.shape, const), pl.BlockSpec(w2v.shape, const),
        ],
        out_specs=[ospec, ospec],
        out_shape=[out, out],
        compiler_params=pltpu.CompilerParams(
            dimension_semantics=("arbitrary",), vmem_limit_bytes=VMEM_LIMIT),
        name="compress",
    )(r3, pek, w1k, w2k, pev, w1v, w2v)


def _nsa_kernel(q_ref, kc_ref, vc_ref, ks_ref, vs_ref, kw_ref, vw_ref, eneg_ref, gt_ref, o_ref,
                s_sc, mx_sc, ls_sc, acc_sc, *, tq, tk, T):
    R = HEADS_PER_GROUP
    M = R * tq
    i = pl.program_id(2)
    t0 = i * tq
    n_cmp = kc_ref.shape[2]
    n_sel = T // SEL_BLOCK

    q = q_ref[...]
    qs = jnp.concatenate([q[:, r * HEAD_DIM:(r + 1) * HEAD_DIM] for r in range(R)], axis=0)

    kc = kc_ref[0, 0]
    vc = vc_ref[0, 0]
    s = _dot_nt(qs, kc)
    trow = t0 + (lax.broadcasted_iota(jnp.int32, (M, n_cmp), 0) & (tq - 1))
    ncol = lax.broadcasted_iota(jnp.int32, (M, n_cmp), 1)
    valid = (CMP_STRIDE * ncol + (CMP_BLOCK - 1)) <= trow
    s = jnp.where(valid, s, NEG)
    e = jnp.exp2(s - jnp.max(s, axis=1, keepdims=True))
    p = jnp.where(valid, e / jnp.sum(e, axis=1, keepdims=True), 0.0)
    o_cmp = _dot(p.astype(bf16), vc)

    psum = p[0:tq]
    for r in range(1, R):
        psum = psum + p[r * tq:(r + 1) * tq]
    p_hi = psum.astype(bf16)
    p_lo = (psum - p_hi.astype(f32)).astype(bf16)
    sb = lax.broadcasted_iota(jnp.int32, (n_sel, n_cmp), 0) * SEL_BLOCK
    cb = lax.broadcasted_iota(jnp.int32, (n_sel, n_cmp), 1) * CMP_STRIDE
    ov_t = jnp.where((cb < sb + SEL_BLOCK) & (cb + CMP_BLOCK > sb), 1.0, 0.0).astype(bf16)
    imp_t = _dot_nt(ov_t, p_hi) + _dot_nt(ov_t, p_lo)

    jj = lax.broadcasted_iota(jnp.int32, (n_sel, tq), 0)
    cur = jnp.right_shift(t0 + lax.broadcasted_iota(jnp.int32, (n_sel, tq), 1), SEL_BLOCK.bit_length() - 1)
    forced = (jj == 0) | (jj == cur) | (jj == cur - 1)
    v = jnp.where(jj > cur, -jnp.inf, jnp.where(forced, jnp.inf, imp_t))
    rank = jnp.zeros((n_sel, tq), f32)
    for c in range(n_sel):
        vc_row = v[c:c + 1, :]
        ahead = (vc_row > v) | ((vc_row == v) & (jj > c))
        rank = rank + jnp.where(ahead, 1.0, 0.0)
    unsel_t = jnp.where(rank < float(min(N_SELECT, n_sel)), 0.0, 1.0)
    unsel_t = jnp.concatenate([unsel_t, jnp.zeros((LANES - n_sel, tq), f32)], axis=0)
    unsel = unsel_t.T.astype(bf16)
    q_aug = jnp.concatenate([qs, jnp.concatenate([unsel] * R, axis=0)], axis=1)

    n_chunk = tk // LANES
    mx_sc[...] = jnp.full(mx_sc.shape, NEG, f32)
    dpos = (lax.broadcasted_iota(jnp.int32, (M, tk), 0) & (tq - 1)) - lax.broadcasted_iota(jnp.int32, (M, tk), 1)

    def score_step(kt, causal):
        k0 = pl.multiple_of(kt * tk, tk)
        k_aug = jnp.concatenate([ks_ref[pl.ds(k0, tk), :], eneg_ref[pl.ds(k0, tk), :]], axis=1)
        sc = _dot_nt(q_aug, k_aug)
        if causal:
            sc = jnp.where(dpos + (t0 - k0) >= 0, sc, NEG)
        s_sc[kt] = sc
        part = sc[:, :LANES]
        for c in range(1, n_chunk):
            part = jnp.maximum(part, sc[:, c * LANES:(c + 1) * LANES])
        mx_sc[...] = jnp.maximum(mx_sc[...], part)

    n_full = t0 // tk

    def score_body(kt, carry):
        score_step(kt, False)
        return carry

    lax.fori_loop(0, n_full, score_body, 0)
    score_step(n_full, True)
    mx_sc[...] = jnp.broadcast_to(jnp.max(mx_sc[...], axis=1, keepdims=True), mx_sc.shape)
    ls_sc[...] = jnp.zeros(ls_sc.shape, f32)
    acc_sc[...] = jnp.zeros(acc_sc.shape, f32)

    def pv_body(kt, carry):
        k0 = pl.multiple_of(kt * tk, tk)
        sc = s_sc[kt]
        mb = mx_sc[...]
        ps = [jnp.exp2(sc[:, c * LANES:(c + 1) * LANES] - mb) for c in range(n_chunk)]
        ls_sc[...] += functools.reduce(lambda a, b: a + b, ps)
        acc_sc[...] += _dot(jnp.concatenate(ps, axis=1).astype(bf16), vs_ref[pl.ds(k0, tk), :])
        return carry

    lax.fori_loop(0, n_full + 1, pv_body, 0)
    o_sel = acc_sc[...] / jnp.sum(ls_sc[...], axis=1, keepdims=True)

    wk = WINDOW + tq
    w0 = pl.multiple_of(jnp.maximum(t0 - WINDOW, 0), tq)
    sw = _dot_nt(qs, kw_ref[pl.ds(w0, wk), :])
    diff = (t0 - w0) + (lax.broadcasted_iota(jnp.int32, (M, wk), 0) & (tq - 1)) \
        - lax.broadcasted_iota(jnp.int32, (M, wk), 1)
    sw = jnp.where((diff >= 0) & (diff < WINDOW), sw, NEG)
    ew = jnp.exp2(sw - jnp.max(sw, axis=1, keepdims=True))
    o_win = _dot(ew.astype(bf16), vw_ref[pl.ds(w0, wk), :]) / jnp.sum(ew, axis=1, keepdims=True)

    gt = jax.nn.sigmoid(gt_ref[...])
    for r in range(R):
        rows = slice(r * tq, (r + 1) * tq)
        c = r * N_BRANCH
        o = (gt[:, c:c + 1] * o_cmp[rows] + gt[:, c + 1:c + 2] * o_sel[rows]
             + gt[:, c + 2:c + 3] * o_win[rows])
        o_ref[:, r * HEAD_DIM:(r + 1) * HEAD_DIM] = o.astype(bf16)


def _nsa(q, kcmp, vcmp, kv, eneg, gates, *, B, T, tq, tk):
    G, R = N_KV_GROUPS, HEADS_PER_GROUP
    nt = T // tq
    n_cmp = kcmp.shape[2]
    assert tq == LANES and T % tk == 0 and tk % tq == 0 and T >= WINDOW + tq
    rowblk = lambda b, g, i: (b * nt + i, g)
    cmp_spec = pl.BlockSpec((1, 1, n_cmp, HEAD_DIM), lambda b, g, i: (b, g, 0, 0))
    kvspec = lambda sec: pl.BlockSpec((T, HEAD_DIM), lambda b, g, i: (b, sec * G + g))
    return pl.pallas_call(
        functools.partial(_nsa_kernel, tq=tq, tk=tk, T=T),
        grid=(B, G, nt),
        in_specs=[
            pl.BlockSpec((tq, R * HEAD_DIM), rowblk),
            cmp_spec, cmp_spec,
            kvspec(0), kvspec(1), kvspec(2), kvspec(3),
            pl.BlockSpec((T, LANES), lambda b, g, i: (0, 0)),
            pl.BlockSpec((tq, LANES), rowblk),
        ],
        out_specs=pl.BlockSpec((tq, R * HEAD_DIM), rowblk),
        out_shape=jax.ShapeDtypeStruct((B * T, G * R * HEAD_DIM), bf16),
        scratch_shapes=[
            pltpu.VMEM((T // tk, R * tq, tk), f32),
            pltpu.VMEM((R * tq, LANES), f32),
            pltpu.VMEM((R * tq, LANES), f32),
            pltpu.VMEM((R * tq, HEAD_DIM), f32),
        ],
        compiler_params=pltpu.CompilerParams(
            dimension_semantics=("arbitrary", "arbitrary", "arbitrary"), vmem_limit_bytes=VMEM_LIMIT),
        name="nsa",
    )(q, kcmp, vcmp, kv, kv, kv, kv, eneg, gates)


def _out_proj_kernel(x_ref, u_ref, uh_ref, y_ref, pw_ref, ps_ref, wo_ref, o_ref, *, tm, T):
    i = pl.program_id(0)
    tiles_per_seq = T // tm
    keep_halo = jnp.where((i % tiles_per_seq) == 0, 0.0, 1.0)
    u = u_ref[...]
    halo = uh_ref[...]
    tt = lax.broadcasted_iota(jnp.int32, (tm, tm), 0)
    ss = lax.broadcasted_iota(jnp.int32, (tm, tm), 1)
    th = lax.broadcasted_iota(jnp.int32, (tm, POOL_HALO), 0)
    rh = lax.broadcasted_iota(jnp.int32, (tm, POOL_HALO), 1)
    tseq = (i % tiles_per_seq) * tm + lax.broadcasted_iota(jnp.int32, (tm, 1), 0)
    n_pool = u.shape[1]
    gd = n_pool // len(POOL_WINDOWS)
    acc = x_ref[...] + _dot(y_ref[...], wo_ref[n_pool:, :])
    for gi, w in enumerate(POOL_WINDOWS):
        cols = slice(gi * gd, (gi + 1) * gd)
        band = jnp.where((tt - ss >= 0) & (tt - ss < w), 1.0, 0.0).astype(bf16)
        band_h = jnp.where(th + POOL_HALO - rh < w, 1.0, 0.0).astype(bf16)
        ug = u[:, cols]
        win_sum = _dot(band, ug) + keep_halo * _dot(band_h, halo[:, cols])
        count = jnp.minimum(tseq + 1, w).astype(f32)
        pooled = win_sum / count - ug.astype(f32)
        yg = _dot(pooled.astype(bf16), pw_ref[gi]) * ps_ref[:, cols]
        acc = acc + _dot(yg.astype(bf16), wo_ref[cols, :])
    o_ref[...] = acc


def _out_proj(x2, u, y_nsa, pool_w, pool_scale, w_out, *, T, tm):
    n_rows, d = x2.shape
    n_pool = u.shape[1]
    hb = tm // POOL_HALO
    row = lambda i: (i, 0)
    const2 = lambda i: (0, 0)
    return pl.pallas_call(
        functools.partial(_out_proj_kernel, tm=tm, T=T),
        grid=(n_rows // tm,),
        in_specs=[
            pl.BlockSpec((tm, d), row),
            pl.BlockSpec((tm, n_pool), row),
            pl.BlockSpec((POOL_HALO, n_pool), lambda i: (jnp.maximum(i * hb - 1, 0), 0)),
            pl.BlockSpec((tm, y_nsa.shape[1]), row),
            pl.BlockSpec(pool_w.shape, lambda i: (0, 0, 0)),
            pl.BlockSpec((1, n_pool), const2),
            pl.BlockSpec(w_out.shape, const2, pipeline_mode=pl.Buffered(1)),
        ],
        out_specs=pl.BlockSpec((tm, d), row),
        out_shape=jax.ShapeDtypeStruct((n_rows, d), f32),
        compiler_params=pltpu.CompilerParams(
            dimension_semantics=("arbitrary",), vmem_limit_bytes=VMEM_LIMIT),
        name="out_proj",
    )(x2, u, u, y_nsa, pool_w, pool_scale, w_out)


def _ffn_kernel(h_ref, g_ref, wg_ref, wu_ref, wd_ref, o_ref, n_sc, acc_sc):
    f = pl.program_id(1)

    @pl.when(f == 0)
    def _():
        n_sc[...] = _rms(h_ref[...], g_ref[...]).astype(bf16)
        acc_sc[...] = jnp.zeros(acc_sc.shape, f32)

    n = n_sc[...]
    gate = _dot(n, wg_ref[...])
    up = _dot(n, wu_ref[...])
    a = gate * jax.nn.sigmoid(gate) * up
    acc_sc[...] += _dot(a.astype(bf16), wd_ref[...])

    @pl.when(f == pl.num_programs(1) - 1)
    def _():
        o_ref[...] = h_ref[...] + acc_sc[...]


def _ffn(h, g, w_gate, w_up, w_down, *, tm, tf):
    n_rows, d = h.shape
    d_ff = w_gate.shape[1]
    return pl.pallas_call(
        _ffn_kernel,
        grid=(n_rows // tm, d_ff // tf),
        in_specs=[
            pl.BlockSpec((tm, d), lambda i, f: (i, 0)),
            pl.BlockSpec((1, d), lambda i, f: (0, 0)),
            pl.BlockSpec((d, tf), lambda i, f: (0, f)),
            pl.BlockSpec((d, tf), lambda i, f: (0, f)),
            pl.BlockSpec((tf, d), lambda i, f: (f, 0)),
        ],
        out_specs=pl.BlockSpec((tm, d), lambda i, f: (i, 0)),
        out_shape=jax.ShapeDtypeStruct((n_rows, d), f32),
        scratch_shapes=[pltpu.VMEM((tm, d), bf16), pltpu.VMEM((tm, d), f32)],
        compiler_params=pltpu.CompilerParams(
            dimension_semantics=("arbitrary", "arbitrary"), vmem_limit_bytes=VMEM_LIMIT),
        name="ffn",
    )(h, g, w_gate, w_up, w_down)


def _ple_kernel(h_ref, p_ref, g_ref, wg_ref, wp_ref, fg_ref, o_ref):
    h = h_ref[...]
    n = _rms(h, g_ref[...]).astype(bf16)
    gate = jax.nn.sigmoid(_dot(n, wg_ref[...]))
    h = h + _dot(p_ref[...].astype(bf16), wp_ref[...]) * gate
    o_ref[...] = _rms(h, fg_ref[...])


def _ple(h, p2, g, w_gate, w_proj, fg, *, tm):
    n_rows, d = h.shape
    row = lambda i: (i, 0)
    const = lambda i: (0, 0)
    return pl.pallas_call(
        _ple_kernel,
        grid=(n_rows // tm,),
        in_specs=[
            pl.BlockSpec((tm, d), row),
            pl.BlockSpec((tm, p2.shape[1]), row),
            pl.BlockSpec((1, d), const),
            pl.BlockSpec(w_gate.shape, const, pipeline_mode=pl.Buffered(1)),
            pl.BlockSpec(w_proj.shape, const),
            pl.BlockSpec((1, d), const),
        ],
        out_specs=pl.BlockSpec((tm, d), row),
        out_shape=jax.ShapeDtypeStruct((n_rows, d), f32),
        compiler_params=pltpu.CompilerParams(
            dimension_semantics=("arbitrary",), vmem_limit_bytes=VMEM_LIMIT),
        name="ple",
    )(h, p2, g, w_gate, w_proj, fg)


def _rope_tables(T):
    pos = jnp.arange(T, dtype=f32)
    inv_freq = ROPE_THETA ** (-jnp.arange(0, ROPE_DIM, 2, dtype=f32) / ROPE_DIM)
    ang = pos[:, None] * inv_freq[None, :]
    cos, sin = jnp.cos(ang), jnp.sin(ang)
    rest = HEAD_DIM - ROPE_DIM
    cosf = jnp.concatenate([cos, cos, jnp.ones((T, rest), f32)], axis=1)
    sinf = jnp.concatenate([-sin, sin, jnp.zeros((T, rest), f32)], axis=1)
    return cosf, sinf


def _mixer_ffn(h2, in_norm_g, w_in, pool_w, pool_scale, cmp_k_pe, cmp_k_w1, cmp_k_w2,
               cmp_v_pe, cmp_v_w1, cmp_v_w2, w_out, ffn_norm_g, w_gate, w_up, w_down, *, B, T):
    n_pool = pool_scale.shape[0]
    n_heads = N_KV_GROUPS * HEADS_PER_GROUP
    nsa_w = n_heads * HEAD_DIM
    kv_w = N_KV_GROUPS * HEAD_DIM
    assert w_in.shape[1] == n_pool + nsa_w + 6 * kv_w + n_heads * N_BRANCH and n_pool == 1024 and nsa_w == 1024

    o0 = n_pool + nsa_w + 6 * kv_w
    w_gt = w_in[:, o0:]
    per_g = HEADS_PER_GROUP * N_BRANCH
    w_gt = jnp.concatenate(
        [jnp.pad(w_gt[:, g * per_g:(g + 1) * per_g], ((0, 0), (0, LANES - per_g))) for g in range(N_KV_GROUPS)],
        axis=1)
    w_p = jnp.concatenate([w_in[:, :o0], w_gt], axis=1).astype(bf16)

    cosf, sinf = _rope_tables(T)
    u, q, kcvc, kv, gates = _in_proj(h2, in_norm_g[None, :], w_p, cosf, sinf, T=T, tm=512)

    r3 = kcvc.reshape(B, T // CMP_STRIDE, CMP_STRIDE * kcvc.shape[1])
    kcmp, vcmp = _compress(
        r3,
        cmp_k_pe.reshape(1, -1), cmp_k_w1.astype(bf16), cmp_k_w2.astype(bf16),
        cmp_v_pe.reshape(1, -1), cmp_v_w1.astype(bf16), cmp_v_w2.astype(bf16))

    key_blk = jnp.arange(T, dtype=jnp.int32)[:, None] // SEL_BLOCK
    eneg = jnp.where(key_blk == jnp.arange(LANES, dtype=jnp.int32)[None, :], NEG, 0.0).astype(bf16)
    y_nsa = _nsa(q, kcmp, vcmp, kv, eneg, gates, B=B, T=T, tq=128, tk=256)

    h2 = _out_proj(h2, u, y_nsa, pool_w.astype(bf16), pool_scale[None, :], w_out.astype(bf16), T=T, tm=512)
    return _ffn(h2, ffn_norm_g[None, :], w_gate.astype(bf16), w_up.astype(bf16), w_down.astype(bf16),
                tm=512, tf=512)


def kernel(x, p, in_norm_g, w_in, pool_w, pool_scale, cmp_k_pe, cmp_k_w1, cmp_k_w2, cmp_v_pe, cmp_v_w1,
           cmp_v_w2, w_out, ffn_norm_g, w_gate, w_up, w_down, ple_norm_g, w_ple_gate, w_ple_proj, final_norm_g):
    B, T, d = x.shape
    depth = w_in.shape[0]
    assert depth == 1, "the final rmsnorm is fused into the last layer's per-layer-embedding kernel"
    h2 = x.reshape(B * T, d)
    i = 0
    h2 = _mixer_ffn(h2, in_norm_g[i], w_in[i], pool_w[i], pool_scale[i], cmp_k_pe[i], cmp_k_w1[i],
                    cmp_k_w2[i], cmp_v_pe[i], cmp_v_w1[i], cmp_v_w2[i], w_out[i], ffn_norm_g[i], w_gate[i],
                    w_up[i], w_down[i], B=B, T=T)
    out = _ple(h2, p[i].reshape(B * T, -1), ple_norm_g[i][None, :], w_ple_gate[i].astype(bf16),
               w_ple_proj[i].astype(bf16), final_norm_g[None, :], tm=512)
    return out.reshape(B, T, d)
```

```python
import functools

import jax
import jax.numpy as jnp
from jax import lax
from jax.experimental import pallas as pl
from jax.experimental.pallas import tpu as pltpu

EPS = 1e-6
NEG = -1e30
LANES = 128
POOL_WINDOWS = (2, 4, 8, 16)
POOL_HALO = 16
HEAD_DIM = 128
N_KV_GROUPS = 2
HEADS_PER_GROUP = 4
N_BRANCH = 3
CMP_BLOCK = 32
CMP_STRIDE = 16
SEL_BLOCK = 64
N_SELECT = 16
WINDOW = 512
ROPE_THETA = 500000.0
ROPE_DIM = HEAD_DIM // 4
ATTN_SCALE = HEAD_DIM ** -0.5
Q_SCALE = ATTN_SCALE * 1.4426950408889634
VMEM_LIMIT = 56 * 1024 * 1024

f32 = jnp.float32
bf16 = jnp.bfloat16


def _dot(a, b):
    return jnp.dot(a, b, preferred_element_type=f32)


def _dot_nt(a, b):
    return lax.dot_general(a, b, (((1,), (1,)), ((), ())), preferred_element_type=f32)


def _rms(x, g):
    return x * lax.rsqrt(jnp.mean(x * x, axis=-1, keepdims=True) + EPS) * g


def _rope(h, cosf, sinf, lane):
    half = ROPE_DIM // 2
    partner = jnp.where(lane < half, pltpu.roll(h, LANES - half, axis=1), pltpu.roll(h, half, axis=1))
    return h * cosf + partner * sinf


def _in_proj_kernel(x_ref, g_ref, w_ref, cos_ref, sin_ref, u_ref, q_ref, kcvc_ref, kv_ref, gt_ref):
    a = _rms(x_ref[...], g_ref[...]).astype(bf16)
    cosf = cos_ref[...]
    sinf = sin_ref[...]
    lane = lax.broadcasted_iota(jnp.int32, cosf.shape, 1)
    nu = u_ref.shape[1]
    nq = q_ref.shape[1]
    off = 0
    for c in range(0, nu, 512):
        u_ref[:, c:c + 512] = _dot(a, w_ref[:, off + c:off + c + 512]).astype(bf16)
    off += nu
    for c in range(0, nq, 512):
        acc = _dot(a, w_ref[:, off + c:off + c + 512])
        for j in range(4):
            h = _rope(acc[:, j * 128:(j + 1) * 128], cosf, sinf, lane) * Q_SCALE
            q_ref[:, c + j * 128:c + (j + 1) * 128] = h.astype(bf16)
    off += nq
    acc = _dot(a, w_ref[:, off:off + 512])
    for j in range(4):
        h = acc[:, j * 128:(j + 1) * 128]
        if j < 2:
            h = _rope(h, cosf, sinf, lane)
        kcvc_ref[:, j * 128:(j + 1) * 128] = h
    off += 512
    acc = _dot(a, w_ref[:, off:off + 512])
    for j in range(4):
        h = acc[:, j * 128:(j + 1) * 128]
        if j < 2:
            h = _rope(h, cosf, sinf, lane)
        kv_ref[:, j * 128:(j + 1) * 128] = h.astype(bf16)
    off += 512
    acc = _dot(a, w_ref[:, off:off + 512])
    for j in range(4):
        h = acc[:, j * 128:(j + 1) * 128]
        if j < 2:
            h = _rope(h, cosf, sinf, lane)
        kv_ref[:, 512 + j * 128:512 + (j + 1) * 128] = h.astype(bf16)
    off += 512
    gt_ref[...] = _dot(a, w_ref[:, off:off + 256])


def _in_proj(x2, g, w_p, cosf, sinf, *, T, tm):
    n_rows, d = x2.shape
    n_w = w_p.shape[1]
    tiles_per_seq = T // tm
    row = lambda i: (i, 0)
    const = lambda i: (0, 0)
    tab = lambda i: (i % tiles_per_seq, 0)
    return pl.pallas_call(
        _in_proj_kernel,
        grid=(n_rows // tm,),
        in_specs=[
            pl.BlockSpec((tm, d), row),
            pl.BlockSpec((1, d), const),
            pl.BlockSpec((d, n_w), const, pipeline_mode=pl.Buffered(1)),
            pl.BlockSpec((tm, LANES), tab),
            pl.BlockSpec((tm, LANES), tab),
        ],
        out_specs=[
            pl.BlockSpec((tm, 1024), row),
            pl.BlockSpec((tm, 1024), row),
            pl.BlockSpec((tm, 512), row),
            pl.BlockSpec((tm, 1024), row),
            pl.BlockSpec((tm, 256), row),
        ],
        out_shape=[
            jax.ShapeDtypeStruct((n_rows, 1024), bf16),
            jax.ShapeDtypeStruct((n_rows, 1024), bf16),
            jax.ShapeDtypeStruct((n_rows, 512), f32),
            jax.ShapeDtypeStruct((n_rows, 1024), bf16),
            jax.ShapeDtypeStruct((n_rows, 256), f32),
        ],
        compiler_params=pltpu.CompilerParams(
            dimension_semantics=("arbitrary",), vmem_limit_bytes=VMEM_LIMIT),
        name="in_proj",
    )(x2, g, w_p, cosf, sinf)


def _gelu_tanh(x):
    return 0.5 * x * (1.0 + jnp.tanh(0.7978845608028654 * (x + 0.044715 * (x * x * x))))


def _compress_kernel(r_ref, pek_ref, w1k_ref, w2k_ref, pev_ref, w1v_ref, w2v_ref, kc_ref, vc_ref):
    r = r_ref[0]
    n_rows = r.shape[0]
    half_k = (CMP_BLOCK // 2) * HEAD_DIM
    rowi = lax.broadcasted_iota(jnp.int32, (n_rows, HEAD_DIM), 0)
    for sec0, pe_ref, w1_ref, w2_ref, o_ref in ((0, pek_ref, w1k_ref, w2k_ref, kc_ref),
                                                (2, pev_ref, w1v_ref, w2v_ref, vc_ref)):
        pe_a = pe_ref[:, :half_k]
        pe_b = pe_ref[:, half_k:]
        for g in range(N_KV_GROUPS):
            base = (sec0 + g) * HEAD_DIM
            x = jnp.concatenate(
                [r[:, l * 512 + base:l * 512 + base + HEAD_DIM] for l in range(CMP_STRIDE)], axis=1)
            za = _dot((x + pe_a).astype(bf16), w1_ref[:half_k, :])
            zb = _dot((x + pe_b).astype(bf16), w1_ref[half_k:, :])
            h = za + pltpu.roll(zb, n_rows - 1, axis=0)
            o = _dot(_gelu_tanh(h).astype(bf16), w2_ref[...])
            o_ref[0, g] = jnp.where(rowi < n_rows - 1, o, 0.0).astype(bf16)


def _compress(r3, pek, w1k, w2k, pev, w1v, w2v):
    B, n_rows, width = r3.shape
    const = lambda b: (0, 0)
    out = jax.ShapeDtypeStruct((B, N_KV_GROUPS, n_rows, HEAD_DIM), bf16)
    ospec = pl.BlockSpec((1, N_KV_GROUPS, n_rows, HEAD_DIM), lambda b: (b, 0, 0, 0))
    return pl.pallas_call(
        _compress_kernel,
        grid=(B,),
        in_specs=[
            pl.BlockSpec((1, n_rows, width), lambda b: (b, 0, 0)),
            pl.BlockSpec(pek.shape, const), pl.BlockSpec(w1k.shape, const), pl.BlockSpec(w2k.shape, const),
            pl.BlockSpec(pev.shape, const), pl.BlockSpec(w1v.shape, const), pl.BlockSpec(w2v.shape, const),
        ],
        out_specs=[ospec, ospec],
        out_shape=[out, out],
        compiler_params=pltpu.CompilerParams(
            dimension_semantics=("arbitrary",), vmem_limit_bytes=VMEM_LIMIT),
        name="compress",
    )(r3, pek, w1k, w2k, pev, w1v, w2v)


def _nsa_kernel(q_ref, kc_ref, vc_ref, ks_ref, vs_ref, kw_ref, vw_ref, eneg_ref, gt_ref, o_ref,
                s_sc, mx_sc, ls_sc, acc_sc, *, tq, tk, T):
    R = HEADS_PER_GROUP
    M = R * tq
    i = pl.program_id(2)
    t0 = i * tq
    n_cmp = kc_ref.shape[2]
    n_sel = T // SEL_BLOCK

    q = q_ref[...]
    qs = jnp.concatenate([q[:, r * HEAD_DIM:(r + 1) * HEAD_DIM] for r in range(R)], axis=0)

    wk = WINDOW + tq
    w0 = pl.multiple_of(jnp.maximum(t0 - WINDOW, 0), tq)
    sw = _dot_nt(qs, kw_ref[pl.ds(w0, wk), :])
    diff = (t0 - w0) + (lax.broadcasted_iota(jnp.int32, (M, wk), 0) & (tq - 1)) \
        - lax.broadcasted_iota(jnp.int32, (M, wk), 1)
    sw = jnp.where((diff & -WINDOW) == 0, sw, NEG)
    ew = jnp.exp2(sw - jnp.max(sw, axis=1, keepdims=True))
    o_win = _dot(ew.astype(bf16), vw_ref[pl.ds(w0, wk), :]) / jnp.sum(ew, axis=1, keepdims=True)

    kc = kc_ref[0, 0]
    vc = vc_ref[0, 0]
    s = _dot_nt(qs, kc)
    trow = t0 + (lax.broadcasted_iota(jnp.int32, (M, n_cmp), 0) & (tq - 1))
    ncol = lax.broadcasted_iota(jnp.int32, (M, n_cmp), 1)
    valid = (CMP_STRIDE * ncol + (CMP_BLOCK - 1)) <= trow
    s = jnp.where(valid, s, NEG)
    e = jnp.exp2(s - jnp.max(s, axis=1, keepdims=True))
    p = jnp.where(valid, e / jnp.sum(e, axis=1, keepdims=True), 0.0)
    o_cmp = _dot(p.astype(bf16), vc)

    psum = p[0:tq]
    for r in range(1, R):
        psum = psum + p[r * tq:(r + 1) * tq]
    p_hi = psum.astype(bf16)
    p_lo = (psum - p_hi.astype(f32)).astype(bf16)
    sb = lax.broadcasted_iota(jnp.int32, (n_sel, n_cmp), 0) * SEL_BLOCK
    cb = lax.broadcasted_iota(jnp.int32, (n_sel, n_cmp), 1) * CMP_STRIDE
    ov_t = jnp.where((cb < sb + SEL_BLOCK) & (cb + CMP_BLOCK > sb), 1.0, 0.0).astype(bf16)
    imp_t = _dot_nt(ov_t, p_hi) + _dot_nt(ov_t, p_lo)

    jj = lax.broadcasted_iota(jnp.int32, (n_sel, tq), 0)
    cur = jnp.right_shift(t0 + lax.broadcasted_iota(jnp.int32, (n_sel, tq), 1), SEL_BLOCK.bit_length() - 1)
    forced = (jj == 0) | (jj == cur) | (jj == cur - 1)
    v = jnp.where(jj > cur, -jnp.inf, jnp.where(forced, jnp.inf, imp_t))
    sub = 8
    slabs = [v[k * sub:(k + 1) * sub] for k in range(n_sel // sub)]
    ranks = [jnp.zeros((sub, tq), f32) for _ in slabs]
    j_in = lax.broadcasted_iota(jnp.int32, (sub, tq), 0)
    for c in range(n_sel):
        v_c = v[c:c + 1, :]
        for k, v_k in enumerate(slabs):
            if k * sub > c:
                ahead = jnp.where(v_c >= v_k, 1.0, 0.0)
            elif (k + 1) * sub <= c:
                ahead = jnp.where(v_c > v_k, 1.0, 0.0)
            else:
                ahead = jnp.where(j_in > c - k * sub, jnp.where(v_c >= v_k, 1.0, 0.0),
                                  jnp.where(v_c > v_k, 1.0, 0.0))
            ranks[k] = ranks[k] + ahead
    rank = jnp.concatenate(ranks, axis=0)
    unsel_t = jnp.where(rank < float(min(N_SELECT, n_sel)), 0.0, 1.0)
    unsel_t = jnp.concatenate([unsel_t, jnp.zeros((LANES - n_sel, tq), f32)], axis=0)
    unsel = unsel_t.T.astype(bf16)
    q_aug = jnp.concatenate([qs, jnp.concatenate([unsel] * R, axis=0)], axis=1)

    n_chunk = tk // LANES
    mx_sc[...] = jnp.full(mx_sc.shape, NEG, f32)
    dpos = (lax.broadcasted_iota(jnp.int32, (M, tk), 0) & (tq - 1)) - lax.broadcasted_iota(jnp.int32, (M, tk), 1)

    def score_step(kt, causal):
        k0 = pl.multiple_of(kt * tk, tk)
        k_aug = jnp.concatenate([ks_ref[pl.ds(k0, tk), :], eneg_ref[pl.ds(k0, tk), :]], axis=1)
        sc = _dot_nt(q_aug, k_aug)
        if causal:
            sc = jnp.where(dpos + (t0 - k0) >= 0, sc, NEG)
        s_sc[kt] = sc
        part = sc[:, :LANES]
        for c in range(1, n_chunk):
            part = jnp.maximum(part, sc[:, c * LANES:(c + 1) * LANES])
        mx_sc[...] = jnp.maximum(mx_sc[...], part)

    n_full = t0 // tk

    def score_body(kt, carry):
        score_step(kt, False)
        return carry

    lax.fori_loop(0, n_full, score_body, 0)
    score_step(n_full, True)
    mx_sc[...] = jnp.broadcast_to(jnp.max(mx_sc[...], axis=1, keepdims=True), mx_sc.shape)
    ls_sc[...] = jnp.zeros(ls_sc.shape, f32)
    acc_sc[...] = jnp.zeros(acc_sc.shape, f32)

    def pv_body(kt, carry):
        k0 = pl.multiple_of(kt * tk, tk)
        sc = s_sc[kt]
        mb = mx_sc[...]
        ps = [jnp.exp2(sc[:, c * LANES:(c + 1) * LANES] - mb) for c in range(n_chunk)]
        ls_sc[...] += functools.reduce(lambda a, b: a + b, ps)
        acc_sc[...] += _dot(jnp.concatenate(ps, axis=1).astype(bf16), vs_ref[pl.ds(k0, tk), :])
        return carry

    lax.fori_loop(0, n_full + 1, pv_body, 0)
    o_sel = acc_sc[...] / jnp.sum(ls_sc[...], axis=1, keepdims=True)

    gt = jax.nn.sigmoid(gt_ref[...])
    for r in range(R):
        rows = slice(r * tq, (r + 1) * tq)
        c = r * N_BRANCH
        o = (gt[:, c:c + 1] * o_cmp[rows] + gt[:, c + 1:c + 2] * o_sel[rows]
             + gt[:, c + 2:c + 3] * o_win[rows])
        o_ref[:, r * HEAD_DIM:(r + 1) * HEAD_DIM] = o.astype(bf16)


def _nsa(q, kcmp, vcmp, kv, eneg, gates, *, B, T, tq, tk):
    G, R = N_KV_GROUPS, HEADS_PER_GROUP
    nt = T // tq
    n_cmp = kcmp.shape[2]
    assert tq % LANES == 0 and tq & (tq - 1) == 0 and T % tk == 0 and tk % tq == 0 and T >= WINDOW + tq
    rowblk = lambda b, g, i: (b * nt + i, g)
    cmp_spec = pl.BlockSpec((1, 1, n_cmp, HEAD_DIM), lambda b, g, i: (b, g, 0, 0))
    kvspec = lambda sec: pl.BlockSpec((T, HEAD_DIM), lambda b, g, i: (b, sec * G + g))
    return pl.pallas_call(
        functools.partial(_nsa_kernel, tq=tq, tk=tk, T=T),
        grid=(B, G, nt),
        in_specs=[
            pl.BlockSpec((tq, R * HEAD_DIM), rowblk),
            cmp_spec, cmp_spec,
            kvspec(0), kvspec(1), kvspec(2), kvspec(3),
            pl.BlockSpec((T, LANES), lambda b, g, i: (0, 0)),
            pl.BlockSpec((tq, LANES), rowblk),
        ],
        out_specs=pl.BlockSpec((tq, R * HEAD_DIM), rowblk),
        out_shape=jax.ShapeDtypeStruct((B * T, G * R * HEAD_DIM), bf16),
        scratch_shapes=[
            pltpu.VMEM((T // tk, R * tq, tk), f32),
            pltpu.VMEM((R * tq, LANES), f32),
            pltpu.VMEM((R * tq, LANES), f32),
            pltpu.VMEM((R * tq, HEAD_DIM), f32),
        ],
        compiler_params=pltpu.CompilerParams(
            dimension_semantics=("arbitrary", "arbitrary", "arbitrary"), vmem_limit_bytes=VMEM_LIMIT),
        name="nsa",
    )(q, kcmp, vcmp, kv, kv, kv, kv, eneg, gates)


def _out_proj_kernel(x_ref, u_ref, uh_ref, y_ref, pw_ref, ps_ref, wo_ref, o_ref, *, tm, T):
    i = pl.program_id(0)
    tiles_per_seq = T // tm
    keep_halo = jnp.where((i % tiles_per_seq) == 0, 0.0, 1.0)
    u = u_ref[...]
    halo = uh_ref[...]
    tt = lax.broadcasted_iota(jnp.int32, (tm, tm), 0)
    ss = lax.broadcasted_iota(jnp.int32, (tm, tm), 1)
    th = lax.broadcasted_iota(jnp.int32, (tm, POOL_HALO), 0)
    rh = lax.broadcasted_iota(jnp.int32, (tm, POOL_HALO), 1)
    tseq = (i % tiles_per_seq) * tm + lax.broadcasted_iota(jnp.int32, (tm, 1), 0)
    n_pool = u.shape[1]
    gd = n_pool // len(POOL_WINDOWS)
    acc = x_ref[...] + _dot(y_ref[...], wo_ref[n_pool:, :])
    for gi, w in enumerate(POOL_WINDOWS):
        cols = slice(gi * gd, (gi + 1) * gd)
        band = jnp.where((tt - ss >= 0) & (tt - ss < w), 1.0, 0.0).astype(bf16)
        band_h = jnp.where(th + POOL_HALO - rh < w, 1.0, 0.0).astype(bf16)
        ug = u[:, cols]
        win_sum = _dot(band, ug) + keep_halo * _dot(band_h, halo[:, cols])
        count = jnp.minimum(tseq + 1, w).astype(f32)
        pooled = win_sum / count - ug.astype(f32)
        yg = _dot(pooled.astype(bf16), pw_ref[gi]) * ps_ref[:, cols]
        acc = acc + _dot(yg.astype(bf16), wo_ref[cols, :])
    o_ref[...] = acc


def _out_proj(x2, u, y_nsa, pool_w, pool_scale, w_out, *, T, tm):
    n_rows, d = x2.shape
    n_pool = u.shape[1]
    hb = tm // POOL_HALO
    row = lambda i: (i, 0)
    const2 = lambda i: (0, 0)
    return pl.pallas_call(
        functools.partial(_out_proj_kernel, tm=tm, T=T),
        grid=(n_rows // tm,),
        in_specs=[
            pl.BlockSpec((tm, d), row),
            pl.BlockSpec((tm, n_pool), row),
            pl.BlockSpec((POOL_HALO, n_pool), lambda i: (jnp.maximum(i * hb - 1, 0), 0)),
            pl.BlockSpec((tm, y_nsa.shape[1]), row),
            pl.BlockSpec(pool_w.shape, lambda i: (0, 0, 0)),
            pl.BlockSpec((1, n_pool), const2),
            pl.BlockSpec(w_out.shape, const2, pipeline_mode=pl.Buffered(1)),
        ],
        out_specs=pl.BlockSpec((tm, d), row),
        out_shape=jax.ShapeDtypeStruct((n_rows, d), f32),
        compiler_params=pltpu.CompilerParams(
            dimension_semantics=("arbitrary",), vmem_limit_bytes=VMEM_LIMIT),
        name="out_proj",
    )(x2, u, u, y_nsa, pool_w, pool_scale, w_out)


def _ffn_kernel(h_ref, g_ref, wg_ref, wu_ref, wd_ref, o_ref, n_sc, acc_sc):
    f = pl.program_id(1)

    @pl.when(f == 0)
    def _():
        n_sc[...] = _rms(h_ref[...], g_ref[...]).astype(bf16)
        acc_sc[...] = jnp.zeros(acc_sc.shape, f32)

    n = n_sc[...]
    gate = _dot(n, wg_ref[...])
    up = _dot(n, wu_ref[...])
    a = gate * jax.nn.sigmoid(gate) * up
    acc_sc[...] += _dot(a.astype(bf16), wd_ref[...])

    @pl.when(f == pl.num_programs(1) - 1)
    def _():
        o_ref[...] = h_ref[...] + acc_sc[...]


def _ffn(h, g, w_gate, w_up, w_down, *, tm, tf):
    n_rows, d = h.shape
    d_ff = w_gate.shape[1]
    return pl.pallas_call(
        _ffn_kernel,
        grid=(n_rows // tm, d_ff // tf),
        in_specs=[
            pl.BlockSpec((tm, d), lambda i, f: (i, 0)),
            pl.BlockSpec((1, d), lambda i, f: (0, 0)),
            pl.BlockSpec((d, tf), lambda i, f: (0, f)),
            pl.BlockSpec((d, tf), lambda i, f: (0, f)),
            pl.BlockSpec((tf, d), lambda i, f: (f, 0)),
        ],
        out_specs=pl.BlockSpec((tm, d), lambda i, f: (i, 0)),
        out_shape=jax.ShapeDtypeStruct((n_rows, d), f32),
        scratch_shapes=[pltpu.VMEM((tm, d), bf16), pltpu.VMEM((tm, d), f32)],
        compiler_params=pltpu.CompilerParams(
            dimension_semantics=("arbitrary", "arbitrary"), vmem_limit_bytes=VMEM_LIMIT),
        name="ffn",
    )(h, g, w_gate, w_up, w_down)


def _ple_kernel(h_ref, p_ref, g_ref, wg_ref, wp_ref, fg_ref, o_ref):
    h = h_ref[...]
    n = _rms(h, g_ref[...]).astype(bf16)
    gate = jax.nn.sigmoid(_dot(n, wg_ref[...]))
    h = h + _dot(p_ref[...].astype(bf16), wp_ref[...]) * gate
    o_ref[...] = _rms(h, fg_ref[...])


def _ple(h, p2, g, w_gate, w_proj, fg, *, tm):
    n_rows, d = h.shape
    row = lambda i: (i, 0)
    const = lambda i: (0, 0)
    return pl.pallas_call(
        _ple_kernel,
        grid=(n_rows // tm,),
        in_specs=[
            pl.BlockSpec((tm, d), row),
            pl.BlockSpec((tm, p2.shape[1]), row),
            pl.BlockSpec((1, d), const),
            pl.BlockSpec(w_gate.shape, const, pipeline_mode=pl.Buffered(1)),
            pl.BlockSpec(w_proj.shape, const),
            pl.BlockSpec((1, d), const),
        ],
        out_specs=pl.BlockSpec((tm, d), row),
        out_shape=jax.ShapeDtypeStruct((n_rows, d), f32),
        compiler_params=pltpu.CompilerParams(
            dimension_semantics=("arbitrary",), vmem_limit_bytes=VMEM_LIMIT),
        name="ple",
    )(h, p2, g, w_gate, w_proj, fg)


def _rope_tables(T):
    pos = jnp.arange(T, dtype=f32)
    inv_freq = ROPE_THETA ** (-jnp.arange(0, ROPE_DIM, 2, dtype=f32) / ROPE_DIM)
    ang = pos[:, None] * inv_freq[None, :]
    cos, sin = jnp.cos(ang), jnp.sin(ang)
    rest = HEAD_DIM - ROPE_DIM
    cosf = jnp.concatenate([cos, cos, jnp.ones((T, rest), f32)], axis=1)
    sinf = jnp.concatenate([-sin, sin, jnp.zeros((T, rest), f32)], axis=1)
    return cosf, sinf


def _mixer_ffn(h2, in_norm_g, w_in, pool_w, pool_scale, cmp_k_pe, cmp_k_w1, cmp_k_w2,
               cmp_v_pe, cmp_v_w1, cmp_v_w2, w_out, ffn_norm_g, w_gate, w_up, w_down, *, B, T):
    n_pool = pool_scale.shape[0]
    n_heads = N_KV_GROUPS * HEADS_PER_GROUP
    nsa_w = n_heads * HEAD_DIM
    kv_w = N_KV_GROUPS * HEAD_DIM
    assert w_in.shape[1] == n_pool + nsa_w + 6 * kv_w + n_heads * N_BRANCH and n_pool == 1024 and nsa_w == 1024

    o0 = n_pool + nsa_w + 6 * kv_w
    w_gt = w_in[:, o0:]
    per_g = HEADS_PER_GROUP * N_BRANCH
    w_gt = jnp.concatenate(
        [jnp.pad(w_gt[:, g * per_g:(g + 1) * per_g], ((0, 0), (0, LANES - per_g))) for g in range(N_KV_GROUPS)],
        axis=1)
    w_p = jnp.concatenate([w_in[:, :o0], w_gt], axis=1).astype(bf16)

    cosf, sinf = _rope_tables(T)
    u, q, kcvc, kv, gates = _in_proj(h2, in_norm_g[None, :], w_p, cosf, sinf, T=T, tm=512)

    r3 = kcvc.reshape(B, T // CMP_STRIDE, CMP_STRIDE * kcvc.shape[1])
    kcmp, vcmp = _compress(
        r3,
        cmp_k_pe.reshape(1, -1), cmp_k_w1.astype(bf16), cmp_k_w2.astype(bf16),
        cmp_v_pe.reshape(1, -1), cmp_v_w1.astype(bf16), cmp_v_w2.astype(bf16))

    key_blk = jnp.arange(T, dtype=jnp.int32)[:, None] // SEL_BLOCK
    eneg = jnp.where(key_blk == jnp.arange(LANES, dtype=jnp.int32)[None, :], NEG, 0.0).astype(bf16)
    y_nsa = _nsa(q, kcmp, vcmp, kv, eneg, gates, B=B, T=T, tq=256, tk=256)

    h2 = _out_proj(h2, u, y_nsa, pool_w.astype(bf16), pool_scale[None, :], w_out.astype(bf16), T=T, tm=512)
    return _ffn(h2, ffn_norm_g[None, :], w_gate.astype(bf16), w_up.astype(bf16), w_down.astype(bf16),
                tm=512, tf=512)


def kernel(x, p, in_norm_g, w_in, pool_w, pool_scale, cmp_k_pe, cmp_k_w1, cmp_k_w2, cmp_v_pe, cmp_v_w1,
           cmp_v_w2, w_out, ffn_norm_g, w_gate, w_up, w_down, ple_norm_g, w_ple_gate, w_ple_proj, final_norm_g):
    B, T, d = x.shape
    depth = w_in.shape[0]
    assert depth == 1, "the final rmsnorm is fused into the last layer's per-layer-embedding kernel"
    h2 = x.reshape(B * T, d)
    i = 0
    h2 = _mixer_ffn(h2, in_norm_g[i], w_in[i], pool_w[i], pool_scale[i], cmp_k_pe[i], cmp_k_w1[i],
                    cmp_k_w2[i], cmp_v_pe[i], cmp_v_w1[i], cmp_v_w2[i], w_out[i], ffn_norm_g[i], w_gate[i],
                    w_up[i], w_down[i], B=B, T=T)
    out = _ple(h2, p[i].reshape(B * T, -1), ple_norm_g[i][None, :], w_ple_gate[i].astype(bf16),
               w_ple_proj[i].astype(bf16), final_norm_g[None, :], tm=512)
    return out.reshape(B, T, d)
```

```python
import functools

import jax
import jax.numpy as jnp
from jax import lax
from jax.experimental import pallas as pl
from jax.experimental.pallas import tpu as pltpu

EPS = 1e-6
NEG = -1e30
LANES = 128
POOL_WINDOWS = (2, 4, 8, 16)
POOL_HALO = 16
HEAD_DIM = 128
N_KV_GROUPS = 2
HEADS_PER_GROUP = 4
N_BRANCH = 3
CMP_BLOCK = 32
CMP_STRIDE = 16
SEL_BLOCK = 64
N_SELECT = 16
WINDOW = 512
ROPE_THETA = 500000.0
ROPE_DIM = HEAD_DIM // 4
ATTN_SCALE = HEAD_DIM ** -0.5
Q_SCALE = ATTN_SCALE * 1.4426950408889634
VMEM_LIMIT = 56 * 1024 * 1024

f32 = jnp.float32
bf16 = jnp.bfloat16


def _dot(a, b):
    return jnp.dot(a, b, preferred_element_type=f32)


def _dot_nt(a, b):
    return lax.dot_general(a, b, (((1,), (1,)), ((), ())), preferred_element_type=f32)


def _rms(x, g):
    return x * lax.rsqrt(jnp.mean(x * x, axis=-1, keepdims=True) + EPS) * g


def _rope(h, cosf, sinf, lane):
    half = ROPE_DIM // 2
    partner = jnp.where(lane < half, pltpu.roll(h, LANES - half, axis=1), pltpu.roll(h, half, axis=1))
    return h * cosf + partner * sinf


def _in_proj_kernel(x_ref, g_ref, w_ref, wgt_ref, cos_ref, sin_ref, u_ref, q_ref, kcvc_ref, kv_ref, gt_ref):
    a = _rms(x_ref[...], g_ref[...]).astype(bf16)
    cosf = cos_ref[...]
    sinf = sin_ref[...]
    lane = lax.broadcasted_iota(jnp.int32, cosf.shape, 1)
    nu = u_ref.shape[1]
    nq = q_ref.shape[1]
    off = 0
    for c in range(0, nu, 512):
        u_ref[:, c:c + 512] = _dot(a, w_ref[:, off + c:off + c + 512]).astype(bf16)
    off += nu
    for c in range(0, nq, 512):
        acc = _dot(a, w_ref[:, off + c:off + c + 512])
        for j in range(4):
            h = _rope(acc[:, j * 128:(j + 1) * 128], cosf, sinf, lane) * Q_SCALE
            q_ref[:, c + j * 128:c + (j + 1) * 128] = h.astype(bf16)
    off += nq
    acc = _dot(a, w_ref[:, off:off + 512])
    for j in range(4):
        h = acc[:, j * 128:(j + 1) * 128]
        if j < 2:
            h = _rope(h, cosf, sinf, lane)
        kcvc_ref[:, j * 128:(j + 1) * 128] = h
    off += 512
    acc = _dot(a, w_ref[:, off:off + 512])
    for j in range(4):
        h = acc[:, j * 128:(j + 1) * 128]
        if j < 2:
            h = _rope(h, cosf, sinf, lane)
        kv_ref[:, j * 128:(j + 1) * 128] = h.astype(bf16)
    off += 512
    acc = _dot(a, w_ref[:, off:off + 512])
    for j in range(4):
        h = acc[:, j * 128:(j + 1) * 128]
        if j < 2:
            h = _rope(h, cosf, sinf, lane)
        kv_ref[:, 512 + j * 128:512 + (j + 1) * 128] = h.astype(bf16)
    off += 512
    gt_ref[...] = _dot(a, wgt_ref[...])


def _in_proj(x2, g, w_p, w_gt, cosf, sinf, *, T, tm):
    n_rows, d = x2.shape
    n_w = w_p.shape[1]
    tiles_per_seq = T // tm
    row = lambda i: (i, 0)
    const = lambda i: (0, 0)
    tab = lambda i: (i % tiles_per_seq, 0)
    return pl.pallas_call(
        _in_proj_kernel,
        grid=(n_rows // tm,),
        in_specs=[
            pl.BlockSpec((tm, d), row),
            pl.BlockSpec((1, d), const),
            pl.BlockSpec((d, n_w), const, pipeline_mode=pl.Buffered(1)),
            pl.BlockSpec((d, LANES), const),
            pl.BlockSpec((tm, LANES), tab),
            pl.BlockSpec((tm, LANES), tab),
        ],
        out_specs=[
            pl.BlockSpec((tm, 1024), row),
            pl.BlockSpec((tm, 1024), row),
            pl.BlockSpec((tm, 512), row),
            pl.BlockSpec((tm, 1024), row),
            pl.BlockSpec((tm, LANES), row),
        ],
        out_shape=[
            jax.ShapeDtypeStruct((n_rows, 1024), bf16),
            jax.ShapeDtypeStruct((n_rows, 1024), bf16),
            jax.ShapeDtypeStruct((n_rows, 512), f32),
            jax.ShapeDtypeStruct((n_rows, 1024), bf16),
            jax.ShapeDtypeStruct((n_rows, LANES), f32),
        ],
        compiler_params=pltpu.CompilerParams(
            dimension_semantics=("arbitrary",), vmem_limit_bytes=VMEM_LIMIT),
        name="in_proj",
    )(x2, g, w_p, w_gt, cosf, sinf)


def _gelu_tanh(x):
    return 0.5 * x * (1.0 + jnp.tanh(0.7978845608028654 * (x + 0.044715 * (x * x * x))))


def _compress_kernel(kc0_ref, kc1_ref, vc0_ref, vc1_ref, pek_ref, w1k_ref, w2k_ref, pev_ref, w1v_ref, w2v_ref,
                     kc_ref, vc_ref):
    n_rows = kc0_ref.shape[1] // CMP_STRIDE
    half_k = (CMP_BLOCK // 2) * HEAD_DIM
    rowi = lax.broadcasted_iota(jnp.int32, (n_rows, HEAD_DIM), 0)
    for srcs, pe_ref, w1_ref, w2_ref, o_ref in (((kc0_ref, kc1_ref), pek_ref, w1k_ref, w2k_ref, kc_ref),
                                                ((vc0_ref, vc1_ref), pev_ref, w1v_ref, w2v_ref, vc_ref)):
        pe_a = pe_ref[:, :half_k]
        pe_b = pe_ref[:, half_k:]
        for g, src in enumerate(srcs):
            x = jnp.concatenate(
                [src[0, pl.ds(l, n_rows, stride=CMP_STRIDE), :] for l in range(CMP_STRIDE)], axis=1)
            za = _dot((x + pe_a).astype(bf16), w1_ref[:half_k, :])
            zb = _dot((x + pe_b).astype(bf16), w1_ref[half_k:, :])
            h = za + pltpu.roll(zb, n_rows - 1, axis=0)
            o = _dot(_gelu_tanh(h).astype(bf16), w2_ref[...])
            o_ref[0, g] = jnp.where(rowi < n_rows - 1, o, 0.0).astype(bf16)


def _compress(r3, pek, w1k, w2k, pev, w1v, w2v):
    B, T, width = r3.shape
    assert width == 2 * N_KV_GROUPS * HEAD_DIM and N_KV_GROUPS == 2
    n_rows = T // CMP_STRIDE
    const = lambda b: (0, 0)
    out = jax.ShapeDtypeStruct((B, N_KV_GROUPS, n_rows, HEAD_DIM), bf16)
    ospec = pl.BlockSpec((1, N_KV_GROUPS, n_rows, HEAD_DIM), lambda b: (b, 0, 0, 0))
    sec = lambda s: pl.BlockSpec((1, T, HEAD_DIM), lambda b: (b, 0, s))
    return pl.pallas_call(
        _compress_kernel,
        grid=(B,),
        in_specs=[
            sec(0), sec(1), sec(2), sec(3),
            pl.BlockSpec(pek.shape, const), pl.BlockSpec(w1k.shape, const), pl.BlockSpec(w2k.shape, const),
            pl.BlockSpec(pev.shape, const), pl.BlockSpec(w1v.shape, const), pl.BlockSpec(w2v.shape, const),
        ],
        out_specs=[ospec, ospec],
        out_shape=[out, out],
        compiler_params=pltpu.CompilerParams(
            dimension_semantics=("arbitrary",), vmem_limit_bytes=VMEM_LIMIT),
        name="compress",
    )(r3, r3, r3, r3, pek, w1k, w2k, pev, w1v, w2v)


def _nsa_kernel(q_ref, kc_ref, vc_ref, ks_ref, vs_ref, kw_ref, vw_ref, eneg_ref, gt_ref, o_ref,
                s_sc, mx_sc, ls_sc, acc_sc, wbias_sc, dbias_sc, *, tq, tk, T):
    R = HEADS_PER_GROUP
    M = R * tq
    i = pl.program_id(2)
    t0 = i * tq
    n_cmp = kc_ref.shape[2]
    n_sel = T // SEL_BLOCK

    q = q_ref[...]
    qs = jnp.concatenate([q[:, r * HEAD_DIM:(r + 1) * HEAD_DIM] for r in range(R)], axis=0)

    wk = WINDOW + tq
    w0 = pl.multiple_of(jnp.maximum(t0 - WINDOW, 0), tq)
    @pl.when(t0 <= WINDOW)
    def _():
        diff = (t0 - w0) + (lax.broadcasted_iota(jnp.int32, (M, wk), 0) & (tq - 1)) \
            - lax.broadcasted_iota(jnp.int32, (M, wk), 1)
        wbias_sc[...] = jnp.where((diff & -WINDOW) == 0, 0.0, NEG)

    sw = _dot_nt(qs, kw_ref[pl.ds(w0, wk), :]) + wbias_sc[...]
    ew = jnp.exp2(sw - jnp.max(sw, axis=1, keepdims=True))
    o_win = _dot(ew.astype(bf16), vw_ref[pl.ds(w0, wk), :]) / jnp.sum(ew, axis=1, keepdims=True)

    kc = kc_ref[0, 0]
    vc = vc_ref[0, 0]
    s = _dot_nt(qs, kc)
    trow = t0 + (lax.broadcasted_iota(jnp.int32, (M, n_cmp), 0) & (tq - 1))
    ncol = lax.broadcasted_iota(jnp.int32, (M, n_cmp), 1)
    valid = (CMP_STRIDE * ncol + (CMP_BLOCK - 1)) <= trow
    s = jnp.where(valid, s, NEG)
    e = jnp.exp2(s - jnp.max(s, axis=1, keepdims=True))
    p = jnp.where(valid, e / jnp.sum(e, axis=1, keepdims=True), 0.0)
    o_cmp = _dot(p.astype(bf16), vc)

    psum = p[0:tq]
    for r in range(1, R):
        psum = psum + p[r * tq:(r + 1) * tq]
    p_hi = psum.astype(bf16)
    p_lo = (psum - p_hi.astype(f32)).astype(bf16)
    sb = lax.broadcasted_iota(jnp.int32, (n_sel, n_cmp), 0) * SEL_BLOCK
    cb = lax.broadcasted_iota(jnp.int32, (n_sel, n_cmp), 1) * CMP_STRIDE
    ov_t = jnp.where((cb < sb + SEL_BLOCK) & (cb + CMP_BLOCK > sb), 1.0, 0.0).astype(bf16)
    imp_t = _dot_nt(ov_t, p_hi) + _dot_nt(ov_t, p_lo)

    jj = lax.broadcasted_iota(jnp.int32, (n_sel, tq), 0)
    cur = jnp.right_shift(t0 + lax.broadcasted_iota(jnp.int32, (n_sel, tq), 1), SEL_BLOCK.bit_length() - 1)
    forced = (jj == 0) | (jj == cur) | (jj == cur - 1)
    v = jnp.where(jj > cur, -jnp.inf, jnp.where(forced, jnp.inf, imp_t))
    sub = 8
    slabs = [v[k * sub:(k + 1) * sub] for k in range(n_sel // sub)]
    ranks = [jnp.zeros((sub, tq), f32) for _ in slabs]
    j_in = lax.broadcasted_iota(jnp.int32, (sub, tq), 0)
    for c in range(n_sel):
        v_c = v[c:c + 1, :]
        for k, v_k in enumerate(slabs):
            if k * sub > c:
                ahead = jnp.where(v_c >= v_k, 1.0, 0.0)
            elif (k + 1) * sub <= c:
                ahead = jnp.where(v_c > v_k, 1.0, 0.0)
            else:
                ahead = jnp.where(j_in > c - k * sub, jnp.where(v_c >= v_k, 1.0, 0.0),
                                  jnp.where(v_c > v_k, 1.0, 0.0))
            ranks[k] = ranks[k] + ahead
    rank = jnp.concatenate(ranks, axis=0)
    unsel_t = jnp.where(rank < float(min(N_SELECT, n_sel)), 0.0, 1.0)
    unsel_t = jnp.concatenate([unsel_t, jnp.zeros((LANES - n_sel, tq), f32)], axis=0)
    unsel = unsel_t.T.astype(bf16)
    q_aug = jnp.concatenate([qs, jnp.concatenate([unsel] * R, axis=0)], axis=1)

    n_chunk = tk // LANES
    mx_sc[...] = jnp.full(mx_sc.shape, NEG, f32)

    @pl.when(i == 0)
    def _():
        dpos = lax.broadcasted_iota(jnp.int32, (M, tk), 0) & (tq - 1)
        dbias_sc[...] = jnp.where(dpos >= lax.broadcasted_iota(jnp.int32, (M, tk), 1), 0.0, NEG)

    def score_step(kt, causal):
        k0 = pl.multiple_of(kt * tk, tk)
        k_aug = jnp.concatenate([ks_ref[pl.ds(k0, tk), :], eneg_ref[pl.ds(k0, tk), :]], axis=1)
        sc = _dot_nt(q_aug, k_aug)
        if causal:
            sc = sc + dbias_sc[...]
        s_sc[kt] = sc
        part = sc[:, :LANES]
        for c in range(1, n_chunk):
            part = jnp.maximum(part, sc[:, c * LANES:(c + 1) * LANES])
        mx_sc[...] = jnp.maximum(mx_sc[...], part)

    n_full = t0 // tk

    def score_body(kt, carry):
        score_step(kt, False)
        return carry

    lax.fori_loop(0, n_full, score_body, 0)
    score_step(n_full, True)
    mx_sc[...] = jnp.broadcast_to(jnp.max(mx_sc[...], axis=1, keepdims=True), mx_sc.shape)
    ls_sc[...] = jnp.zeros(ls_sc.shape, f32)
    acc_sc[...] = jnp.zeros(acc_sc.shape, f32)

    def pv_body(kt, carry):
        k0 = pl.multiple_of(kt * tk, tk)
        sc = s_sc[kt]
        mb = mx_sc[...]
        ps = [jnp.exp2(sc[:, c * LANES:(c + 1) * LANES] - mb) for c in range(n_chunk)]
        ls_sc[...] += functools.reduce(lambda a, b: a + b, ps)
        acc_sc[...] += _dot(jnp.concatenate(ps, axis=1).astype(bf16), vs_ref[pl.ds(k0, tk), :])
        return carry

    lax.fori_loop(0, n_full + 1, pv_body, 0)
    o_sel = acc_sc[...] / jnp.sum(ls_sc[...], axis=1, keepdims=True)

    gt = jax.nn.sigmoid(gt_ref[...])
    gt = jnp.where(pl.program_id(1) == 0, gt, pltpu.roll(gt, LANES - R * N_BRANCH, axis=1))
    for r in range(R):
        rows = slice(r * tq, (r + 1) * tq)
        c = r * N_BRANCH
        o = (gt[:, c:c + 1] * o_cmp[rows] + gt[:, c + 1:c + 2] * o_sel[rows]
             + gt[:, c + 2:c + 3] * o_win[rows])
        o_ref[:, r * HEAD_DIM:(r + 1) * HEAD_DIM] = o.astype(bf16)


def _nsa(q, kcmp, vcmp, kv, eneg, gates, *, B, T, tq, tk):
    G, R = N_KV_GROUPS, HEADS_PER_GROUP
    nt = T // tq
    n_cmp = kcmp.shape[2]
    assert tq % LANES == 0 and tq & (tq - 1) == 0 and T % tk == 0 and tk == tq and T >= WINDOW + tq
    assert WINDOW % tq == 0 and WINDOW & (WINDOW - 1) == 0
    rowblk = lambda b, g, i: (b * nt + i, g)
    cmp_spec = pl.BlockSpec((1, 1, n_cmp, HEAD_DIM), lambda b, g, i: (b, g, 0, 0))
    kvspec = lambda sec: pl.BlockSpec((T, HEAD_DIM), lambda b, g, i: (b, sec * G + g))
    return pl.pallas_call(
        functools.partial(_nsa_kernel, tq=tq, tk=tk, T=T),
        grid=(B, G, nt),
        in_specs=[
            pl.BlockSpec((tq, R * HEAD_DIM), rowblk),
            cmp_spec, cmp_spec,
            kvspec(0), kvspec(1), kvspec(2), kvspec(3),
            pl.BlockSpec((T, LANES), lambda b, g, i: (0, 0)),
            pl.BlockSpec((tq, LANES), lambda b, g, i: (b * nt + i, 0)),
        ],
        out_specs=pl.BlockSpec((tq, R * HEAD_DIM), rowblk),
        out_shape=jax.ShapeDtypeStruct((B * T, G * R * HEAD_DIM), bf16),
        scratch_shapes=[
            pltpu.VMEM((T // tk, R * tq, tk), f32),
            pltpu.VMEM((R * tq, LANES), f32),
            pltpu.VMEM((R * tq, LANES), f32),
            pltpu.VMEM((R * tq, HEAD_DIM), f32),
            pltpu.VMEM((R * tq, WINDOW + tq), f32),
            pltpu.VMEM((R * tq, tk), f32),
        ],
        compiler_params=pltpu.CompilerParams(
            dimension_semantics=("arbitrary", "arbitrary", "arbitrary"), vmem_limit_bytes=VMEM_LIMIT),
        name="nsa",
    )(q, kcmp, vcmp, kv, kv, kv, kv, eneg, gates)


def _out_proj_kernel(x_ref, u_ref, uh_ref, y_ref, pw_ref, ps_ref, wo_ref, o_ref, *, tm, T):
    i = pl.program_id(0)
    tiles_per_seq = T // tm
    keep_halo = jnp.where((i % tiles_per_seq) == 0, 0.0, 1.0)
    u = u_ref[...]
    halo = uh_ref[...]
    tt = lax.broadcasted_iota(jnp.int32, (tm, tm), 0)
    ss = lax.broadcasted_iota(jnp.int32, (tm, tm), 1)
    th = lax.broadcasted_iota(jnp.int32, (tm, POOL_HALO), 0)
    rh = lax.broadcasted_iota(jnp.int32, (tm, POOL_HALO), 1)
    tseq = (i % tiles_per_seq) * tm + lax.broadcasted_iota(jnp.int32, (tm, 1), 0)
    n_pool = u.shape[1]
    gd = n_pool // len(POOL_WINDOWS)
    acc = x_ref[...] + _dot(y_ref[...], wo_ref[n_pool:, :])
    for gi, w in enumerate(POOL_WINDOWS):
        cols = slice(gi * gd, (gi + 1) * gd)
        band = jnp.where((tt - ss >= 0) & (tt - ss < w), 1.0, 0.0).astype(bf16)
        band_h = jnp.where(th + POOL_HALO - rh < w, 1.0, 0.0).astype(bf16)
        ug = u[:, cols]
        win_sum = _dot(band, ug) + keep_halo * _dot(band_h, halo[:, cols])
        count = jnp.minimum(tseq + 1, w).astype(f32)
        pooled = win_sum / count - ug.astype(f32)
        yg = _dot(pooled.astype(bf16), pw_ref[gi]) * ps_ref[:, cols]
        acc = acc + _dot(yg.astype(bf16), wo_ref[cols, :])
    o_ref[...] = acc


def _out_proj(x2, u, y_nsa, pool_w, pool_scale, w_out, *, T, tm):
    n_rows, d = x2.shape
    n_pool = u.shape[1]
    hb = tm // POOL_HALO
    row = lambda i: (i, 0)
    const2 = lambda i: (0, 0)
    return pl.pallas_call(
        functools.partial(_out_proj_kernel, tm=tm, T=T),
        grid=(n_rows // tm,),
        in_specs=[
            pl.BlockSpec((tm, d), row),
            pl.BlockSpec((tm, n_pool), row),
            pl.BlockSpec((POOL_HALO, n_pool), lambda i: (jnp.maximum(i * hb - 1, 0), 0)),
            pl.BlockSpec((tm, y_nsa.shape[1]), row),
            pl.BlockSpec(pool_w.shape, lambda i: (0, 0, 0)),
            pl.BlockSpec((1, n_pool), const2),
            pl.BlockSpec(w_out.shape, const2, pipeline_mode=pl.Buffered(1)),
        ],
        out_specs=pl.BlockSpec((tm, d), row),
        out_shape=jax.ShapeDtypeStruct((n_rows, d), f32),
        compiler_params=pltpu.CompilerParams(
            dimension_semantics=("arbitrary",), vmem_limit_bytes=VMEM_LIMIT),
        name="out_proj",
    )(x2, u, u, y_nsa, pool_w, pool_scale, w_out)


def _ffn_kernel(h_ref, g_ref, wg_ref, wu_ref, wd_ref, o_ref, n_sc):
    @pl.when(pl.program_id(1) == 0)
    def _():
        h = h_ref[...]
        n_sc[...] = _rms(h, g_ref[...]).astype(bf16)
        o_ref[...] = h

    n = n_sc[...]
    gate = _dot(n, wg_ref[...])
    up = _dot(n, wu_ref[...])
    a = gate * jax.nn.sigmoid(gate) * up
    o_ref[...] += _dot(a.astype(bf16), wd_ref[...])


def _ffn(h, g, w_gate, w_up, w_down, *, tm, tf):
    n_rows, d = h.shape
    d_ff = w_gate.shape[1]
    return pl.pallas_call(
        _ffn_kernel,
        grid=(n_rows // tm, d_ff // tf),
        in_specs=[
            pl.BlockSpec((tm, d), lambda i, f: (i, 0)),
            pl.BlockSpec((1, d), lambda i, f: (0, 0)),
            pl.BlockSpec((d, tf), lambda i, f: (0, f)),
            pl.BlockSpec((d, tf), lambda i, f: (0, f)),
            pl.BlockSpec((tf, d), lambda i, f: (f, 0)),
        ],
        out_specs=pl.BlockSpec((tm, d), lambda i, f: (i, 0)),
        out_shape=jax.ShapeDtypeStruct((n_rows, d), f32),
        scratch_shapes=[pltpu.VMEM((tm, d), bf16)],
        compiler_params=pltpu.CompilerParams(
            dimension_semantics=("arbitrary", "arbitrary"), vmem_limit_bytes=VMEM_LIMIT),
        name="ffn",
    )(h, g, w_gate, w_up, w_down)


def _ple_kernel(h_ref, p_ref, g_ref, wg_ref, wp_ref, fg_ref, o_ref):
    h = h_ref[...]
    n = _rms(h, g_ref[...]).astype(bf16)
    gate = jax.nn.sigmoid(_dot(n, wg_ref[...]))
    h = h + _dot(p_ref[...].astype(bf16), wp_ref[...]) * gate
    o_ref[...] = _rms(h, fg_ref[...])


def _ple(h, p2, g, w_gate, w_proj, fg, *, tm):
    n_rows, d = h.shape
    row = lambda i: (i, 0)
    const = lambda i: (0, 0)
    return pl.pallas_call(
        _ple_kernel,
        grid=(n_rows // tm,),
        in_specs=[
            pl.BlockSpec((tm, d), row),
            pl.BlockSpec((tm, p2.shape[1]), row),
            pl.BlockSpec((1, d), const),
            pl.BlockSpec(w_gate.shape, const, pipeline_mode=pl.Buffered(1)),
            pl.BlockSpec(w_proj.shape, const),
            pl.BlockSpec((1, d), const),
        ],
        out_specs=pl.BlockSpec((tm, d), row),
        out_shape=jax.ShapeDtypeStruct((n_rows, d), f32),
        compiler_params=pltpu.CompilerParams(
            dimension_semantics=("arbitrary",), vmem_limit_bytes=VMEM_LIMIT),
        name="ple",
    )(h, p2, g, w_gate, w_proj, fg)


def _rope_tables(T):
    pos = jnp.arange(T, dtype=f32)
    inv_freq = ROPE_THETA ** (-jnp.arange(0, ROPE_DIM, 2, dtype=f32) / ROPE_DIM)
    ang = pos[:, None] * inv_freq[None, :]
    cos, sin = jnp.cos(ang), jnp.sin(ang)
    rest = HEAD_DIM - ROPE_DIM
    cosf = jnp.concatenate([cos, cos, jnp.ones((T, rest), f32)], axis=1)
    sinf = jnp.concatenate([-sin, sin, jnp.zeros((T, rest), f32)], axis=1)
    return cosf, sinf


def _mixer_ffn(h2, in_norm_g, w_in, pool_w, pool_scale, cmp_k_pe, cmp_k_w1, cmp_k_w2,
               cmp_v_pe, cmp_v_w1, cmp_v_w2, w_out, ffn_norm_g, w_gate, w_up, w_down, *, B, T):
    n_pool = pool_scale.shape[0]
    n_heads = N_KV_GROUPS * HEADS_PER_GROUP
    nsa_w = n_heads * HEAD_DIM
    kv_w = N_KV_GROUPS * HEAD_DIM
    assert w_in.shape[1] == n_pool + nsa_w + 6 * kv_w + n_heads * N_BRANCH and n_pool == 1024 and nsa_w == 1024

    o0 = n_pool + nsa_w + 6 * kv_w
    w_p = w_in.astype(bf16)
    w_gt = jnp.pad(w_p[:, o0:], ((0, 0), (0, LANES - n_heads * N_BRANCH)))

    cosf, sinf = _rope_tables(T)
    u, q, kcvc, kv, gates = _in_proj(h2, in_norm_g[None, :], w_p, w_gt, cosf, sinf, T=T, tm=512)

    kcmp, vcmp = _compress(
        kcvc.reshape(B, T, kcvc.shape[1]),
        cmp_k_pe.reshape(1, -1), cmp_k_w1.astype(bf16), cmp_k_w2.astype(bf16),
        cmp_v_pe.reshape(1, -1), cmp_v_w1.astype(bf16), cmp_v_w2.astype(bf16))

    key_blk = jnp.arange(T, dtype=jnp.int32)[:, None] // SEL_BLOCK
    eneg = jnp.where(key_blk == jnp.arange(LANES, dtype=jnp.int32)[None, :], NEG, 0.0).astype(bf16)
    y_nsa = _nsa(q, kcmp, vcmp, kv, eneg, gates, B=B, T=T, tq=256, tk=256)

    h2 = _out_proj(h2, u, y_nsa, pool_w.astype(bf16), pool_scale[None, :], w_out.astype(bf16), T=T, tm=512)
    return _ffn(h2, ffn_norm_g[None, :], w_gate.astype(bf16), w_up.astype(bf16), w_down.astype(bf16),
                tm=1024, tf=512)


def kernel(x, p, in_norm_g, w_in, pool_w, pool_scale, cmp_k_pe, cmp_k_w1, cmp_k_w2, cmp_v_pe, cmp_v_w1,
           cmp_v_w2, w_out, ffn_norm_g, w_gate, w_up, w_down, ple_norm_g, w_ple_gate, w_ple_proj, final_norm_g):
    B, T, d = x.shape
    depth = w_in.shape[0]
    assert depth == 1, "the final rmsnorm is fused into the last layer's per-layer-embedding kernel"
    h2 = x.reshape(B * T, d)
    i = 0
    h2 = _mixer_ffn(h2, in_norm_g[i], w_in[i], pool_w[i], pool_scale[i], cmp_k_pe[i], cmp_k_w1[i],
                    cmp_k_w2[i], cmp_v_pe[i], cmp_v_w1[i], cmp_v_w2[i], w_out[i], ffn_norm_g[i], w_gate[i],
                    w_up[i], w_down[i], B=B, T=T)
    out = _ple(h2, p[i].reshape(B * T, -1), ple_norm_g[i][None, :], w_ple_gate[i].astype(bf16),
               w_ple_proj[i].astype(bf16), final_norm_g[None, :], tm=512)
    return out.reshape(B, T, d)
```

```python
import functools

import jax
import jax.numpy as jnp
from jax import lax
from jax.experimental import pallas as pl
from jax.experimental.pallas import tpu as pltpu

EPS = 1e-6
NEG = -1e30
LANES = 128
POOL_WINDOWS = (2, 4, 8, 16)
POOL_HALO = 16
HEAD_DIM = 128
N_KV_GROUPS = 2
HEADS_PER_GROUP = 4
N_BRANCH = 3
CMP_BLOCK = 32
CMP_STRIDE = 16
SEL_BLOCK = 64
N_SELECT = 16
WINDOW = 512
ROPE_THETA = 500000.0
ROPE_DIM = HEAD_DIM // 4
ATTN_SCALE = HEAD_DIM ** -0.5
Q_SCALE = ATTN_SCALE * 1.4426950408889634
VMEM_LIMIT = 60 * 1024 * 1024

f32 = jnp.float32
bf16 = jnp.bfloat16


def _dot(a, b):
    return jnp.dot(a, b, preferred_element_type=f32)


def _dot_nt(a, b):
    return lax.dot_general(a, b, (((1,), (1,)), ((), ())), preferred_element_type=f32)


def _rms(x, g):
    return x * lax.rsqrt(jnp.mean(x * x, axis=-1, keepdims=True) + EPS) * g


def _rope(h, cosf, sinf, lane):
    half = ROPE_DIM // 2
    partner = jnp.where(lane < half, pltpu.roll(h, LANES - half, axis=1), pltpu.roll(h, half, axis=1))
    return h * cosf + partner * sinf


def _in_proj_kernel(x_ref, g_ref, w_ref, wgt_ref, cos_ref, sin_ref, ca_ref, cb_ref, cc_ref,
                    u_ref, q_ref, kcvc_ref, kv_ref, gt_ref, ca_out, cb_out, cc_out):
    ca_out[...] = ca_ref[...].astype(bf16)
    cb_out[...] = cb_ref[...].astype(bf16)
    cc_out[...] = cc_ref[...].astype(bf16)
    a = _rms(x_ref[...], g_ref[...]).astype(bf16)
    cosf = cos_ref[...]
    sinf = sin_ref[...]
    lane = lax.broadcasted_iota(jnp.int32, cosf.shape, 1)
    nu = u_ref.shape[1]
    nq = q_ref.shape[1]
    off = 0
    for c in range(0, nu, 512):
        u_ref[:, c:c + 512] = _dot(a, w_ref[:, off + c:off + c + 512]).astype(bf16)
    off += nu
    for c in range(0, nq, 512):
        acc = _dot(a, w_ref[:, off + c:off + c + 512])
        for j in range(4):
            h = _rope(acc[:, j * 128:(j + 1) * 128], cosf, sinf, lane) * Q_SCALE
            q_ref[:, c + j * 128:c + (j + 1) * 128] = h.astype(bf16)
    off += nq
    acc = _dot(a, w_ref[:, off:off + 512])
    for j in range(4):
        h = acc[:, j * 128:(j + 1) * 128]
        if j < 2:
            h = _rope(h, cosf, sinf, lane)
        kcvc_ref[:, j * 128:(j + 1) * 128] = h
    off += 512
    acc = _dot(a, w_ref[:, off:off + 512])
    for j in range(4):
        h = acc[:, j * 128:(j + 1) * 128]
        if j < 2:
            h = _rope(h, cosf, sinf, lane)
        kv_ref[:, j * 128:(j + 1) * 128] = h.astype(bf16)
    off += 512
    acc = _dot(a, w_ref[:, off:off + 512])
    for j in range(4):
        h = acc[:, j * 128:(j + 1) * 128]
        if j < 2:
            h = _rope(h, cosf, sinf, lane)
        kv_ref[:, 512 + j * 128:512 + (j + 1) * 128] = h.astype(bf16)
    off += 512
    gt_ref[...] = _dot(a, wgt_ref[...])


def _in_proj(x2, g, w_p, w_gt, cosf, sinf, cast_a, cast_b, cast_c, *, T, tm):
    n_rows, d = x2.shape
    n_w = w_p.shape[1]
    tiles_per_seq = T // tm
    n_steps = n_rows // tm
    row = lambda i: (i, 0)
    cast_spec = lambda w: pl.BlockSpec((w.shape[0] // n_steps, w.shape[1]), row)
    const = lambda i: (0, 0)
    tab = lambda i: (i % tiles_per_seq, 0)
    return pl.pallas_call(
        _in_proj_kernel,
        grid=(n_rows // tm,),
        in_specs=[
            pl.BlockSpec((tm, d), row),
            pl.BlockSpec((1, d), const),
            pl.BlockSpec((d, n_w), const, pipeline_mode=pl.Buffered(1)),
            pl.BlockSpec((d, LANES), const),
            pl.BlockSpec((tm, LANES), tab),
            pl.BlockSpec((tm, LANES), tab),
            cast_spec(cast_a), cast_spec(cast_b), cast_spec(cast_c),
        ],
        out_specs=[
            pl.BlockSpec((tm, 1024), row),
            pl.BlockSpec((tm, 1024), row),
            pl.BlockSpec((tm, 512), row),
            pl.BlockSpec((tm, 1024), row),
            pl.BlockSpec((tm, LANES), row),
            cast_spec(cast_a), cast_spec(cast_b), cast_spec(cast_c),
        ],
        out_shape=[
            jax.ShapeDtypeStruct((n_rows, 1024), bf16),
            jax.ShapeDtypeStruct((n_rows, 1024), bf16),
            jax.ShapeDtypeStruct((n_rows, 512), f32),
            jax.ShapeDtypeStruct((n_rows, 1024), bf16),
            jax.ShapeDtypeStruct((n_rows, LANES), f32),
            jax.ShapeDtypeStruct(cast_a.shape, bf16),
            jax.ShapeDtypeStruct(cast_b.shape, bf16),
            jax.ShapeDtypeStruct(cast_c.shape, bf16),
        ],
        compiler_params=pltpu.CompilerParams(
            dimension_semantics=("arbitrary",), vmem_limit_bytes=VMEM_LIMIT),
        name="in_proj",
    )(x2, g, w_p, w_gt, cosf, sinf, cast_a, cast_b, cast_c)


def _gelu_tanh(x):
    return 0.5 * x * (1.0 + jnp.tanh(0.7978845608028654 * (x + 0.044715 * (x * x * x))))


def _compress_kernel(kc0_ref, kc1_ref, vc0_ref, vc1_ref, pek_ref, w1k_ref, w2k_ref, pev_ref, w1v_ref, w2v_ref,
                     kc_ref, vc_ref):
    n_rows = kc0_ref.shape[1] // CMP_STRIDE
    half_k = (CMP_BLOCK // 2) * HEAD_DIM
    rowi = lax.broadcasted_iota(jnp.int32, (n_rows, HEAD_DIM), 0)
    for srcs, pe_ref, w1_ref, w2_ref, o_ref in (((kc0_ref, kc1_ref), pek_ref, w1k_ref, w2k_ref, kc_ref),
                                                ((vc0_ref, vc1_ref), pev_ref, w1v_ref, w2v_ref, vc_ref)):
        pe_a = pe_ref[:, :half_k]
        pe_b = pe_ref[:, half_k:]
        for g, src in enumerate(srcs):
            x = jnp.concatenate(
                [src[0, pl.ds(l, n_rows, stride=CMP_STRIDE), :] for l in range(CMP_STRIDE)], axis=1)
            za = _dot((x + pe_a).astype(bf16), w1_ref[:half_k, :])
            zb = _dot((x + pe_b).astype(bf16), w1_ref[half_k:, :])
            h = za + pltpu.roll(zb, n_rows - 1, axis=0)
            o = _dot(_gelu_tanh(h).astype(bf16), w2_ref[...])
            o_ref[0, g] = jnp.where(rowi < n_rows - 1, o, 0.0).astype(bf16)


def _compress(r3, pek, w1k, w2k, pev, w1v, w2v):
    B, T, width = r3.shape
    assert width == 2 * N_KV_GROUPS * HEAD_DIM and N_KV_GROUPS == 2
    n_rows = T // CMP_STRIDE
    const = lambda b: (0, 0)
    out = jax.ShapeDtypeStruct((B, N_KV_GROUPS, n_rows, HEAD_DIM), bf16)
    ospec = pl.BlockSpec((1, N_KV_GROUPS, n_rows, HEAD_DIM), lambda b: (b, 0, 0, 0))
    sec = lambda s: pl.BlockSpec((1, T, HEAD_DIM), lambda b: (b, 0, s))
    return pl.pallas_call(
        _compress_kernel,
        grid=(B,),
        in_specs=[
            sec(0), sec(1), sec(2), sec(3),
            pl.BlockSpec(pek.shape, const), pl.BlockSpec(w1k.shape, const), pl.BlockSpec(w2k.shape, const),
            pl.BlockSpec(pev.shape, const), pl.BlockSpec(w1v.shape, const), pl.BlockSpec(w2v.shape, const),
        ],
        out_specs=[ospec, ospec],
        out_shape=[out, out],
        compiler_params=pltpu.CompilerParams(
            dimension_semantics=("arbitrary",), vmem_limit_bytes=VMEM_LIMIT),
        name="compress",
    )(r3, r3, r3, r3, pek, w1k, w2k, pev, w1v, w2v)


def _nsa_kernel(q_ref, kc_ref, vc_ref, ks_ref, vs_ref, kw_ref, vw_ref, eneg_ref, gt_ref, o_ref,
                s_sc, mx_sc, ls_sc, acc_sc, *, tq, tk, T):
    R = HEADS_PER_GROUP
    M = R * tq
    i = pl.program_id(2)
    t0 = i * tq
    n_cmp = kc_ref.shape[2]
    n_sel = T // SEL_BLOCK

    q = q_ref[...]
    qs = jnp.concatenate([q[:, r * HEAD_DIM:(r + 1) * HEAD_DIM] for r in range(R)], axis=0)

    wk = WINDOW + tq
    w0 = pl.multiple_of(jnp.maximum(t0 - WINDOW, 0), tq)
    sw = _dot_nt(qs, kw_ref[pl.ds(w0, wk), :])
    diff = (t0 - w0) + (lax.broadcasted_iota(jnp.int32, (M, wk), 0) & (tq - 1)) \
        - lax.broadcasted_iota(jnp.int32, (M, wk), 1)
    sw = jnp.where((diff & -WINDOW) == 0, sw, NEG)
    ew = jnp.exp2(sw - jnp.max(sw, axis=1, keepdims=True))
    o_win = _dot(ew.astype(bf16), vw_ref[pl.ds(w0, wk), :]) / jnp.sum(ew, axis=1, keepdims=True)

    kc = kc_ref[0, 0]
    vc = vc_ref[0, 0]
    s = _dot_nt(qs, kc)
    trow = t0 + (lax.broadcasted_iota(jnp.int32, (M, n_cmp), 0) & (tq - 1))
    ncol = lax.broadcasted_iota(jnp.int32, (M, n_cmp), 1)
    valid = (CMP_STRIDE * ncol + (CMP_BLOCK - 1)) <= trow
    s = jnp.where(valid, s, NEG)
    e = jnp.exp2(s - jnp.max(s, axis=1, keepdims=True))
    p = jnp.where(valid, e / jnp.sum(e, axis=1, keepdims=True), 0.0)
    o_cmp = _dot(p.astype(bf16), vc)

    psum = p[0:tq]
    for r in range(1, R):
        psum = psum + p[r * tq:(r + 1) * tq]
    p_hi = psum.astype(bf16)
    p_lo = (psum - p_hi.astype(f32)).astype(bf16)
    sb = lax.broadcasted_iota(jnp.int32, (n_sel, n_cmp), 0) * SEL_BLOCK
    cb = lax.broadcasted_iota(jnp.int32, (n_sel, n_cmp), 1) * CMP_STRIDE
    ov_t = jnp.where((cb < sb + SEL_BLOCK) & (cb + CMP_BLOCK > sb), 1.0, 0.0).astype(bf16)
    imp_t = _dot_nt(ov_t, p_hi) + _dot_nt(ov_t, p_lo)

    jj = lax.broadcasted_iota(jnp.int32, (n_sel, tq), 0)
    cur = jnp.right_shift(t0 + lax.broadcasted_iota(jnp.int32, (n_sel, tq), 1), SEL_BLOCK.bit_length() - 1)
    forced = (jj == 0) | (jj == cur) | (jj == cur - 1)
    v = jnp.where(jj > cur, -jnp.inf, jnp.where(forced, jnp.inf, imp_t))
    sub = 8
    slabs = [v[k * sub:(k + 1) * sub] for k in range(n_sel // sub)]
    ranks = [jnp.zeros((sub, tq), f32) for _ in slabs]
    j_in = lax.broadcasted_iota(jnp.int32, (sub, tq), 0)
    for c in range(n_sel):
        v_c = v[c:c + 1, :]
        for k, v_k in enumerate(slabs):
            if k * sub > c:
                ahead = jnp.where(v_c >= v_k, 1.0, 0.0)
            elif (k + 1) * sub <= c:
                ahead = jnp.where(v_c > v_k, 1.0, 0.0)
            else:
                ahead = jnp.where(j_in > c - k * sub, jnp.where(v_c >= v_k, 1.0, 0.0),
                                  jnp.where(v_c > v_k, 1.0, 0.0))
            ranks[k] = ranks[k] + ahead
    rank = jnp.concatenate(ranks, axis=0)
    unsel_t = jnp.where(rank < float(min(N_SELECT, n_sel)), 0.0, 1.0)
    unsel_t = jnp.concatenate([unsel_t, jnp.zeros((LANES - n_sel, tq), f32)], axis=0)
    unsel = unsel_t.T.astype(bf16)
    q_aug = jnp.concatenate([qs, jnp.concatenate([unsel] * R, axis=0)], axis=1)

    n_chunk = tk // LANES
    mx_sc[...] = jnp.full(mx_sc.shape, NEG, f32)
    dpos = (lax.broadcasted_iota(jnp.int32, (M, tk), 0) & (tq - 1)) - lax.broadcasted_iota(jnp.int32, (M, tk), 1)

    def score_step(kt, causal):
        k0 = pl.multiple_of(kt * tk, tk)
        k_aug = jnp.concatenate([ks_ref[pl.ds(k0, tk), :], eneg_ref[pl.ds(k0, tk), :]], axis=1)
        sc = _dot_nt(q_aug, k_aug)
        if causal:
            sc = jnp.where(dpos >= 0, sc, NEG)
        s_sc[kt] = sc
        part = sc[:, :LANES]
        for c in range(1, n_chunk):
            part = jnp.maximum(part, sc[:, c * LANES:(c + 1) * LANES])
        mx_sc[...] = jnp.maximum(mx_sc[...], part)

    n_full = t0 // tk

    def score_body(kt, carry):
        score_step(kt, False)
        return carry

    lax.fori_loop(0, n_full, score_body, 0)
    score_step(n_full, True)
    mx_sc[...] = jnp.broadcast_to(jnp.max(mx_sc[...], axis=1, keepdims=True), mx_sc.shape)
    ls_sc[...] = jnp.zeros(ls_sc.shape, f32)
    acc_sc[...] = jnp.zeros(acc_sc.shape, f32)

    def pv_body(kt, carry):
        k0 = pl.multiple_of(kt * tk, tk)
        sc = s_sc[kt]
        mb = mx_sc[...]
        ps = [jnp.exp2(sc[:, c * LANES:(c + 1) * LANES] - mb) for c in range(n_chunk)]
        ls_sc[...] += functools.reduce(lambda a, b: a + b, ps)
        acc_sc[...] += _dot(jnp.concatenate(ps, axis=1).astype(bf16), vs_ref[pl.ds(k0, tk), :])
        return carry

    lax.fori_loop(0, n_full + 1, pv_body, 0)
    o_sel = acc_sc[...] / jnp.sum(ls_sc[...], axis=1, keepdims=True)

    gt = jax.nn.sigmoid(gt_ref[...])
    gt = jnp.where(pl.program_id(1) == 0, gt, pltpu.roll(gt, LANES - R * N_BRANCH, axis=1))
    for r in range(R):
        rows = slice(r * tq, (r + 1) * tq)
        c = r * N_BRANCH
        o = (gt[:, c:c + 1] * o_cmp[rows] + gt[:, c + 1:c + 2] * o_sel[rows]
             + gt[:, c + 2:c + 3] * o_win[rows])
        o_ref[:, r * HEAD_DIM:(r + 1) * HEAD_DIM] = o.astype(bf16)


def _nsa(q, kcmp, vcmp, kv, eneg, gates, *, B, T, tq, tk):
    G, R = N_KV_GROUPS, HEADS_PER_GROUP
    nt = T // tq
    n_cmp = kcmp.shape[2]
    assert tq % LANES == 0 and tq & (tq - 1) == 0 and T % tk == 0 and tk == tq and T >= WINDOW + tq
    assert WINDOW % tq == 0 and WINDOW & (WINDOW - 1) == 0
    rowblk = lambda b, g, i: (b * nt + i, g)
    cmp_spec = pl.BlockSpec((1, 1, n_cmp, HEAD_DIM), lambda b, g, i: (b, g, 0, 0))
    kvspec = lambda sec: pl.BlockSpec((T, HEAD_DIM), lambda b, g, i: (b, sec * G + g))
    return pl.pallas_call(
        functools.partial(_nsa_kernel, tq=tq, tk=tk, T=T),
        grid=(B, G, nt),
        in_specs=[
            pl.BlockSpec((tq, R * HEAD_DIM), rowblk),
            cmp_spec, cmp_spec,
            kvspec(0), kvspec(1), kvspec(2), kvspec(3),
            pl.BlockSpec((T, LANES), lambda b, g, i: (0, 0)),
            pl.BlockSpec((tq, LANES), lambda b, g, i: (b * nt + i, 0)),
        ],
        out_specs=pl.BlockSpec((tq, R * HEAD_DIM), rowblk),
        out_shape=jax.ShapeDtypeStruct((B * T, G * R * HEAD_DIM), bf16),
        scratch_shapes=[
            pltpu.VMEM((T // tk, R * tq, tk), f32),
            pltpu.VMEM((R * tq, LANES), f32),
            pltpu.VMEM((R * tq, LANES), f32),
            pltpu.VMEM((R * tq, HEAD_DIM), f32),
        ],
        compiler_params=pltpu.CompilerParams(
            dimension_semantics=("arbitrary", "arbitrary", "arbitrary"), vmem_limit_bytes=VMEM_LIMIT),
        name="nsa",
    )(q, kcmp, vcmp, kv, kv, kv, kv, eneg, gates)


def _out_proj_kernel(x_ref, u_ref, uh_ref, y_ref, pw_ref, ps_ref, wo_ref, ca_ref, cb_ref,
                     o_ref, ca_out, cb_out, *, tm, T):
    ca_out[...] = ca_ref[...].astype(bf16)
    cb_out[...] = cb_ref[...].astype(bf16)
    i = pl.program_id(0)
    tiles_per_seq = T // tm
    keep_halo = jnp.where((i % tiles_per_seq) == 0, 0.0, 1.0)
    u = u_ref[...]
    halo = uh_ref[...]
    tt = lax.broadcasted_iota(jnp.int32, (tm, tm), 0)
    ss = lax.broadcasted_iota(jnp.int32, (tm, tm), 1)
    th = lax.broadcasted_iota(jnp.int32, (tm, POOL_HALO), 0)
    rh = lax.broadcasted_iota(jnp.int32, (tm, POOL_HALO), 1)
    tseq = (i % tiles_per_seq) * tm + lax.broadcasted_iota(jnp.int32, (tm, 1), 0)
    n_pool = u.shape[1]
    gd = n_pool // len(POOL_WINDOWS)
    acc = x_ref[...] + _dot(y_ref[...], wo_ref[n_pool:, :])
    for gi, w in enumerate(POOL_WINDOWS):
        cols = slice(gi * gd, (gi + 1) * gd)
        band = jnp.where((tt - ss >= 0) & (tt - ss < w), 1.0, 0.0).astype(bf16)
        band_h = jnp.where(th + POOL_HALO - rh < w, 1.0, 0.0).astype(bf16)
        ug = u[:, cols]
        win_sum = _dot(band, ug) + keep_halo * _dot(band_h, halo[:, cols])
        count = jnp.minimum(tseq + 1, w).astype(f32)
        pooled = win_sum / count - ug.astype(f32)
        yg = _dot(pooled.astype(bf16), pw_ref[gi]) * ps_ref[:, cols]
        acc = acc + _dot(yg.astype(bf16), wo_ref[cols, :])
    o_ref[...] = acc


def _out_proj(x2, u, y_nsa, pool_w, pool_scale, w_out, cast_a, cast_b, *, T, tm):
    n_rows, d = x2.shape
    n_pool = u.shape[1]
    hb = tm // POOL_HALO
    n_steps = n_rows // tm
    row = lambda i: (i, 0)
    cast_spec = lambda w: pl.BlockSpec((w.shape[0] // n_steps, w.shape[1]), row)
    const2 = lambda i: (0, 0)
    return pl.pallas_call(
        functools.partial(_out_proj_kernel, tm=tm, T=T),
        grid=(n_rows // tm,),
        in_specs=[
            pl.BlockSpec((tm, d), row),
            pl.BlockSpec((tm, n_pool), row),
            pl.BlockSpec((POOL_HALO, n_pool), lambda i: (jnp.maximum(i * hb - 1, 0), 0)),
            pl.BlockSpec((tm, y_nsa.shape[1]), row),
            pl.BlockSpec(pool_w.shape, lambda i: (0, 0, 0)),
            pl.BlockSpec((1, n_pool), const2),
            pl.BlockSpec(w_out.shape, const2, pipeline_mode=pl.Buffered(1)),
            cast_spec(cast_a), cast_spec(cast_b),
        ],
        out_specs=[pl.BlockSpec((tm, d), row), cast_spec(cast_a), cast_spec(cast_b)],
        out_shape=[jax.ShapeDtypeStruct((n_rows, d), f32), jax.ShapeDtypeStruct(cast_a.shape, bf16),
                   jax.ShapeDtypeStruct(cast_b.shape, bf16)],
        compiler_params=pltpu.CompilerParams(
            dimension_semantics=("arbitrary",), vmem_limit_bytes=VMEM_LIMIT),
        name="out_proj",
    )(x2, u, u, y_nsa, pool_w, pool_scale, w_out, cast_a, cast_b)


def _ffn_kernel(h_ref, g_ref, wg_ref, wu_ref, wd_ref, o_ref, n_sc):
    @pl.when(pl.program_id(1) == 0)
    def _():
        h = h_ref[...]
        n_sc[...] = _rms(h, g_ref[...]).astype(bf16)
        o_ref[...] = h

    n = n_sc[...]
    gate = _dot(n, wg_ref[...])
    up = _dot(n, wu_ref[...])
    a = gate * jax.nn.sigmoid(gate) * up
    o_ref[...] += _dot(a.astype(bf16), wd_ref[...])


def _ffn(h, g, w_gate, w_up, w_down, *, tm, tf):
    n_rows, d = h.shape
    d_ff = w_gate.shape[1]
    return pl.pallas_call(
        _ffn_kernel,
        grid=(n_rows // tm, d_ff // tf),
        in_specs=[
            pl.BlockSpec((tm, d), lambda i, f: (i, 0)),
            pl.BlockSpec((1, d), lambda i, f: (0, 0)),
            pl.BlockSpec((d, tf), lambda i, f: (0, f)),
            pl.BlockSpec((d, tf), lambda i, f: (0, f)),
            pl.BlockSpec((tf, d), lambda i, f: (f, 0)),
        ],
        out_specs=pl.BlockSpec((tm, d), lambda i, f: (i, 0)),
        out_shape=jax.ShapeDtypeStruct((n_rows, d), f32),
        scratch_shapes=[pltpu.VMEM((tm, d), bf16)],
        compiler_params=pltpu.CompilerParams(
            dimension_semantics=("arbitrary", "arbitrary"), vmem_limit_bytes=VMEM_LIMIT),
        name="ffn",
    )(h, g, w_gate, w_up, w_down)


def _ple_kernel(h_ref, p_ref, g_ref, wg_ref, wp_ref, fg_ref, o_ref):
    h = h_ref[...]
    n = _rms(h, g_ref[...]).astype(bf16)
    gate = jax.nn.sigmoid(_dot(n, wg_ref[...]))
    h = h + _dot(p_ref[...].astype(bf16), wp_ref[...]) * gate
    o_ref[...] = _rms(h, fg_ref[...])


def _ple(h, p2, g, w_gate, w_proj, fg, *, tm):
    n_rows, d = h.shape
    row = lambda i: (i, 0)
    const = lambda i: (0, 0)
    return pl.pallas_call(
        _ple_kernel,
        grid=(n_rows // tm,),
        in_specs=[
            pl.BlockSpec((tm, d), row),
            pl.BlockSpec((tm, p2.shape[1]), row),
            pl.BlockSpec((1, d), const),
            pl.BlockSpec(w_gate.shape, const, pipeline_mode=pl.Buffered(1)),
            pl.BlockSpec(w_proj.shape, const),
            pl.BlockSpec((1, d), const),
        ],
        out_specs=pl.BlockSpec((tm, d), row),
        out_shape=jax.ShapeDtypeStruct((n_rows, d), f32),
        compiler_params=pltpu.CompilerParams(
            dimension_semantics=("arbitrary",), vmem_limit_bytes=VMEM_LIMIT),
        name="ple",
    )(h, p2, g, w_gate, w_proj, fg)


def _rope_tables(T):
    pos = jnp.arange(T, dtype=f32)
    inv_freq = ROPE_THETA ** (-jnp.arange(0, ROPE_DIM, 2, dtype=f32) / ROPE_DIM)
    ang = pos[:, None] * inv_freq[None, :]
    cos, sin = jnp.cos(ang), jnp.sin(ang)
    rest = HEAD_DIM - ROPE_DIM
    cosf = jnp.concatenate([cos, cos, jnp.ones((T, rest), f32)], axis=1)
    sinf = jnp.concatenate([-sin, sin, jnp.zeros((T, rest), f32)], axis=1)
    return cosf, sinf


def _mixer_ffn(h2, in_norm_g, w_in, pool_w, pool_scale, cmp_k_pe, cmp_k_w1, cmp_k_w2,
               cmp_v_pe, cmp_v_w1, cmp_v_w2, w_out, ffn_norm_g, w_gate, w_up, w_down, w_ple_gate, *, B, T):
    n_pool = pool_scale.shape[0]
    n_heads = N_KV_GROUPS * HEADS_PER_GROUP
    nsa_w = n_heads * HEAD_DIM
    kv_w = N_KV_GROUPS * HEAD_DIM
    assert w_in.shape[1] == n_pool + nsa_w + 6 * kv_w + n_heads * N_BRANCH and n_pool == 1024 and nsa_w == 1024

    o0 = n_pool + nsa_w + 6 * kv_w
    w_p = w_in.astype(bf16)
    w_gt = jnp.pad(w_p[:, o0:], ((0, 0), (0, LANES - n_heads * N_BRANCH)))

    cosf, sinf = _rope_tables(T)
    u, q, kcvc, kv, gates, w_out_b, w_gate_b, w_ple_gate_b = _in_proj(
        h2, in_norm_g[None, :], w_p, w_gt, cosf, sinf, w_out, w_gate, w_ple_gate, T=T, tm=512)

    kcmp, vcmp = _compress(
        kcvc.reshape(B, T, kcvc.shape[1]),
        cmp_k_pe.reshape(1, -1), cmp_k_w1.astype(bf16), cmp_k_w2.astype(bf16),
        cmp_v_pe.reshape(1, -1), cmp_v_w1.astype(bf16), cmp_v_w2.astype(bf16))

    key_blk = jnp.arange(T, dtype=jnp.int32)[:, None] // SEL_BLOCK
    eneg = jnp.where(key_blk == jnp.arange(LANES, dtype=jnp.int32)[None, :], NEG, 0.0).astype(bf16)
    y_nsa = _nsa(q, kcmp, vcmp, kv, eneg, gates, B=B, T=T, tq=256, tk=256)

    h2, w_up_b, w_down_b = _out_proj(h2, u, y_nsa, pool_w.astype(bf16), pool_scale[None, :], w_out_b,
                                     w_up, w_down, T=T, tm=512)
    return _ffn(h2, ffn_norm_g[None, :], w_gate_b, w_up_b, w_down_b, tm=1024, tf=512), w_ple_gate_b


def kernel(x, p, in_norm_g, w_in, pool_w, pool_scale, cmp_k_pe, cmp_k_w1, cmp_k_w2, cmp_v_pe, cmp_v_w1,
           cmp_v_w2, w_out, ffn_norm_g, w_gate, w_up, w_down, ple_norm_g, w_ple_gate, w_ple_proj, final_norm_g):
    B, T, d = x.shape
    depth = w_in.shape[0]
    assert depth == 1, "the final rmsnorm is fused into the last layer's per-layer-embedding kernel"
    h2 = x.reshape(B * T, d)
    i = 0
    h2, w_ple_gate_b = _mixer_ffn(
        h2, in_norm_g[i], w_in[i], pool_w[i], pool_scale[i], cmp_k_pe[i], cmp_k_w1[i], cmp_k_w2[i],
        cmp_v_pe[i], cmp_v_w1[i], cmp_v_w2[i], w_out[i], ffn_norm_g[i], w_gate[i], w_up[i], w_down[i],
        w_ple_gate[i], B=B, T=T)
    out = _ple(h2, p[i].reshape(B * T, -1), ple_norm_g[i][None, :], w_ple_gate_b,
               w_ple_proj[i].astype(bf16), final_norm_g[None, :], tm=512)
    return out.reshape(B, T, d)
```

```python
import functools

import jax
import jax.numpy as jnp
from jax import lax
from jax.experimental import pallas as pl
from jax.experimental.pallas import tpu as pltpu

EPS = 1e-6
NEG = -1e30
LANES = 128
POOL_WINDOWS = (2, 4, 8, 16)
POOL_HALO = 16
HEAD_DIM = 128
N_KV_GROUPS = 2
HEADS_PER_GROUP = 4
N_BRANCH = 3
CMP_BLOCK = 32
CMP_STRIDE = 16
SEL_BLOCK = 64
N_SELECT = 16
WINDOW = 512
ROPE_THETA = 500000.0
ROPE_DIM = HEAD_DIM // 4
ATTN_SCALE = HEAD_DIM ** -0.5
Q_SCALE = ATTN_SCALE * 1.4426950408889634
VMEM_LIMIT = 60 * 1024 * 1024

f32 = jnp.float32
bf16 = jnp.bfloat16


def _dot(a, b):
    return jnp.dot(a, b, preferred_element_type=f32)


def _dot_nt(a, b):
    return lax.dot_general(a, b, (((1,), (1,)), ((), ())), preferred_element_type=f32)


def _rms(x, g):
    return x * lax.rsqrt(jnp.mean(x * x, axis=-1, keepdims=True) + EPS) * g


def _rope(h, cosf, sinf, lane):
    half = ROPE_DIM // 2
    partner = jnp.where(lane < half, pltpu.roll(h, LANES - half, axis=1), pltpu.roll(h, half, axis=1))
    return h * cosf + partner * sinf


def _in_proj_kernel(x_ref, g_ref, w_ref, wgt_ref, cos_ref, sin_ref, ca_ref, cb_ref, cc_ref,
                    u_ref, q_ref, kcvc_ref, kv_ref, gt_ref, ca_out, cb_out, cc_out):
    ca_out[...] = ca_ref[...].astype(bf16)
    cb_out[...] = cb_ref[...].astype(bf16)
    cc_out[...] = cc_ref[...].astype(bf16)
    a = _rms(x_ref[...], g_ref[...]).astype(bf16)
    cosf = cos_ref[...]
    sinf = sin_ref[...]
    lane = lax.broadcasted_iota(jnp.int32, cosf.shape, 1)
    nu = u_ref.shape[1]
    nq = q_ref.shape[1]
    off = 0
    for c in range(0, nu, 512):
        u_ref[:, c:c + 512] = _dot(a, w_ref[:, off + c:off + c + 512]).astype(bf16)
    off += nu
    for c in range(0, nq, 512):
        acc = _dot(a, w_ref[:, off + c:off + c + 512])
        for j in range(4):
            h = _rope(acc[:, j * 128:(j + 1) * 128], cosf, sinf, lane) * Q_SCALE
            q_ref[:, c + j * 128:c + (j + 1) * 128] = h.astype(bf16)
    off += nq
    acc = _dot(a, w_ref[:, off:off + 512])
    for j in range(4):
        h = acc[:, j * 128:(j + 1) * 128]
        if j < 2:
            h = _rope(h, cosf, sinf, lane)
        kcvc_ref[:, j * 128:(j + 1) * 128] = h
    off += 512
    acc = _dot(a, w_ref[:, off:off + 512])
    for j in range(4):
        h = acc[:, j * 128:(j + 1) * 128]
        if j < 2:
            h = _rope(h, cosf, sinf, lane)
        kv_ref[:, j * 128:(j + 1) * 128] = h.astype(bf16)
    off += 512
    acc = _dot(a, w_ref[:, off:off + 512])
    for j in range(4):
        h = acc[:, j * 128:(j + 1) * 128]
        if j < 2:
            h = _rope(h, cosf, sinf, lane)
        kv_ref[:, 512 + j * 128:512 + (j + 1) * 128] = h.astype(bf16)
    off += 512
    gt_ref[...] = _dot(a, wgt_ref[...])


def _in_proj(x2, g, w_p, w_gt, cosf, sinf, cast_a, cast_b, cast_c, *, T, tm):
    n_rows, d = x2.shape
    n_w = w_p.shape[1]
    tiles_per_seq = T // tm
    n_steps = n_rows // tm
    row = lambda i: (i, 0)
    cast_spec = lambda w: pl.BlockSpec((w.shape[0] // n_steps, w.shape[1]), row)
    const = lambda i: (0, 0)
    tab = lambda i: (i % tiles_per_seq, 0)
    return pl.pallas_call(
        _in_proj_kernel,
        grid=(n_rows // tm,),
        in_specs=[
            pl.BlockSpec((tm, d), row),
            pl.BlockSpec((1, d), const),
            pl.BlockSpec((d, n_w), const, pipeline_mode=pl.Buffered(1)),
            pl.BlockSpec((d, LANES), const),
            pl.BlockSpec((tm, LANES), tab),
            pl.BlockSpec((tm, LANES), tab),
            cast_spec(cast_a), cast_spec(cast_b), cast_spec(cast_c),
        ],
        out_specs=[
            pl.BlockSpec((tm, 1024), row),
            pl.BlockSpec((tm, 1024), row),
            pl.BlockSpec((tm, 512), row),
            pl.BlockSpec((tm, 1024), row),
            pl.BlockSpec((tm, LANES), row),
            cast_spec(cast_a), cast_spec(cast_b), cast_spec(cast_c),
        ],
        out_shape=[
            jax.ShapeDtypeStruct((n_rows, 1024), bf16),
            jax.ShapeDtypeStruct((n_rows, 1024), bf16),
            jax.ShapeDtypeStruct((n_rows, 512), f32),
            jax.ShapeDtypeStruct((n_rows, 1024), bf16),
            jax.ShapeDtypeStruct((n_rows, LANES), f32),
            jax.ShapeDtypeStruct(cast_a.shape, bf16),
            jax.ShapeDtypeStruct(cast_b.shape, bf16),
            jax.ShapeDtypeStruct(cast_c.shape, bf16),
        ],
        compiler_params=pltpu.CompilerParams(
            dimension_semantics=("arbitrary",), vmem_limit_bytes=VMEM_LIMIT),
        name="in_proj",
    )(x2, g, w_p, w_gt, cosf, sinf, cast_a, cast_b, cast_c)


def _gelu_tanh(x):
    return 0.5 * x * (1.0 + jnp.tanh(0.7978845608028654 * (x + 0.044715 * (x * x * x))))


def _compress_kernel(kc0_ref, kc1_ref, vc0_ref, vc1_ref, pek_ref, w1k_ref, w2k_ref, pev_ref, w1v_ref, w2v_ref,
                     kc_ref, vc_ref):
    n_rows = kc0_ref.shape[1] // CMP_STRIDE
    half_k = (CMP_BLOCK // 2) * HEAD_DIM
    rowi = lax.broadcasted_iota(jnp.int32, (n_rows, HEAD_DIM), 0)
    for srcs, pe_ref, w1_ref, w2_ref, o_ref in (((kc0_ref, kc1_ref), pek_ref, w1k_ref, w2k_ref, kc_ref),
                                                ((vc0_ref, vc1_ref), pev_ref, w1v_ref, w2v_ref, vc_ref)):
        pe_a = pe_ref[:, :half_k]
        pe_b = pe_ref[:, half_k:]
        for g, src in enumerate(srcs):
            x = jnp.concatenate(
                [src[0, pl.ds(l, n_rows, stride=CMP_STRIDE), :] for l in range(CMP_STRIDE)], axis=1)
            za = _dot((x + pe_a).astype(bf16), w1_ref[:half_k, :])
            zb = _dot((x + pe_b).astype(bf16), w1_ref[half_k:, :])
            h = za + pltpu.roll(zb, n_rows - 1, axis=0)
            o = _dot(_gelu_tanh(h).astype(bf16), w2_ref[...])
            o_ref[0, g] = jnp.where(rowi < n_rows - 1, o, 0.0).astype(bf16)


def _compress(r3, pek, w1k, w2k, pev, w1v, w2v):
    B, T, width = r3.shape
    assert width == 2 * N_KV_GROUPS * HEAD_DIM and N_KV_GROUPS == 2
    n_rows = T // CMP_STRIDE
    const = lambda b: (0, 0)
    out = jax.ShapeDtypeStruct((B, N_KV_GROUPS, n_rows, HEAD_DIM), bf16)
    ospec = pl.BlockSpec((1, N_KV_GROUPS, n_rows, HEAD_DIM), lambda b: (b, 0, 0, 0))
    sec = lambda s: pl.BlockSpec((1, T, HEAD_DIM), lambda b: (b, 0, s))
    return pl.pallas_call(
        _compress_kernel,
        grid=(B,),
        in_specs=[
            sec(0), sec(1), sec(2), sec(3),
            pl.BlockSpec(pek.shape, const), pl.BlockSpec(w1k.shape, const), pl.BlockSpec(w2k.shape, const),
            pl.BlockSpec(pev.shape, const), pl.BlockSpec(w1v.shape, const), pl.BlockSpec(w2v.shape, const),
        ],
        out_specs=[ospec, ospec],
        out_shape=[out, out],
        compiler_params=pltpu.CompilerParams(
            dimension_semantics=("arbitrary",), vmem_limit_bytes=VMEM_LIMIT),
        name="compress",
    )(r3, r3, r3, r3, pek, w1k, w2k, pev, w1v, w2v)


def _nsa_kernel(q_ref, kc_ref, vc_ref, ks_ref, vs_ref, kw_ref, vw_ref, eneg_ref, gt_ref, o_ref,
                s_sc, mx_sc, ls_sc, acc_sc, *, tq, tk, T):
    R = HEADS_PER_GROUP
    M = R * tq
    i = pl.program_id(2)
    t0 = i * tq
    n_cmp = kc_ref.shape[2]
    n_sel = T // SEL_BLOCK

    q = q_ref[...]
    qs = jnp.concatenate([q[:, r * HEAD_DIM:(r + 1) * HEAD_DIM] for r in range(R)], axis=0)

    wk = WINDOW + tq
    w0 = pl.multiple_of(jnp.maximum(t0 - WINDOW, 0), tq)
    sw = _dot_nt(qs, kw_ref[pl.ds(w0, wk), :])
    diff = (t0 - w0) + (lax.broadcasted_iota(jnp.int32, (M, wk), 0) & (tq - 1)) \
        - lax.broadcasted_iota(jnp.int32, (M, wk), 1)
    sw = jnp.where((diff & -WINDOW) == 0, sw, NEG)
    ew = jnp.exp2(sw - jnp.max(sw, axis=1, keepdims=True))
    o_win = _dot(ew.astype(bf16), vw_ref[pl.ds(w0, wk), :]) / jnp.sum(ew, axis=1, keepdims=True)

    kc = kc_ref[0, 0]
    vc = vc_ref[0, 0]
    s = _dot_nt(qs, kc)
    trow = t0 + (lax.broadcasted_iota(jnp.int32, (M, n_cmp), 0) & (tq - 1))
    ncol = lax.broadcasted_iota(jnp.int32, (M, n_cmp), 1)
    valid = (CMP_STRIDE * ncol + (CMP_BLOCK - 1)) <= trow
    s = jnp.where(valid, s, NEG)
    e = jnp.exp2(s - jnp.max(s, axis=1, keepdims=True))
    p = jnp.where(valid, e / jnp.sum(e, axis=1, keepdims=True), 0.0)
    o_cmp = _dot(p.astype(bf16), vc)

    psum = p[0:tq]
    for r in range(1, R):
        psum = psum + p[r * tq:(r + 1) * tq]
    p_hi = psum.astype(bf16)
    p_lo = (psum - p_hi.astype(f32)).astype(bf16)
    sb = lax.broadcasted_iota(jnp.int32, (n_sel, n_cmp), 0) * SEL_BLOCK
    cb = lax.broadcasted_iota(jnp.int32, (n_sel, n_cmp), 1) * CMP_STRIDE
    ov_t = jnp.where((cb < sb + SEL_BLOCK) & (cb + CMP_BLOCK > sb), 1.0, 0.0).astype(bf16)
    imp_t = _dot_nt(ov_t, p_hi) + _dot_nt(ov_t, p_lo)

    jj = lax.broadcasted_iota(jnp.int32, (n_sel, tq), 0)
    cur = jnp.right_shift(t0 + lax.broadcasted_iota(jnp.int32, (n_sel, tq), 1), SEL_BLOCK.bit_length() - 1)
    forced = (jj == 0) | (jj == cur) | (jj == cur - 1)
    v = jnp.where(jj > cur, -jnp.inf, jnp.where(forced, jnp.inf, imp_t))
    sub = 8
    slabs = [v[k * sub:(k + 1) * sub] for k in range(n_sel // sub)]
    ranks = [jnp.zeros((sub, tq), f32) for _ in slabs]
    j_in = lax.broadcasted_iota(jnp.int32, (sub, tq), 0)
    for c in range(n_sel):
        v_c = v[c:c + 1, :]
        for k, v_k in enumerate(slabs):
            if k * sub > c:
                ahead = jnp.where(v_c >= v_k, 1.0, 0.0)
            elif (k + 1) * sub <= c:
                ahead = jnp.where(v_c > v_k, 1.0, 0.0)
            else:
                ahead = jnp.where(j_in > c - k * sub, jnp.where(v_c >= v_k, 1.0, 0.0),
                                  jnp.where(v_c > v_k, 1.0, 0.0))
            ranks[k] = ranks[k] + ahead
    rank = jnp.concatenate(ranks, axis=0)
    unsel_t = jnp.where(rank < float(min(N_SELECT, n_sel)), 0.0, 1.0)
    unsel_t = jnp.concatenate([unsel_t, jnp.zeros((LANES - n_sel, tq), f32)], axis=0)
    unsel = unsel_t.T.astype(bf16)
    q_aug = jnp.concatenate([qs, jnp.concatenate([unsel] * R, axis=0)], axis=1)

    n_chunk = tk // LANES
    mx_sc[...] = jnp.full(mx_sc.shape, NEG, f32)
    dpos = (lax.broadcasted_iota(jnp.int32, (M, tk), 0) & (tq - 1)) - lax.broadcasted_iota(jnp.int32, (M, tk), 1)

    def score_step(kt, causal):
        k0 = pl.multiple_of(kt * tk, tk)
        k_aug = jnp.concatenate([ks_ref[pl.ds(k0, tk), :], eneg_ref[pl.ds(k0, tk), :]], axis=1)
        sc = _dot_nt(q_aug, k_aug)
        if causal:
            sc = jnp.where(dpos + (t0 - k0) >= 0, sc, NEG)
        s_sc[kt] = sc
        part = sc[:, :LANES]
        for c in range(1, n_chunk):
            part = jnp.maximum(part, sc[:, c * LANES:(c + 1) * LANES])
        mx_sc[...] = jnp.maximum(mx_sc[...], part)

    n_full = t0 // tk

    def score_body(kt, carry):
        score_step(kt, False)
        return carry

    lax.fori_loop(0, n_full, score_body, 0)
    score_step(n_full, True)
    mx_sc[...] = jnp.broadcast_to(jnp.max(mx_sc[...], axis=1, keepdims=True), mx_sc.shape)
    ls_sc[...] = jnp.zeros(ls_sc.shape, f32)
    acc_sc[...] = jnp.zeros(acc_sc.shape, f32)

    def pv_body(kt, carry):
        k0 = pl.multiple_of(kt * tk, tk)
        sc = s_sc[kt]
        mb = mx_sc[...]
        ps = [jnp.exp2(sc[:, c * LANES:(c + 1) * LANES] - mb) for c in range(n_chunk)]
        ls_sc[...] += functools.reduce(lambda a, b: a + b, ps)
        acc_sc[...] += _dot(jnp.concatenate(ps, axis=1).astype(bf16), vs_ref[pl.ds(k0, tk), :])
        return carry

    lax.fori_loop(0, n_full + 1, pv_body, 0)
    o_sel = acc_sc[...] / jnp.sum(ls_sc[...], axis=1, keepdims=True)

    gt = jax.nn.sigmoid(gt_ref[...])
    gt = jnp.where(pl.program_id(1) == 0, gt, pltpu.roll(gt, LANES - R * N_BRANCH, axis=1))
    for r in range(R):
        rows = slice(r * tq, (r + 1) * tq)
        c = r * N_BRANCH
        o = (gt[:, c:c + 1] * o_cmp[rows] + gt[:, c + 1:c + 2] * o_sel[rows]
             + gt[:, c + 2:c + 3] * o_win[rows])
        o_ref[:, r * HEAD_DIM:(r + 1) * HEAD_DIM] = o.astype(bf16)


def _nsa(q, kcmp, vcmp, kv, eneg, gates, *, B, T, tq, tk):
    G, R = N_KV_GROUPS, HEADS_PER_GROUP
    nt = T // tq
    n_cmp = kcmp.shape[2]
    assert tq % LANES == 0 and tq & (tq - 1) == 0 and T % tk == 0 and tk % tq == 0 and T >= WINDOW + tq
    assert WINDOW % tq == 0 and WINDOW & (WINDOW - 1) == 0
    rowblk = lambda b, g, i: (b * nt + i, g)
    cmp_spec = pl.BlockSpec((1, 1, n_cmp, HEAD_DIM), lambda b, g, i: (b, g, 0, 0))
    kvspec = lambda sec: pl.BlockSpec((T, HEAD_DIM), lambda b, g, i: (b, sec * G + g))
    return pl.pallas_call(
        functools.partial(_nsa_kernel, tq=tq, tk=tk, T=T),
        grid=(B, G, nt),
        in_specs=[
            pl.BlockSpec((tq, R * HEAD_DIM), rowblk),
            cmp_spec, cmp_spec,
            kvspec(0), kvspec(1), kvspec(2), kvspec(3),
            pl.BlockSpec((T, LANES), lambda b, g, i: (0, 0)),
            pl.BlockSpec((tq, LANES), lambda b, g, i: (b * nt + i, 0)),
        ],
        out_specs=pl.BlockSpec((tq, R * HEAD_DIM), rowblk),
        out_shape=jax.ShapeDtypeStruct((B * T, G * R * HEAD_DIM), bf16),
        scratch_shapes=[
            pltpu.VMEM((T // tk, R * tq, tk), f32),
            pltpu.VMEM((R * tq, LANES), f32),
            pltpu.VMEM((R * tq, LANES), f32),
            pltpu.VMEM((R * tq, HEAD_DIM), f32),
        ],
        compiler_params=pltpu.CompilerParams(
            dimension_semantics=("arbitrary", "arbitrary", "arbitrary"), vmem_limit_bytes=VMEM_LIMIT),
        name="nsa",
    )(q, kcmp, vcmp, kv, kv, kv, kv, eneg, gates)


def _out_proj_kernel(x_ref, u_ref, uh_ref, y_ref, pw_ref, ps_ref, wo_ref, ca_ref, cb_ref,
                     o_ref, ca_out, cb_out, *, tm, T):
    ca_out[...] = ca_ref[...].astype(bf16)
    cb_out[...] = cb_ref[...].astype(bf16)
    i = pl.program_id(0)
    tiles_per_seq = T // tm
    keep_halo = jnp.where((i % tiles_per_seq) == 0, 0.0, 1.0)
    u = u_ref[...]
    halo = uh_ref[...]
    tt = lax.broadcasted_iota(jnp.int32, (tm, tm), 0)
    ss = lax.broadcasted_iota(jnp.int32, (tm, tm), 1)
    th = lax.broadcasted_iota(jnp.int32, (tm, POOL_HALO), 0)
    rh = lax.broadcasted_iota(jnp.int32, (tm, POOL_HALO), 1)
    tseq = (i % tiles_per_seq) * tm + lax.broadcasted_iota(jnp.int32, (tm, 1), 0)
    n_pool = u.shape[1]
    gd = n_pool // len(POOL_WINDOWS)
    acc = x_ref[...] + _dot(y_ref[...], wo_ref[n_pool:, :])
    for gi, w in enumerate(POOL_WINDOWS):
        cols = slice(gi * gd, (gi + 1) * gd)
        band = jnp.where((tt - ss >= 0) & (tt - ss < w), 1.0, 0.0).astype(bf16)
        band_h = jnp.where(th + POOL_HALO - rh < w, 1.0, 0.0).astype(bf16)
        ug = u[:, cols]
        win_sum = _dot(band, ug) + keep_halo * _dot(band_h, halo[:, cols])
        count = jnp.minimum(tseq + 1, w).astype(f32)
        pooled = win_sum / count - ug.astype(f32)
        yg = _dot(pooled.astype(bf16), pw_ref[gi]) * ps_ref[:, cols]
        acc = acc + _dot(yg.astype(bf16), wo_ref[cols, :])
    o_ref[...] = acc


def _out_proj(x2, u, y_nsa, pool_w, pool_scale, w_out, cast_a, cast_b, *, T, tm):
    n_rows, d = x2.shape
    n_pool = u.shape[1]
    hb = tm // POOL_HALO
    n_steps = n_rows // tm
    row = lambda i: (i, 0)
    cast_spec = lambda w: pl.BlockSpec((w.shape[0] // n_steps, w.shape[1]), row)
    const2 = lambda i: (0, 0)
    return pl.pallas_call(
        functools.partial(_out_proj_kernel, tm=tm, T=T),
        grid=(n_rows // tm,),
        in_specs=[
            pl.BlockSpec((tm, d), row),
            pl.BlockSpec((tm, n_pool), row),
            pl.BlockSpec((POOL_HALO, n_pool), lambda i: (jnp.maximum(i * hb - 1, 0), 0)),
            pl.BlockSpec((tm, y_nsa.shape[1]), row),
            pl.BlockSpec(pool_w.shape, lambda i: (0, 0, 0)),
            pl.BlockSpec((1, n_pool), const2),
            pl.BlockSpec(w_out.shape, const2, pipeline_mode=pl.Buffered(1)),
            cast_spec(cast_a), cast_spec(cast_b),
        ],
        out_specs=[pl.BlockSpec((tm, d), row), cast_spec(cast_a), cast_spec(cast_b)],
        out_shape=[jax.ShapeDtypeStruct((n_rows, d), f32), jax.ShapeDtypeStruct(cast_a.shape, bf16),
                   jax.ShapeDtypeStruct(cast_b.shape, bf16)],
        compiler_params=pltpu.CompilerParams(
            dimension_semantics=("arbitrary",), vmem_limit_bytes=VMEM_LIMIT),
        name="out_proj",
    )(x2, u, u, y_nsa, pool_w, pool_scale, w_out, cast_a, cast_b)


def _ffn_kernel(h_ref, g_ref, wg_ref, wu_ref, wd_ref, o_ref, n_sc):
    @pl.when(pl.program_id(1) == 0)
    def _():
        h = h_ref[...]
        n_sc[...] = _rms(h, g_ref[...]).astype(bf16)
        o_ref[...] = h

    n = n_sc[...]
    gate = _dot(n, wg_ref[...])
    up = _dot(n, wu_ref[...])
    a = gate * jax.nn.sigmoid(gate) * up
    o_ref[...] += _dot(a.astype(bf16), wd_ref[...])


def _ffn(h, g, w_gate, w_up, w_down, *, tm, tf):
    n_rows, d = h.shape
    d_ff = w_gate.shape[1]
    return pl.pallas_call(
        _ffn_kernel,
        grid=(n_rows // tm, d_ff // tf),
        in_specs=[
            pl.BlockSpec((tm, d), lambda i, f: (i, 0)),
            pl.BlockSpec((1, d), lambda i, f: (0, 0)),
            pl.BlockSpec((d, tf), lambda i, f: (0, f)),
            pl.BlockSpec((d, tf), lambda i, f: (0, f)),
            pl.BlockSpec((tf, d), lambda i, f: (f, 0)),
        ],
        out_specs=pl.BlockSpec((tm, d), lambda i, f: (i, 0)),
        out_shape=jax.ShapeDtypeStruct((n_rows, d), f32),
        scratch_shapes=[pltpu.VMEM((tm, d), bf16)],
        compiler_params=pltpu.CompilerParams(
            dimension_semantics=("arbitrary", "arbitrary"), vmem_limit_bytes=VMEM_LIMIT),
        name="ffn",
    )(h, g, w_gate, w_up, w_down)


def _ple_kernel(h_ref, p_ref, g_ref, wg_ref, wp_ref, fg_ref, o_ref):
    h = h_ref[...]
    n = _rms(h, g_ref[...]).astype(bf16)
    gate = jax.nn.sigmoid(_dot(n, wg_ref[...]))
    h = h + _dot(p_ref[...].astype(bf16), wp_ref[...]) * gate
    o_ref[...] = _rms(h, fg_ref[...])


def _ple(h, p2, g, w_gate, w_proj, fg, *, tm):
    n_rows, d = h.shape
    row = lambda i: (i, 0)
    const = lambda i: (0, 0)
    return pl.pallas_call(
        _ple_kernel,
        grid=(n_rows // tm,),
        in_specs=[
            pl.BlockSpec((tm, d), row),
            pl.BlockSpec((tm, p2.shape[1]), row),
            pl.BlockSpec((1, d), const),
            pl.BlockSpec(w_gate.shape, const, pipeline_mode=pl.Buffered(1)),
            pl.BlockSpec(w_proj.shape, const),
            pl.BlockSpec((1, d), const),
        ],
        out_specs=pl.BlockSpec((tm, d), row),
        out_shape=jax.ShapeDtypeStruct((n_rows, d), f32),
        compiler_params=pltpu.CompilerParams(
            dimension_semantics=("arbitrary",), vmem_limit_bytes=VMEM_LIMIT),
        name="ple",
    )(h, p2, g, w_gate, w_proj, fg)


def _rope_tables(T):
    pos = jnp.arange(T, dtype=f32)
    inv_freq = ROPE_THETA ** (-jnp.arange(0, ROPE_DIM, 2, dtype=f32) / ROPE_DIM)
    ang = pos[:, None] * inv_freq[None, :]
    cos, sin = jnp.cos(ang), jnp.sin(ang)
    rest = HEAD_DIM - ROPE_DIM
    cosf = jnp.concatenate([cos, cos, jnp.ones((T, rest), f32)], axis=1)
    sinf = jnp.concatenate([-sin, sin, jnp.zeros((T, rest), f32)], axis=1)
    return cosf, sinf


def _mixer_ffn(h2, in_norm_g, w_in, pool_w, pool_scale, cmp_k_pe, cmp_k_w1, cmp_k_w2,
               cmp_v_pe, cmp_v_w1, cmp_v_w2, w_out, ffn_norm_g, w_gate, w_up, w_down, w_ple_gate, *, B, T):
    n_pool = pool_scale.shape[0]
    n_heads = N_KV_GROUPS * HEADS_PER_GROUP
    nsa_w = n_heads * HEAD_DIM
    kv_w = N_KV_GROUPS * HEAD_DIM
    assert w_in.shape[1] == n_pool + nsa_w + 6 * kv_w + n_heads * N_BRANCH and n_pool == 1024 and nsa_w == 1024

    o0 = n_pool + nsa_w + 6 * kv_w
    w_p = w_in.astype(bf16)
    w_gt = jnp.pad(w_p[:, o0:], ((0, 0), (0, LANES - n_heads * N_BRANCH)))

    cosf, sinf = _rope_tables(T)
    u, q, kcvc, kv, gates, w_out_b, w_gate_b, w_ple_gate_b = _in_proj(
        h2, in_norm_g[None, :], w_p, w_gt, cosf, sinf, w_out, w_gate, w_ple_gate, T=T, tm=512)

    kcmp, vcmp = _compress(
        kcvc.reshape(B, T, kcvc.shape[1]),
        cmp_k_pe.reshape(1, -1), cmp_k_w1.astype(bf16), cmp_k_w2.astype(bf16),
        cmp_v_pe.reshape(1, -1), cmp_v_w1.astype(bf16), cmp_v_w2.astype(bf16))

    key_blk = jnp.arange(T, dtype=jnp.int32)[:, None] // SEL_BLOCK
    eneg = jnp.where(key_blk == jnp.arange(LANES, dtype=jnp.int32)[None, :], NEG, 0.0).astype(bf16)
    y_nsa = _nsa(q, kcmp, vcmp, kv, eneg, gates, B=B, T=T, tq=256, tk=512)

    h2, w_up_b, w_down_b = _out_proj(h2, u, y_nsa, pool_w.astype(bf16), pool_scale[None, :], w_out_b,
                                     w_up, w_down, T=T, tm=512)
    return _ffn(h2, ffn_norm_g[None, :], w_gate_b, w_up_b, w_down_b, tm=1024, tf=512), w_ple_gate_b


def kernel(x, p, in_norm_g, w_in, pool_w, pool_scale, cmp_k_pe, cmp_k_w1, cmp_k_w2, cmp_v_pe, cmp_v_w1,
           cmp_v_w2, w_out, ffn_norm_g, w_gate, w_up, w_down, ple_norm_g, w_ple_gate, w_ple_proj, final_norm_g):
    B, T, d = x.shape
    depth = w_in.shape[0]
    assert depth == 1, "the final rmsnorm is fused into the last layer's per-layer-embedding kernel"
    h2 = x.reshape(B * T, d)
    i = 0
    h2, w_ple_gate_b = _mixer_ffn(
        h2, in_norm_g[i], w_in[i], pool_w[i], pool_scale[i], cmp_k_pe[i], cmp_k_w1[i], cmp_k_w2[i],
        cmp_v_pe[i], cmp_v_w1[i], cmp_v_w2[i], w_out[i], ffn_norm_g[i], w_gate[i], w_up[i], w_down[i],
        w_ple_gate[i], B=B, T=T)
    out = _ple(h2, p[i].reshape(B * T, -1), ple_norm_g[i][None, :], w_ple_gate_b,
               w_ple_proj[i].astype(bf16), final_norm_g[None, :], tm=512)
    return out.reshape(B, T, d)
```

```python
import functools

import jax
import jax.numpy as jnp
from jax import lax
from jax.experimental import pallas as pl
from jax.experimental.pallas import tpu as pltpu

EPS = 1e-6
NEG = -1e30
LANES = 128
POOL_WINDOWS = (2, 4, 8, 16)
POOL_HALO = 16
HEAD_DIM = 128
N_KV_GROUPS = 2
HEADS_PER_GROUP = 4
N_BRANCH = 3
CMP_BLOCK = 32
CMP_STRIDE = 16
SEL_BLOCK = 64
N_SELECT = 16
WINDOW = 512
ROPE_THETA = 500000.0
ROPE_DIM = HEAD_DIM // 4
ATTN_SCALE = HEAD_DIM ** -0.5
Q_SCALE = ATTN_SCALE * 1.4426950408889634
VMEM_LIMIT = 60 * 1024 * 1024

f32 = jnp.float32
bf16 = jnp.bfloat16


def _dot(a, b):
    return jnp.dot(a, b, preferred_element_type=f32)


def _dot_nt(a, b):
    return lax.dot_general(a, b, (((1,), (1,)), ((), ())), preferred_element_type=f32)


def _rms(x, g):
    return x * lax.rsqrt(jnp.mean(x * x, axis=-1, keepdims=True) + EPS) * g


def _rope(h, cosf, sinf, lane):
    half = ROPE_DIM // 2
    partner = jnp.where(lane < half, pltpu.roll(h, LANES - half, axis=1), pltpu.roll(h, half, axis=1))
    return h * cosf + partner * sinf


def _in_proj_kernel(x_ref, g_ref, w_ref, wgt_ref, cos_ref, sin_ref, ca_ref, cb_ref, cc_ref,
                    u_ref, q_ref, kcvc_ref, kv_ref, gt_ref, ca_out, cb_out, cc_out):
    ca_out[...] = ca_ref[...].astype(bf16)
    cb_out[...] = cb_ref[...].astype(bf16)
    cc_out[...] = cc_ref[...].astype(bf16)
    a = _rms(x_ref[...], g_ref[...]).astype(bf16)
    cosf = cos_ref[...]
    sinf = sin_ref[...]
    lane = lax.broadcasted_iota(jnp.int32, cosf.shape, 1)
    nu = u_ref.shape[1]
    nq = q_ref.shape[1]
    off = 0
    for c in range(0, nu, 512):
        u_ref[:, c:c + 512] = _dot(a, w_ref[:, off + c:off + c + 512]).astype(bf16)
    off += nu
    for c in range(0, nq, 512):
        acc = _dot(a, w_ref[:, off + c:off + c + 512])
        for j in range(4):
            h = _rope(acc[:, j * 128:(j + 1) * 128], cosf, sinf, lane) * Q_SCALE
            q_ref[:, c + j * 128:c + (j + 1) * 128] = h.astype(bf16)
    off += nq
    acc = _dot(a, w_ref[:, off:off + 512])
    for j in range(4):
        h = acc[:, j * 128:(j + 1) * 128]
        if j < 2:
            h = _rope(h, cosf, sinf, lane)
        kcvc_ref[:, j * 128:(j + 1) * 128] = h
    off += 512
    acc = _dot(a, w_ref[:, off:off + 512])
    for j in range(4):
        h = acc[:, j * 128:(j + 1) * 128]
        if j < 2:
            h = _rope(h, cosf, sinf, lane)
        kv_ref[:, j * 128:(j + 1) * 128] = h.astype(bf16)
    off += 512
    acc = _dot(a, w_ref[:, off:off + 512])
    for j in range(4):
        h = acc[:, j * 128:(j + 1) * 128]
        if j < 2:
            h = _rope(h, cosf, sinf, lane)
        kv_ref[:, 512 + j * 128:512 + (j + 1) * 128] = h.astype(bf16)
    off += 512
    gt_ref[...] = _dot(a, wgt_ref[...])


def _in_proj(x2, g, w_p, w_gt, cosf, sinf, cast_a, cast_b, cast_c, *, T, tm):
    n_rows, d = x2.shape
    n_w = w_p.shape[1]
    tiles_per_seq = T // tm
    n_steps = n_rows // tm
    row = lambda i: (i, 0)
    cast_spec = lambda w: pl.BlockSpec((w.shape[0] // n_steps, w.shape[1]), row)
    const = lambda i: (0, 0)
    tab = lambda i: (i % tiles_per_seq, 0)
    return pl.pallas_call(
        _in_proj_kernel,
        grid=(n_rows // tm,),
        in_specs=[
            pl.BlockSpec((tm, d), row),
            pl.BlockSpec((1, d), const),
            pl.BlockSpec((d, n_w), const, pipeline_mode=pl.Buffered(1)),
            pl.BlockSpec((d, LANES), const),
            pl.BlockSpec((tm, LANES), tab),
            pl.BlockSpec((tm, LANES), tab),
            cast_spec(cast_a), cast_spec(cast_b), cast_spec(cast_c),
        ],
        out_specs=[
            pl.BlockSpec((tm, 1024), row),
            pl.BlockSpec((tm, 1024), row),
            pl.BlockSpec((tm, 512), row),
            pl.BlockSpec((tm, 1024), row),
            pl.BlockSpec((tm, LANES), row),
            cast_spec(cast_a), cast_spec(cast_b), cast_spec(cast_c),
        ],
        out_shape=[
            jax.ShapeDtypeStruct((n_rows, 1024), bf16),
            jax.ShapeDtypeStruct((n_rows, 1024), bf16),
            jax.ShapeDtypeStruct((n_rows, 512), f32),
            jax.ShapeDtypeStruct((n_rows, 1024), bf16),
            jax.ShapeDtypeStruct((n_rows, LANES), f32),
            jax.ShapeDtypeStruct(cast_a.shape, bf16),
            jax.ShapeDtypeStruct(cast_b.shape, bf16),
            jax.ShapeDtypeStruct(cast_c.shape, bf16),
        ],
        compiler_params=pltpu.CompilerParams(
            dimension_semantics=("arbitrary",), vmem_limit_bytes=VMEM_LIMIT),
        name="in_proj",
    )(x2, g, w_p, w_gt, cosf, sinf, cast_a, cast_b, cast_c)


def _gelu_tanh(x):
    return 0.5 * x * (1.0 + jnp.tanh(0.7978845608028654 * (x + 0.044715 * (x * x * x))))


def _compress_kernel(kc0_ref, kc1_ref, vc0_ref, vc1_ref, pek_ref, w1k_ref, w2k_ref, pev_ref, w1v_ref, w2v_ref,
                     kc_ref, vc_ref):
    n_rows = kc0_ref.shape[1] // CMP_STRIDE
    half_k = (CMP_BLOCK // 2) * HEAD_DIM
    rowi = lax.broadcasted_iota(jnp.int32, (n_rows, HEAD_DIM), 0)
    for srcs, pe_ref, w1_ref, w2_ref, o_ref in (((kc0_ref, kc1_ref), pek_ref, w1k_ref, w2k_ref, kc_ref),
                                                ((vc0_ref, vc1_ref), pev_ref, w1v_ref, w2v_ref, vc_ref)):
        pe_a = pe_ref[:, :half_k]
        pe_b = pe_ref[:, half_k:]
        for g, src in enumerate(srcs):
            x = jnp.concatenate(
                [src[0, pl.ds(l, n_rows, stride=CMP_STRIDE), :] for l in range(CMP_STRIDE)], axis=1)
            za = _dot((x + pe_a).astype(bf16), w1_ref[:half_k, :])
            zb = _dot((x + pe_b).astype(bf16), w1_ref[half_k:, :])
            h = za + pltpu.roll(zb, n_rows - 1, axis=0)
            o = _dot(_gelu_tanh(h).astype(bf16), w2_ref[...])
            o_ref[0, g] = jnp.where(rowi < n_rows - 1, o, 0.0).astype(bf16)


def _compress(r3, pek, w1k, w2k, pev, w1v, w2v):
    B, T, width = r3.shape
    assert width == 2 * N_KV_GROUPS * HEAD_DIM and N_KV_GROUPS == 2
    n_rows = T // CMP_STRIDE
    const = lambda b: (0, 0)
    out = jax.ShapeDtypeStruct((B, N_KV_GROUPS, n_rows, HEAD_DIM), bf16)
    ospec = pl.BlockSpec((1, N_KV_GROUPS, n_rows, HEAD_DIM), lambda b: (b, 0, 0, 0))
    sec = lambda s: pl.BlockSpec((1, T, HEAD_DIM), lambda b: (b, 0, s))
    return pl.pallas_call(
        _compress_kernel,
        grid=(B,),
        in_specs=[
            sec(0), sec(1), sec(2), sec(3),
            pl.BlockSpec(pek.shape, const), pl.BlockSpec(w1k.shape, const), pl.BlockSpec(w2k.shape, const),
            pl.BlockSpec(pev.shape, const), pl.BlockSpec(w1v.shape, const), pl.BlockSpec(w2v.shape, const),
        ],
        out_specs=[ospec, ospec],
        out_shape=[out, out],
        compiler_params=pltpu.CompilerParams(
            dimension_semantics=("arbitrary",), vmem_limit_bytes=VMEM_LIMIT),
        name="compress",
    )(r3, r3, r3, r3, pek, w1k, w2k, pev, w1v, w2v)


def _nsa_front(qs, t0, kc, vc, kw_ref, vw_ref, *, tq, T):
    R = HEADS_PER_GROUP
    M = R * tq
    n_cmp = kc.shape[0]
    n_sel = T // SEL_BLOCK

    wk = WINDOW + tq
    w0 = pl.multiple_of(jnp.maximum(t0 - WINDOW, 0), tq)
    sw = _dot_nt(qs, kw_ref[pl.ds(w0, wk), :])
    diff = (t0 - w0) + (lax.broadcasted_iota(jnp.int32, (M, wk), 0) & (tq - 1)) \
        - lax.broadcasted_iota(jnp.int32, (M, wk), 1)
    sw = jnp.where((diff & -WINDOW) == 0, sw, NEG)
    ew = jnp.exp2(sw - jnp.max(sw, axis=1, keepdims=True))
    o_win = _dot(ew.astype(bf16), vw_ref[pl.ds(w0, wk), :]) / jnp.sum(ew, axis=1, keepdims=True)

    s = _dot_nt(qs, kc)
    trow = t0 + (lax.broadcasted_iota(jnp.int32, (M, n_cmp), 0) & (tq - 1))
    ncol = lax.broadcasted_iota(jnp.int32, (M, n_cmp), 1)
    valid = (CMP_STRIDE * ncol + (CMP_BLOCK - 1)) <= trow
    s = jnp.where(valid, s, NEG)
    e = jnp.exp2(s - jnp.max(s, axis=1, keepdims=True))
    p = jnp.where(valid, e / jnp.sum(e, axis=1, keepdims=True), 0.0)
    o_cmp = _dot(p.astype(bf16), vc)

    psum = p[0:tq]
    for r in range(1, R):
        psum = psum + p[r * tq:(r + 1) * tq]
    p_hi = psum.astype(bf16)
    p_lo = (psum - p_hi.astype(f32)).astype(bf16)
    sb = lax.broadcasted_iota(jnp.int32, (n_sel, n_cmp), 0) * SEL_BLOCK
    cb = lax.broadcasted_iota(jnp.int32, (n_sel, n_cmp), 1) * CMP_STRIDE
    ov_t = jnp.where((cb < sb + SEL_BLOCK) & (cb + CMP_BLOCK > sb), 1.0, 0.0).astype(bf16)
    imp_t = _dot_nt(ov_t, p_hi) + _dot_nt(ov_t, p_lo)

    jj = lax.broadcasted_iota(jnp.int32, (n_sel, tq), 0)
    cur = jnp.right_shift(t0 + lax.broadcasted_iota(jnp.int32, (n_sel, tq), 1), SEL_BLOCK.bit_length() - 1)
    forced = (jj == 0) | (jj == cur) | (jj == cur - 1)
    v = jnp.where(jj > cur, -jnp.inf, jnp.where(forced, jnp.inf, imp_t))
    sub = 8
    slabs = [v[k * sub:(k + 1) * sub] for k in range(n_sel // sub)]
    ranks = [jnp.zeros((sub, tq), f32) for _ in slabs]
    j_in = lax.broadcasted_iota(jnp.int32, (sub, tq), 0)
    for c in range(n_sel):
        v_c = v[c:c + 1, :]
        for k, v_k in enumerate(slabs):
            if k * sub > c:
                ahead = jnp.where(v_c >= v_k, 1.0, 0.0)
            elif (k + 1) * sub <= c:
                ahead = jnp.where(v_c > v_k, 1.0, 0.0)
            else:
                ahead = jnp.where(j_in > c - k * sub, jnp.where(v_c >= v_k, 1.0, 0.0),
                                  jnp.where(v_c > v_k, 1.0, 0.0))
            ranks[k] = ranks[k] + ahead
    rank = jnp.concatenate(ranks, axis=0)
    unsel_t = jnp.where(rank < float(min(N_SELECT, n_sel)), 0.0, 1.0)
    unsel_t = jnp.concatenate([unsel_t, jnp.zeros((LANES - n_sel, tq), f32)], axis=0)
    return o_win, o_cmp, unsel_t.T.astype(bf16)


def _nsa_kernel(q_ref, kc_ref, vc_ref, ks_ref, vs_ref, kw_ref, vw_ref, eneg_ref, gt_ref, o_ref,
                s_sc, mx_sc, ls_sc, acc_sc, *, tq, n_sub, T):
    R = HEADS_PER_GROUP
    M = R * tq
    tk = n_sub * tq
    i = pl.program_id(2)
    t0 = i * tk
    kc = kc_ref[0, 0]
    vc = vc_ref[0, 0]

    fronts = []
    q_rows = []
    for sb in range(n_sub):
        q = q_ref[sb * tq:(sb + 1) * tq, :]
        qs = jnp.concatenate([q[:, r * HEAD_DIM:(r + 1) * HEAD_DIM] for r in range(R)], axis=0)
        o_win, o_cmp, unsel = _nsa_front(qs, t0 + sb * tq, kc, vc, kw_ref, vw_ref, tq=tq, T=T)
        fronts.append((o_win, o_cmp))
        q_rows.append(jnp.concatenate([qs, jnp.concatenate([unsel] * R, axis=0)], axis=1))
    q_aug = jnp.concatenate(q_rows, axis=0)
    MM = n_sub * M

    n_chunk = tk // LANES
    mx_sc[...] = jnp.full(mx_sc.shape, NEG, f32)

    def score_step(kt, causal):
        k0 = pl.multiple_of(kt * tk, tk)
        k_aug = jnp.concatenate([ks_ref[pl.ds(k0, tk), :], eneg_ref[pl.ds(k0, tk), :]], axis=1)
        sc = _dot_nt(q_aug, k_aug)
        if causal:
            rowi = lax.broadcasted_iota(jnp.int32, (MM, tk), 0)
            qoff = jnp.right_shift(rowi, M.bit_length() - 1) * tq + (rowi & (tq - 1))
            sc = jnp.where(qoff >= lax.broadcasted_iota(jnp.int32, (MM, tk), 1), sc, NEG)
        s_sc[kt] = sc
        part = sc[:, :LANES]
        for c in range(1, n_chunk):
            part = jnp.maximum(part, sc[:, c * LANES:(c + 1) * LANES])
        mx_sc[...] = jnp.maximum(mx_sc[...], part)

    def score_body(kt, carry):
        score_step(kt, False)
        return carry

    lax.fori_loop(0, i, score_body, 0)
    score_step(i, True)
    mx_sc[...] = jnp.broadcast_to(jnp.max(mx_sc[...], axis=1, keepdims=True), mx_sc.shape)
    ls_sc[...] = jnp.zeros(ls_sc.shape, f32)
    acc_sc[...] = jnp.zeros(acc_sc.shape, f32)

    def pv_body(kt, carry):
        k0 = pl.multiple_of(kt * tk, tk)
        sc = s_sc[kt]
        mb = mx_sc[...]
        ps = [jnp.exp2(sc[:, c * LANES:(c + 1) * LANES] - mb) for c in range(n_chunk)]
        ls_sc[...] += functools.reduce(lambda a, b: a + b, ps)
        acc_sc[...] += _dot(jnp.concatenate(ps, axis=1).astype(bf16), vs_ref[pl.ds(k0, tk), :])
        return carry

    lax.fori_loop(0, i + 1, pv_body, 0)
    o_sel = acc_sc[...] / jnp.sum(ls_sc[...], axis=1, keepdims=True)

    gt_all = jax.nn.sigmoid(gt_ref[...])
    gt_all = jnp.where(pl.program_id(1) == 0, gt_all, pltpu.roll(gt_all, LANES - R * N_BRANCH, axis=1))
    for sb, (o_win, o_cmp) in enumerate(fronts):
        gt = gt_all[sb * tq:(sb + 1) * tq]
        for r in range(R):
            rows = slice(r * tq, (r + 1) * tq)
            srows = slice(sb * M + r * tq, sb * M + (r + 1) * tq)
            c = r * N_BRANCH
            o = (gt[:, c:c + 1] * o_cmp[rows] + gt[:, c + 1:c + 2] * o_sel[srows]
                 + gt[:, c + 2:c + 3] * o_win[rows])
            o_ref[sb * tq:(sb + 1) * tq, r * HEAD_DIM:(r + 1) * HEAD_DIM] = o.astype(bf16)


def _nsa(q, kcmp, vcmp, kv, eneg, gates, *, B, T, tq, n_sub):
    G, R = N_KV_GROUPS, HEADS_PER_GROUP
    tk = n_sub * tq
    nt = T // tk
    n_cmp = kcmp.shape[2]
    assert tq % LANES == 0 and tq & (tq - 1) == 0 and T % tk == 0 and T >= WINDOW + tq
    assert WINDOW % tq == 0 and WINDOW & (WINDOW - 1) == 0 and (R * tq) & (R * tq - 1) == 0
    rowblk = lambda b, g, i: (b * nt + i, g)
    cmp_spec = pl.BlockSpec((1, 1, n_cmp, HEAD_DIM), lambda b, g, i: (b, g, 0, 0))
    kvspec = lambda sec: pl.BlockSpec((T, HEAD_DIM), lambda b, g, i: (b, sec * G + g))
    return pl.pallas_call(
        functools.partial(_nsa_kernel, tq=tq, n_sub=n_sub, T=T),
        grid=(B, G, nt),
        in_specs=[
            pl.BlockSpec((tk, R * HEAD_DIM), rowblk),
            cmp_spec, cmp_spec,
            kvspec(0), kvspec(1), kvspec(2), kvspec(3),
            pl.BlockSpec((T, LANES), lambda b, g, i: (0, 0)),
            pl.BlockSpec((tk, LANES), lambda b, g, i: (b * nt + i, 0)),
        ],
        out_specs=pl.BlockSpec((tk, R * HEAD_DIM), rowblk),
        out_shape=jax.ShapeDtypeStruct((B * T, G * R * HEAD_DIM), bf16),
        scratch_shapes=[
            pltpu.VMEM((T // tk, n_sub * R * tq, tk), f32),
            pltpu.VMEM((n_sub * R * tq, LANES), f32),
            pltpu.VMEM((n_sub * R * tq, LANES), f32),
            pltpu.VMEM((n_sub * R * tq, HEAD_DIM), f32),
        ],
        compiler_params=pltpu.CompilerParams(
            dimension_semantics=("arbitrary", "arbitrary", "arbitrary"), vmem_limit_bytes=VMEM_LIMIT),
        name="nsa",
    )(q, kcmp, vcmp, kv, kv, kv, kv, eneg, gates)


def _out_proj_kernel(x_ref, u_ref, uh_ref, y_ref, pw_ref, ps_ref, wo_ref, ca_ref, cb_ref,
                     o_ref, ca_out, cb_out, *, tm, T):
    ca_out[...] = ca_ref[...].astype(bf16)
    cb_out[...] = cb_ref[...].astype(bf16)
    i = pl.program_id(0)
    tiles_per_seq = T // tm
    keep_halo = jnp.where((i % tiles_per_seq) == 0, 0.0, 1.0)
    u = u_ref[...]
    halo = uh_ref[...]
    tt = lax.broadcasted_iota(jnp.int32, (tm, tm), 0)
    ss = lax.broadcasted_iota(jnp.int32, (tm, tm), 1)
    th = lax.broadcasted_iota(jnp.int32, (tm, POOL_HALO), 0)
    rh = lax.broadcasted_iota(jnp.int32, (tm, POOL_HALO), 1)
    tseq = (i % tiles_per_seq) * tm + lax.broadcasted_iota(jnp.int32, (tm, 1), 0)
    n_pool = u.shape[1]
    gd = n_pool // len(POOL_WINDOWS)
    acc = x_ref[...] + _dot(y_ref[...], wo_ref[n_pool:, :])
    for gi, w in enumerate(POOL_WINDOWS):
        cols = slice(gi * gd, (gi + 1) * gd)
        band = jnp.where((tt - ss >= 0) & (tt - ss < w), 1.0, 0.0).astype(bf16)
        band_h = jnp.where(th + POOL_HALO - rh < w, 1.0, 0.0).astype(bf16)
        ug = u[:, cols]
        win_sum = _dot(band, ug) + keep_halo * _dot(band_h, halo[:, cols])
        count = jnp.minimum(tseq + 1, w).astype(f32)
        pooled = win_sum / count - ug.astype(f32)
        yg = _dot(pooled.astype(bf16), pw_ref[gi]) * ps_ref[:, cols]
        acc = acc + _dot(yg.astype(bf16), wo_ref[cols, :])
    o_ref[...] = acc


def _out_proj(x2, u, y_nsa, pool_w, pool_scale, w_out, cast_a, cast_b, *, T, tm):
    n_rows, d = x2.shape
    n_pool = u.shape[1]
    hb = tm // POOL_HALO
    n_steps = n_rows // tm
    row = lambda i: (i, 0)
    cast_spec = lambda w: pl.BlockSpec((w.shape[0] // n_steps, w.shape[1]), row)
    const2 = lambda i: (0, 0)
    return pl.pallas_call(
        functools.partial(_out_proj_kernel, tm=tm, T=T),
        grid=(n_rows // tm,),
        in_specs=[
            pl.BlockSpec((tm, d), row),
            pl.BlockSpec((tm, n_pool), row),
            pl.BlockSpec((POOL_HALO, n_pool), lambda i: (jnp.maximum(i * hb - 1, 0), 0)),
            pl.BlockSpec((tm, y_nsa.shape[1]), row),
            pl.BlockSpec(pool_w.shape, lambda i: (0, 0, 0)),
            pl.BlockSpec((1, n_pool), const2),
            pl.BlockSpec(w_out.shape, const2, pipeline_mode=pl.Buffered(1)),
            cast_spec(cast_a), cast_spec(cast_b),
        ],
        out_specs=[pl.BlockSpec((tm, d), row), cast_spec(cast_a), cast_spec(cast_b)],
        out_shape=[jax.ShapeDtypeStruct((n_rows, d), f32), jax.ShapeDtypeStruct(cast_a.shape, bf16),
                   jax.ShapeDtypeStruct(cast_b.shape, bf16)],
        compiler_params=pltpu.CompilerParams(
            dimension_semantics=("arbitrary",), vmem_limit_bytes=VMEM_LIMIT),
        name="out_proj",
    )(x2, u, u, y_nsa, pool_w, pool_scale, w_out, cast_a, cast_b)


def _ffn_kernel(h_ref, g_ref, wg_ref, wu_ref, wd_ref, o_ref, n_sc):
    @pl.when(pl.program_id(1) == 0)
    def _():
        h = h_ref[...]
        n_sc[...] = _rms(h, g_ref[...]).astype(bf16)
        o_ref[...] = h

    n = n_sc[...]
    gate = _dot(n, wg_ref[...])
    up = _dot(n, wu_ref[...])
    a = gate * jax.nn.sigmoid(gate) * up
    o_ref[...] += _dot(a.astype(bf16), wd_ref[...])


def _ffn(h, g, w_gate, w_up, w_down, *, tm, tf):
    n_rows, d = h.shape
    d_ff = w_gate.shape[1]
    return pl.pallas_call(
        _ffn_kernel,
        grid=(n_rows // tm, d_ff // tf),
        in_specs=[
            pl.BlockSpec((tm, d), lambda i, f: (i, 0)),
            pl.BlockSpec((1, d), lambda i, f: (0, 0)),
            pl.BlockSpec((d, tf), lambda i, f: (0, f)),
            pl.BlockSpec((d, tf), lambda i, f: (0, f)),
            pl.BlockSpec((tf, d), lambda i, f: (f, 0)),
        ],
        out_specs=pl.BlockSpec((tm, d), lambda i, f: (i, 0)),
        out_shape=jax.ShapeDtypeStruct((n_rows, d), f32),
        scratch_shapes=[pltpu.VMEM((tm, d), bf16)],
        compiler_params=pltpu.CompilerParams(
            dimension_semantics=("arbitrary", "arbitrary"), vmem_limit_bytes=VMEM_LIMIT),
        name="ffn",
    )(h, g, w_gate, w_up, w_down)


def _ple_kernel(h_ref, p_ref, g_ref, wg_ref, wp_ref, fg_ref, o_ref):
    h = h_ref[...]
    n = _rms(h, g_ref[...]).astype(bf16)
    gate = jax.nn.sigmoid(_dot(n, wg_ref[...]))
    h = h + _dot(p_ref[...].astype(bf16), wp_ref[...]) * gate
    o_ref[...] = _rms(h, fg_ref[...])


def _ple(h, p2, g, w_gate, w_proj, fg, *, tm):
    n_rows, d = h.shape
    row = lambda i: (i, 0)
    const = lambda i: (0, 0)
    return pl.pallas_call(
        _ple_kernel,
        grid=(n_rows // tm,),
        in_specs=[
            pl.BlockSpec((tm, d), row),
            pl.BlockSpec((tm, p2.shape[1]), row),
            pl.BlockSpec((1, d), const),
            pl.BlockSpec(w_gate.shape, const, pipeline_mode=pl.Buffered(1)),
            pl.BlockSpec(w_proj.shape, const),
            pl.BlockSpec((1, d), const),
        ],
        out_specs=pl.BlockSpec((tm, d), row),
        out_shape=jax.ShapeDtypeStruct((n_rows, d), f32),
        compiler_params=pltpu.CompilerParams(
            dimension_semantics=("arbitrary",), vmem_limit_bytes=VMEM_LIMIT),
        name="ple",
    )(h, p2, g, w_gate, w_proj, fg)


def _rope_tables(T):
    pos = jnp.arange(T, dtype=f32)
    inv_freq = ROPE_THETA ** (-jnp.arange(0, ROPE_DIM, 2, dtype=f32) / ROPE_DIM)
    ang = pos[:, None] * inv_freq[None, :]
    cos, sin = jnp.cos(ang), jnp.sin(ang)
    rest = HEAD_DIM - ROPE_DIM
    cosf = jnp.concatenate([cos, cos, jnp.ones((T, rest), f32)], axis=1)
    sinf = jnp.concatenate([-sin, sin, jnp.zeros((T, rest), f32)], axis=1)
    return cosf, sinf


def _mixer_ffn(h2, in_norm_g, w_in, pool_w, pool_scale, cmp_k_pe, cmp_k_w1, cmp_k_w2,
               cmp_v_pe, cmp_v_w1, cmp_v_w2, w_out, ffn_norm_g, w_gate, w_up, w_down, w_ple_gate, *, B, T):
    n_pool = pool_scale.shape[0]
    n_heads = N_KV_GROUPS * HEADS_PER_GROUP
    nsa_w = n_heads * HEAD_DIM
    kv_w = N_KV_GROUPS * HEAD_DIM
    assert w_in.shape[1] == n_pool + nsa_w + 6 * kv_w + n_heads * N_BRANCH and n_pool == 1024 and nsa_w == 1024

    o0 = n_pool + nsa_w + 6 * kv_w
    w_p = w_in.astype(bf16)
    w_gt = jnp.pad(w_p[:, o0:], ((0, 0), (0, LANES - n_heads * N_BRANCH)))

    cosf, sinf = _rope_tables(T)
    u, q, kcvc, kv, gates, w_out_b, w_gate_b, w_ple_gate_b = _in_proj(
        h2, in_norm_g[None, :], w_p, w_gt, cosf, sinf, w_out, w_gate, w_ple_gate, T=T, tm=512)

    kcmp, vcmp = _compress(
        kcvc.reshape(B, T, kcvc.shape[1]),
        cmp_k_pe.reshape(1, -1), cmp_k_w1.astype(bf16), cmp_k_w2.astype(bf16),
        cmp_v_pe.reshape(1, -1), cmp_v_w1.astype(bf16), cmp_v_w2.astype(bf16))

    key_blk = jnp.arange(T, dtype=jnp.int32)[:, None] // SEL_BLOCK
    eneg = jnp.where(key_blk == jnp.arange(LANES, dtype=jnp.int32)[None, :], NEG, 0.0).astype(bf16)
    y_nsa = _nsa(q, kcmp, vcmp, kv, eneg, gates, B=B, T=T, tq=256, n_sub=2)

    h2, w_up_b, w_down_b = _out_proj(h2, u, y_nsa, pool_w.astype(bf16), pool_scale[None, :], w_out_b,
                                     w_up, w_down, T=T, tm=512)
    return _ffn(h2, ffn_norm_g[None, :], w_gate_b, w_up_b, w_down_b, tm=1024, tf=512), w_ple_gate_b


def kernel(x, p, in_norm_g, w_in, pool_w, pool_scale, cmp_k_pe, cmp_k_w1, cmp_k_w2, cmp_v_pe, cmp_v_w1,
           cmp_v_w2, w_out, ffn_norm_g, w_gate, w_up, w_down, ple_norm_g, w_ple_gate, w_ple_proj, final_norm_g):
    B, T, d = x.shape
    depth = w_in.shape[0]
    assert depth == 1, "the final rmsnorm is fused into the last layer's per-layer-embedding kernel"
    h2 = x.reshape(B * T, d)
    i = 0
    h2, w_ple_gate_b = _mixer_ffn(
        h2, in_norm_g[i], w_in[i], pool_w[i], pool_scale[i], cmp_k_pe[i], cmp_k_w1[i], cmp_k_w2[i],
        cmp_v_pe[i], cmp_v_w1[i], cmp_v_w2[i], w_out[i], ffn_norm_g[i], w_gate[i], w_up[i], w_down[i],
        w_ple_gate[i], B=B, T=T)
    out = _ple(h2, p[i].reshape(B * T, -1), ple_norm_g[i][None, :], w_ple_gate_b,
               w_ple_proj[i].astype(bf16), final_norm_g[None, :], tm=512)
    return out.reshape(B, T, d)
```

```python
import functools

import jax
import jax.numpy as jnp
from jax import lax
from jax.experimental import pallas as pl
from jax.experimental.pallas import tpu as pltpu

EPS = 1e-6
NEG = -1e30
LANES = 128
POOL_WINDOWS = (2, 4, 8, 16)
POOL_HALO = 16
HEAD_DIM = 128
N_KV_GROUPS = 2
HEADS_PER_GROUP = 4
N_BRANCH = 3
CMP_BLOCK = 32
CMP_STRIDE = 16
SEL_BLOCK = 64
N_SELECT = 16
WINDOW = 512
ROPE_THETA = 500000.0
ROPE_DIM = HEAD_DIM // 4
ATTN_SCALE = HEAD_DIM ** -0.5
Q_SCALE = ATTN_SCALE * 1.4426950408889634
VMEM_LIMIT = 60 * 1024 * 1024

f32 = jnp.float32
bf16 = jnp.bfloat16


def _dot(a, b):
    return jnp.dot(a, b, preferred_element_type=f32)


def _dot_nt(a, b):
    return lax.dot_general(a, b, (((1,), (1,)), ((), ())), preferred_element_type=f32)


def _rms(x, g):
    return x * lax.rsqrt(jnp.mean(x * x, axis=-1, keepdims=True) + EPS) * g


def _rope(h, cosf, sinf, lane):
    half = ROPE_DIM // 2
    partner = jnp.where(lane < half, pltpu.roll(h, LANES - half, axis=1), pltpu.roll(h, half, axis=1))
    return h * cosf + partner * sinf


def _in_proj_kernel(x_ref, g_ref, w_ref, wgt_ref, cos_ref, sin_ref, ca_ref, cb_ref, cc_ref,
                    u_ref, q_ref, kcvc_ref, kv_ref, gt_ref, ca_out, cb_out, cc_out):
    ca_out[...] = ca_ref[...].astype(bf16)
    cb_out[...] = cb_ref[...].astype(bf16)
    cc_out[...] = cc_ref[...].astype(bf16)
    a = _rms(x_ref[...], g_ref[...]).astype(bf16)
    cosf = cos_ref[...]
    sinf = sin_ref[...]
    lane = lax.broadcasted_iota(jnp.int32, cosf.shape, 1)
    nu = u_ref.shape[1]
    nq = q_ref.shape[1]
    off = 0
    for c in range(0, nu, 512):
        u_ref[:, c:c + 512] = _dot(a, w_ref[:, off + c:off + c + 512]).astype(bf16)
    off += nu
    for c in range(0, nq, 512):
        acc = _dot(a, w_ref[:, off + c:off + c + 512])
        for j in range(4):
            h = _rope(acc[:, j * 128:(j + 1) * 128], cosf, sinf, lane) * Q_SCALE
            q_ref[:, c + j * 128:c + (j + 1) * 128] = h.astype(bf16)
    off += nq
    acc = _dot(a, w_ref[:, off:off + 512])
    for j in range(4):
        h = acc[:, j * 128:(j + 1) * 128]
        if j < 2:
            h = _rope(h, cosf, sinf, lane)
        kcvc_ref[:, j * 128:(j + 1) * 128] = h
    off += 512
    acc = _dot(a, w_ref[:, off:off + 512])
    for j in range(4):
        h = acc[:, j * 128:(j + 1) * 128]
        if j < 2:
            h = _rope(h, cosf, sinf, lane)
        kv_ref[:, j * 128:(j + 1) * 128] = h.astype(bf16)
    off += 512
    acc = _dot(a, w_ref[:, off:off + 512])
    for j in range(4):
        h = acc[:, j * 128:(j + 1) * 128]
        if j < 2:
            h = _rope(h, cosf, sinf, lane)
        kv_ref[:, 512 + j * 128:512 + (j + 1) * 128] = h.astype(bf16)
    off += 512
    gt_ref[...] = _dot(a, wgt_ref[...])


def _in_proj(x2, g, w_p, w_gt, cosf, sinf, cast_a, cast_b, cast_c, *, T, tm):
    n_rows, d = x2.shape
    n_w = w_p.shape[1]
    tiles_per_seq = T // tm
    n_steps = n_rows // tm
    row = lambda i: (i, 0)
    cast_spec = lambda w: pl.BlockSpec((w.shape[0] // n_steps, w.shape[1]), row)
    const = lambda i: (0, 0)
    tab = lambda i: (i % tiles_per_seq, 0)
    return pl.pallas_call(
        _in_proj_kernel,
        grid=(n_rows // tm,),
        in_specs=[
            pl.BlockSpec((tm, d), row),
            pl.BlockSpec((1, d), const),
            pl.BlockSpec((d, n_w), const, pipeline_mode=pl.Buffered(1)),
            pl.BlockSpec((d, LANES), const),
            pl.BlockSpec((tm, LANES), tab),
            pl.BlockSpec((tm, LANES), tab),
            cast_spec(cast_a), cast_spec(cast_b), cast_spec(cast_c),
        ],
        out_specs=[
            pl.BlockSpec((tm, 1024), row),
            pl.BlockSpec((tm, 1024), row),
            pl.BlockSpec((tm, 512), row),
            pl.BlockSpec((tm, 1024), row),
            pl.BlockSpec((tm, LANES), row),
            cast_spec(cast_a), cast_spec(cast_b), cast_spec(cast_c),
        ],
        out_shape=[
            jax.ShapeDtypeStruct((n_rows, 1024), bf16),
            jax.ShapeDtypeStruct((n_rows, 1024), bf16),
            jax.ShapeDtypeStruct((n_rows, 512), f32),
            jax.ShapeDtypeStruct((n_rows, 1024), bf16),
            jax.ShapeDtypeStruct((n_rows, LANES), f32),
            jax.ShapeDtypeStruct(cast_a.shape, bf16),
            jax.ShapeDtypeStruct(cast_b.shape, bf16),
            jax.ShapeDtypeStruct(cast_c.shape, bf16),
        ],
        compiler_params=pltpu.CompilerParams(
            dimension_semantics=("arbitrary",), vmem_limit_bytes=VMEM_LIMIT),
        name="in_proj",
    )(x2, g, w_p, w_gt, cosf, sinf, cast_a, cast_b, cast_c)


def _gelu_tanh(x):
    return 0.5 * x * (1.0 + jnp.tanh(0.7978845608028654 * (x + 0.044715 * (x * x * x))))


def _compress_kernel(kc0_ref, kc1_ref, vc0_ref, vc1_ref, pek_ref, w1k_ref, w2k_ref, pev_ref, w1v_ref, w2v_ref,
                     kc_ref, vc_ref):
    n_rows = kc0_ref.shape[1] // CMP_STRIDE
    half_k = (CMP_BLOCK // 2) * HEAD_DIM
    rowi = lax.broadcasted_iota(jnp.int32, (n_rows, HEAD_DIM), 0)
    for srcs, pe_ref, w1_ref, w2_ref, o_ref in (((kc0_ref, kc1_ref), pek_ref, w1k_ref, w2k_ref, kc_ref),
                                                ((vc0_ref, vc1_ref), pev_ref, w1v_ref, w2v_ref, vc_ref)):
        pe_a = pe_ref[:, :half_k]
        pe_b = pe_ref[:, half_k:]
        for g, src in enumerate(srcs):
            x = jnp.concatenate(
                [src[0, pl.ds(l, n_rows, stride=CMP_STRIDE), :] for l in range(CMP_STRIDE)], axis=1)
            za = _dot((x + pe_a).astype(bf16), w1_ref[:half_k, :])
            zb = _dot((x + pe_b).astype(bf16), w1_ref[half_k:, :])
            h = za + pltpu.roll(zb, n_rows - 1, axis=0)
            o = _dot(_gelu_tanh(h).astype(bf16), w2_ref[...])
            o_ref[0, g] = jnp.where(rowi < n_rows - 1, o, 0.0).astype(bf16)


def _compress(r3, pek, w1k, w2k, pev, w1v, w2v):
    B, T, width = r3.shape
    assert width == 2 * N_KV_GROUPS * HEAD_DIM and N_KV_GROUPS == 2
    n_rows = T // CMP_STRIDE
    const = lambda b: (0, 0)
    out = jax.ShapeDtypeStruct((B, N_KV_GROUPS, n_rows, HEAD_DIM), bf16)
    ospec = pl.BlockSpec((1, N_KV_GROUPS, n_rows, HEAD_DIM), lambda b: (b, 0, 0, 0))
    sec = lambda s: pl.BlockSpec((1, T, HEAD_DIM), lambda b: (b, 0, s))
    return pl.pallas_call(
        _compress_kernel,
        grid=(B,),
        in_specs=[
            sec(0), sec(1), sec(2), sec(3),
            pl.BlockSpec(pek.shape, const), pl.BlockSpec(w1k.shape, const), pl.BlockSpec(w2k.shape, const),
            pl.BlockSpec(pev.shape, const), pl.BlockSpec(w1v.shape, const), pl.BlockSpec(w2v.shape, const),
        ],
        out_specs=[ospec, ospec],
        out_shape=[out, out],
        compiler_params=pltpu.CompilerParams(
            dimension_semantics=("arbitrary",), vmem_limit_bytes=VMEM_LIMIT),
        name="compress",
    )(r3, r3, r3, r3, pek, w1k, w2k, pev, w1v, w2v)


def _nsa_front(qs, t0, kc, vc, kw_ref, vw_ref, *, tq, T):
    R = HEADS_PER_GROUP
    M = R * tq
    n_cmp = kc.shape[0]
    n_sel = T // SEL_BLOCK

    wk = WINDOW + tq
    w0 = pl.multiple_of(jnp.maximum(t0 - WINDOW, 0), tq)
    sw = _dot_nt(qs, kw_ref[pl.ds(w0, wk), :])
    diff = (t0 - w0) + (lax.broadcasted_iota(jnp.int32, (M, wk), 0) & (tq - 1)) \
        - lax.broadcasted_iota(jnp.int32, (M, wk), 1)
    sw = jnp.where((diff & -WINDOW) == 0, sw, NEG)
    ew = jnp.exp2(sw - jnp.max(sw, axis=1, keepdims=True))
    o_win = _dot(ew.astype(bf16), vw_ref[pl.ds(w0, wk), :]) / jnp.sum(ew, axis=1, keepdims=True)

    s = _dot_nt(qs, kc)
    trow = t0 + (lax.broadcasted_iota(jnp.int32, (M, n_cmp), 0) & (tq - 1))
    ncol = lax.broadcasted_iota(jnp.int32, (M, n_cmp), 1)
    valid = (CMP_STRIDE * ncol + (CMP_BLOCK - 1)) <= trow
    s = jnp.where(valid, s, NEG)
    e = jnp.exp2(s - jnp.max(s, axis=1, keepdims=True))
    p = jnp.where(valid, e / jnp.sum(e, axis=1, keepdims=True), 0.0)
    o_cmp = _dot(p.astype(bf16), vc)

    psum = p[0:tq]
    for r in range(1, R):
        psum = psum + p[r * tq:(r + 1) * tq]
    p_hi = psum.astype(bf16)
    p_lo = (psum - p_hi.astype(f32)).astype(bf16)
    sb = lax.broadcasted_iota(jnp.int32, (n_sel, n_cmp), 0) * SEL_BLOCK
    cb = lax.broadcasted_iota(jnp.int32, (n_sel, n_cmp), 1) * CMP_STRIDE
    ov_t = jnp.where((cb < sb + SEL_BLOCK) & (cb + CMP_BLOCK > sb), 1.0, 0.0).astype(bf16)
    imp_t = _dot_nt(ov_t, p_hi) + _dot_nt(ov_t, p_lo)

    jj = lax.broadcasted_iota(jnp.int32, (n_sel, tq), 0)
    cur = jnp.right_shift(t0 + lax.broadcasted_iota(jnp.int32, (n_sel, tq), 1), SEL_BLOCK.bit_length() - 1)
    forced = (jj == 0) | (jj == cur) | (jj == cur - 1)
    v = jnp.where(jj > cur, -jnp.inf, jnp.where(forced, jnp.inf, imp_t))
    sub = 8
    slabs = [v[k * sub:(k + 1) * sub] for k in range(n_sel // sub)]
    ranks = [jnp.zeros((sub, tq), f32) for _ in slabs]
    j_in = lax.broadcasted_iota(jnp.int32, (sub, tq), 0)
    for c in range(n_sel):
        v_c = v[c:c + 1, :]
        for k, v_k in enumerate(slabs):
            if k * sub > c:
                ahead = jnp.where(v_c >= v_k, 1.0, 0.0)
            elif (k + 1) * sub <= c:
                ahead = jnp.where(v_c > v_k, 1.0, 0.0)
            else:
                ahead = jnp.where(j_in > c - k * sub, jnp.where(v_c >= v_k, 1.0, 0.0),
                                  jnp.where(v_c > v_k, 1.0, 0.0))
            ranks[k] = ranks[k] + ahead
    rank = jnp.concatenate(ranks, axis=0)
    unsel_t = jnp.where(rank < float(min(N_SELECT, n_sel)), 0.0, 1.0)
    unsel_t = jnp.concatenate([unsel_t, jnp.zeros((LANES - n_sel, tq), f32)], axis=0)
    return o_win, o_cmp, unsel_t.T.astype(bf16)


def _nsa_kernel(q_ref, kc_ref, vc_ref, ks_ref, vs_ref, kw_ref, vw_ref, eneg_ref, gt_ref, o_ref,
                s_sc, mx_sc, ls_sc, acc_sc, *, tq, n_sub, T):
    R = HEADS_PER_GROUP
    M = R * tq
    tk = n_sub * tq
    i = pl.program_id(2)
    t0 = i * tk
    kc = kc_ref[0, 0]
    vc = vc_ref[0, 0]

    fronts = []
    q_rows = []
    for sb in range(n_sub):
        q = q_ref[sb * tq:(sb + 1) * tq, :]
        qs = jnp.concatenate([q[:, r * HEAD_DIM:(r + 1) * HEAD_DIM] for r in range(R)], axis=0)
        o_win, o_cmp, unsel = _nsa_front(qs, t0 + sb * tq, kc, vc, kw_ref, vw_ref, tq=tq, T=T)
        fronts.append((o_win, o_cmp))
        q_rows.append(jnp.concatenate([qs, jnp.concatenate([unsel] * R, axis=0)], axis=1))
    q_aug = jnp.concatenate(q_rows, axis=0)
    MM = n_sub * M

    n_chunk = tk // LANES
    mx_sc[...] = jnp.full(mx_sc.shape, NEG, f32)

    def score_step(kt, causal):
        k0 = pl.multiple_of(kt * tk, tk)
        k_aug = jnp.concatenate([ks_ref[pl.ds(k0, tk), :], eneg_ref[pl.ds(k0, tk), :]], axis=1)
        sc = _dot_nt(q_aug, k_aug)
        if causal:
            rowi = lax.broadcasted_iota(jnp.int32, (MM, tk), 0)
            qoff = jnp.right_shift(rowi, M.bit_length() - 1) * tq + (rowi & (tq - 1))
            sc = jnp.where(qoff >= lax.broadcasted_iota(jnp.int32, (MM, tk), 1), sc, NEG)
        s_sc[kt] = sc
        part = sc[:, :LANES]
        for c in range(1, n_chunk):
            part = jnp.maximum(part, sc[:, c * LANES:(c + 1) * LANES])
        mx_sc[...] = jnp.maximum(mx_sc[...], part)

    def score_body(kt, carry):
        score_step(kt, False)
        return carry

    lax.fori_loop(0, i, score_body, 0)
    score_step(i, True)
    mx_sc[...] = jnp.broadcast_to(jnp.max(mx_sc[...], axis=1, keepdims=True), mx_sc.shape)
    ls_sc[...] = jnp.zeros(ls_sc.shape, f32)
    acc_sc[...] = jnp.zeros(acc_sc.shape, f32)

    def pv_body(kt, carry):
        k0 = pl.multiple_of(kt * tk, tk)
        sc = s_sc[kt]
        mb = mx_sc[...]
        ps = [jnp.exp2(sc[:, c * LANES:(c + 1) * LANES] - mb) for c in range(n_chunk)]
        ls_sc[...] += functools.reduce(lambda a, b: a + b, ps)
        acc_sc[...] += _dot(jnp.concatenate(ps, axis=1).astype(bf16), vs_ref[pl.ds(k0, tk), :])
        return carry

    lax.fori_loop(0, i + 1, pv_body, 0)
    o_sel = acc_sc[...] / jnp.sum(ls_sc[...], axis=1, keepdims=True)

    gt_all = jax.nn.sigmoid(gt_ref[...])
    gt_all = jnp.where(pl.program_id(1) == 0, gt_all, pltpu.roll(gt_all, LANES - R * N_BRANCH, axis=1))
    for sb, (o_win, o_cmp) in enumerate(fronts):
        gt = gt_all[sb * tq:(sb + 1) * tq]
        for r in range(R):
            rows = slice(r * tq, (r + 1) * tq)
            srows = slice(sb * M + r * tq, sb * M + (r + 1) * tq)
            c = r * N_BRANCH
            o = (gt[:, c:c + 1] * o_cmp[rows] + gt[:, c + 1:c + 2] * o_sel[srows]
                 + gt[:, c + 2:c + 3] * o_win[rows])
            o_ref[sb * tq:(sb + 1) * tq, r * HEAD_DIM:(r + 1) * HEAD_DIM] = o.astype(bf16)


def _nsa(q, kcmp, vcmp, kv, eneg, gates, *, B, T, tq, n_sub):
    G, R = N_KV_GROUPS, HEADS_PER_GROUP
    tk = n_sub * tq
    nt = T // tk
    n_cmp = kcmp.shape[2]
    assert tq % LANES == 0 and tq & (tq - 1) == 0 and T % tk == 0 and T >= WINDOW + tq
    assert WINDOW % tq == 0 and WINDOW & (WINDOW - 1) == 0 and (R * tq) & (R * tq - 1) == 0
    rowblk = lambda b, g, i: (b * nt + i, g)
    cmp_spec = pl.BlockSpec((1, 1, n_cmp, HEAD_DIM), lambda b, g, i: (b, g, 0, 0))
    kvspec = lambda sec: pl.BlockSpec((T, HEAD_DIM), lambda b, g, i: (b, sec * G + g))
    return pl.pallas_call(
        functools.partial(_nsa_kernel, tq=tq, n_sub=n_sub, T=T),
        grid=(B, G, nt),
        in_specs=[
            pl.BlockSpec((tk, R * HEAD_DIM), rowblk),
            cmp_spec, cmp_spec,
            kvspec(0), kvspec(1), kvspec(2), kvspec(3),
            pl.BlockSpec((T, LANES), lambda b, g, i: (0, 0)),
            pl.BlockSpec((tk, LANES), lambda b, g, i: (b * nt + i, 0)),
        ],
        out_specs=pl.BlockSpec((tk, R * HEAD_DIM), rowblk),
        out_shape=jax.ShapeDtypeStruct((B * T, G * R * HEAD_DIM), bf16),
        scratch_shapes=[
            pltpu.VMEM((T // tk, n_sub * R * tq, tk), f32),
            pltpu.VMEM((n_sub * R * tq, LANES), f32),
            pltpu.VMEM((n_sub * R * tq, LANES), f32),
            pltpu.VMEM((n_sub * R * tq, HEAD_DIM), f32),
        ],
        compiler_params=pltpu.CompilerParams(
            dimension_semantics=("arbitrary", "arbitrary", "arbitrary"), vmem_limit_bytes=VMEM_LIMIT),
        name="nsa",
    )(q, kcmp, vcmp, kv, kv, kv, kv, eneg, gates)


def _out_proj_kernel(x_ref, u_ref, uh_ref, y_ref, pw_ref, ps_ref, wo_ref, ca_ref, cb_ref,
                     o_ref, ca_out, cb_out, *, tm, T):
    ca_out[...] = ca_ref[...].astype(bf16)
    cb_out[...] = cb_ref[...].astype(bf16)
    i = pl.program_id(0)
    tiles_per_seq = T // tm
    keep_halo = jnp.where((i % tiles_per_seq) == 0, 0.0, 1.0)
    u = u_ref[...]
    halo = uh_ref[...]
    tt = lax.broadcasted_iota(jnp.int32, (tm, tm), 0)
    ss = lax.broadcasted_iota(jnp.int32, (tm, tm), 1)
    th = lax.broadcasted_iota(jnp.int32, (tm, POOL_HALO), 0)
    rh = lax.broadcasted_iota(jnp.int32, (tm, POOL_HALO), 1)
    tseq = (i % tiles_per_seq) * tm + lax.broadcasted_iota(jnp.int32, (tm, 1), 0)
    n_pool = u.shape[1]
    gd = n_pool // len(POOL_WINDOWS)
    acc = x_ref[...] + _dot(y_ref[...], wo_ref[n_pool:, :])
    for gi, w in enumerate(POOL_WINDOWS):
        cols = slice(gi * gd, (gi + 1) * gd)
        band = jnp.where((tt - ss >= 0) & (tt - ss < w), 1.0, 0.0).astype(bf16)
        band_h = jnp.where(th + POOL_HALO - rh < w, 1.0, 0.0).astype(bf16)
        ug = u[:, cols]
        win_sum = _dot(band, ug) + keep_halo * _dot(band_h, halo[:, cols])
        count = jnp.minimum(tseq + 1, w).astype(f32)
        pooled = win_sum / count - ug.astype(f32)
        yg = _dot(pooled.astype(bf16), pw_ref[gi]) * ps_ref[:, cols]
        acc = acc + _dot(yg.astype(bf16), wo_ref[cols, :])
    o_ref[...] = acc


def _out_proj(x2, u, y_nsa, pool_w, pool_scale, w_out, cast_a, cast_b, *, T, tm):
    n_rows, d = x2.shape
    n_pool = u.shape[1]
    hb = tm // POOL_HALO
    n_steps = n_rows // tm
    row = lambda i: (i, 0)
    cast_spec = lambda w: pl.BlockSpec((w.shape[0] // n_steps, w.shape[1]), row)
    const2 = lambda i: (0, 0)
    return pl.pallas_call(
        functools.partial(_out_proj_kernel, tm=tm, T=T),
        grid=(n_rows // tm,),
        in_specs=[
            pl.BlockSpec((tm, d), row),
            pl.BlockSpec((tm, n_pool), row),
            pl.BlockSpec((POOL_HALO, n_pool), lambda i: (jnp.maximum(i * hb - 1, 0), 0)),
            pl.BlockSpec((tm, y_nsa.shape[1]), row),
            pl.BlockSpec(pool_w.shape, lambda i: (0, 0, 0)),
            pl.BlockSpec((1, n_pool), const2),
            pl.BlockSpec(w_out.shape, const2, pipeline_mode=pl.Buffered(1)),
            cast_spec(cast_a), cast_spec(cast_b),
        ],
        out_specs=[pl.BlockSpec((tm, d), row), cast_spec(cast_a), cast_spec(cast_b)],
        out_shape=[jax.ShapeDtypeStruct((n_rows, d), f32), jax.ShapeDtypeStruct(cast_a.shape, bf16),
                   jax.ShapeDtypeStruct(cast_b.shape, bf16)],
        compiler_params=pltpu.CompilerParams(
            dimension_semantics=("arbitrary",), vmem_limit_bytes=VMEM_LIMIT),
        name="out_proj",
    )(x2, u, u, y_nsa, pool_w, pool_scale, w_out, cast_a, cast_b)


def _ffn_kernel(h_ref, g_ref, wg_ref, wu_ref, wd_ref, o_ref, n_sc):
    @pl.when(pl.program_id(1) == 0)
    def _():
        h = h_ref[...]
        n_sc[...] = _rms(h, g_ref[...]).astype(bf16)
        o_ref[...] = h

    n = n_sc[...]
    half = wg_ref.shape[1] // 2
    acc = None
    for c in range(2):
        cols = slice(c * half, (c + 1) * half)
        gate = _dot(n, wg_ref[:, cols])
        up = _dot(n, wu_ref[:, cols])
        a = gate * jax.nn.sigmoid(gate) * up
        d = _dot(a.astype(bf16), wd_ref[cols, :])
        acc = d if acc is None else acc + d
    o_ref[...] += acc


def _ffn(h, g, w_gate, w_up, w_down, *, tm, tf):
    n_rows, d = h.shape
    d_ff = w_gate.shape[1]
    return pl.pallas_call(
        _ffn_kernel,
        grid=(n_rows // tm, d_ff // tf),
        in_specs=[
            pl.BlockSpec((tm, d), lambda i, f: (i, 0)),
            pl.BlockSpec((1, d), lambda i, f: (0, 0)),
            pl.BlockSpec((d, tf), lambda i, f: (0, f)),
            pl.BlockSpec((d, tf), lambda i, f: (0, f)),
            pl.BlockSpec((tf, d), lambda i, f: (f, 0)),
        ],
        out_specs=pl.BlockSpec((tm, d), lambda i, f: (i, 0)),
        out_shape=jax.ShapeDtypeStruct((n_rows, d), f32),
        scratch_shapes=[pltpu.VMEM((tm, d), bf16)],
        compiler_params=pltpu.CompilerParams(
            dimension_semantics=("arbitrary", "arbitrary"), vmem_limit_bytes=VMEM_LIMIT),
        name="ffn",
    )(h, g, w_gate, w_up, w_down)


def _ple_kernel(h_ref, p_ref, g_ref, wg_ref, wp_ref, fg_ref, o_ref):
    h = h_ref[...]
    n = _rms(h, g_ref[...]).astype(bf16)
    gate = jax.nn.sigmoid(_dot(n, wg_ref[...]))
    h = h + _dot(p_ref[...].astype(bf16), wp_ref[...]) * gate
    o_ref[...] = _rms(h, fg_ref[...])


def _ple(h, p2, g, w_gate, w_proj, fg, *, tm):
    n_rows, d = h.shape
    row = lambda i: (i, 0)
    const = lambda i: (0, 0)
    return pl.pallas_call(
        _ple_kernel,
        grid=(n_rows // tm,),
        in_specs=[
            pl.BlockSpec((tm, d), row),
            pl.BlockSpec((tm, p2.shape[1]), row),
            pl.BlockSpec((1, d), const),
            pl.BlockSpec(w_gate.shape, const, pipeline_mode=pl.Buffered(1)),
            pl.BlockSpec(w_proj.shape, const),
            pl.BlockSpec((1, d), const),
        ],
        out_specs=pl.BlockSpec((tm, d), row),
        out_shape=jax.ShapeDtypeStruct((n_rows, d), f32),
        compiler_params=pltpu.CompilerParams(
            dimension_semantics=("arbitrary",), vmem_limit_bytes=VMEM_LIMIT),
        name="ple",
    )(h, p2, g, w_gate, w_proj, fg)


def _rope_tables(T):
    pos = jnp.arange(T, dtype=f32)
    inv_freq = ROPE_THETA ** (-jnp.arange(0, ROPE_DIM, 2, dtype=f32) / ROPE_DIM)
    ang = pos[:, None] * inv_freq[None, :]
    cos, sin = jnp.cos(ang), jnp.sin(ang)
    rest = HEAD_DIM - ROPE_DIM
    cosf = jnp.concatenate([cos, cos, jnp.ones((T, rest), f32)], axis=1)
    sinf = jnp.concatenate([-sin, sin, jnp.zeros((T, rest), f32)], axis=1)
    return cosf, sinf


def _mixer_ffn(h2, in_norm_g, w_in, pool_w, pool_scale, cmp_k_pe, cmp_k_w1, cmp_k_w2,
               cmp_v_pe, cmp_v_w1, cmp_v_w2, w_out, ffn_norm_g, w_gate, w_up, w_down, w_ple_gate, *, B, T):
    n_pool = pool_scale.shape[0]
    n_heads = N_KV_GROUPS * HEADS_PER_GROUP
    nsa_w = n_heads * HEAD_DIM
    kv_w = N_KV_GROUPS * HEAD_DIM
    assert w_in.shape[1] == n_pool + nsa_w + 6 * kv_w + n_heads * N_BRANCH and n_pool == 1024 and nsa_w == 1024

    o0 = n_pool + nsa_w + 6 * kv_w
    w_p = w_in.astype(bf16)
    w_gt = jnp.pad(w_p[:, o0:], ((0, 0), (0, LANES - n_heads * N_BRANCH)))

    cosf, sinf = _rope_tables(T)
    u, q, kcvc, kv, gates, w_out_b, w_gate_b, w_ple_gate_b = _in_proj(
        h2, in_norm_g[None, :], w_p, w_gt, cosf, sinf, w_out, w_gate, w_ple_gate, T=T, tm=512)

    kcmp, vcmp = _compress(
        kcvc.reshape(B, T, kcvc.shape[1]),
        cmp_k_pe.reshape(1, -1), cmp_k_w1.astype(bf16), cmp_k_w2.astype(bf16),
        cmp_v_pe.reshape(1, -1), cmp_v_w1.astype(bf16), cmp_v_w2.astype(bf16))

    key_blk = jnp.arange(T, dtype=jnp.int32)[:, None] // SEL_BLOCK
    eneg = jnp.where(key_blk == jnp.arange(LANES, dtype=jnp.int32)[None, :], NEG, 0.0).astype(bf16)
    y_nsa = _nsa(q, kcmp, vcmp, kv, eneg, gates, B=B, T=T, tq=256, n_sub=2)

    h2, w_up_b, w_down_b = _out_proj(h2, u, y_nsa, pool_w.astype(bf16), pool_scale[None, :], w_out_b,
                                     w_up, w_down, T=T, tm=512)
    return _ffn(h2, ffn_norm_g[None, :], w_gate_b, w_up_b, w_down_b, tm=1024, tf=512), w_ple_gate_b


def kernel(x, p, in_norm_g, w_in, pool_w, pool_scale, cmp_k_pe, cmp_k_w1, cmp_k_w2, cmp_v_pe, cmp_v_w1,
           cmp_v_w2, w_out, ffn_norm_g, w_gate, w_up, w_down, ple_norm_g, w_ple_gate, w_ple_proj, final_norm_g):
    B, T, d = x.shape
    depth = w_in.shape[0]
    assert depth == 1, "the final rmsnorm is fused into the last layer's per-layer-embedding kernel"
    h2 = x.reshape(B * T, d)
    i = 0
    h2, w_ple_gate_b = _mixer_ffn(
        h2, in_norm_g[i], w_in[i], pool_w[i], pool_scale[i], cmp_k_pe[i], cmp_k_w1[i], cmp_k_w2[i],
        cmp_v_pe[i], cmp_v_w1[i], cmp_v_w2[i], w_out[i], ffn_norm_g[i], w_gate[i], w_up[i], w_down[i],
        w_ple_gate[i], B=B, T=T)
    out = _ple(h2, p[i].reshape(B * T, -1), ple_norm_g[i][None, :], w_ple_gate_b,
               w_ple_proj[i].astype(bf16), final_norm_g[None, :], tm=512)
    return out.reshape(B, T, d)
```

```python
import functools

import jax
import jax.numpy as jnp
from jax import lax
from jax.experimental import pallas as pl
from jax.experimental.pallas import tpu as pltpu

EPS = 1e-6
NEG = -1e30
LANES = 128
POOL_WINDOWS = (2, 4, 8, 16)
POOL_HALO = 16
HEAD_DIM = 128
N_KV_GROUPS = 2
HEADS_PER_GROUP = 4
N_BRANCH = 3
CMP_BLOCK = 32
CMP_STRIDE = 16
SEL_BLOCK = 64
N_SELECT = 16
WINDOW = 512
ROPE_THETA = 500000.0
ROPE_DIM = HEAD_DIM // 4
ATTN_SCALE = HEAD_DIM ** -0.5
Q_SCALE = ATTN_SCALE * 1.4426950408889634
VMEM_LIMIT = 60 * 1024 * 1024
W_CHUNK = 256

f32 = jnp.float32
bf16 = jnp.bfloat16


def _dot(a, b):
    return jnp.dot(a, b, preferred_element_type=f32)


def _dot_nt(a, b):
    return lax.dot_general(a, b, (((1,), (1,)), ((), ())), preferred_element_type=f32)


def _rms(x, g):
    return x * lax.rsqrt(jnp.mean(x * x, axis=-1, keepdims=True) + EPS) * g


def _rope(h, cosf, sinf, lane):
    half = ROPE_DIM // 2
    partner = jnp.where(lane < half, pltpu.roll(h, LANES - half, axis=1), pltpu.roll(h, half, axis=1))
    return h * cosf + partner * sinf


def _w_chunk_copy(w_hbm, stage_sc, sem, c):
    return pltpu.make_async_copy(w_hbm.at[:, pl.ds(c * W_CHUNK, W_CHUNK)], stage_sc.at[c % 2], sem.at[c % 2])


def _in_proj_kernel(x_ref, g_ref, w_hbm, wgt_ref, cos_ref, sin_ref, ca_ref, cb_ref, cc_ref,
                    u_ref, q_ref, kcvc_ref, kv_ref, gt_ref, ca_out, cb_out, cc_out,
                    w_ref, stage_sc, sem):
    @pl.when(pl.program_id(0) == 0)
    def _():
        n_chunk = w_ref.shape[1] // W_CHUNK
        _w_chunk_copy(w_hbm, stage_sc, sem, 0).start()
        for c in range(n_chunk):
            if c + 1 < n_chunk:
                _w_chunk_copy(w_hbm, stage_sc, sem, c + 1).start()
            _w_chunk_copy(w_hbm, stage_sc, sem, c).wait()
            w_ref[:, c * W_CHUNK:(c + 1) * W_CHUNK] = stage_sc[c % 2].astype(bf16)

    ca_out[...] = ca_ref[...].astype(bf16)
    cb_out[...] = cb_ref[...].astype(bf16)
    cc_out[...] = cc_ref[...].astype(bf16)
    a = _rms(x_ref[...], g_ref[...]).astype(bf16)
    cosf = cos_ref[...]
    sinf = sin_ref[...]
    lane = lax.broadcasted_iota(jnp.int32, cosf.shape, 1)
    nu = u_ref.shape[1]
    nq = q_ref.shape[1]
    off = 0
    for c in range(0, nu, 512):
        u_ref[:, c:c + 512] = _dot(a, w_ref[:, off + c:off + c + 512]).astype(bf16)
    off += nu
    for c in range(0, nq, 512):
        acc = _dot(a, w_ref[:, off + c:off + c + 512])
        for j in range(4):
            h = _rope(acc[:, j * 128:(j + 1) * 128], cosf, sinf, lane) * Q_SCALE
            q_ref[:, c + j * 128:c + (j + 1) * 128] = h.astype(bf16)
    off += nq
    acc = _dot(a, w_ref[:, off:off + 512])
    for j in range(4):
        h = acc[:, j * 128:(j + 1) * 128]
        if j < 2:
            h = _rope(h, cosf, sinf, lane)
        kcvc_ref[:, j * 128:(j + 1) * 128] = h
    off += 512
    acc = _dot(a, w_ref[:, off:off + 512])
    for j in range(4):
        h = acc[:, j * 128:(j + 1) * 128]
        if j < 2:
            h = _rope(h, cosf, sinf, lane)
        kv_ref[:, j * 128:(j + 1) * 128] = h.astype(bf16)
    off += 512
    acc = _dot(a, w_ref[:, off:off + 512])
    for j in range(4):
        h = acc[:, j * 128:(j + 1) * 128]
        if j < 2:
            h = _rope(h, cosf, sinf, lane)
        kv_ref[:, 512 + j * 128:512 + (j + 1) * 128] = h.astype(bf16)
    off += 512
    gt_ref[...] = _dot(a, wgt_ref[...])


def _in_proj(x2, g, w_p, w_gt, cosf, sinf, cast_a, cast_b, cast_c, *, T, tm):
    n_rows, d = x2.shape
    n_w = (w_p.shape[1] // W_CHUNK) * W_CHUNK
    assert w_p.dtype == f32 and n_w + w_gt.shape[1] >= w_p.shape[1]
    tiles_per_seq = T // tm
    n_steps = n_rows // tm
    row = lambda i: (i, 0)
    cast_spec = lambda w: pl.BlockSpec((w.shape[0] // n_steps, w.shape[1]), row)
    const = lambda i: (0, 0)
    tab = lambda i: (i % tiles_per_seq, 0)
    return pl.pallas_call(
        _in_proj_kernel,
        grid=(n_rows // tm,),
        in_specs=[
            pl.BlockSpec((tm, d), row),
            pl.BlockSpec((1, d), const),
            pl.BlockSpec(memory_space=pl.ANY),
            pl.BlockSpec((d, LANES), const),
            pl.BlockSpec((tm, LANES), tab),
            pl.BlockSpec((tm, LANES), tab),
            cast_spec(cast_a), cast_spec(cast_b), cast_spec(cast_c),
        ],
        out_specs=[
            pl.BlockSpec((tm, 1024), row),
            pl.BlockSpec((tm, 1024), row),
            pl.BlockSpec((tm, 512), row),
            pl.BlockSpec((tm, 1024), row),
            pl.BlockSpec((tm, LANES), row),
            cast_spec(cast_a), cast_spec(cast_b), cast_spec(cast_c),
        ],
        out_shape=[
            jax.ShapeDtypeStruct((n_rows, 1024), bf16),
            jax.ShapeDtypeStruct((n_rows, 1024), bf16),
            jax.ShapeDtypeStruct((n_rows, 512), f32),
            jax.ShapeDtypeStruct((n_rows, 1024), bf16),
            jax.ShapeDtypeStruct((n_rows, LANES), f32),
            jax.ShapeDtypeStruct(cast_a.shape, bf16),
            jax.ShapeDtypeStruct(cast_b.shape, bf16),
            jax.ShapeDtypeStruct(cast_c.shape, bf16),
        ],
        scratch_shapes=[
            pltpu.VMEM((d, n_w), bf16),
            pltpu.VMEM((2, d, W_CHUNK), f32),
            pltpu.SemaphoreType.DMA((2,)),
        ],
        compiler_params=pltpu.CompilerParams(
            dimension_semantics=("arbitrary",), vmem_limit_bytes=VMEM_LIMIT),
        name="in_proj",
    )(x2, g, w_p, w_gt, cosf, sinf, cast_a, cast_b, cast_c)


def _gelu_tanh(x):
    return 0.5 * x * (1.0 + jnp.tanh(0.7978845608028654 * (x + 0.044715 * (x * x * x))))


def _compress_kernel(kc0_ref, kc1_ref, vc0_ref, vc1_ref, pek_ref, w1k_ref, w2k_ref, pev_ref, w1v_ref, w2v_ref,
                     kc_ref, vc_ref):
    n_rows = kc0_ref.shape[1] // CMP_STRIDE
    half_k = (CMP_BLOCK // 2) * HEAD_DIM
    rowi = lax.broadcasted_iota(jnp.int32, (n_rows, HEAD_DIM), 0)
    for srcs, pe_ref, w1_ref, w2_ref, o_ref in (((kc0_ref, kc1_ref), pek_ref, w1k_ref, w2k_ref, kc_ref),
                                                ((vc0_ref, vc1_ref), pev_ref, w1v_ref, w2v_ref, vc_ref)):
        pe_a = pe_ref[:, :half_k]
        pe_b = pe_ref[:, half_k:]
        for g, src in enumerate(srcs):
            x = jnp.concatenate(
                [src[0, pl.ds(l, n_rows, stride=CMP_STRIDE), :] for l in range(CMP_STRIDE)], axis=1)
            za = _dot((x + pe_a).astype(bf16), w1_ref[:half_k, :])
            zb = _dot((x + pe_b).astype(bf16), w1_ref[half_k:, :])
            h = za + pltpu.roll(zb, n_rows - 1, axis=0)
            o = _dot(_gelu_tanh(h).astype(bf16), w2_ref[...])
            o_ref[0, g] = jnp.where(rowi < n_rows - 1, o, 0.0).astype(bf16)


def _compress(r3, pek, w1k, w2k, pev, w1v, w2v):
    B, T, width = r3.shape
    assert width == 2 * N_KV_GROUPS * HEAD_DIM and N_KV_GROUPS == 2
    n_rows = T // CMP_STRIDE
    const = lambda b: (0, 0)
    out = jax.ShapeDtypeStruct((B, N_KV_GROUPS, n_rows, HEAD_DIM), bf16)
    ospec = pl.BlockSpec((1, N_KV_GROUPS, n_rows, HEAD_DIM), lambda b: (b, 0, 0, 0))
    sec = lambda s: pl.BlockSpec((1, T, HEAD_DIM), lambda b: (b, 0, s))
    return pl.pallas_call(
        _compress_kernel,
        grid=(B,),
        in_specs=[
            sec(0), sec(1), sec(2), sec(3),
            pl.BlockSpec(pek.shape, const), pl.BlockSpec(w1k.shape, const), pl.BlockSpec(w2k.shape, const),
            pl.BlockSpec(pev.shape, const), pl.BlockSpec(w1v.shape, const), pl.BlockSpec(w2v.shape, const),
        ],
        out_specs=[ospec, ospec],
        out_shape=[out, out],
        compiler_params=pltpu.CompilerParams(
            dimension_semantics=("arbitrary",), vmem_limit_bytes=VMEM_LIMIT),
        name="compress",
    )(r3, r3, r3, r3, pek, w1k, w2k, pev, w1v, w2v)


def _nsa_front(qs, t0, kc, vc, kw_ref, vw_ref, *, tq, T):
    R = HEADS_PER_GROUP
    M = R * tq
    n_cmp = kc.shape[0]
    n_sel = T // SEL_BLOCK

    wk = WINDOW + tq
    w0 = pl.multiple_of(jnp.maximum(t0 - WINDOW, 0), tq)
    sw = _dot_nt(qs, kw_ref[pl.ds(w0, wk), :])
    diff = (t0 - w0) + (lax.broadcasted_iota(jnp.int32, (M, wk), 0) & (tq - 1)) \
        - lax.broadcasted_iota(jnp.int32, (M, wk), 1)
    sw = jnp.where((diff & -WINDOW) == 0, sw, NEG)
    ew = jnp.exp2(sw - jnp.max(sw, axis=1, keepdims=True))
    o_win = _dot(ew.astype(bf16), vw_ref[pl.ds(w0, wk), :]) / jnp.sum(ew, axis=1, keepdims=True)

    s = _dot_nt(qs, kc)
    trow = t0 + (lax.broadcasted_iota(jnp.int32, (M, n_cmp), 0) & (tq - 1))
    ncol = lax.broadcasted_iota(jnp.int32, (M, n_cmp), 1)
    valid = (CMP_STRIDE * ncol + (CMP_BLOCK - 1)) <= trow
    s = jnp.where(valid, s, NEG)
    e = jnp.exp2(s - jnp.max(s, axis=1, keepdims=True))
    p = jnp.where(valid, e / jnp.sum(e, axis=1, keepdims=True), 0.0)
    o_cmp = _dot(p.astype(bf16), vc)

    psum = p[0:tq]
    for r in range(1, R):
        psum = psum + p[r * tq:(r + 1) * tq]
    p_hi = psum.astype(bf16)
    p_lo = (psum - p_hi.astype(f32)).astype(bf16)
    sb = lax.broadcasted_iota(jnp.int32, (n_sel, n_cmp), 0) * SEL_BLOCK
    cb = lax.broadcasted_iota(jnp.int32, (n_sel, n_cmp), 1) * CMP_STRIDE
    ov_t = jnp.where((cb < sb + SEL_BLOCK) & (cb + CMP_BLOCK > sb), 1.0, 0.0).astype(bf16)
    imp_t = _dot_nt(ov_t, p_hi) + _dot_nt(ov_t, p_lo)

    jj = lax.broadcasted_iota(jnp.int32, (n_sel, tq), 0)
    cur = jnp.right_shift(t0 + lax.broadcasted_iota(jnp.int32, (n_sel, tq), 1), SEL_BLOCK.bit_length() - 1)
    forced = (jj == 0) | (jj == cur) | (jj == cur - 1)
    v = jnp.where(jj > cur, -jnp.inf, jnp.where(forced, jnp.inf, imp_t))
    sub = 8
    slabs = [v[k * sub:(k + 1) * sub] for k in range(n_sel // sub)]
    ranks = [jnp.zeros((sub, tq), f32) for _ in slabs]
    j_in = lax.broadcasted_iota(jnp.int32, (sub, tq), 0)
    for c in range(n_sel):
        v_c = v[c:c + 1, :]
        for k, v_k in enumerate(slabs):
            if k * sub > c:
                ahead = jnp.where(v_c >= v_k, 1.0, 0.0)
            elif (k + 1) * sub <= c:
                ahead = jnp.where(v_c > v_k, 1.0, 0.0)
            else:
                ahead = jnp.where(j_in > c - k * sub, jnp.where(v_c >= v_k, 1.0, 0.0),
                                  jnp.where(v_c > v_k, 1.0, 0.0))
            ranks[k] = ranks[k] + ahead
    rank = jnp.concatenate(ranks, axis=0)
    unsel_t = jnp.where(rank < float(min(N_SELECT, n_sel)), 0.0, 1.0)
    unsel_t = jnp.concatenate([unsel_t, jnp.zeros((LANES - n_sel, tq), f32)], axis=0)
    return o_win, o_cmp, unsel_t.T.astype(bf16)


def _nsa_kernel(q_ref, kc_ref, vc_ref, ks_ref, vs_ref, kw_ref, vw_ref, eneg_ref, gt_ref, o_ref,
                s_sc, mx_sc, ls_sc, acc_sc, *, tq, n_sub, T):
    R = HEADS_PER_GROUP
    M = R * tq
    tk = n_sub * tq
    i = pl.program_id(2)
    t0 = i * tk
    kc = kc_ref[0, 0]
    vc = vc_ref[0, 0]

    fronts = []
    q_rows = []
    for sb in range(n_sub):
        q = q_ref[sb * tq:(sb + 1) * tq, :]
        qs = jnp.concatenate([q[:, r * HEAD_DIM:(r + 1) * HEAD_DIM] for r in range(R)], axis=0)
        o_win, o_cmp, unsel = _nsa_front(qs, t0 + sb * tq, kc, vc, kw_ref, vw_ref, tq=tq, T=T)
        fronts.append((o_win, o_cmp))
        q_rows.append(jnp.concatenate([qs, jnp.concatenate([unsel] * R, axis=0)], axis=1))
    q_aug = jnp.concatenate(q_rows, axis=0)
    MM = n_sub * M

    n_chunk = tk // LANES
    mx_sc[...] = jnp.full(mx_sc.shape, NEG, f32)

    def score_step(kt, causal):
        k0 = pl.multiple_of(kt * tk, tk)
        k_aug = jnp.concatenate([ks_ref[pl.ds(k0, tk), :], eneg_ref[pl.ds(k0, tk), :]], axis=1)
        sc = _dot_nt(q_aug, k_aug)
        if causal:
            rowi = lax.broadcasted_iota(jnp.int32, (MM, tk), 0)
            qoff = jnp.right_shift(rowi, M.bit_length() - 1) * tq + (rowi & (tq - 1))
            sc = jnp.where(qoff >= lax.broadcasted_iota(jnp.int32, (MM, tk), 1), sc, NEG)
        s_sc[kt] = sc
        part = sc[:, :LANES]
        for c in range(1, n_chunk):
            part = jnp.maximum(part, sc[:, c * LANES:(c + 1) * LANES])
        mx_sc[...] = jnp.maximum(mx_sc[...], part)

    def score_body(kt, carry):
        score_step(kt, False)
        return carry

    lax.fori_loop(0, i, score_body, 0)
    score_step(i, True)
    mx_sc[...] = jnp.broadcast_to(jnp.max(mx_sc[...], axis=1, keepdims=True), mx_sc.shape)
    ls_sc[...] = jnp.zeros(ls_sc.shape, f32)
    acc_sc[...] = jnp.zeros(acc_sc.shape, f32)

    def pv_body(kt, carry):
        k0 = pl.multiple_of(kt * tk, tk)
        sc = s_sc[kt]
        mb = mx_sc[...]
        ps = [jnp.exp2(sc[:, c * LANES:(c + 1) * LANES] - mb) for c in range(n_chunk)]
        ls_sc[...] += functools.reduce(lambda a, b: a + b, ps)
        acc_sc[...] += _dot(jnp.concatenate(ps, axis=1).astype(bf16), vs_ref[pl.ds(k0, tk), :])
        return carry

    lax.fori_loop(0, i + 1, pv_body, 0)
    o_sel = acc_sc[...] / jnp.sum(ls_sc[...], axis=1, keepdims=True)

    gt_all = jax.nn.sigmoid(gt_ref[...])
    gt_all = jnp.where(pl.program_id(1) == 0, gt_all, pltpu.roll(gt_all, LANES - R * N_BRANCH, axis=1))
    for sb, (o_win, o_cmp) in enumerate(fronts):
        gt = gt_all[sb * tq:(sb + 1) * tq]
        for r in range(R):
            rows = slice(r * tq, (r + 1) * tq)
            srows = slice(sb * M + r * tq, sb * M + (r + 1) * tq)
            c = r * N_BRANCH
            o = (gt[:, c:c + 1] * o_cmp[rows] + gt[:, c + 1:c + 2] * o_sel[srows]
                 + gt[:, c + 2:c + 3] * o_win[rows])
            o_ref[sb * tq:(sb + 1) * tq, r * HEAD_DIM:(r + 1) * HEAD_DIM] = o.astype(bf16)


def _nsa(q, kcmp, vcmp, kv, eneg, gates, *, B, T, tq, n_sub):
    G, R = N_KV_GROUPS, HEADS_PER_GROUP
    tk = n_sub * tq
    nt = T // tk
    n_cmp = kcmp.shape[2]
    assert tq % LANES == 0 and tq & (tq - 1) == 0 and T % tk == 0 and T >= WINDOW + tq
    assert WINDOW % tq == 0 and WINDOW & (WINDOW - 1) == 0 and (R * tq) & (R * tq - 1) == 0
    rowblk = lambda b, g, i: (b * nt + i, g)
    cmp_spec = pl.BlockSpec((1, 1, n_cmp, HEAD_DIM), lambda b, g, i: (b, g, 0, 0))
    kvspec = lambda sec: pl.BlockSpec((T, HEAD_DIM), lambda b, g, i: (b, sec * G + g))
    return pl.pallas_call(
        functools.partial(_nsa_kernel, tq=tq, n_sub=n_sub, T=T),
        grid=(B, G, nt),
        in_specs=[
            pl.BlockSpec((tk, R * HEAD_DIM), rowblk),
            cmp_spec, cmp_spec,
            kvspec(0), kvspec(1), kvspec(2), kvspec(3),
            pl.BlockSpec((T, LANES), lambda b, g, i: (0, 0)),
            pl.BlockSpec((tk, LANES), lambda b, g, i: (b * nt + i, 0)),
        ],
        out_specs=pl.BlockSpec((tk, R * HEAD_DIM), rowblk),
        out_shape=jax.ShapeDtypeStruct((B * T, G * R * HEAD_DIM), bf16),
        scratch_shapes=[
            pltpu.VMEM((T // tk, n_sub * R * tq, tk), f32),
            pltpu.VMEM((n_sub * R * tq, LANES), f32),
            pltpu.VMEM((n_sub * R * tq, LANES), f32),
            pltpu.VMEM((n_sub * R * tq, HEAD_DIM), f32),
        ],
        compiler_params=pltpu.CompilerParams(
            dimension_semantics=("arbitrary", "arbitrary", "arbitrary"), vmem_limit_bytes=VMEM_LIMIT),
        name="nsa",
    )(q, kcmp, vcmp, kv, kv, kv, kv, eneg, gates)


def _out_proj_kernel(x_ref, u_ref, uh_ref, y_ref, pw_ref, ps_ref, wo_ref, ca_ref, cb_ref,
                     o_ref, ca_out, cb_out, *, tm, T):
    ca_out[...] = ca_ref[...].astype(bf16)
    cb_out[...] = cb_ref[...].astype(bf16)
    i = pl.program_id(0)
    tiles_per_seq = T // tm
    keep_halo = jnp.where((i % tiles_per_seq) == 0, 0.0, 1.0)
    u = u_ref[...]
    halo = uh_ref[...]
    tt = lax.broadcasted_iota(jnp.int32, (tm, tm), 0)
    ss = lax.broadcasted_iota(jnp.int32, (tm, tm), 1)
    th = lax.broadcasted_iota(jnp.int32, (tm, POOL_HALO), 0)
    rh = lax.broadcasted_iota(jnp.int32, (tm, POOL_HALO), 1)
    tseq = (i % tiles_per_seq) * tm + lax.broadcasted_iota(jnp.int32, (tm, 1), 0)
    n_pool = u.shape[1]
    gd = n_pool // len(POOL_WINDOWS)
    acc = x_ref[...] + _dot(y_ref[...], wo_ref[n_pool:, :])
    for gi, w in enumerate(POOL_WINDOWS):
        cols = slice(gi * gd, (gi + 1) * gd)
        band = jnp.where((tt - ss >= 0) & (tt - ss < w), 1.0, 0.0).astype(bf16)
        band_h = jnp.where(th + POOL_HALO - rh < w, 1.0, 0.0).astype(bf16)
        ug = u[:, cols]
        win_sum = _dot(band, ug) + keep_halo * _dot(band_h, halo[:, cols])
        count = jnp.minimum(tseq + 1, w).astype(f32)
        pooled = win_sum / count - ug.astype(f32)
        yg = _dot(pooled.astype(bf16), pw_ref[gi]) * ps_ref[:, cols]
        acc = acc + _dot(yg.astype(bf16), wo_ref[cols, :])
    o_ref[...] = acc


def _out_proj(x2, u, y_nsa, pool_w, pool_scale, w_out, cast_a, cast_b, *, T, tm):
    n_rows, d = x2.shape
    n_pool = u.shape[1]
    hb = tm // POOL_HALO
    n_steps = n_rows // tm
    row = lambda i: (i, 0)
    cast_spec = lambda w: pl.BlockSpec((w.shape[0] // n_steps, w.shape[1]), row)
    const2 = lambda i: (0, 0)
    return pl.pallas_call(
        functools.partial(_out_proj_kernel, tm=tm, T=T),
        grid=(n_rows // tm,),
        in_specs=[
            pl.BlockSpec((tm, d), row),
            pl.BlockSpec((tm, n_pool), row),
            pl.BlockSpec((POOL_HALO, n_pool), lambda i: (jnp.maximum(i * hb - 1, 0), 0)),
            pl.BlockSpec((tm, y_nsa.shape[1]), row),
            pl.BlockSpec(pool_w.shape, lambda i: (0, 0, 0)),
            pl.BlockSpec((1, n_pool), const2),
            pl.BlockSpec(w_out.shape, const2, pipeline_mode=pl.Buffered(1)),
            cast_spec(cast_a), cast_spec(cast_b),
        ],
        out_specs=[pl.BlockSpec((tm, d), row), cast_spec(cast_a), cast_spec(cast_b)],
        out_shape=[jax.ShapeDtypeStruct((n_rows, d), f32), jax.ShapeDtypeStruct(cast_a.shape, bf16),
                   jax.ShapeDtypeStruct(cast_b.shape, bf16)],
        compiler_params=pltpu.CompilerParams(
            dimension_semantics=("arbitrary",), vmem_limit_bytes=VMEM_LIMIT),
        name="out_proj",
    )(x2, u, u, y_nsa, pool_w, pool_scale, w_out, cast_a, cast_b)


def _ffn_kernel(h_ref, g_ref, wg_ref, wu_ref, wd_ref, o_ref, n_sc):
    @pl.when(pl.program_id(1) == 0)
    def _():
        h = h_ref[...]
        n_sc[...] = _rms(h, g_ref[...]).astype(bf16)
        o_ref[...] = h

    n = n_sc[...]
    half = wg_ref.shape[1] // 2
    acc = None
    for c in range(2):
        cols = slice(c * half, (c + 1) * half)
        gate = _dot(n, wg_ref[:, cols])
        up = _dot(n, wu_ref[:, cols])
        a = gate * jax.nn.sigmoid(gate) * up
        d = _dot(a.astype(bf16), wd_ref[cols, :])
        acc = d if acc is None else acc + d
    o_ref[...] += acc


def _ffn(h, g, w_gate, w_up, w_down, *, tm, tf):
    n_rows, d = h.shape
    d_ff = w_gate.shape[1]
    return pl.pallas_call(
        _ffn_kernel,
        grid=(n_rows // tm, d_ff // tf),
        in_specs=[
            pl.BlockSpec((tm, d), lambda i, f: (i, 0)),
            pl.BlockSpec((1, d), lambda i, f: (0, 0)),
            pl.BlockSpec((d, tf), lambda i, f: (0, f)),
            pl.BlockSpec((d, tf), lambda i, f: (0, f)),
            pl.BlockSpec((tf, d), lambda i, f: (f, 0)),
        ],
        out_specs=pl.BlockSpec((tm, d), lambda i, f: (i, 0)),
        out_shape=jax.ShapeDtypeStruct((n_rows, d), f32),
        scratch_shapes=[pltpu.VMEM((tm, d), bf16)],
        compiler_params=pltpu.CompilerParams(
            dimension_semantics=("arbitrary", "arbitrary"), vmem_limit_bytes=VMEM_LIMIT),
        name="ffn",
    )(h, g, w_gate, w_up, w_down)


def _ple_kernel(h_ref, p_ref, g_ref, wg_ref, wp_ref, fg_ref, o_ref):
    h = h_ref[...]
    n = _rms(h, g_ref[...]).astype(bf16)
    gate = jax.nn.sigmoid(_dot(n, wg_ref[...]))
    h = h + _dot(p_ref[...].astype(bf16), wp_ref[...]) * gate
    o_ref[...] = _rms(h, fg_ref[...])


def _ple(h, p2, g, w_gate, w_proj, fg, *, tm):
    n_rows, d = h.shape
    row = lambda i: (i, 0)
    const = lambda i: (0, 0)
    return pl.pallas_call(
        _ple_kernel,
        grid=(n_rows // tm,),
        in_specs=[
            pl.BlockSpec((tm, d), row),
            pl.BlockSpec((tm, p2.shape[1]), row),
            pl.BlockSpec((1, d), const),
            pl.BlockSpec(w_gate.shape, const, pipeline_mode=pl.Buffered(1)),
            pl.BlockSpec(w_proj.shape, const),
            pl.BlockSpec((1, d), const),
        ],
        out_specs=pl.BlockSpec((tm, d), row),
        out_shape=jax.ShapeDtypeStruct((n_rows, d), f32),
        compiler_params=pltpu.CompilerParams(
            dimension_semantics=("arbitrary",), vmem_limit_bytes=VMEM_LIMIT),
        name="ple",
    )(h, p2, g, w_gate, w_proj, fg)


def _rope_tables(T):
    pos = jnp.arange(T, dtype=f32)
    inv_freq = ROPE_THETA ** (-jnp.arange(0, ROPE_DIM, 2, dtype=f32) / ROPE_DIM)
    ang = pos[:, None] * inv_freq[None, :]
    cos, sin = jnp.cos(ang), jnp.sin(ang)
    rest = HEAD_DIM - ROPE_DIM
    cosf = jnp.concatenate([cos, cos, jnp.ones((T, rest), f32)], axis=1)
    sinf = jnp.concatenate([-sin, sin, jnp.zeros((T, rest), f32)], axis=1)
    return cosf, sinf


def _mixer_ffn(h2, in_norm_g, w_in, pool_w, pool_scale, cmp_k_pe, cmp_k_w1, cmp_k_w2,
               cmp_v_pe, cmp_v_w1, cmp_v_w2, w_out, ffn_norm_g, w_gate, w_up, w_down, w_ple_gate, *, B, T):
    n_pool = pool_scale.shape[0]
    n_heads = N_KV_GROUPS * HEADS_PER_GROUP
    nsa_w = n_heads * HEAD_DIM
    kv_w = N_KV_GROUPS * HEAD_DIM
    assert w_in.shape[1] == n_pool + nsa_w + 6 * kv_w + n_heads * N_BRANCH and n_pool == 1024 and nsa_w == 1024

    o0 = n_pool + nsa_w + 6 * kv_w
    w_p = w_in
    w_gt = jnp.pad(w_in[:, o0:].astype(bf16), ((0, 0), (0, LANES - n_heads * N_BRANCH)))

    cosf, sinf = _rope_tables(T)
    u, q, kcvc, kv, gates, w_out_b, w_gate_b, w_ple_gate_b = _in_proj(
        h2, in_norm_g[None, :], w_p, w_gt, cosf, sinf, w_out, w_gate, w_ple_gate, T=T, tm=512)

    kcmp, vcmp = _compress(
        kcvc.reshape(B, T, kcvc.shape[1]),
        cmp_k_pe.reshape(1, -1), cmp_k_w1.astype(bf16), cmp_k_w2.astype(bf16),
        cmp_v_pe.reshape(1, -1), cmp_v_w1.astype(bf16), cmp_v_w2.astype(bf16))

    key_blk = jnp.arange(T, dtype=jnp.int32)[:, None] // SEL_BLOCK
    eneg = jnp.where(key_blk == jnp.arange(LANES, dtype=jnp.int32)[None, :], NEG, 0.0).astype(bf16)
    y_nsa = _nsa(q, kcmp, vcmp, kv, eneg, gates, B=B, T=T, tq=256, n_sub=2)

    h2, w_up_b, w_down_b = _out_proj(h2, u, y_nsa, pool_w.astype(bf16), pool_scale[None, :], w_out_b,
                                     w_up, w_down, T=T, tm=512)
    return _ffn(h2, ffn_norm_g[None, :], w_gate_b, w_up_b, w_down_b, tm=1024, tf=512), w_ple_gate_b


def kernel(x, p, in_norm_g, w_in, pool_w, pool_scale, cmp_k_pe, cmp_k_w1, cmp_k_w2, cmp_v_pe, cmp_v_w1,
           cmp_v_w2, w_out, ffn_norm_g, w_gate, w_up, w_down, ple_norm_g, w_ple_gate, w_ple_proj, final_norm_g):
    B, T, d = x.shape
    depth = w_in.shape[0]
    assert depth == 1, "the final rmsnorm is fused into the last layer's per-layer-embedding kernel"
    h2 = x.reshape(B * T, d)
    i = 0
    h2, w_ple_gate_b = _mixer_ffn(
        h2, in_norm_g[i], w_in[i], pool_w[i], pool_scale[i], cmp_k_pe[i], cmp_k_w1[i], cmp_k_w2[i],
        cmp_v_pe[i], cmp_v_w1[i], cmp_v_w2[i], w_out[i], ffn_norm_g[i], w_gate[i], w_up[i], w_down[i],
        w_ple_gate[i], B=B, T=T)
    out = _ple(h2, p[i].reshape(B * T, -1), ple_norm_g[i][None, :], w_ple_gate_b,
               w_ple_proj[i].astype(bf16), final_norm_g[None, :], tm=512)
    return out.reshape(B, T, d)
```

```python
import functools

import jax
import jax.numpy as jnp
from jax import lax
from jax.experimental import pallas as pl
from jax.experimental.pallas import tpu as pltpu

EPS = 1e-6
NEG = -1e30
LANES = 128
POOL_WINDOWS = (2, 4, 8, 16)
POOL_HALO = 16
HEAD_DIM = 128
N_KV_GROUPS = 2
HEADS_PER_GROUP = 4
N_BRANCH = 3
CMP_BLOCK = 32
CMP_STRIDE = 16
SEL_BLOCK = 64
N_SELECT = 16
WINDOW = 512
ROPE_THETA = 500000.0
ROPE_DIM = HEAD_DIM // 4
ATTN_SCALE = HEAD_DIM ** -0.5
Q_SCALE = ATTN_SCALE * 1.4426950408889634
VMEM_LIMIT = 60 * 1024 * 1024
W_CHUNK = 256

f32 = jnp.float32
bf16 = jnp.bfloat16


def _dot(a, b):
    return jnp.dot(a, b, preferred_element_type=f32)


def _dot_nt(a, b):
    return lax.dot_general(a, b, (((1,), (1,)), ((), ())), preferred_element_type=f32)


def _rms(x, g):
    return x * lax.rsqrt(jnp.mean(x * x, axis=-1, keepdims=True) + EPS) * g


def _rope(h, cosf, sinf, lane):
    half = ROPE_DIM // 2
    partner = jnp.where(lane < half, pltpu.roll(h, LANES - half, axis=1), pltpu.roll(h, half, axis=1))
    return h * cosf + partner * sinf


def _w_chunk_copy(w_hbm, stage_sc, sem, c):
    return pltpu.make_async_copy(w_hbm.at[pl.ds(c * W_CHUNK, W_CHUNK), :], stage_sc.at[c % 2], sem.at[c % 2])


def _in_proj_kernel(x_ref, g_ref, w_hbm, wgt_ref, cos_ref, sin_ref, ca_ref, cb_ref, cc_ref,
                    u_ref, q_ref, kcvc_ref, kv_ref, gt_ref, ca_out, cb_out, cc_out,
                    w_ref, stage_sc, sem):
    @pl.when(pl.program_id(0) == 0)
    def _():
        n_chunk = w_ref.shape[0] // W_CHUNK
        _w_chunk_copy(w_hbm, stage_sc, sem, 0).start()
        for c in range(n_chunk):
            if c + 1 < n_chunk:
                _w_chunk_copy(w_hbm, stage_sc, sem, c + 1).start()
            _w_chunk_copy(w_hbm, stage_sc, sem, c).wait()
            w_ref[c * W_CHUNK:(c + 1) * W_CHUNK, :] = stage_sc[c % 2].astype(bf16)

    ca_out[...] = ca_ref[...].astype(bf16)
    cb_out[...] = cb_ref[...].astype(bf16)
    cc_out[...] = cc_ref[...].astype(bf16)
    a = _rms(x_ref[...], g_ref[...]).astype(bf16)
    cosf = cos_ref[...]
    sinf = sin_ref[...]
    lane = lax.broadcasted_iota(jnp.int32, cosf.shape, 1)
    nu = u_ref.shape[1]
    nq = q_ref.shape[1]
    off = 0
    for c in range(0, nu, 512):
        u_ref[:, c:c + 512] = _dot_nt(a, w_ref[off + c:off + c + 512, :]).astype(bf16)
    off += nu
    for c in range(0, nq, 512):
        acc = _dot_nt(a, w_ref[off + c:off + c + 512, :])
        for j in range(4):
            h = _rope(acc[:, j * 128:(j + 1) * 128], cosf, sinf, lane) * Q_SCALE
            q_ref[:, c + j * 128:c + (j + 1) * 128] = h.astype(bf16)
    off += nq
    acc = _dot_nt(a, w_ref[off:off + 512, :])
    for j in range(4):
        h = acc[:, j * 128:(j + 1) * 128]
        if j < 2:
            h = _rope(h, cosf, sinf, lane)
        kcvc_ref[:, j * 128:(j + 1) * 128] = h
    off += 512
    acc = _dot_nt(a, w_ref[off:off + 512, :])
    for j in range(4):
        h = acc[:, j * 128:(j + 1) * 128]
        if j < 2:
            h = _rope(h, cosf, sinf, lane)
        kv_ref[:, j * 128:(j + 1) * 128] = h.astype(bf16)
    off += 512
    acc = _dot_nt(a, w_ref[off:off + 512, :])
    for j in range(4):
        h = acc[:, j * 128:(j + 1) * 128]
        if j < 2:
            h = _rope(h, cosf, sinf, lane)
        kv_ref[:, 512 + j * 128:512 + (j + 1) * 128] = h.astype(bf16)
    off += 512
    gt_ref[...] = _dot_nt(a, wgt_ref[...])


def _in_proj(x2, g, w_p, w_gt, cosf, sinf, cast_a, cast_b, cast_c, *, T, tm):
    n_rows, d = x2.shape
    n_w = (w_p.shape[0] // W_CHUNK) * W_CHUNK
    assert w_p.dtype == f32 and n_w + w_gt.shape[0] >= w_p.shape[0]
    tiles_per_seq = T // tm
    n_steps = n_rows // tm
    row = lambda i: (i, 0)
    cast_spec = lambda w: pl.BlockSpec((w.shape[0] // n_steps, w.shape[1]), row)
    const = lambda i: (0, 0)
    tab = lambda i: (i % tiles_per_seq, 0)
    return pl.pallas_call(
        _in_proj_kernel,
        grid=(n_rows // tm,),
        in_specs=[
            pl.BlockSpec((tm, d), row),
            pl.BlockSpec((1, d), const),
            pl.BlockSpec(memory_space=pl.ANY),
            pl.BlockSpec((LANES, d), const),
            pl.BlockSpec((tm, LANES), tab),
            pl.BlockSpec((tm, LANES), tab),
            cast_spec(cast_a), cast_spec(cast_b), cast_spec(cast_c),
        ],
        out_specs=[
            pl.BlockSpec((tm, 1024), row),
            pl.BlockSpec((tm, 1024), row),
            pl.BlockSpec((tm, 512), row),
            pl.BlockSpec((tm, 1024), row),
            pl.BlockSpec((tm, LANES), row),
            cast_spec(cast_a), cast_spec(cast_b), cast_spec(cast_c),
        ],
        out_shape=[
            jax.ShapeDtypeStruct((n_rows, 1024), bf16),
            jax.ShapeDtypeStruct((n_rows, 1024), bf16),
            jax.ShapeDtypeStruct((n_rows, 512), f32),
            jax.ShapeDtypeStruct((n_rows, 1024), bf16),
            jax.ShapeDtypeStruct((n_rows, LANES), f32),
            jax.ShapeDtypeStruct(cast_a.shape, bf16),
            jax.ShapeDtypeStruct(cast_b.shape, bf16),
            jax.ShapeDtypeStruct(cast_c.shape, bf16),
        ],
        scratch_shapes=[
            pltpu.VMEM((n_w, d), bf16),
            pltpu.VMEM((2, W_CHUNK, d), f32),
            pltpu.SemaphoreType.DMA((2,)),
        ],
        compiler_params=pltpu.CompilerParams(
            dimension_semantics=("arbitrary",), vmem_limit_bytes=VMEM_LIMIT),
        name="in_proj",
    )(x2, g, w_p, w_gt, cosf, sinf, cast_a, cast_b, cast_c)


def _gelu_tanh(x):
    return 0.5 * x * (1.0 + jnp.tanh(0.7978845608028654 * (x + 0.044715 * (x * x * x))))


def _compress_kernel(kc0_ref, kc1_ref, vc0_ref, vc1_ref, pek_ref, w1k_ref, w2k_ref, pev_ref, w1v_ref, w2v_ref,
                     kc_ref, vc_ref):
    n_rows = kc0_ref.shape[1] // CMP_STRIDE
    half_k = (CMP_BLOCK // 2) * HEAD_DIM
    rowi = lax.broadcasted_iota(jnp.int32, (n_rows, HEAD_DIM), 0)
    for srcs, pe_ref, w1_ref, w2_ref, o_ref in (((kc0_ref, kc1_ref), pek_ref, w1k_ref, w2k_ref, kc_ref),
                                                ((vc0_ref, vc1_ref), pev_ref, w1v_ref, w2v_ref, vc_ref)):
        pe_a = pe_ref[:, :half_k]
        pe_b = pe_ref[:, half_k:]
        for g, src in enumerate(srcs):
            x = jnp.concatenate(
                [src[0, pl.ds(l, n_rows, stride=CMP_STRIDE), :] for l in range(CMP_STRIDE)], axis=1)
            za = _dot((x + pe_a).astype(bf16), w1_ref[:half_k, :])
            zb = _dot((x + pe_b).astype(bf16), w1_ref[half_k:, :])
            h = za + pltpu.roll(zb, n_rows - 1, axis=0)
            o = _dot(_gelu_tanh(h).astype(bf16), w2_ref[...])
            o_ref[0, g] = jnp.where(rowi < n_rows - 1, o, 0.0).astype(bf16)


def _compress(r3, pek, w1k, w2k, pev, w1v, w2v):
    B, T, width = r3.shape
    assert width == 2 * N_KV_GROUPS * HEAD_DIM and N_KV_GROUPS == 2
    n_rows = T // CMP_STRIDE
    const = lambda b: (0, 0)
    out = jax.ShapeDtypeStruct((B, N_KV_GROUPS, n_rows, HEAD_DIM), bf16)
    ospec = pl.BlockSpec((1, N_KV_GROUPS, n_rows, HEAD_DIM), lambda b: (b, 0, 0, 0))
    sec = lambda s: pl.BlockSpec((1, T, HEAD_DIM), lambda b: (b, 0, s))
    return pl.pallas_call(
        _compress_kernel,
        grid=(B,),
        in_specs=[
            sec(0), sec(1), sec(2), sec(3),
            pl.BlockSpec(pek.shape, const), pl.BlockSpec(w1k.shape, const), pl.BlockSpec(w2k.shape, const),
            pl.BlockSpec(pev.shape, const), pl.BlockSpec(w1v.shape, const), pl.BlockSpec(w2v.shape, const),
        ],
        out_specs=[ospec, ospec],
        out_shape=[out, out],
        compiler_params=pltpu.CompilerParams(
            dimension_semantics=("arbitrary",), vmem_limit_bytes=VMEM_LIMIT),
        name="compress",
    )(r3, r3, r3, r3, pek, w1k, w2k, pev, w1v, w2v)


def _nsa_front(qs, t0, kc, vc, kw_ref, vw_ref, *, tq, T):
    R = HEADS_PER_GROUP
    M = R * tq
    n_cmp = kc.shape[0]
    n_sel = T // SEL_BLOCK

    wk = WINDOW + tq
    w0 = pl.multiple_of(jnp.maximum(t0 - WINDOW, 0), tq)
    sw = _dot_nt(qs, kw_ref[pl.ds(w0, wk), :])
    diff = (t0 - w0) + (lax.broadcasted_iota(jnp.int32, (M, wk), 0) & (tq - 1)) \
        - lax.broadcasted_iota(jnp.int32, (M, wk), 1)
    sw = jnp.where((diff & -WINDOW) == 0, sw, NEG)
    ew = jnp.exp2(sw - jnp.max(sw, axis=1, keepdims=True))
    o_win = _dot(ew.astype(bf16), vw_ref[pl.ds(w0, wk), :]) / jnp.sum(ew, axis=1, keepdims=True)

    s = _dot_nt(qs, kc)
    trow = t0 + (lax.broadcasted_iota(jnp.int32, (M, n_cmp), 0) & (tq - 1))
    ncol = lax.broadcasted_iota(jnp.int32, (M, n_cmp), 1)
    valid = (CMP_STRIDE * ncol + (CMP_BLOCK - 1)) <= trow
    s = jnp.where(valid, s, NEG)
    e = jnp.exp2(s - jnp.max(s, axis=1, keepdims=True))
    p = jnp.where(valid, e / jnp.sum(e, axis=1, keepdims=True), 0.0)
    o_cmp = _dot(p.astype(bf16), vc)

    psum = p[0:tq]
    for r in range(1, R):
        psum = psum + p[r * tq:(r + 1) * tq]
    p_hi = psum.astype(bf16)
    p_lo = (psum - p_hi.astype(f32)).astype(bf16)
    sb = lax.broadcasted_iota(jnp.int32, (n_sel, n_cmp), 0) * SEL_BLOCK
    cb = lax.broadcasted_iota(jnp.int32, (n_sel, n_cmp), 1) * CMP_STRIDE
    ov_t = jnp.where((cb < sb + SEL_BLOCK) & (cb + CMP_BLOCK > sb), 1.0, 0.0).astype(bf16)
    imp_t = _dot_nt(ov_t, p_hi) + _dot_nt(ov_t, p_lo)

    jj = lax.broadcasted_iota(jnp.int32, (n_sel, tq), 0)
    cur = jnp.right_shift(t0 + lax.broadcasted_iota(jnp.int32, (n_sel, tq), 1), SEL_BLOCK.bit_length() - 1)
    forced = (jj == 0) | (jj == cur) | (jj == cur - 1)
    v = jnp.where(jj > cur, -jnp.inf, jnp.where(forced, jnp.inf, imp_t))
    sub = 8
    slabs = [v[k * sub:(k + 1) * sub] for k in range(n_sel // sub)]
    ranks = [jnp.zeros((sub, tq), f32) for _ in slabs]
    j_in = lax.broadcasted_iota(jnp.int32, (sub, tq), 0)
    for c in range(n_sel):
        v_c = v[c:c + 1, :]
        for k, v_k in enumerate(slabs):
            if k * sub > c:
                ahead = jnp.where(v_c >= v_k, 1.0, 0.0)
            elif (k + 1) * sub <= c:
                ahead = jnp.where(v_c > v_k, 1.0, 0.0)
            else:
                ahead = jnp.where(j_in > c - k * sub, jnp.where(v_c >= v_k, 1.0, 0.0),
                                  jnp.where(v_c > v_k, 1.0, 0.0))
            ranks[k] = ranks[k] + ahead
    rank = jnp.concatenate(ranks, axis=0)
    unsel_t = jnp.where(rank < float(min(N_SELECT, n_sel)), 0.0, 1.0)
    unsel_t = jnp.concatenate([unsel_t, jnp.zeros((LANES - n_sel, tq), f32)], axis=0)
    return o_win, o_cmp, unsel_t.T.astype(bf16)


def _nsa_kernel(q_ref, kc_ref, vc_ref, ks_ref, vs_ref, kw_ref, vw_ref, eneg_ref, gt_ref, o_ref,
                s_sc, mx_sc, ls_sc, acc_sc, *, tq, n_sub, T):
    R = HEADS_PER_GROUP
    M = R * tq
    tk = n_sub * tq
    i = pl.program_id(2)
    t0 = i * tk
    kc = kc_ref[0, 0]
    vc = vc_ref[0, 0]

    fronts = []
    q_rows = []
    for sb in range(n_sub):
        q = q_ref[sb * tq:(sb + 1) * tq, :]
        qs = jnp.concatenate([q[:, r * HEAD_DIM:(r + 1) * HEAD_DIM] for r in range(R)], axis=0)
        o_win, o_cmp, unsel = _nsa_front(qs, t0 + sb * tq, kc, vc, kw_ref, vw_ref, tq=tq, T=T)
        fronts.append((o_win, o_cmp))
        q_rows.append(jnp.concatenate([qs, jnp.concatenate([unsel] * R, axis=0)], axis=1))
    q_aug = jnp.concatenate(q_rows, axis=0)
    MM = n_sub * M

    n_chunk = tk // LANES
    mx_sc[...] = jnp.full(mx_sc.shape, NEG, f32)

    def score_step(kt, causal):
        k0 = pl.multiple_of(kt * tk, tk)
        k_aug = jnp.concatenate([ks_ref[pl.ds(k0, tk), :], eneg_ref[pl.ds(k0, tk), :]], axis=1)
        sc = _dot_nt(q_aug, k_aug)
        if causal:
            rowi = lax.broadcasted_iota(jnp.int32, (MM, tk), 0)
            qoff = jnp.right_shift(rowi, M.bit_length() - 1) * tq + (rowi & (tq - 1))
            sc = jnp.where(qoff >= lax.broadcasted_iota(jnp.int32, (MM, tk), 1), sc, NEG)
        s_sc[kt] = sc
        part = sc[:, :LANES]
        for c in range(1, n_chunk):
            part = jnp.maximum(part, sc[:, c * LANES:(c + 1) * LANES])
        mx_sc[...] = jnp.maximum(mx_sc[...], part)

    def score_body(kt, carry):
        score_step(kt, False)
        return carry

    lax.fori_loop(0, i, score_body, 0)
    score_step(i, True)
    mx_sc[...] = jnp.broadcast_to(jnp.max(mx_sc[...], axis=1, keepdims=True), mx_sc.shape)
    ls_sc[...] = jnp.zeros(ls_sc.shape, f32)
    acc_sc[...] = jnp.zeros(acc_sc.shape, f32)

    def pv_body(kt, carry):
        k0 = pl.multiple_of(kt * tk, tk)
        sc = s_sc[kt]
        mb = mx_sc[...]
        ps = [jnp.exp2(sc[:, c * LANES:(c + 1) * LANES] - mb) for c in range(n_chunk)]
        ls_sc[...] += functools.reduce(lambda a, b: a + b, ps)
        acc_sc[...] += _dot(jnp.concatenate(ps, axis=1).astype(bf16), vs_ref[pl.ds(k0, tk), :])
        return carry

    lax.fori_loop(0, i + 1, pv_body, 0)
    o_sel = acc_sc[...] / jnp.sum(ls_sc[...], axis=1, keepdims=True)

    gt_all = jax.nn.sigmoid(gt_ref[...])
    gt_all = jnp.where(pl.program_id(1) == 0, gt_all, pltpu.roll(gt_all, LANES - R * N_BRANCH, axis=1))
    for sb, (o_win, o_cmp) in enumerate(fronts):
        gt = gt_all[sb * tq:(sb + 1) * tq]
        for r in range(R):
            rows = slice(r * tq, (r + 1) * tq)
            srows = slice(sb * M + r * tq, sb * M + (r + 1) * tq)
            c = r * N_BRANCH
            o = (gt[:, c:c + 1] * o_cmp[rows] + gt[:, c + 1:c + 2] * o_sel[srows]
                 + gt[:, c + 2:c + 3] * o_win[rows])
            o_ref[sb * tq:(sb + 1) * tq, r * HEAD_DIM:(r + 1) * HEAD_DIM] = o.astype(bf16)


def _nsa(q, kcmp, vcmp, kv, eneg, gates, *, B, T, tq, n_sub):
    G, R = N_KV_GROUPS, HEADS_PER_GROUP
    tk = n_sub * tq
    nt = T // tk
    n_cmp = kcmp.shape[2]
    assert tq % LANES == 0 and tq & (tq - 1) == 0 and T % tk == 0 and T >= WINDOW + tq
    assert WINDOW % tq == 0 and WINDOW & (WINDOW - 1) == 0 and (R * tq) & (R * tq - 1) == 0
    rowblk = lambda b, g, i: (b * nt + i, g)
    cmp_spec = pl.BlockSpec((1, 1, n_cmp, HEAD_DIM), lambda b, g, i: (b, g, 0, 0))
    kvspec = lambda sec: pl.BlockSpec((T, HEAD_DIM), lambda b, g, i: (b, sec * G + g))
    return pl.pallas_call(
        functools.partial(_nsa_kernel, tq=tq, n_sub=n_sub, T=T),
        grid=(B, G, nt),
        in_specs=[
            pl.BlockSpec((tk, R * HEAD_DIM), rowblk),
            cmp_spec, cmp_spec,
            kvspec(0), kvspec(1), kvspec(2), kvspec(3),
            pl.BlockSpec((T, LANES), lambda b, g, i: (0, 0)),
            pl.BlockSpec((tk, LANES), lambda b, g, i: (b * nt + i, 0)),
        ],
        out_specs=pl.BlockSpec((tk, R * HEAD_DIM), rowblk),
        out_shape=jax.ShapeDtypeStruct((B * T, G * R * HEAD_DIM), bf16),
        scratch_shapes=[
            pltpu.VMEM((T // tk, n_sub * R * tq, tk), f32),
            pltpu.VMEM((n_sub * R * tq, LANES), f32),
            pltpu.VMEM((n_sub * R * tq, LANES), f32),
            pltpu.VMEM((n_sub * R * tq, HEAD_DIM), f32),
        ],
        compiler_params=pltpu.CompilerParams(
            dimension_semantics=("arbitrary", "arbitrary", "arbitrary"), vmem_limit_bytes=VMEM_LIMIT),
        name="nsa",
    )(q, kcmp, vcmp, kv, kv, kv, kv, eneg, gates)


def _out_proj_kernel(x_ref, u_ref, uh_ref, y_ref, pw_ref, ps_ref, wo_ref, ca_ref, cb_ref,
                     o_ref, ca_out, cb_out, *, tm, T):
    ca_out[...] = ca_ref[...].astype(bf16)
    cb_out[...] = cb_ref[...].astype(bf16)
    i = pl.program_id(0)
    tiles_per_seq = T // tm
    keep_halo = jnp.where((i % tiles_per_seq) == 0, 0.0, 1.0)
    u = u_ref[...]
    halo = uh_ref[...]
    tt = lax.broadcasted_iota(jnp.int32, (tm, tm), 0)
    ss = lax.broadcasted_iota(jnp.int32, (tm, tm), 1)
    th = lax.broadcasted_iota(jnp.int32, (tm, POOL_HALO), 0)
    rh = lax.broadcasted_iota(jnp.int32, (tm, POOL_HALO), 1)
    tseq = (i % tiles_per_seq) * tm + lax.broadcasted_iota(jnp.int32, (tm, 1), 0)
    n_pool = u.shape[1]
    gd = n_pool // len(POOL_WINDOWS)
    acc = x_ref[...] + _dot(y_ref[...], wo_ref[n_pool:, :])
    for gi, w in enumerate(POOL_WINDOWS):
        cols = slice(gi * gd, (gi + 1) * gd)
        band = jnp.where((tt - ss >= 0) & (tt - ss < w), 1.0, 0.0).astype(bf16)
        band_h = jnp.where(th + POOL_HALO - rh < w, 1.0, 0.0).astype(bf16)
        ug = u[:, cols]
        win_sum = _dot(band, ug) + keep_halo * _dot(band_h, halo[:, cols])
        count = jnp.minimum(tseq + 1, w).astype(f32)
        pooled = win_sum / count - ug.astype(f32)
        yg = _dot(pooled.astype(bf16), pw_ref[gi]) * ps_ref[:, cols]
        acc = acc + _dot(yg.astype(bf16), wo_ref[cols, :])
    o_ref[...] = acc


def _out_proj(x2, u, y_nsa, pool_w, pool_scale, w_out, cast_a, cast_b, *, T, tm):
    n_rows, d = x2.shape
    n_pool = u.shape[1]
    hb = tm // POOL_HALO
    n_steps = n_rows // tm
    row = lambda i: (i, 0)
    cast_spec = lambda w: pl.BlockSpec((w.shape[0] // n_steps, w.shape[1]), row)
    const2 = lambda i: (0, 0)
    return pl.pallas_call(
        functools.partial(_out_proj_kernel, tm=tm, T=T),
        grid=(n_rows // tm,),
        in_specs=[
            pl.BlockSpec((tm, d), row),
            pl.BlockSpec((tm, n_pool), row),
            pl.BlockSpec((POOL_HALO, n_pool), lambda i: (jnp.maximum(i * hb - 1, 0), 0)),
            pl.BlockSpec((tm, y_nsa.shape[1]), row),
            pl.BlockSpec(pool_w.shape, lambda i: (0, 0, 0)),
            pl.BlockSpec((1, n_pool), const2),
            pl.BlockSpec(w_out.shape, const2, pipeline_mode=pl.Buffered(1)),
            cast_spec(cast_a), cast_spec(cast_b),
        ],
        out_specs=[pl.BlockSpec((tm, d), row), cast_spec(cast_a), cast_spec(cast_b)],
        out_shape=[jax.ShapeDtypeStruct((n_rows, d), f32), jax.ShapeDtypeStruct(cast_a.shape, bf16),
                   jax.ShapeDtypeStruct(cast_b.shape, bf16)],
        compiler_params=pltpu.CompilerParams(
            dimension_semantics=("arbitrary",), vmem_limit_bytes=VMEM_LIMIT),
        name="out_proj",
    )(x2, u, u, y_nsa, pool_w, pool_scale, w_out, cast_a, cast_b)


def _ffn_kernel(h_ref, g_ref, wg_ref, wu_ref, wd_ref, o_ref, n_sc):
    @pl.when(pl.program_id(1) == 0)
    def _():
        h = h_ref[...]
        n_sc[...] = _rms(h, g_ref[...]).astype(bf16)
        o_ref[...] = h

    n = n_sc[...]
    half = wg_ref.shape[1] // 2
    acc = None
    for c in range(2):
        cols = slice(c * half, (c + 1) * half)
        gate = _dot(n, wg_ref[:, cols])
        up = _dot(n, wu_ref[:, cols])
        a = gate * jax.nn.sigmoid(gate) * up
        d = _dot(a.astype(bf16), wd_ref[cols, :])
        acc = d if acc is None else acc + d
    o_ref[...] += acc


def _ffn(h, g, w_gate, w_up, w_down, *, tm, tf):
    n_rows, d = h.shape
    d_ff = w_gate.shape[1]
    return pl.pallas_call(
        _ffn_kernel,
        grid=(n_rows // tm, d_ff // tf),
        in_specs=[
            pl.BlockSpec((tm, d), lambda i, f: (i, 0)),
            pl.BlockSpec((1, d), lambda i, f: (0, 0)),
            pl.BlockSpec((d, tf), lambda i, f: (0, f)),
            pl.BlockSpec((d, tf), lambda i, f: (0, f)),
            pl.BlockSpec((tf, d), lambda i, f: (f, 0)),
        ],
        out_specs=pl.BlockSpec((tm, d), lambda i, f: (i, 0)),
        out_shape=jax.ShapeDtypeStruct((n_rows, d), f32),
        scratch_shapes=[pltpu.VMEM((tm, d), bf16)],
        compiler_params=pltpu.CompilerParams(
            dimension_semantics=("arbitrary", "arbitrary"), vmem_limit_bytes=VMEM_LIMIT),
        name="ffn",
    )(h, g, w_gate, w_up, w_down)


def _ple_kernel(h_ref, p_ref, g_ref, wg_ref, wp_ref, fg_ref, o_ref):
    h = h_ref[...]
    n = _rms(h, g_ref[...]).astype(bf16)
    gate = jax.nn.sigmoid(_dot(n, wg_ref[...]))
    h = h + _dot(p_ref[...].astype(bf16), wp_ref[...]) * gate
    o_ref[...] = _rms(h, fg_ref[...])


def _ple(h, p2, g, w_gate, w_proj, fg, *, tm):
    n_rows, d = h.shape
    row = lambda i: (i, 0)
    const = lambda i: (0, 0)
    return pl.pallas_call(
        _ple_kernel,
        grid=(n_rows // tm,),
        in_specs=[
            pl.BlockSpec((tm, d), row),
            pl.BlockSpec((tm, p2.shape[1]), row),
            pl.BlockSpec((1, d), const),
            pl.BlockSpec(w_gate.shape, const, pipeline_mode=pl.Buffered(1)),
            pl.BlockSpec(w_proj.shape, const),
            pl.BlockSpec((1, d), const),
        ],
        out_specs=pl.BlockSpec((tm, d), row),
        out_shape=jax.ShapeDtypeStruct((n_rows, d), f32),
        compiler_params=pltpu.CompilerParams(
            dimension_semantics=("arbitrary",), vmem_limit_bytes=VMEM_LIMIT),
        name="ple",
    )(h, p2, g, w_gate, w_proj, fg)


def _rope_tables(T):
    pos = jnp.arange(T, dtype=f32)
    inv_freq = ROPE_THETA ** (-jnp.arange(0, ROPE_DIM, 2, dtype=f32) / ROPE_DIM)
    ang = pos[:, None] * inv_freq[None, :]
    cos, sin = jnp.cos(ang), jnp.sin(ang)
    rest = HEAD_DIM - ROPE_DIM
    cosf = jnp.concatenate([cos, cos, jnp.ones((T, rest), f32)], axis=1)
    sinf = jnp.concatenate([-sin, sin, jnp.zeros((T, rest), f32)], axis=1)
    return cosf, sinf


def _mixer_ffn(h2, in_norm_g, w_in, pool_w, pool_scale, cmp_k_pe, cmp_k_w1, cmp_k_w2,
               cmp_v_pe, cmp_v_w1, cmp_v_w2, w_out, ffn_norm_g, w_gate, w_up, w_down, w_ple_gate, *, B, T):
    n_pool = pool_scale.shape[0]
    n_heads = N_KV_GROUPS * HEADS_PER_GROUP
    nsa_w = n_heads * HEAD_DIM
    kv_w = N_KV_GROUPS * HEAD_DIM
    assert w_in.shape[1] == n_pool + nsa_w + 6 * kv_w + n_heads * N_BRANCH and n_pool == 1024 and nsa_w == 1024

    o0 = n_pool + nsa_w + 6 * kv_w
    w_p = w_in.T
    w_gt = jnp.pad(w_p[o0:].astype(bf16), ((0, LANES - n_heads * N_BRANCH), (0, 0)))

    cosf, sinf = _rope_tables(T)
    u, q, kcvc, kv, gates, w_out_b, w_gate_b, w_ple_gate_b = _in_proj(
        h2, in_norm_g[None, :], w_p, w_gt, cosf, sinf, w_out, w_gate, w_ple_gate, T=T, tm=512)

    kcmp, vcmp = _compress(
        kcvc.reshape(B, T, kcvc.shape[1]),
        cmp_k_pe.reshape(1, -1), cmp_k_w1.astype(bf16), cmp_k_w2.astype(bf16),
        cmp_v_pe.reshape(1, -1), cmp_v_w1.astype(bf16), cmp_v_w2.astype(bf16))

    key_blk = jnp.arange(T, dtype=jnp.int32)[:, None] // SEL_BLOCK
    eneg = jnp.where(key_blk == jnp.arange(LANES, dtype=jnp.int32)[None, :], NEG, 0.0).astype(bf16)
    y_nsa = _nsa(q, kcmp, vcmp, kv, eneg, gates, B=B, T=T, tq=256, n_sub=2)

    h2, w_up_b, w_down_b = _out_proj(h2, u, y_nsa, pool_w.astype(bf16), pool_scale[None, :], w_out_b,
                                     w_up, w_down, T=T, tm=512)
    return _ffn(h2, ffn_norm_g[None, :], w_gate_b, w_up_b, w_down_b, tm=1024, tf=512), w_ple_gate_b


def kernel(x, p, in_norm_g, w_in, pool_w, pool_scale, cmp_k_pe, cmp_k_w1, cmp_k_w2, cmp_v_pe, cmp_v_w1,
           cmp_v_w2, w_out, ffn_norm_g, w_gate, w_up, w_down, ple_norm_g, w_ple_gate, w_ple_proj, final_norm_g):
    B, T, d = x.shape
    depth = w_in.shape[0]
    assert depth == 1, "the final rmsnorm is fused into the last layer's per-layer-embedding kernel"
    h2 = x.reshape(B * T, d)
    i = 0
    h2, w_ple_gate_b = _mixer_ffn(
        h2, in_norm_g[i], w_in[i], pool_w[i], pool_scale[i], cmp_k_pe[i], cmp_k_w1[i], cmp_k_w2[i],
        cmp_v_pe[i], cmp_v_w1[i], cmp_v_w2[i], w_out[i], ffn_norm_g[i], w_gate[i], w_up[i], w_down[i],
        w_ple_gate[i], B=B, T=T)
    out = _ple(h2, p[i].reshape(B * T, -1), ple_norm_g[i][None, :], w_ple_gate_b,
               w_ple_proj[i].astype(bf16), final_norm_g[None, :], tm=512)
    return out.reshape(B, T, d)
```

```python
import functools

import jax
import jax.numpy as jnp
from jax import lax
from jax.experimental import pallas as pl
from jax.experimental.pallas import tpu as pltpu

EPS = 1e-6
NEG = -1e30
LANES = 128
POOL_WINDOWS = (2, 4, 8, 16)
POOL_HALO = 16
HEAD_DIM = 128
N_KV_GROUPS = 2
HEADS_PER_GROUP = 4
N_BRANCH = 3
CMP_BLOCK = 32
CMP_STRIDE = 16
SEL_BLOCK = 64
N_SELECT = 16
WINDOW = 512
ROPE_THETA = 500000.0
ROPE_DIM = HEAD_DIM // 4
ATTN_SCALE = HEAD_DIM ** -0.5
Q_SCALE = ATTN_SCALE * 1.4426950408889634
VMEM_LIMIT = 60 * 1024 * 1024
W_CHUNK = 256

f32 = jnp.float32
bf16 = jnp.bfloat16


def _dot(a, b):
    return jnp.dot(a, b, preferred_element_type=f32)


def _dot_nt(a, b):
    return lax.dot_general(a, b, (((1,), (1,)), ((), ())), preferred_element_type=f32)


def _rms(x, g):
    return x * lax.rsqrt(jnp.mean(x * x, axis=-1, keepdims=True) + EPS) * g


def _rope(h, cosf, sinf, lane):
    half = ROPE_DIM // 2
    partner = jnp.where(lane < half, pltpu.roll(h, LANES - half, axis=1), pltpu.roll(h, half, axis=1))
    return h * cosf + partner * sinf


def _w_chunk_copy(w_hbm, stage_sc, sem, c):
    return pltpu.make_async_copy(w_hbm.at[pl.ds(c * W_CHUNK, W_CHUNK), :], stage_sc.at[c % 2], sem.at[c % 2])


def _in_proj_kernel(x_ref, g_ref, w_hbm, wgt_ref, cos_ref, sin_ref, ca_ref, cb_ref, cc_ref,
                    u_ref, q_ref, kcvc_ref, kv_ref, gt_ref, ca_out, cb_out, cc_out,
                    w_ref, stage_sc, sem):
    @pl.when(pl.program_id(0) == 0)
    def _():
        n_chunk = w_ref.shape[0] // W_CHUNK
        _w_chunk_copy(w_hbm, stage_sc, sem, 0).start()
        for c in range(n_chunk):
            if c + 1 < n_chunk:
                _w_chunk_copy(w_hbm, stage_sc, sem, c + 1).start()
            _w_chunk_copy(w_hbm, stage_sc, sem, c).wait()
            w_ref[c * W_CHUNK:(c + 1) * W_CHUNK, :] = stage_sc[c % 2].astype(bf16)

    ca_out[...] = ca_ref[...].astype(bf16)
    cb_out[...] = cb_ref[...].astype(bf16)
    cc_out[...] = cc_ref[...].astype(bf16)
    a = _rms(x_ref[...], g_ref[...]).astype(bf16)
    cosf = cos_ref[...]
    sinf = sin_ref[...]
    lane = lax.broadcasted_iota(jnp.int32, cosf.shape, 1)
    nu = u_ref.shape[1]
    nq = q_ref.shape[1]
    off = 0
    for c in range(0, nu, 512):
        u_ref[:, c:c + 512] = _dot_nt(a, w_ref[off + c:off + c + 512, :]).astype(bf16)
    off += nu
    for c in range(0, nq, 512):
        acc = _dot_nt(a, w_ref[off + c:off + c + 512, :])
        for j in range(4):
            h = _rope(acc[:, j * 128:(j + 1) * 128], cosf, sinf, lane) * Q_SCALE
            q_ref[:, c + j * 128:c + (j + 1) * 128] = h.astype(bf16)
    off += nq
    acc = _dot_nt(a, w_ref[off:off + 512, :])
    for j in range(4):
        h = acc[:, j * 128:(j + 1) * 128]
        if j < 2:
            h = _rope(h, cosf, sinf, lane)
        kcvc_ref[:, j * 128:(j + 1) * 128] = h
    off += 512
    acc = _dot_nt(a, w_ref[off:off + 512, :])
    for j in range(4):
        h = acc[:, j * 128:(j + 1) * 128]
        if j < 2:
            h = _rope(h, cosf, sinf, lane)
        kv_ref[:, j * 128:(j + 1) * 128] = h.astype(bf16)
    off += 512
    acc = _dot_nt(a, w_ref[off:off + 512, :])
    for j in range(4):
        h = acc[:, j * 128:(j + 1) * 128]
        if j < 2:
            h = _rope(h, cosf, sinf, lane)
        kv_ref[:, 512 + j * 128:512 + (j + 1) * 128] = h.astype(bf16)
    off += 512
    gt_ref[...] = _dot_nt(a, wgt_ref[...])


def _in_proj(x2, g, w_p, w_gt, cosf, sinf, cast_a, cast_b, cast_c, *, T, tm):
    n_rows, d = x2.shape
    n_w = (w_p.shape[0] // W_CHUNK) * W_CHUNK
    assert w_p.dtype == f32 and n_w + w_gt.shape[0] >= w_p.shape[0]
    tiles_per_seq = T // tm
    n_steps = n_rows // tm
    row = lambda i: (i, 0)
    cast_spec = lambda w: pl.BlockSpec((w.shape[0] // n_steps, w.shape[1]), row)
    const = lambda i: (0, 0)
    tab = lambda i: (i % tiles_per_seq, 0)
    return pl.pallas_call(
        _in_proj_kernel,
        grid=(n_rows // tm,),
        in_specs=[
            pl.BlockSpec((tm, d), row),
            pl.BlockSpec((1, d), const),
            pl.BlockSpec(memory_space=pl.ANY),
            pl.BlockSpec((LANES, d), const),
            pl.BlockSpec((tm, LANES), tab),
            pl.BlockSpec((tm, LANES), tab),
            cast_spec(cast_a), cast_spec(cast_b), cast_spec(cast_c),
        ],
        out_specs=[
            pl.BlockSpec((tm, 1024), row),
            pl.BlockSpec((tm, 1024), row),
            pl.BlockSpec((tm, 512), row),
            pl.BlockSpec((tm, 1024), row),
            pl.BlockSpec((tm, LANES), row),
            cast_spec(cast_a), cast_spec(cast_b), cast_spec(cast_c),
        ],
        out_shape=[
            jax.ShapeDtypeStruct((n_rows, 1024), bf16),
            jax.ShapeDtypeStruct((n_rows, 1024), bf16),
            jax.ShapeDtypeStruct((n_rows, 512), f32),
            jax.ShapeDtypeStruct((n_rows, 1024), bf16),
            jax.ShapeDtypeStruct((n_rows, LANES), f32),
            jax.ShapeDtypeStruct(cast_a.shape, bf16),
            jax.ShapeDtypeStruct(cast_b.shape, bf16),
            jax.ShapeDtypeStruct(cast_c.shape, bf16),
        ],
        scratch_shapes=[
            pltpu.VMEM((n_w, d), bf16),
            pltpu.VMEM((2, W_CHUNK, d), f32),
            pltpu.SemaphoreType.DMA((2,)),
        ],
        compiler_params=pltpu.CompilerParams(
            dimension_semantics=("arbitrary",), vmem_limit_bytes=VMEM_LIMIT),
        name="in_proj",
    )(x2, g, w_p, w_gt, cosf, sinf, cast_a, cast_b, cast_c)


def _gelu_tanh(x):
    return 0.5 * x * (1.0 + jnp.tanh(0.7978845608028654 * (x + 0.044715 * (x * x * x))))


def _compress_kernel(kc0_ref, kc1_ref, vc0_ref, vc1_ref, pek_ref, w1k_ref, w2k_ref, pev_ref, w1v_ref, w2v_ref,
                     kc_ref, vc_ref):
    n_rows = kc0_ref.shape[1] // CMP_STRIDE
    half_k = (CMP_BLOCK // 2) * HEAD_DIM
    rowi = lax.broadcasted_iota(jnp.int32, (n_rows, HEAD_DIM), 0)
    for srcs, pe_ref, w1_ref, w2_ref, o_ref in (((kc0_ref, kc1_ref), pek_ref, w1k_ref, w2k_ref, kc_ref),
                                                ((vc0_ref, vc1_ref), pev_ref, w1v_ref, w2v_ref, vc_ref)):
        pe_a = pe_ref[:, :half_k]
        pe_b = pe_ref[:, half_k:]
        for g, src in enumerate(srcs):
            x = jnp.concatenate(
                [src[0, pl.ds(l, n_rows, stride=CMP_STRIDE), :] for l in range(CMP_STRIDE)], axis=1)
            za = _dot((x + pe_a).astype(bf16), w1_ref[:half_k, :])
            zb = _dot((x + pe_b).astype(bf16), w1_ref[half_k:, :])
            h = za + pltpu.roll(zb, n_rows - 1, axis=0)
            o = _dot(_gelu_tanh(h).astype(bf16), w2_ref[...])
            o_ref[0, g] = jnp.where(rowi < n_rows - 1, o, 0.0).astype(bf16)


def _compress(r3, pek, w1k, w2k, pev, w1v, w2v):
    B, T, width = r3.shape
    assert width == 2 * N_KV_GROUPS * HEAD_DIM and N_KV_GROUPS == 2
    n_rows = T // CMP_STRIDE
    const = lambda b: (0, 0)
    out = jax.ShapeDtypeStruct((B, N_KV_GROUPS, n_rows, HEAD_DIM), bf16)
    ospec = pl.BlockSpec((1, N_KV_GROUPS, n_rows, HEAD_DIM), lambda b: (b, 0, 0, 0))
    sec = lambda s: pl.BlockSpec((1, T, HEAD_DIM), lambda b: (b, 0, s))
    return pl.pallas_call(
        _compress_kernel,
        grid=(B,),
        in_specs=[
            sec(0), sec(1), sec(2), sec(3),
            pl.BlockSpec(pek.shape, const), pl.BlockSpec(w1k.shape, const), pl.BlockSpec(w2k.shape, const),
            pl.BlockSpec(pev.shape, const), pl.BlockSpec(w1v.shape, const), pl.BlockSpec(w2v.shape, const),
        ],
        out_specs=[ospec, ospec],
        out_shape=[out, out],
        compiler_params=pltpu.CompilerParams(
            dimension_semantics=("arbitrary",), vmem_limit_bytes=VMEM_LIMIT),
        name="compress",
    )(r3, r3, r3, r3, pek, w1k, w2k, pev, w1v, w2v)


def _nsa_front(qs, t0, kc, vc, kw_ref, vw_ref, *, tq, T):
    R = HEADS_PER_GROUP
    M = R * tq
    n_cmp = kc.shape[0]
    n_sel = T // SEL_BLOCK

    wk = WINDOW + tq
    w0 = pl.multiple_of(jnp.maximum(t0 - WINDOW, 0), tq)
    sw = _dot_nt(qs, kw_ref[pl.ds(w0, wk), :])
    diff = (t0 - w0) + (lax.broadcasted_iota(jnp.int32, (M, wk), 0) & (tq - 1)) \
        - lax.broadcasted_iota(jnp.int32, (M, wk), 1)
    sw = jnp.where((diff & -WINDOW) == 0, sw, NEG)
    ew = jnp.exp2(sw - jnp.max(sw, axis=1, keepdims=True))
    o_win = _dot(ew.astype(bf16), vw_ref[pl.ds(w0, wk), :]) / jnp.sum(ew, axis=1, keepdims=True)

    s = _dot_nt(qs, kc)
    trow = t0 + (lax.broadcasted_iota(jnp.int32, (M, n_cmp), 0) & (tq - 1))
    ncol = lax.broadcasted_iota(jnp.int32, (M, n_cmp), 1)
    valid = (CMP_STRIDE * ncol + (CMP_BLOCK - 1)) <= trow
    s = jnp.where(valid, s, NEG)
    e = jnp.exp2(s - jnp.max(s, axis=1, keepdims=True))
    p = jnp.where(valid, e / jnp.sum(e, axis=1, keepdims=True), 0.0)
    o_cmp = _dot(p.astype(bf16), vc)

    psum = p[0:tq]
    for r in range(1, R):
        psum = psum + p[r * tq:(r + 1) * tq]
    p_hi = psum.astype(bf16)
    p_lo = (psum - p_hi.astype(f32)).astype(bf16)
    sb = lax.broadcasted_iota(jnp.int32, (n_sel, n_cmp), 0) * SEL_BLOCK
    cb = lax.broadcasted_iota(jnp.int32, (n_sel, n_cmp), 1) * CMP_STRIDE
    ov_t = jnp.where((cb < sb + SEL_BLOCK) & (cb + CMP_BLOCK > sb), 1.0, 0.0).astype(bf16)
    imp_t = _dot_nt(ov_t, p_hi) + _dot_nt(ov_t, p_lo)

    jj = lax.broadcasted_iota(jnp.int32, (n_sel, tq), 0)
    cur = jnp.right_shift(t0 + lax.broadcasted_iota(jnp.int32, (n_sel, tq), 1), SEL_BLOCK.bit_length() - 1)
    forced = (jj == 0) | (jj == cur) | (jj == cur - 1)
    v = jnp.where(jj > cur, -jnp.inf, jnp.where(forced, jnp.inf, imp_t))
    sub = 8
    slabs = [v[k * sub:(k + 1) * sub] for k in range(n_sel // sub)]
    ranks = [jnp.zeros((sub, tq), f32) for _ in slabs]
    j_in = lax.broadcasted_iota(jnp.int32, (sub, tq), 0)
    for c in range(n_sel):
        v_c = v[c:c + 1, :]
        for k, v_k in enumerate(slabs):
            if k * sub > c:
                ahead = jnp.where(v_c >= v_k, 1.0, 0.0)
            elif (k + 1) * sub <= c:
                ahead = jnp.where(v_c > v_k, 1.0, 0.0)
            else:
                ahead = jnp.where(j_in > c - k * sub, jnp.where(v_c >= v_k, 1.0, 0.0),
                                  jnp.where(v_c > v_k, 1.0, 0.0))
            ranks[k] = ranks[k] + ahead
    rank = jnp.concatenate(ranks, axis=0)
    unsel_t = jnp.where(rank < float(min(N_SELECT, n_sel)), 0.0, 1.0)
    unsel_t = jnp.concatenate([unsel_t, jnp.zeros((LANES - n_sel, tq), f32)], axis=0)
    return o_win, o_cmp, unsel_t.T.astype(bf16)


def _nsa_kernel(q_ref, kc_ref, vc_ref, ks_ref, vs_ref, kw_ref, vw_ref, eneg_ref, gt_ref, ca_ref, cb_ref,
                o_ref, ca_out, cb_out, s_sc, mx_sc, ls_sc, acc_sc, *, tq, n_sub, T):
    ca_out[...] = ca_ref[...].astype(bf16)
    cb_out[...] = cb_ref[...].astype(bf16)
    R = HEADS_PER_GROUP
    M = R * tq
    tk = n_sub * tq
    i = pl.program_id(2)
    t0 = i * tk
    kc = kc_ref[0, 0]
    vc = vc_ref[0, 0]

    fronts = []
    q_rows = []
    for sb in range(n_sub):
        q = q_ref[sb * tq:(sb + 1) * tq, :]
        qs = jnp.concatenate([q[:, r * HEAD_DIM:(r + 1) * HEAD_DIM] for r in range(R)], axis=0)
        o_win, o_cmp, unsel = _nsa_front(qs, t0 + sb * tq, kc, vc, kw_ref, vw_ref, tq=tq, T=T)
        fronts.append((o_win, o_cmp))
        q_rows.append(jnp.concatenate([qs, jnp.concatenate([unsel] * R, axis=0)], axis=1))
    q_aug = jnp.concatenate(q_rows, axis=0)
    MM = n_sub * M

    n_chunk = tk // LANES
    mx_sc[...] = jnp.full(mx_sc.shape, NEG, f32)

    def score_step(kt, causal):
        k0 = pl.multiple_of(kt * tk, tk)
        k_aug = jnp.concatenate([ks_ref[pl.ds(k0, tk), :], eneg_ref[pl.ds(k0, tk), :]], axis=1)
        sc = _dot_nt(q_aug, k_aug)
        if causal:
            rowi = lax.broadcasted_iota(jnp.int32, (MM, tk), 0)
            qoff = jnp.right_shift(rowi, M.bit_length() - 1) * tq + (rowi & (tq - 1))
            sc = jnp.where(qoff >= lax.broadcasted_iota(jnp.int32, (MM, tk), 1), sc, NEG)
        s_sc[kt] = sc
        part = sc[:, :LANES]
        for c in range(1, n_chunk):
            part = jnp.maximum(part, sc[:, c * LANES:(c + 1) * LANES])
        mx_sc[...] = jnp.maximum(mx_sc[...], part)

    def score_body(kt, carry):
        score_step(kt, False)
        return carry

    lax.fori_loop(0, i, score_body, 0)
    score_step(i, True)
    mx_sc[...] = jnp.broadcast_to(jnp.max(mx_sc[...], axis=1, keepdims=True), mx_sc.shape)
    ls_sc[...] = jnp.zeros(ls_sc.shape, f32)
    acc_sc[...] = jnp.zeros(acc_sc.shape, f32)

    def pv_body(kt, carry):
        k0 = pl.multiple_of(kt * tk, tk)
        sc = s_sc[kt]
        mb = mx_sc[...]
        ps = [jnp.exp2(sc[:, c * LANES:(c + 1) * LANES] - mb) for c in range(n_chunk)]
        ls_sc[...] += functools.reduce(lambda a, b: a + b, ps)
        acc_sc[...] += _dot(jnp.concatenate(ps, axis=1).astype(bf16), vs_ref[pl.ds(k0, tk), :])
        return carry

    lax.fori_loop(0, i + 1, pv_body, 0)
    o_sel = acc_sc[...] / jnp.sum(ls_sc[...], axis=1, keepdims=True)

    gt_all = jax.nn.sigmoid(gt_ref[...])
    gt_all = jnp.where(pl.program_id(1) == 0, gt_all, pltpu.roll(gt_all, LANES - R * N_BRANCH, axis=1))
    for sb, (o_win, o_cmp) in enumerate(fronts):
        gt = gt_all[sb * tq:(sb + 1) * tq]
        for r in range(R):
            rows = slice(r * tq, (r + 1) * tq)
            srows = slice(sb * M + r * tq, sb * M + (r + 1) * tq)
            c = r * N_BRANCH
            o = (gt[:, c:c + 1] * o_cmp[rows] + gt[:, c + 1:c + 2] * o_sel[srows]
                 + gt[:, c + 2:c + 3] * o_win[rows])
            o_ref[sb * tq:(sb + 1) * tq, r * HEAD_DIM:(r + 1) * HEAD_DIM] = o.astype(bf16)


def _nsa(q, kcmp, vcmp, kv, eneg, gates, cast_a, cast_b, *, B, T, tq, n_sub):
    G, R = N_KV_GROUPS, HEADS_PER_GROUP
    tk = n_sub * tq
    nt = T // tk
    n_cmp = kcmp.shape[2]
    assert tq % LANES == 0 and tq & (tq - 1) == 0 and T % tk == 0 and T >= WINDOW + tq
    assert WINDOW % tq == 0 and WINDOW & (WINDOW - 1) == 0 and (R * tq) & (R * tq - 1) == 0
    rowblk = lambda b, g, i: (b * nt + i, g)
    cmp_spec = pl.BlockSpec((1, 1, n_cmp, HEAD_DIM), lambda b, g, i: (b, g, 0, 0))
    kvspec = lambda sec: pl.BlockSpec((T, HEAD_DIM), lambda b, g, i: (b, sec * G + g))
    n_steps = B * G * nt
    cast_spec = lambda w: pl.BlockSpec((w.shape[0] // n_steps, w.shape[1]), lambda b, g, i: ((b * G + g) * nt + i, 0))
    return pl.pallas_call(
        functools.partial(_nsa_kernel, tq=tq, n_sub=n_sub, T=T),
        grid=(B, G, nt),
        in_specs=[
            pl.BlockSpec((tk, R * HEAD_DIM), rowblk),
            cmp_spec, cmp_spec,
            kvspec(0), kvspec(1), kvspec(2), kvspec(3),
            pl.BlockSpec((T, LANES), lambda b, g, i: (0, 0)),
            pl.BlockSpec((tk, LANES), lambda b, g, i: (b * nt + i, 0)),
            cast_spec(cast_a), cast_spec(cast_b),
        ],
        out_specs=[pl.BlockSpec((tk, R * HEAD_DIM), rowblk), cast_spec(cast_a), cast_spec(cast_b)],
        out_shape=[jax.ShapeDtypeStruct((B * T, G * R * HEAD_DIM), bf16), jax.ShapeDtypeStruct(cast_a.shape, bf16),
                   jax.ShapeDtypeStruct(cast_b.shape, bf16)],
        scratch_shapes=[
            pltpu.VMEM((T // tk, n_sub * R * tq, tk), f32),
            pltpu.VMEM((n_sub * R * tq, LANES), f32),
            pltpu.VMEM((n_sub * R * tq, LANES), f32),
            pltpu.VMEM((n_sub * R * tq, HEAD_DIM), f32),
        ],
        compiler_params=pltpu.CompilerParams(
            dimension_semantics=("arbitrary", "arbitrary", "arbitrary"), vmem_limit_bytes=VMEM_LIMIT),
        name="nsa",
    )(q, kcmp, vcmp, kv, kv, kv, kv, eneg, gates, cast_a, cast_b)


def _out_proj_kernel(x_ref, u_ref, uh_ref, y_ref, pw_ref, ps_ref, wo_ref, o_ref, *, tm, T):
    i = pl.program_id(0)
    tiles_per_seq = T // tm
    keep_halo = jnp.where((i % tiles_per_seq) == 0, 0.0, 1.0)
    u = u_ref[...]
    halo = uh_ref[...]
    tt = lax.broadcasted_iota(jnp.int32, (tm, tm), 0)
    ss = lax.broadcasted_iota(jnp.int32, (tm, tm), 1)
    th = lax.broadcasted_iota(jnp.int32, (tm, POOL_HALO), 0)
    rh = lax.broadcasted_iota(jnp.int32, (tm, POOL_HALO), 1)
    tseq = (i % tiles_per_seq) * tm + lax.broadcasted_iota(jnp.int32, (tm, 1), 0)
    n_pool = u.shape[1]
    gd = n_pool // len(POOL_WINDOWS)
    acc = x_ref[...] + _dot(y_ref[...], wo_ref[n_pool:, :])
    for gi, w in enumerate(POOL_WINDOWS):
        cols = slice(gi * gd, (gi + 1) * gd)
        band = jnp.where((tt - ss >= 0) & (tt - ss < w), 1.0, 0.0).astype(bf16)
        band_h = jnp.where(th + POOL_HALO - rh < w, 1.0, 0.0).astype(bf16)
        ug = u[:, cols]
        win_sum = _dot(band, ug) + keep_halo * _dot(band_h, halo[:, cols])
        count = jnp.minimum(tseq + 1, w).astype(f32)
        pooled = win_sum / count - ug.astype(f32)
        yg = _dot(pooled.astype(bf16), pw_ref[gi]) * ps_ref[:, cols]
        acc = acc + _dot(yg.astype(bf16), wo_ref[cols, :])
    o_ref[...] = acc


def _out_proj(x2, u, y_nsa, pool_w, pool_scale, w_out, *, T, tm):
    n_rows, d = x2.shape
    n_pool = u.shape[1]
    hb = tm // POOL_HALO
    row = lambda i: (i, 0)
    const2 = lambda i: (0, 0)
    return pl.pallas_call(
        functools.partial(_out_proj_kernel, tm=tm, T=T),
        grid=(n_rows // tm,),
        in_specs=[
            pl.BlockSpec((tm, d), row),
            pl.BlockSpec((tm, n_pool), row),
            pl.BlockSpec((POOL_HALO, n_pool), lambda i: (jnp.maximum(i * hb - 1, 0), 0)),
            pl.BlockSpec((tm, y_nsa.shape[1]), row),
            pl.BlockSpec(pool_w.shape, lambda i: (0, 0, 0)),
            pl.BlockSpec((1, n_pool), const2),
            pl.BlockSpec(w_out.shape, const2, pipeline_mode=pl.Buffered(1)),
        ],
        out_specs=pl.BlockSpec((tm, d), row),
        out_shape=jax.ShapeDtypeStruct((n_rows, d), f32),
        compiler_params=pltpu.CompilerParams(
            dimension_semantics=("arbitrary",), vmem_limit_bytes=VMEM_LIMIT),
        name="out_proj",
    )(x2, u, u, y_nsa, pool_w, pool_scale, w_out)


def _ffn_kernel(h_ref, g_ref, wg_ref, wu_ref, wd_ref, o_ref, n_sc):
    @pl.when(pl.program_id(1) == 0)
    def _():
        h = h_ref[...]
        n_sc[...] = _rms(h, g_ref[...]).astype(bf16)
        o_ref[...] = h

    n = n_sc[...]
    half = wg_ref.shape[1] // 2
    acc = None
    for c in range(2):
        cols = slice(c * half, (c + 1) * half)
        gate = _dot(n, wg_ref[:, cols])
        up = _dot(n, wu_ref[:, cols])
        a = gate * jax.nn.sigmoid(gate) * up
        d = _dot(a.astype(bf16), wd_ref[cols, :])
        acc = d if acc is None else acc + d
    o_ref[...] += acc


def _ffn(h, g, w_gate, w_up, w_down, *, tm, tf):
    n_rows, d = h.shape
    d_ff = w_gate.shape[1]
    return pl.pallas_call(
        _ffn_kernel,
        grid=(n_rows // tm, d_ff // tf),
        in_specs=[
            pl.BlockSpec((tm, d), lambda i, f: (i, 0)),
            pl.BlockSpec((1, d), lambda i, f: (0, 0)),
            pl.BlockSpec((d, tf), lambda i, f: (0, f)),
            pl.BlockSpec((d, tf), lambda i, f: (0, f)),
            pl.BlockSpec((tf, d), lambda i, f: (f, 0)),
        ],
        out_specs=pl.BlockSpec((tm, d), lambda i, f: (i, 0)),
        out_shape=jax.ShapeDtypeStruct((n_rows, d), f32),
        scratch_shapes=[pltpu.VMEM((tm, d), bf16)],
        compiler_params=pltpu.CompilerParams(
            dimension_semantics=("arbitrary", "arbitrary"), vmem_limit_bytes=VMEM_LIMIT),
        name="ffn",
    )(h, g, w_gate, w_up, w_down)


def _ple_kernel(h_ref, p_ref, g_ref, wg_ref, wp_ref, fg_ref, o_ref):
    h = h_ref[...]
    n = _rms(h, g_ref[...]).astype(bf16)
    gate = jax.nn.sigmoid(_dot(n, wg_ref[...]))
    h = h + _dot(p_ref[...].astype(bf16), wp_ref[...]) * gate
    o_ref[...] = _rms(h, fg_ref[...])


def _ple(h, p2, g, w_gate, w_proj, fg, *, tm):
    n_rows, d = h.shape
    row = lambda i: (i, 0)
    const = lambda i: (0, 0)
    return pl.pallas_call(
        _ple_kernel,
        grid=(n_rows // tm,),
        in_specs=[
            pl.BlockSpec((tm, d), row),
            pl.BlockSpec((tm, p2.shape[1]), row),
            pl.BlockSpec((1, d), const),
            pl.BlockSpec(w_gate.shape, const, pipeline_mode=pl.Buffered(1)),
            pl.BlockSpec(w_proj.shape, const),
            pl.BlockSpec((1, d), const),
        ],
        out_specs=pl.BlockSpec((tm, d), row),
        out_shape=jax.ShapeDtypeStruct((n_rows, d), f32),
        compiler_params=pltpu.CompilerParams(
            dimension_semantics=("arbitrary",), vmem_limit_bytes=VMEM_LIMIT),
        name="ple",
    )(h, p2, g, w_gate, w_proj, fg)


def _rope_tables(T):
    pos = jnp.arange(T, dtype=f32)
    inv_freq = ROPE_THETA ** (-jnp.arange(0, ROPE_DIM, 2, dtype=f32) / ROPE_DIM)
    ang = pos[:, None] * inv_freq[None, :]
    cos, sin = jnp.cos(ang), jnp.sin(ang)
    rest = HEAD_DIM - ROPE_DIM
    cosf = jnp.concatenate([cos, cos, jnp.ones((T, rest), f32)], axis=1)
    sinf = jnp.concatenate([-sin, sin, jnp.zeros((T, rest), f32)], axis=1)
    return cosf, sinf


def _mixer_ffn(h2, in_norm_g, w_in, pool_w, pool_scale, cmp_k_pe, cmp_k_w1, cmp_k_w2,
               cmp_v_pe, cmp_v_w1, cmp_v_w2, w_out, ffn_norm_g, w_gate, w_up, w_down, w_ple_gate, *, B, T):
    n_pool = pool_scale.shape[0]
    n_heads = N_KV_GROUPS * HEADS_PER_GROUP
    nsa_w = n_heads * HEAD_DIM
    kv_w = N_KV_GROUPS * HEAD_DIM
    assert w_in.shape[1] == n_pool + nsa_w + 6 * kv_w + n_heads * N_BRANCH and n_pool == 1024 and nsa_w == 1024

    o0 = n_pool + nsa_w + 6 * kv_w
    w_p = w_in.T
    w_gt = jnp.pad(w_p[o0:].astype(bf16), ((0, LANES - n_heads * N_BRANCH), (0, 0)))

    cosf, sinf = _rope_tables(T)
    u, q, kcvc, kv, gates, w_out_b, w_gate_b, w_ple_gate_b = _in_proj(
        h2, in_norm_g[None, :], w_p, w_gt, cosf, sinf, w_out, w_gate, w_ple_gate, T=T, tm=512)

    kcmp, vcmp = _compress(
        kcvc.reshape(B, T, kcvc.shape[1]),
        cmp_k_pe.reshape(1, -1), cmp_k_w1.astype(bf16), cmp_k_w2.astype(bf16),
        cmp_v_pe.reshape(1, -1), cmp_v_w1.astype(bf16), cmp_v_w2.astype(bf16))

    key_blk = jnp.arange(T, dtype=jnp.int32)[:, None] // SEL_BLOCK
    eneg = jnp.where(key_blk == jnp.arange(LANES, dtype=jnp.int32)[None, :], NEG, 0.0).astype(bf16)
    y_nsa, w_up_b, w_down_b = _nsa(q, kcmp, vcmp, kv, eneg, gates, w_up, w_down, B=B, T=T, tq=256, n_sub=2)

    h2 = _out_proj(h2, u, y_nsa, pool_w.astype(bf16), pool_scale[None, :], w_out_b, T=T, tm=512)
    return _ffn(h2, ffn_norm_g[None, :], w_gate_b, w_up_b, w_down_b, tm=1024, tf=512), w_ple_gate_b


def kernel(x, p, in_norm_g, w_in, pool_w, pool_scale, cmp_k_pe, cmp_k_w1, cmp_k_w2, cmp_v_pe, cmp_v_w1,
           cmp_v_w2, w_out, ffn_norm_g, w_gate, w_up, w_down, ple_norm_g, w_ple_gate, w_ple_proj, final_norm_g):
    B, T, d = x.shape
    depth = w_in.shape[0]
    assert depth == 1, "the final rmsnorm is fused into the last layer's per-layer-embedding kernel"
    h2 = x.reshape(B * T, d)
    i = 0
    h2, w_ple_gate_b = _mixer_ffn(
        h2, in_norm_g[i], w_in[i], pool_w[i], pool_scale[i], cmp_k_pe[i], cmp_k_w1[i], cmp_k_w2[i],
        cmp_v_pe[i], cmp_v_w1[i], cmp_v_w2[i], w_out[i], ffn_norm_g[i], w_gate[i], w_up[i], w_down[i],
        w_ple_gate[i], B=B, T=T)
    out = _ple(h2, p[i].reshape(B * T, -1), ple_norm_g[i][None, :], w_ple_gate_b,
               w_ple_proj[i].astype(bf16), final_norm_g[None, :], tm=512)
    return out.reshape(B, T, d)
```

```python
import functools

import jax
import jax.numpy as jnp
from jax import lax
from jax.experimental import pallas as pl
from jax.experimental.pallas import tpu as pltpu

EPS = 1e-6
NEG = -1e30
LANES = 128
POOL_WINDOWS = (2, 4, 8, 16)
POOL_HALO = 16
HEAD_DIM = 128
N_KV_GROUPS = 2
HEADS_PER_GROUP = 4
N_BRANCH = 3
CMP_BLOCK = 32
CMP_STRIDE = 16
SEL_BLOCK = 64
N_SELECT = 16
WINDOW = 512
ROPE_THETA = 500000.0
ROPE_DIM = HEAD_DIM // 4
ATTN_SCALE = HEAD_DIM ** -0.5
Q_SCALE = ATTN_SCALE * 1.4426950408889634
VMEM_LIMIT = 60 * 1024 * 1024
W_CHUNK = 256

f32 = jnp.float32
bf16 = jnp.bfloat16


def _dot(a, b):
    return jnp.dot(a, b, preferred_element_type=f32)


def _dot_nt(a, b):
    return lax.dot_general(a, b, (((1,), (1,)), ((), ())), preferred_element_type=f32)


def _rms(x, g):
    return x * lax.rsqrt(jnp.mean(x * x, axis=-1, keepdims=True) + EPS) * g


def _rope(h, cosf, sinf, lane):
    half = ROPE_DIM // 2
    partner = jnp.where(lane < half, pltpu.roll(h, LANES - half, axis=1), pltpu.roll(h, half, axis=1))
    return h * cosf + partner * sinf


def _w_chunk_copy(w_hbm, stage_sc, sem, c):
    return pltpu.make_async_copy(w_hbm.at[pl.ds(c * W_CHUNK, W_CHUNK), :], stage_sc.at[c % 2], sem.at[c % 2])


def _in_proj_kernel(x_ref, g_ref, w_hbm, wgt_ref, cos_ref, sin_ref, ca_ref, cb_ref, cc_ref,
                    u_ref, q_ref, kcvc_ref, kv_ref, gt_ref, ca_out, cb_out, cc_out,
                    w_ref, stage_sc, sem):
    @pl.when(pl.program_id(0) == 0)
    def _():
        n_chunk = w_ref.shape[0] // W_CHUNK
        _w_chunk_copy(w_hbm, stage_sc, sem, 0).start()
        for c in range(n_chunk):
            if c + 1 < n_chunk:
                _w_chunk_copy(w_hbm, stage_sc, sem, c + 1).start()
            _w_chunk_copy(w_hbm, stage_sc, sem, c).wait()
            w_ref[c * W_CHUNK:(c + 1) * W_CHUNK, :] = stage_sc[c % 2].astype(bf16)

    ca_out[...] = ca_ref[...].astype(bf16)
    cb_out[...] = cb_ref[...].astype(bf16)
    cc_out[...] = cc_ref[...].astype(bf16)
    a = _rms(x_ref[...], g_ref[...]).astype(bf16)
    cosf = cos_ref[...]
    sinf = sin_ref[...]
    lane = lax.broadcasted_iota(jnp.int32, cosf.shape, 1)
    nu = u_ref.shape[1]
    nq = q_ref.shape[1]
    off = 0
    for c in range(0, nu, 512):
        u_ref[:, c:c + 512] = _dot_nt(a, w_ref[off + c:off + c + 512, :]).astype(bf16)
    off += nu
    for c in range(0, nq, 512):
        acc = _dot_nt(a, w_ref[off + c:off + c + 512, :])
        for j in range(4):
            h = _rope(acc[:, j * 128:(j + 1) * 128], cosf, sinf, lane) * Q_SCALE
            q_ref[:, c + j * 128:c + (j + 1) * 128] = h.astype(bf16)
    off += nq
    acc = _dot_nt(a, w_ref[off:off + 512, :])
    for j in range(4):
        h = acc[:, j * 128:(j + 1) * 128]
        if j < 2:
            h = _rope(h, cosf, sinf, lane)
        kcvc_ref[:, j * 128:(j + 1) * 128] = h
    off += 512
    acc = _dot_nt(a, w_ref[off:off + 512, :])
    for j in range(4):
        h = acc[:, j * 128:(j + 1) * 128]
        if j < 2:
            h = _rope(h, cosf, sinf, lane)
        kv_ref[:, j * 128:(j + 1) * 128] = h.astype(bf16)
    off += 512
    acc = _dot_nt(a, w_ref[off:off + 512, :])
    for j in range(4):
        h = acc[:, j * 128:(j + 1) * 128]
        if j < 2:
            h = _rope(h, cosf, sinf, lane)
        kv_ref[:, 512 + j * 128:512 + (j + 1) * 128] = h.astype(bf16)
    off += 512
    gt_ref[...] = _dot_nt(a, wgt_ref[...])


def _in_proj(x2, g, w_p, w_gt, cosf, sinf, cast_a, cast_b, cast_c, *, T, tm):
    n_rows, d = x2.shape
    n_w = (w_p.shape[0] // W_CHUNK) * W_CHUNK
    assert w_p.dtype == f32 and n_w + w_gt.shape[0] >= w_p.shape[0]
    tiles_per_seq = T // tm
    n_steps = n_rows // tm
    row = lambda i: (i, 0)
    cast_spec = lambda w: pl.BlockSpec((w.shape[0] // n_steps, w.shape[1]), row)
    const = lambda i: (0, 0)
    tab = lambda i: (i % tiles_per_seq, 0)
    return pl.pallas_call(
        _in_proj_kernel,
        grid=(n_rows // tm,),
        in_specs=[
            pl.BlockSpec((tm, d), row),
            pl.BlockSpec((1, d), const),
            pl.BlockSpec(memory_space=pl.ANY),
            pl.BlockSpec((LANES, d), const),
            pl.BlockSpec((tm, LANES), tab),
            pl.BlockSpec((tm, LANES), tab),
            cast_spec(cast_a), cast_spec(cast_b), cast_spec(cast_c),
        ],
        out_specs=[
            pl.BlockSpec((tm, 1024), row),
            pl.BlockSpec((tm, 1024), row),
            pl.BlockSpec((tm, 512), row),
            pl.BlockSpec((tm, 1024), row),
            pl.BlockSpec((tm, LANES), row),
            cast_spec(cast_a), cast_spec(cast_b), cast_spec(cast_c),
        ],
        out_shape=[
            jax.ShapeDtypeStruct((n_rows, 1024), bf16),
            jax.ShapeDtypeStruct((n_rows, 1024), bf16),
            jax.ShapeDtypeStruct((n_rows, 512), f32),
            jax.ShapeDtypeStruct((n_rows, 1024), bf16),
            jax.ShapeDtypeStruct((n_rows, LANES), f32),
            jax.ShapeDtypeStruct(cast_a.shape, bf16),
            jax.ShapeDtypeStruct(cast_b.shape, bf16),
            jax.ShapeDtypeStruct(cast_c.shape, bf16),
        ],
        scratch_shapes=[
            pltpu.VMEM((n_w, d), bf16),
            pltpu.VMEM((2, W_CHUNK, d), f32),
            pltpu.SemaphoreType.DMA((2,)),
        ],
        compiler_params=pltpu.CompilerParams(
            dimension_semantics=("arbitrary",), vmem_limit_bytes=VMEM_LIMIT),
        name="in_proj",
    )(x2, g, w_p, w_gt, cosf, sinf, cast_a, cast_b, cast_c)


def _gelu_tanh(x):
    return 0.5 * x * (1.0 + jnp.tanh(0.7978845608028654 * (x + 0.044715 * (x * x * x))))


def _compress_kernel(kc0_ref, kc1_ref, vc0_ref, vc1_ref, pek_ref, w1k_ref, w2k_ref, pev_ref, w1v_ref, w2v_ref,
                     kc_ref, vc_ref):
    bb = kc0_ref.shape[0]
    n_rows = kc0_ref.shape[1] // CMP_STRIDE
    half_k = (CMP_BLOCK // 2) * HEAD_DIM
    n_stack = bb * N_KV_GROUPS * n_rows
    rowi = lax.broadcasted_iota(jnp.int32, (n_stack, HEAD_DIM), 0)
    for srcs, pe_ref, w1_ref, w2_ref, o_ref in (((kc0_ref, kc1_ref), pek_ref, w1k_ref, w2k_ref, kc_ref),
                                                ((vc0_ref, vc1_ref), pev_ref, w1v_ref, w2v_ref, vc_ref)):
        x = jnp.concatenate(
            [jnp.concatenate([src[b, pl.ds(l, n_rows, stride=CMP_STRIDE), :] for l in range(CMP_STRIDE)], axis=1)
             for b in range(bb) for src in srcs], axis=0)
        za = _dot((x + pe_ref[:, :half_k]).astype(bf16), w1_ref[:half_k, :])
        zb = _dot((x + pe_ref[:, half_k:]).astype(bf16), w1_ref[half_k:, :])
        h = za + pltpu.roll(zb, n_stack - 1, axis=0)
        o = _dot(_gelu_tanh(h).astype(bf16), w2_ref[...])
        o = jnp.where((rowi & (n_rows - 1)) < n_rows - 1, o, 0.0).astype(bf16)
        for b in range(bb):
            for g in range(N_KV_GROUPS):
                r0 = (b * N_KV_GROUPS + g) * n_rows
                o_ref[b, g] = o[r0:r0 + n_rows]


def _compress(r3, pek, w1k, w2k, pev, w1v, w2v, *, bb):
    B, T, width = r3.shape
    n_rows = T // CMP_STRIDE
    assert width == 2 * N_KV_GROUPS * HEAD_DIM and N_KV_GROUPS == 2 and B % bb == 0 and n_rows & (n_rows - 1) == 0
    const = lambda b: (0, 0)
    out = jax.ShapeDtypeStruct((B, N_KV_GROUPS, n_rows, HEAD_DIM), bf16)
    ospec = pl.BlockSpec((bb, N_KV_GROUPS, n_rows, HEAD_DIM), lambda b: (b, 0, 0, 0))
    sec = lambda s: pl.BlockSpec((bb, T, HEAD_DIM), lambda b: (b, 0, s))
    return pl.pallas_call(
        _compress_kernel,
        grid=(B // bb,),
        in_specs=[
            sec(0), sec(1), sec(2), sec(3),
            pl.BlockSpec(pek.shape, const), pl.BlockSpec(w1k.shape, const), pl.BlockSpec(w2k.shape, const),
            pl.BlockSpec(pev.shape, const), pl.BlockSpec(w1v.shape, const), pl.BlockSpec(w2v.shape, const),
        ],
        out_specs=[ospec, ospec],
        out_shape=[out, out],
        compiler_params=pltpu.CompilerParams(
            dimension_semantics=("arbitrary",), vmem_limit_bytes=VMEM_LIMIT),
        name="compress",
    )(r3, r3, r3, r3, pek, w1k, w2k, pev, w1v, w2v)


def _nsa_front(qs, t0, kc, vc, kw_ref, vw_ref, *, tq, T):
    R = HEADS_PER_GROUP
    M = R * tq
    n_cmp = kc.shape[0]
    n_sel = T // SEL_BLOCK

    wk = WINDOW + tq
    w0 = pl.multiple_of(jnp.maximum(t0 - WINDOW, 0), tq)
    sw = _dot_nt(qs, kw_ref[pl.ds(w0, wk), :])
    diff = (t0 - w0) + (lax.broadcasted_iota(jnp.int32, (M, wk), 0) & (tq - 1)) \
        - lax.broadcasted_iota(jnp.int32, (M, wk), 1)
    sw = jnp.where((diff & -WINDOW) == 0, sw, NEG)
    ew = jnp.exp2(sw - jnp.max(sw, axis=1, keepdims=True))
    o_win = _dot(ew.astype(bf16), vw_ref[pl.ds(w0, wk), :]) / jnp.sum(ew, axis=1, keepdims=True)

    s = _dot_nt(qs, kc)
    trow = t0 + (lax.broadcasted_iota(jnp.int32, (M, n_cmp), 0) & (tq - 1))
    ncol = lax.broadcasted_iota(jnp.int32, (M, n_cmp), 1)
    valid = (CMP_STRIDE * ncol + (CMP_BLOCK - 1)) <= trow
    s = jnp.where(valid, s, NEG)
    e = jnp.exp2(s - jnp.max(s, axis=1, keepdims=True))
    p = jnp.where(valid, e / jnp.sum(e, axis=1, keepdims=True), 0.0)
    o_cmp = _dot(p.astype(bf16), vc)

    psum = p[0:tq]
    for r in range(1, R):
        psum = psum + p[r * tq:(r + 1) * tq]
    p_hi = psum.astype(bf16)
    p_lo = (psum - p_hi.astype(f32)).astype(bf16)
    sb = lax.broadcasted_iota(jnp.int32, (n_sel, n_cmp), 0) * SEL_BLOCK
    cb = lax.broadcasted_iota(jnp.int32, (n_sel, n_cmp), 1) * CMP_STRIDE
    ov_t = jnp.where((cb < sb + SEL_BLOCK) & (cb + CMP_BLOCK > sb), 1.0, 0.0).astype(bf16)
    imp_t = _dot_nt(ov_t, p_hi) + _dot_nt(ov_t, p_lo)

    jj = lax.broadcasted_iota(jnp.int32, (n_sel, tq), 0)
    cur = jnp.right_shift(t0 + lax.broadcasted_iota(jnp.int32, (n_sel, tq), 1), SEL_BLOCK.bit_length() - 1)
    forced = (jj == 0) | (jj == cur) | (jj == cur - 1)
    v = jnp.where(jj > cur, -jnp.inf, jnp.where(forced, jnp.inf, imp_t))
    sub = 8
    slabs = [v[k * sub:(k + 1) * sub] for k in range(n_sel // sub)]
    ranks = [jnp.zeros((sub, tq), f32) for _ in slabs]
    j_in = lax.broadcasted_iota(jnp.int32, (sub, tq), 0)
    for c in range(n_sel):
        v_c = v[c:c + 1, :]
        for k, v_k in enumerate(slabs):
            if k * sub > c:
                ahead = jnp.where(v_c >= v_k, 1.0, 0.0)
            elif (k + 1) * sub <= c:
                ahead = jnp.where(v_c > v_k, 1.0, 0.0)
            else:
                ahead = jnp.where(j_in > c - k * sub, jnp.where(v_c >= v_k, 1.0, 0.0),
                                  jnp.where(v_c > v_k, 1.0, 0.0))
            ranks[k] = ranks[k] + ahead
    rank = jnp.concatenate(ranks, axis=0)
    unsel_t = jnp.where(rank < float(min(N_SELECT, n_sel)), 0.0, 1.0)
    unsel_t = jnp.concatenate([unsel_t, jnp.zeros((LANES - n_sel, tq), f32)], axis=0)
    return o_win, o_cmp, unsel_t.T.astype(bf16)


def _nsa_kernel(q_ref, kc_ref, vc_ref, ks_ref, vs_ref, kw_ref, vw_ref, eneg_ref, gt_ref, ca_ref, cb_ref,
                o_ref, ca_out, cb_out, s_sc, mx_sc, ls_sc, acc_sc, *, tq, n_sub, T):
    ca_out[...] = ca_ref[...].astype(bf16)
    cb_out[...] = cb_ref[...].astype(bf16)
    R = HEADS_PER_GROUP
    M = R * tq
    tk = n_sub * tq
    i = pl.program_id(2)
    t0 = i * tk
    kc = kc_ref[0, 0]
    vc = vc_ref[0, 0]

    fronts = []
    q_rows = []
    for sb in range(n_sub):
        q = q_ref[sb * tq:(sb + 1) * tq, :]
        qs = jnp.concatenate([q[:, r * HEAD_DIM:(r + 1) * HEAD_DIM] for r in range(R)], axis=0)
        o_win, o_cmp, unsel = _nsa_front(qs, t0 + sb * tq, kc, vc, kw_ref, vw_ref, tq=tq, T=T)
        fronts.append((o_win, o_cmp))
        q_rows.append(jnp.concatenate([qs, jnp.concatenate([unsel] * R, axis=0)], axis=1))
    q_aug = jnp.concatenate(q_rows, axis=0)
    MM = n_sub * M

    n_chunk = tk // LANES
    mx_sc[...] = jnp.full(mx_sc.shape, NEG, f32)

    def score_step(kt, causal):
        k0 = pl.multiple_of(kt * tk, tk)
        k_aug = jnp.concatenate([ks_ref[pl.ds(k0, tk), :], eneg_ref[pl.ds(k0, tk), :]], axis=1)
        sc = _dot_nt(q_aug, k_aug)
        if causal:
            rowi = lax.broadcasted_iota(jnp.int32, (MM, tk), 0)
            qoff = jnp.right_shift(rowi, M.bit_length() - 1) * tq + (rowi & (tq - 1))
            sc = jnp.where(qoff >= lax.broadcasted_iota(jnp.int32, (MM, tk), 1), sc, NEG)
        s_sc[kt] = sc
        part = sc[:, :LANES]
        for c in range(1, n_chunk):
            part = jnp.maximum(part, sc[:, c * LANES:(c + 1) * LANES])
        mx_sc[...] = jnp.maximum(mx_sc[...], part)

    def score_body(kt, carry):
        score_step(kt, False)
        return carry

    lax.fori_loop(0, i, score_body, 0)
    score_step(i, True)
    mx_sc[...] = jnp.broadcast_to(jnp.max(mx_sc[...], axis=1, keepdims=True), mx_sc.shape)
    ls_sc[...] = jnp.zeros(ls_sc.shape, f32)
    acc_sc[...] = jnp.zeros(acc_sc.shape, f32)

    def pv_body(kt, carry):
        k0 = pl.multiple_of(kt * tk, tk)
        sc = s_sc[kt]
        mb = mx_sc[...]
        ps = [jnp.exp2(sc[:, c * LANES:(c + 1) * LANES] - mb) for c in range(n_chunk)]
        ls_sc[...] += functools.reduce(lambda a, b: a + b, ps)
        acc_sc[...] += _dot(jnp.concatenate(ps, axis=1).astype(bf16), vs_ref[pl.ds(k0, tk), :])
        return carry

    lax.fori_loop(0, i + 1, pv_body, 0)
    o_sel = acc_sc[...] / jnp.sum(ls_sc[...], axis=1, keepdims=True)

    gt_all = jax.nn.sigmoid(gt_ref[...])
    gt_all = jnp.where(pl.program_id(1) == 0, gt_all, pltpu.roll(gt_all, LANES - R * N_BRANCH, axis=1))
    for sb, (o_win, o_cmp) in enumerate(fronts):
        gt = gt_all[sb * tq:(sb + 1) * tq]
        for r in range(R):
            rows = slice(r * tq, (r + 1) * tq)
            srows = slice(sb * M + r * tq, sb * M + (r + 1) * tq)
            c = r * N_BRANCH
            o = (gt[:, c:c + 1] * o_cmp[rows] + gt[:, c + 1:c + 2] * o_sel[srows]
                 + gt[:, c + 2:c + 3] * o_win[rows])
            o_ref[sb * tq:(sb + 1) * tq, r * HEAD_DIM:(r + 1) * HEAD_DIM] = o.astype(bf16)


def _nsa(q, kcmp, vcmp, kv, eneg, gates, cast_a, cast_b, *, B, T, tq, n_sub):
    G, R = N_KV_GROUPS, HEADS_PER_GROUP
    tk = n_sub * tq
    nt = T // tk
    n_cmp = kcmp.shape[2]
    assert tq % LANES == 0 and tq & (tq - 1) == 0 and T % tk == 0 and T >= WINDOW + tq
    assert WINDOW % tq == 0 and WINDOW & (WINDOW - 1) == 0 and (R * tq) & (R * tq - 1) == 0
    rowblk = lambda b, g, i: (b * nt + i, g)
    cmp_spec = pl.BlockSpec((1, 1, n_cmp, HEAD_DIM), lambda b, g, i: (b, g, 0, 0))
    kvspec = lambda sec: pl.BlockSpec((T, HEAD_DIM), lambda b, g, i: (b, sec * G + g))
    n_steps = B * G * nt
    cast_spec = lambda w: pl.BlockSpec((w.shape[0] // n_steps, w.shape[1]), lambda b, g, i: ((b * G + g) * nt + i, 0))
    return pl.pallas_call(
        functools.partial(_nsa_kernel, tq=tq, n_sub=n_sub, T=T),
        grid=(B, G, nt),
        in_specs=[
            pl.BlockSpec((tk, R * HEAD_DIM), rowblk),
            cmp_spec, cmp_spec,
            kvspec(0), kvspec(1), kvspec(2), kvspec(3),
            pl.BlockSpec((T, LANES), lambda b, g, i: (0, 0)),
            pl.BlockSpec((tk, LANES), lambda b, g, i: (b * nt + i, 0)),
            cast_spec(cast_a), cast_spec(cast_b),
        ],
        out_specs=[pl.BlockSpec((tk, R * HEAD_DIM), rowblk), cast_spec(cast_a), cast_spec(cast_b)],
        out_shape=[jax.ShapeDtypeStruct((B * T, G * R * HEAD_DIM), bf16), jax.ShapeDtypeStruct(cast_a.shape, bf16),
                   jax.ShapeDtypeStruct(cast_b.shape, bf16)],
        scratch_shapes=[
            pltpu.VMEM((T // tk, n_sub * R * tq, tk), f32),
            pltpu.VMEM((n_sub * R * tq, LANES), f32),
            pltpu.VMEM((n_sub * R * tq, LANES), f32),
            pltpu.VMEM((n_sub * R * tq, HEAD_DIM), f32),
        ],
        compiler_params=pltpu.CompilerParams(
            dimension_semantics=("arbitrary", "arbitrary", "arbitrary"), vmem_limit_bytes=VMEM_LIMIT),
        name="nsa",
    )(q, kcmp, vcmp, kv, kv, kv, kv, eneg, gates, cast_a, cast_b)


def _out_proj_kernel(x_ref, u_ref, uh_ref, y_ref, pw_ref, ps_ref, wo_ref, o_ref, *, tm, T):
    i = pl.program_id(0)
    tiles_per_seq = T // tm
    keep_halo = jnp.where((i % tiles_per_seq) == 0, 0.0, 1.0)
    u = u_ref[...]
    halo = uh_ref[...]
    tt = lax.broadcasted_iota(jnp.int32, (tm, tm), 0)
    ss = lax.broadcasted_iota(jnp.int32, (tm, tm), 1)
    th = lax.broadcasted_iota(jnp.int32, (tm, POOL_HALO), 0)
    rh = lax.broadcasted_iota(jnp.int32, (tm, POOL_HALO), 1)
    tseq = (i % tiles_per_seq) * tm + lax.broadcasted_iota(jnp.int32, (tm, 1), 0)
    n_pool = u.shape[1]
    gd = n_pool // len(POOL_WINDOWS)
    acc = x_ref[...] + _dot(y_ref[...], wo_ref[n_pool:, :])
    for gi, w in enumerate(POOL_WINDOWS):
        cols = slice(gi * gd, (gi + 1) * gd)
        band = jnp.where((tt - ss >= 0) & (tt - ss < w), 1.0, 0.0).astype(bf16)
        band_h = jnp.where(th + POOL_HALO - rh < w, 1.0, 0.0).astype(bf16)
        ug = u[:, cols]
        win_sum = _dot(band, ug) + keep_halo * _dot(band_h, halo[:, cols])
        count = jnp.minimum(tseq + 1, w).astype(f32)
        pooled = win_sum / count - ug.astype(f32)
        yg = _dot(pooled.astype(bf16), pw_ref[gi]) * ps_ref[:, cols]
        acc = acc + _dot(yg.astype(bf16), wo_ref[cols, :])
    o_ref[...] = acc


def _out_proj(x2, u, y_nsa, pool_w, pool_scale, w_out, *, T, tm):
    n_rows, d = x2.shape
    n_pool = u.shape[1]
    hb = tm // POOL_HALO
    row = lambda i: (i, 0)
    const2 = lambda i: (0, 0)
    return pl.pallas_call(
        functools.partial(_out_proj_kernel, tm=tm, T=T),
        grid=(n_rows // tm,),
        in_specs=[
            pl.BlockSpec((tm, d), row),
            pl.BlockSpec((tm, n_pool), row),
            pl.BlockSpec((POOL_HALO, n_pool), lambda i: (jnp.maximum(i * hb - 1, 0), 0)),
            pl.BlockSpec((tm, y_nsa.shape[1]), row),
            pl.BlockSpec(pool_w.shape, lambda i: (0, 0, 0)),
            pl.BlockSpec((1, n_pool), const2),
            pl.BlockSpec(w_out.shape, const2, pipeline_mode=pl.Buffered(1)),
        ],
        out_specs=pl.BlockSpec((tm, d), row),
        out_shape=jax.ShapeDtypeStruct((n_rows, d), f32),
        compiler_params=pltpu.CompilerParams(
            dimension_semantics=("arbitrary",), vmem_limit_bytes=VMEM_LIMIT),
        name="out_proj",
    )(x2, u, u, y_nsa, pool_w, pool_scale, w_out)


def _ffn_kernel(h_ref, g_ref, wg_ref, wu_ref, wd_ref, o_ref, n_sc):
    @pl.when(pl.program_id(1) == 0)
    def _():
        h = h_ref[...]
        n_sc[...] = _rms(h, g_ref[...]).astype(bf16)
        o_ref[...] = h

    n = n_sc[...]
    half = wg_ref.shape[1] // 2
    acc = None
    for c in range(2):
        cols = slice(c * half, (c + 1) * half)
        gate = _dot(n, wg_ref[:, cols])
        up = _dot(n, wu_ref[:, cols])
        a = gate * jax.nn.sigmoid(gate) * up
        d = _dot(a.astype(bf16), wd_ref[cols, :])
        acc = d if acc is None else acc + d
    o_ref[...] += acc


def _ffn(h, g, w_gate, w_up, w_down, *, tm, tf):
    n_rows, d = h.shape
    d_ff = w_gate.shape[1]
    return pl.pallas_call(
        _ffn_kernel,
        grid=(n_rows // tm, d_ff // tf),
        in_specs=[
            pl.BlockSpec((tm, d), lambda i, f: (i, 0)),
            pl.BlockSpec((1, d), lambda i, f: (0, 0)),
            pl.BlockSpec((d, tf), lambda i, f: (0, f)),
            pl.BlockSpec((d, tf), lambda i, f: (0, f)),
            pl.BlockSpec((tf, d), lambda i, f: (f, 0)),
        ],
        out_specs=pl.BlockSpec((tm, d), lambda i, f: (i, 0)),
        out_shape=jax.ShapeDtypeStruct((n_rows, d), f32),
        scratch_shapes=[pltpu.VMEM((tm, d), bf16)],
        compiler_params=pltpu.CompilerParams(
            dimension_semantics=("arbitrary", "arbitrary"), vmem_limit_bytes=VMEM_LIMIT),
        name="ffn",
    )(h, g, w_gate, w_up, w_down)


def _ple_kernel(h_ref, p_ref, g_ref, wg_ref, wp_ref, fg_ref, o_ref):
    h = h_ref[...]
    n = _rms(h, g_ref[...]).astype(bf16)
    gate = jax.nn.sigmoid(_dot(n, wg_ref[...]))
    h = h + _dot(p_ref[...].astype(bf16), wp_ref[...]) * gate
    o_ref[...] = _rms(h, fg_ref[...])


def _ple(h, p2, g, w_gate, w_proj, fg, *, tm):
    n_rows, d = h.shape
    row = lambda i: (i, 0)
    const = lambda i: (0, 0)
    return pl.pallas_call(
        _ple_kernel,
        grid=(n_rows // tm,),
        in_specs=[
            pl.BlockSpec((tm, d), row),
            pl.BlockSpec((tm, p2.shape[1]), row),
            pl.BlockSpec((1, d), const),
            pl.BlockSpec(w_gate.shape, const, pipeline_mode=pl.Buffered(1)),
            pl.BlockSpec(w_proj.shape, const),
            pl.BlockSpec((1, d), const),
        ],
        out_specs=pl.BlockSpec((tm, d), row),
        out_shape=jax.ShapeDtypeStruct((n_rows, d), f32),
        compiler_params=pltpu.CompilerParams(
            dimension_semantics=("arbitrary",), vmem_limit_bytes=VMEM_LIMIT),
        name="ple",
    )(h, p2, g, w_gate, w_proj, fg)


def _rope_tables(T):
    pos = jnp.arange(T, dtype=f32)
    inv_freq = ROPE_THETA ** (-jnp.arange(0, ROPE_DIM, 2, dtype=f32) / ROPE_DIM)
    ang = pos[:, None] * inv_freq[None, :]
    cos, sin = jnp.cos(ang), jnp.sin(ang)
    rest = HEAD_DIM - ROPE_DIM
    cosf = jnp.concatenate([cos, cos, jnp.ones((T, rest), f32)], axis=1)
    sinf = jnp.concatenate([-sin, sin, jnp.zeros((T, rest), f32)], axis=1)
    return cosf, sinf


def _mixer_ffn(h2, in_norm_g, w_in, pool_w, pool_scale, cmp_k_pe, cmp_k_w1, cmp_k_w2,
               cmp_v_pe, cmp_v_w1, cmp_v_w2, w_out, ffn_norm_g, w_gate, w_up, w_down, w_ple_gate, *, B, T):
    n_pool = pool_scale.shape[0]
    n_heads = N_KV_GROUPS * HEADS_PER_GROUP
    nsa_w = n_heads * HEAD_DIM
    kv_w = N_KV_GROUPS * HEAD_DIM
    assert w_in.shape[1] == n_pool + nsa_w + 6 * kv_w + n_heads * N_BRANCH and n_pool == 1024 and nsa_w == 1024

    o0 = n_pool + nsa_w + 6 * kv_w
    w_p = w_in.T
    w_gt = jnp.pad(w_p[o0:].astype(bf16), ((0, LANES - n_heads * N_BRANCH), (0, 0)))

    cosf, sinf = _rope_tables(T)
    u, q, kcvc, kv, gates, w_out_b, w_gate_b, w_ple_gate_b = _in_proj(
        h2, in_norm_g[None, :], w_p, w_gt, cosf, sinf, w_out, w_gate, w_ple_gate, T=T, tm=512)

    kcmp, vcmp = _compress(
        kcvc.reshape(B, T, kcvc.shape[1]),
        cmp_k_pe.reshape(1, -1), cmp_k_w1.astype(bf16), cmp_k_w2.astype(bf16),
        cmp_v_pe.reshape(1, -1), cmp_v_w1.astype(bf16), cmp_v_w2.astype(bf16), bb=2 if B % 2 == 0 else 1)

    key_blk = jnp.arange(T, dtype=jnp.int32)[:, None] // SEL_BLOCK
    eneg = jnp.where(key_blk == jnp.arange(LANES, dtype=jnp.int32)[None, :], NEG, 0.0).astype(bf16)
    y_nsa, w_up_b, w_down_b = _nsa(q, kcmp, vcmp, kv, eneg, gates, w_up, w_down, B=B, T=T, tq=256, n_sub=2)

    h2 = _out_proj(h2, u, y_nsa, pool_w.astype(bf16), pool_scale[None, :], w_out_b, T=T, tm=512)
    return _ffn(h2, ffn_norm_g[None, :], w_gate_b, w_up_b, w_down_b, tm=1024, tf=512), w_ple_gate_b


def kernel(x, p, in_norm_g, w_in, pool_w, pool_scale, cmp_k_pe, cmp_k_w1, cmp_k_w2, cmp_v_pe, cmp_v_w1,
           cmp_v_w2, w_out, ffn_norm_g, w_gate, w_up, w_down, ple_norm_g, w_ple_gate, w_ple_proj, final_norm_g):
    B, T, d = x.shape
    depth = w_in.shape[0]
    assert depth == 1, "the final rmsnorm is fused into the last layer's per-layer-embedding kernel"
    h2 = x.reshape(B * T, d)
    i = 0
    h2, w_ple_gate_b = _mixer_ffn(
        h2, in_norm_g[i], w_in[i], pool_w[i], pool_scale[i], cmp_k_pe[i], cmp_k_w1[i], cmp_k_w2[i],
        cmp_v_pe[i], cmp_v_w1[i], cmp_v_w2[i], w_out[i], ffn_norm_g[i], w_gate[i], w_up[i], w_down[i],
        w_ple_gate[i], B=B, T=T)
    out = _ple(h2, p[i].reshape(B * T, -1), ple_norm_g[i][None, :], w_ple_gate_b,
               w_ple_proj[i].astype(bf16), final_norm_g[None, :], tm=512)
    return out.reshape(B, T, d)
```

```python
import functools

import jax
import jax.numpy as jnp
from jax import lax
from jax.experimental import pallas as pl
from jax.experimental.pallas import tpu as pltpu

EPS = 1e-6
NEG = -1e30
LANES = 128
POOL_WINDOWS = (2, 4, 8, 16)
POOL_HALO = 16
HEAD_DIM = 128
N_KV_GROUPS = 2
HEADS_PER_GROUP = 4
N_BRANCH = 3
CMP_BLOCK = 32
CMP_STRIDE = 16
SEL_BLOCK = 64
N_SELECT = 16
WINDOW = 512
ROPE_THETA = 500000.0
ROPE_DIM = HEAD_DIM // 4
ATTN_SCALE = HEAD_DIM ** -0.5
Q_SCALE = ATTN_SCALE * 1.4426950408889634
VMEM_LIMIT = 60 * 1024 * 1024
W_CHUNK = 256
IN_SEC = 512

f32 = jnp.float32
bf16 = jnp.bfloat16


def _dot(a, b):
    return jnp.dot(a, b, preferred_element_type=f32)


def _dot_nt(a, b):
    return lax.dot_general(a, b, (((1,), (1,)), ((), ())), preferred_element_type=f32)


def _rms(x, g):
    return x * lax.rsqrt(jnp.mean(x * x, axis=-1, keepdims=True) + EPS) * g


def _rope(h, cosf, sinf, lane):
    half = ROPE_DIM // 2
    partner = jnp.where(lane < half, pltpu.roll(h, LANES - half, axis=1), pltpu.roll(h, half, axis=1))
    return h * cosf + partner * sinf


def _w_chunk_copy(w_hbm, stage_sc, sem, c):
    return pltpu.make_async_copy(w_hbm.at[pl.ds(c * W_CHUNK, W_CHUNK), :], stage_sc.at[c % 2], sem.at[c % 2])


def _in_proj_kernel(x_ref, g_ref, w_hbm, wgt_ref, cos_ref, sin_ref, ca_ref, cb_ref, cc_ref,
                    u_ref, q_ref, kcvc_ref, kv_ref, gt_ref, ca_out, cb_out, cc_out,
                    w_ref, stage_sc, sem):
    n_chunk = w_ref.shape[0] // W_CHUNK
    per_sec = IN_SEC // W_CHUNK
    nu = u_ref.shape[1]
    nq = q_ref.shape[1]

    def compute(stream_weights):
        if stream_weights:
            for c in range(min(2, n_chunk)):
                _w_chunk_copy(w_hbm, stage_sc, sem, c).start()

        def section(s):
            if stream_weights:
                for c in range(s * per_sec, (s + 1) * per_sec):
                    _w_chunk_copy(w_hbm, stage_sc, sem, c).wait()
                    w_ref[c * W_CHUNK:(c + 1) * W_CHUNK, :] = stage_sc[c % 2].astype(bf16)
                    if c + 2 < n_chunk:
                        _w_chunk_copy(w_hbm, stage_sc, sem, c + 2).start()
            return w_ref[s * IN_SEC:(s + 1) * IN_SEC, :]

        ca_out[...] = ca_ref[...].astype(bf16)
        cb_out[...] = cb_ref[...].astype(bf16)
        cc_out[...] = cc_ref[...].astype(bf16)
        a = _rms(x_ref[...], g_ref[...]).astype(bf16)
        cosf = cos_ref[...]
        sinf = sin_ref[...]
        lane = lax.broadcasted_iota(jnp.int32, cosf.shape, 1)
        heads = IN_SEC // HEAD_DIM
        s = 0
        for c in range(0, nu, IN_SEC):
            u_ref[:, c:c + IN_SEC] = _dot_nt(a, section(s)).astype(bf16)
            s += 1
        for c in range(0, nq, IN_SEC):
            acc = _dot_nt(a, section(s))
            s += 1
            for j in range(heads):
                h = _rope(acc[:, j * HEAD_DIM:(j + 1) * HEAD_DIM], cosf, sinf, lane) * Q_SCALE
                q_ref[:, c + j * HEAD_DIM:c + (j + 1) * HEAD_DIM] = h.astype(bf16)
        for dst, base, dt in ((kcvc_ref, 0, f32), (kv_ref, 0, bf16), (kv_ref, IN_SEC, bf16)):
            acc = _dot_nt(a, section(s))
            s += 1
            for j in range(heads):
                h = acc[:, j * HEAD_DIM:(j + 1) * HEAD_DIM]
                if j < N_KV_GROUPS:
                    h = _rope(h, cosf, sinf, lane)
                dst[:, base + j * HEAD_DIM:base + (j + 1) * HEAD_DIM] = h.astype(dt)
        gt_ref[...] = _dot_nt(a, wgt_ref[...])

    first = pl.program_id(0) == 0
    pl.when(first)(functools.partial(compute, True))
    pl.when(jnp.logical_not(first))(functools.partial(compute, False))


def _in_proj(x2, g, w_p, w_gt, cosf, sinf, cast_a, cast_b, cast_c, *, T, tm):
    n_rows, d = x2.shape
    n_w = (w_p.shape[0] // W_CHUNK) * W_CHUNK
    assert w_p.dtype == f32 and n_w + w_gt.shape[0] >= w_p.shape[0]
    tiles_per_seq = T // tm
    n_steps = n_rows // tm
    row = lambda i: (i, 0)
    cast_spec = lambda w: pl.BlockSpec((w.shape[0] // n_steps, w.shape[1]), row)
    const = lambda i: (0, 0)
    tab = lambda i: (i % tiles_per_seq, 0)
    return pl.pallas_call(
        _in_proj_kernel,
        grid=(n_rows // tm,),
        in_specs=[
            pl.BlockSpec((tm, d), row),
            pl.BlockSpec((1, d), const),
            pl.BlockSpec(memory_space=pl.ANY),
            pl.BlockSpec((LANES, d), const),
            pl.BlockSpec((tm, LANES), tab),
            pl.BlockSpec((tm, LANES), tab),
            cast_spec(cast_a), cast_spec(cast_b), cast_spec(cast_c),
        ],
        out_specs=[
            pl.BlockSpec((tm, 1024), row),
            pl.BlockSpec((tm, 1024), row),
            pl.BlockSpec((tm, 512), row),
            pl.BlockSpec((tm, 1024), row),
            pl.BlockSpec((tm, LANES), row),
            cast_spec(cast_a), cast_spec(cast_b), cast_spec(cast_c),
        ],
        out_shape=[
            jax.ShapeDtypeStruct((n_rows, 1024), bf16),
            jax.ShapeDtypeStruct((n_rows, 1024), bf16),
            jax.ShapeDtypeStruct((n_rows, 512), f32),
            jax.ShapeDtypeStruct((n_rows, 1024), bf16),
            jax.ShapeDtypeStruct((n_rows, LANES), f32),
            jax.ShapeDtypeStruct(cast_a.shape, bf16),
            jax.ShapeDtypeStruct(cast_b.shape, bf16),
            jax.ShapeDtypeStruct(cast_c.shape, bf16),
        ],
        scratch_shapes=[
            pltpu.VMEM((n_w, d), bf16),
            pltpu.VMEM((2, W_CHUNK, d), f32),
            pltpu.SemaphoreType.DMA((2,)),
        ],
        compiler_params=pltpu.CompilerParams(
            dimension_semantics=("arbitrary",), vmem_limit_bytes=VMEM_LIMIT),
        name="in_proj",
    )(x2, g, w_p, w_gt, cosf, sinf, cast_a, cast_b, cast_c)


def _gelu_tanh(x):
    return 0.5 * x * (1.0 + jnp.tanh(0.7978845608028654 * (x + 0.044715 * (x * x * x))))


def _compress_kernel(kc0_ref, kc1_ref, vc0_ref, vc1_ref, pek_ref, w1k_ref, w2k_ref, pev_ref, w1v_ref, w2v_ref,
                     kc_ref, vc_ref):
    bb = kc0_ref.shape[0]
    n_rows = kc0_ref.shape[1] // CMP_STRIDE
    half_k = (CMP_BLOCK // 2) * HEAD_DIM
    n_stack = bb * N_KV_GROUPS * n_rows
    rowi = lax.broadcasted_iota(jnp.int32, (n_stack, HEAD_DIM), 0)
    for srcs, pe_ref, w1_ref, w2_ref, o_ref in (((kc0_ref, kc1_ref), pek_ref, w1k_ref, w2k_ref, kc_ref),
                                                ((vc0_ref, vc1_ref), pev_ref, w1v_ref, w2v_ref, vc_ref)):
        x = jnp.concatenate(
            [jnp.concatenate([src[b, pl.ds(l, n_rows, stride=CMP_STRIDE), :] for l in range(CMP_STRIDE)], axis=1)
             for b in range(bb) for src in srcs], axis=0)
        za = _dot((x + pe_ref[:, :half_k]).astype(bf16), w1_ref[:half_k, :])
        zb = _dot((x + pe_ref[:, half_k:]).astype(bf16), w1_ref[half_k:, :])
        h = za + pltpu.roll(zb, n_stack - 1, axis=0)
        o = _dot(_gelu_tanh(h).astype(bf16), w2_ref[...])
        o = jnp.where((rowi & (n_rows - 1)) < n_rows - 1, o, 0.0).astype(bf16)
        for b in range(bb):
            for g in range(N_KV_GROUPS):
                r0 = (b * N_KV_GROUPS + g) * n_rows
                o_ref[b, g] = o[r0:r0 + n_rows]


def _compress(r3, pek, w1k, w2k, pev, w1v, w2v, *, bb):
    B, T, width = r3.shape
    n_rows = T // CMP_STRIDE
    assert width == 2 * N_KV_GROUPS * HEAD_DIM and N_KV_GROUPS == 2 and B % bb == 0 and n_rows & (n_rows - 1) == 0
    const = lambda b: (0, 0)
    out = jax.ShapeDtypeStruct((B, N_KV_GROUPS, n_rows, HEAD_DIM), bf16)
    ospec = pl.BlockSpec((bb, N_KV_GROUPS, n_rows, HEAD_DIM), lambda b: (b, 0, 0, 0))
    sec = lambda s: pl.BlockSpec((bb, T, HEAD_DIM), lambda b: (b, 0, s))
    return pl.pallas_call(
        _compress_kernel,
        grid=(B // bb,),
        in_specs=[
            sec(0), sec(1), sec(2), sec(3),
            pl.BlockSpec(pek.shape, const), pl.BlockSpec(w1k.shape, const), pl.BlockSpec(w2k.shape, const),
            pl.BlockSpec(pev.shape, const), pl.BlockSpec(w1v.shape, const), pl.BlockSpec(w2v.shape, const),
        ],
        out_specs=[ospec, ospec],
        out_shape=[out, out],
        compiler_params=pltpu.CompilerParams(
            dimension_semantics=("arbitrary",), vmem_limit_bytes=VMEM_LIMIT),
        name="compress",
    )(r3, r3, r3, r3, pek, w1k, w2k, pev, w1v, w2v)


def _nsa_front(qs, t0, kc, vc, kw_ref, vw_ref, *, tq, T):
    R = HEADS_PER_GROUP
    M = R * tq
    n_cmp = kc.shape[0]
    n_sel = T // SEL_BLOCK

    wk = WINDOW + tq
    w0 = pl.multiple_of(jnp.maximum(t0 - WINDOW, 0), tq)
    sw = _dot_nt(qs, kw_ref[pl.ds(w0, wk), :])
    diff = (t0 - w0) + (lax.broadcasted_iota(jnp.int32, (M, wk), 0) & (tq - 1)) \
        - lax.broadcasted_iota(jnp.int32, (M, wk), 1)
    sw = jnp.where((diff & -WINDOW) == 0, sw, NEG)
    ew = jnp.exp2(sw - jnp.max(sw, axis=1, keepdims=True))
    o_win = _dot(ew.astype(bf16), vw_ref[pl.ds(w0, wk), :]) / jnp.sum(ew, axis=1, keepdims=True)

    s = _dot_nt(qs, kc)
    trow = t0 + (lax.broadcasted_iota(jnp.int32, (M, n_cmp), 0) & (tq - 1))
    ncol = lax.broadcasted_iota(jnp.int32, (M, n_cmp), 1)
    valid = (CMP_STRIDE * ncol + (CMP_BLOCK - 1)) <= trow
    s = jnp.where(valid, s, NEG)
    e = jnp.exp2(s - jnp.max(s, axis=1, keepdims=True))
    p = jnp.where(valid, e / jnp.sum(e, axis=1, keepdims=True), 0.0)
    o_cmp = _dot(p.astype(bf16), vc)

    psum = p[0:tq]
    for r in range(1, R):
        psum = psum + p[r * tq:(r + 1) * tq]
    p_hi = psum.astype(bf16)
    p_lo = (psum - p_hi.astype(f32)).astype(bf16)
    sb = lax.broadcasted_iota(jnp.int32, (n_sel, n_cmp), 0) * SEL_BLOCK
    cb = lax.broadcasted_iota(jnp.int32, (n_sel, n_cmp), 1) * CMP_STRIDE
    ov_t = jnp.where((cb < sb + SEL_BLOCK) & (cb + CMP_BLOCK > sb), 1.0, 0.0).astype(bf16)
    imp_t = _dot_nt(ov_t, p_hi) + _dot_nt(ov_t, p_lo)

    jj = lax.broadcasted_iota(jnp.int32, (n_sel, tq), 0)
    cur = jnp.right_shift(t0 + lax.broadcasted_iota(jnp.int32, (n_sel, tq), 1), SEL_BLOCK.bit_length() - 1)
    forced = (jj == 0) | (jj == cur) | (jj == cur - 1)
    v = jnp.where(jj > cur, -jnp.inf, jnp.where(forced, jnp.inf, imp_t))
    sub = 8
    slabs = [v[k * sub:(k + 1) * sub] for k in range(n_sel // sub)]
    ranks = [jnp.zeros((sub, tq), f32) for _ in slabs]
    j_in = lax.broadcasted_iota(jnp.int32, (sub, tq), 0)
    for c in range(n_sel):
        v_c = v[c:c + 1, :]
        for k, v_k in enumerate(slabs):
            if k * sub > c:
                ahead = jnp.where(v_c >= v_k, 1.0, 0.0)
            elif (k + 1) * sub <= c:
                ahead = jnp.where(v_c > v_k, 1.0, 0.0)
            else:
                ahead = jnp.where(j_in > c - k * sub, jnp.where(v_c >= v_k, 1.0, 0.0),
                                  jnp.where(v_c > v_k, 1.0, 0.0))
            ranks[k] = ranks[k] + ahead
    rank = jnp.concatenate(ranks, axis=0)
    unsel_t = jnp.where(rank < float(min(N_SELECT, n_sel)), 0.0, 1.0)
    unsel_t = jnp.concatenate([unsel_t, jnp.zeros((LANES - n_sel, tq), f32)], axis=0)
    return o_win, o_cmp, unsel_t.T.astype(bf16)


def _nsa_kernel(q_ref, kc_ref, vc_ref, ks_ref, vs_ref, kw_ref, vw_ref, eneg_ref, gt_ref, ca_ref, cb_ref,
                o_ref, ca_out, cb_out, s_sc, mx_sc, ls_sc, acc_sc, *, tq, n_sub, T):
    ca_out[...] = ca_ref[...].astype(bf16)
    cb_out[...] = cb_ref[...].astype(bf16)
    R = HEADS_PER_GROUP
    M = R * tq
    tk = n_sub * tq
    i = pl.program_id(2)
    t0 = i * tk
    kc = kc_ref[0, 0]
    vc = vc_ref[0, 0]

    fronts = []
    q_rows = []
    for sb in range(n_sub):
        q = q_ref[sb * tq:(sb + 1) * tq, :]
        qs = jnp.concatenate([q[:, r * HEAD_DIM:(r + 1) * HEAD_DIM] for r in range(R)], axis=0)
        o_win, o_cmp, unsel = _nsa_front(qs, t0 + sb * tq, kc, vc, kw_ref, vw_ref, tq=tq, T=T)
        fronts.append((o_win, o_cmp))
        q_rows.append(jnp.concatenate([qs, jnp.concatenate([unsel] * R, axis=0)], axis=1))
    q_aug = jnp.concatenate(q_rows, axis=0)
    MM = n_sub * M

    n_chunk = tk // LANES
    mx_sc[...] = jnp.full(mx_sc.shape, NEG, f32)

    def score_step(kt, causal):
        k0 = pl.multiple_of(kt * tk, tk)
        k_aug = jnp.concatenate([ks_ref[pl.ds(k0, tk), :], eneg_ref[pl.ds(k0, tk), :]], axis=1)
        sc = _dot_nt(q_aug, k_aug)
        if causal:
            rowi = lax.broadcasted_iota(jnp.int32, (MM, tk), 0)
            qoff = jnp.right_shift(rowi, M.bit_length() - 1) * tq + (rowi & (tq - 1))
            sc = jnp.where(qoff >= lax.broadcasted_iota(jnp.int32, (MM, tk), 1), sc, NEG)
        s_sc[kt] = sc
        part = sc[:, :LANES]
        for c in range(1, n_chunk):
            part = jnp.maximum(part, sc[:, c * LANES:(c + 1) * LANES])
        mx_sc[...] = jnp.maximum(mx_sc[...], part)

    def score_body(kt, carry):
        score_step(kt, False)
        return carry

    lax.fori_loop(0, i, score_body, 0)
    score_step(i, True)
    mx_sc[...] = jnp.broadcast_to(jnp.max(mx_sc[...], axis=1, keepdims=True), mx_sc.shape)
    ls_sc[...] = jnp.zeros(ls_sc.shape, f32)
    acc_sc[...] = jnp.zeros(acc_sc.shape, f32)

    def pv_body(kt, carry):
        k0 = pl.multiple_of(kt * tk, tk)
        sc = s_sc[kt]
        mb = mx_sc[...]
        ps = [jnp.exp2(sc[:, c * LANES:(c + 1) * LANES] - mb) for c in range(n_chunk)]
        ls_sc[...] += functools.reduce(lambda a, b: a + b, ps)
        acc_sc[...] += _dot(jnp.concatenate(ps, axis=1).astype(bf16), vs_ref[pl.ds(k0, tk), :])
        return carry

    lax.fori_loop(0, i + 1, pv_body, 0)
    o_sel = acc_sc[...] / jnp.sum(ls_sc[...], axis=1, keepdims=True)

    gt_all = jax.nn.sigmoid(gt_ref[...])
    gt_all = jnp.where(pl.program_id(1) == 0, gt_all, pltpu.roll(gt_all, LANES - R * N_BRANCH, axis=1))
    for sb, (o_win, o_cmp) in enumerate(fronts):
        gt = gt_all[sb * tq:(sb + 1) * tq]
        for r in range(R):
            rows = slice(r * tq, (r + 1) * tq)
            srows = slice(sb * M + r * tq, sb * M + (r + 1) * tq)
            c = r * N_BRANCH
            o = (gt[:, c:c + 1] * o_cmp[rows] + gt[:, c + 1:c + 2] * o_sel[srows]
                 + gt[:, c + 2:c + 3] * o_win[rows])
            o_ref[sb * tq:(sb + 1) * tq, r * HEAD_DIM:(r + 1) * HEAD_DIM] = o.astype(bf16)


def _nsa(q, kcmp, vcmp, kv, eneg, gates, cast_a, cast_b, *, B, T, tq, n_sub):
    G, R = N_KV_GROUPS, HEADS_PER_GROUP
    tk = n_sub * tq
    nt = T // tk
    n_cmp = kcmp.shape[2]
    assert tq % LANES == 0 and tq & (tq - 1) == 0 and T % tk == 0 and T >= WINDOW + tq
    assert WINDOW % tq == 0 and WINDOW & (WINDOW - 1) == 0 and (R * tq) & (R * tq - 1) == 0
    rowblk = lambda b, g, i: (b * nt + i, g)
    cmp_spec = pl.BlockSpec((1, 1, n_cmp, HEAD_DIM), lambda b, g, i: (b, g, 0, 0))
    kvspec = lambda sec: pl.BlockSpec((T, HEAD_DIM), lambda b, g, i: (b, sec * G + g))
    n_steps = B * G * nt
    cast_spec = lambda w: pl.BlockSpec((w.shape[0] // n_steps, w.shape[1]), lambda b, g, i: ((b * G + g) * nt + i, 0))
    return pl.pallas_call(
        functools.partial(_nsa_kernel, tq=tq, n_sub=n_sub, T=T),
        grid=(B, G, nt),
        in_specs=[
            pl.BlockSpec((tk, R * HEAD_DIM), rowblk),
            cmp_spec, cmp_spec,
            kvspec(0), kvspec(1), kvspec(2), kvspec(3),
            pl.BlockSpec((T, LANES), lambda b, g, i: (0, 0)),
            pl.BlockSpec((tk, LANES), lambda b, g, i: (b * nt + i, 0)),
            cast_spec(cast_a), cast_spec(cast_b),
        ],
        out_specs=[pl.BlockSpec((tk, R * HEAD_DIM), rowblk), cast_spec(cast_a), cast_spec(cast_b)],
        out_shape=[jax.ShapeDtypeStruct((B * T, G * R * HEAD_DIM), bf16), jax.ShapeDtypeStruct(cast_a.shape, bf16),
                   jax.ShapeDtypeStruct(cast_b.shape, bf16)],
        scratch_shapes=[
            pltpu.VMEM((T // tk, n_sub * R * tq, tk), f32),
            pltpu.VMEM((n_sub * R * tq, LANES), f32),
            pltpu.VMEM((n_sub * R * tq, LANES), f32),
            pltpu.VMEM((n_sub * R * tq, HEAD_DIM), f32),
        ],
        compiler_params=pltpu.CompilerParams(
            dimension_semantics=("arbitrary", "arbitrary", "arbitrary"), vmem_limit_bytes=VMEM_LIMIT),
        name="nsa",
    )(q, kcmp, vcmp, kv, kv, kv, kv, eneg, gates, cast_a, cast_b)


def _out_proj_kernel(x_ref, u_ref, uh_ref, y_ref, pw_ref, ps_ref, wo_ref, o_ref, *, tm, T):
    i = pl.program_id(0)
    tiles_per_seq = T // tm
    keep_halo = jnp.where((i % tiles_per_seq) == 0, 0.0, 1.0)
    u = u_ref[...]
    halo = uh_ref[...]
    tt = lax.broadcasted_iota(jnp.int32, (tm, tm), 0)
    ss = lax.broadcasted_iota(jnp.int32, (tm, tm), 1)
    th = lax.broadcasted_iota(jnp.int32, (tm, POOL_HALO), 0)
    rh = lax.broadcasted_iota(jnp.int32, (tm, POOL_HALO), 1)
    tseq = (i % tiles_per_seq) * tm + lax.broadcasted_iota(jnp.int32, (tm, 1), 0)
    n_pool = u.shape[1]
    gd = n_pool // len(POOL_WINDOWS)
    acc = x_ref[...] + _dot(y_ref[...], wo_ref[n_pool:, :])
    for gi, w in enumerate(POOL_WINDOWS):
        cols = slice(gi * gd, (gi + 1) * gd)
        band = jnp.where((tt - ss >= 0) & (tt - ss < w), 1.0, 0.0).astype(bf16)
        band_h = jnp.where(th + POOL_HALO - rh < w, 1.0, 0.0).astype(bf16)
        ug = u[:, cols]
        win_sum = _dot(band, ug) + keep_halo * _dot(band_h, halo[:, cols])
        count = jnp.minimum(tseq + 1, w).astype(f32)
        pooled = win_sum / count - ug.astype(f32)
        yg = _dot(pooled.astype(bf16), pw_ref[gi]) * ps_ref[:, cols]
        acc = acc + _dot(yg.astype(bf16), wo_ref[cols, :])
    o_ref[...] = acc


def _out_proj(x2, u, y_nsa, pool_w, pool_scale, w_out, *, T, tm):
    n_rows, d = x2.shape
    n_pool = u.shape[1]
    hb = tm // POOL_HALO
    row = lambda i: (i, 0)
    const2 = lambda i: (0, 0)
    return pl.pallas_call(
        functools.partial(_out_proj_kernel, tm=tm, T=T),
        grid=(n_rows // tm,),
        in_specs=[
            pl.BlockSpec((tm, d), row),
            pl.BlockSpec((tm, n_pool), row),
            pl.BlockSpec((POOL_HALO, n_pool), lambda i: (jnp.maximum(i * hb - 1, 0), 0)),
            pl.BlockSpec((tm, y_nsa.shape[1]), row),
            pl.BlockSpec(pool_w.shape, lambda i: (0, 0, 0)),
            pl.BlockSpec((1, n_pool), const2),
            pl.BlockSpec(w_out.shape, const2, pipeline_mode=pl.Buffered(1)),
        ],
        out_specs=pl.BlockSpec((tm, d), row),
        out_shape=jax.ShapeDtypeStruct((n_rows, d), f32),
        compiler_params=pltpu.CompilerParams(
            dimension_semantics=("arbitrary",), vmem_limit_bytes=VMEM_LIMIT),
        name="out_proj",
    )(x2, u, u, y_nsa, pool_w, pool_scale, w_out)


def _ffn_kernel(h_ref, g_ref, wg_ref, wu_ref, wd_ref, o_ref, n_sc):
    @pl.when(pl.program_id(1) == 0)
    def _():
        h = h_ref[...]
        n_sc[...] = _rms(h, g_ref[...]).astype(bf16)
        o_ref[...] = h

    n = n_sc[...]
    half = wg_ref.shape[1] // 2
    acc = None
    for c in range(2):
        cols = slice(c * half, (c + 1) * half)
        gate = _dot(n, wg_ref[:, cols])
        up = _dot(n, wu_ref[:, cols])
        a = gate * jax.nn.sigmoid(gate) * up
        d = _dot(a.astype(bf16), wd_ref[cols, :])
        acc = d if acc is None else acc + d
    o_ref[...] += acc


def _ffn(h, g, w_gate, w_up, w_down, *, tm, tf):
    n_rows, d = h.shape
    d_ff = w_gate.shape[1]
    return pl.pallas_call(
        _ffn_kernel,
        grid=(n_rows // tm, d_ff // tf),
        in_specs=[
            pl.BlockSpec((tm, d), lambda i, f: (i, 0)),
            pl.BlockSpec((1, d), lambda i, f: (0, 0)),
            pl.BlockSpec((d, tf), lambda i, f: (0, f)),
            pl.BlockSpec((d, tf), lambda i, f: (0, f)),
            pl.BlockSpec((tf, d), lambda i, f: (f, 0)),
        ],
        out_specs=pl.BlockSpec((tm, d), lambda i, f: (i, 0)),
        out_shape=jax.ShapeDtypeStruct((n_rows, d), f32),
        scratch_shapes=[pltpu.VMEM((tm, d), bf16)],
        compiler_params=pltpu.CompilerParams(
            dimension_semantics=("arbitrary", "arbitrary"), vmem_limit_bytes=VMEM_LIMIT),
        name="ffn",
    )(h, g, w_gate, w_up, w_down)


def _ple_kernel(h_ref, p_ref, g_ref, wg_ref, wp_ref, fg_ref, o_ref):
    h = h_ref[...]
    n = _rms(h, g_ref[...]).astype(bf16)
    gate = jax.nn.sigmoid(_dot(n, wg_ref[...]))
    h = h + _dot(p_ref[...].astype(bf16), wp_ref[...]) * gate
    o_ref[...] = _rms(h, fg_ref[...])


def _ple(h, p2, g, w_gate, w_proj, fg, *, tm):
    n_rows, d = h.shape
    row = lambda i: (i, 0)
    const = lambda i: (0, 0)
    return pl.pallas_call(
        _ple_kernel,
        grid=(n_rows // tm,),
        in_specs=[
            pl.BlockSpec((tm, d), row),
            pl.BlockSpec((tm, p2.shape[1]), row),
            pl.BlockSpec((1, d), const),
            pl.BlockSpec(w_gate.shape, const, pipeline_mode=pl.Buffered(1)),
            pl.BlockSpec(w_proj.shape, const),
            pl.BlockSpec((1, d), const),
        ],
        out_specs=pl.BlockSpec((tm, d), row),
        out_shape=jax.ShapeDtypeStruct((n_rows, d), f32),
        compiler_params=pltpu.CompilerParams(
            dimension_semantics=("arbitrary",), vmem_limit_bytes=VMEM_LIMIT),
        name="ple",
    )(h, p2, g, w_gate, w_proj, fg)


def _rope_tables(T):
    pos = jnp.arange(T, dtype=f32)
    inv_freq = ROPE_THETA ** (-jnp.arange(0, ROPE_DIM, 2, dtype=f32) / ROPE_DIM)
    ang = pos[:, None] * inv_freq[None, :]
    cos, sin = jnp.cos(ang), jnp.sin(ang)
    rest = HEAD_DIM - ROPE_DIM
    cosf = jnp.concatenate([cos, cos, jnp.ones((T, rest), f32)], axis=1)
    sinf = jnp.concatenate([-sin, sin, jnp.zeros((T, rest), f32)], axis=1)
    return cosf, sinf


def _mixer_ffn(h2, in_norm_g, w_in, pool_w, pool_scale, cmp_k_pe, cmp_k_w1, cmp_k_w2,
               cmp_v_pe, cmp_v_w1, cmp_v_w2, w_out, ffn_norm_g, w_gate, w_up, w_down, w_ple_gate, *, B, T):
    n_pool = pool_scale.shape[0]
    n_heads = N_KV_GROUPS * HEADS_PER_GROUP
    nsa_w = n_heads * HEAD_DIM
    kv_w = N_KV_GROUPS * HEAD_DIM
    assert w_in.shape[1] == n_pool + nsa_w + 6 * kv_w + n_heads * N_BRANCH and n_pool == 1024 and nsa_w == 1024

    o0 = n_pool + nsa_w + 6 * kv_w
    w_p = w_in.T
    w_gt = jnp.pad(w_p[o0:].astype(bf16), ((0, LANES - n_heads * N_BRANCH), (0, 0)))

    cosf, sinf = _rope_tables(T)
    u, q, kcvc, kv, gates, w_out_b, w_gate_b, w_ple_gate_b = _in_proj(
        h2, in_norm_g[None, :], w_p, w_gt, cosf, sinf, w_out, w_gate, w_ple_gate, T=T, tm=512)

    kcmp, vcmp = _compress(
        kcvc.reshape(B, T, kcvc.shape[1]),
        cmp_k_pe.reshape(1, -1), cmp_k_w1.astype(bf16), cmp_k_w2.astype(bf16),
        cmp_v_pe.reshape(1, -1), cmp_v_w1.astype(bf16), cmp_v_w2.astype(bf16), bb=2 if B % 2 == 0 else 1)

    key_blk = jnp.arange(T, dtype=jnp.int32)[:, None] // SEL_BLOCK
    eneg = jnp.where(key_blk == jnp.arange(LANES, dtype=jnp.int32)[None, :], NEG, 0.0).astype(bf16)
    y_nsa, w_up_b, w_down_b = _nsa(q, kcmp, vcmp, kv, eneg, gates, w_up, w_down, B=B, T=T, tq=256, n_sub=2)

    h2 = _out_proj(h2, u, y_nsa, pool_w.astype(bf16), pool_scale[None, :], w_out_b, T=T, tm=512)
    return _ffn(h2, ffn_norm_g[None, :], w_gate_b, w_up_b, w_down_b, tm=1024, tf=512), w_ple_gate_b


def kernel(x, p, in_norm_g, w_in, pool_w, pool_scale, cmp_k_pe, cmp_k_w1, cmp_k_w2, cmp_v_pe, cmp_v_w1,
           cmp_v_w2, w_out, ffn_norm_g, w_gate, w_up, w_down, ple_norm_g, w_ple_gate, w_ple_proj, final_norm_g):
    B, T, d = x.shape
    depth = w_in.shape[0]
    assert depth == 1, "the final rmsnorm is fused into the last layer's per-layer-embedding kernel"
    h2 = x.reshape(B * T, d)
    i = 0
    h2, w_ple_gate_b = _mixer_ffn(
        h2, in_norm_g[i], w_in[i], pool_w[i], pool_scale[i], cmp_k_pe[i], cmp_k_w1[i], cmp_k_w2[i],
        cmp_v_pe[i], cmp_v_w1[i], cmp_v_w2[i], w_out[i], ffn_norm_g[i], w_gate[i], w_up[i], w_down[i],
        w_ple_gate[i], B=B, T=T)
    out = _ple(h2, p[i].reshape(B * T, -1), ple_norm_g[i][None, :], w_ple_gate_b,
               w_ple_proj[i].astype(bf16), final_norm_g[None, :], tm=1024)
    return out.reshape(B, T, d)
```

```python
import functools

import jax
import jax.numpy as jnp
from jax import lax
from jax.experimental import pallas as pl
from jax.experimental.pallas import tpu as pltpu

EPS = 1e-6
NEG = -1e30
LANES = 128
POOL_WINDOWS = (2, 4, 8, 16)
POOL_HALO = 16
HEAD_DIM = 128
N_KV_GROUPS = 2
HEADS_PER_GROUP = 4
N_BRANCH = 3
CMP_BLOCK = 32
CMP_STRIDE = 16
SEL_BLOCK = 64
N_SELECT = 16
WINDOW = 512
ROPE_THETA = 500000.0
ROPE_DIM = HEAD_DIM // 4
ATTN_SCALE = HEAD_DIM ** -0.5
Q_SCALE = ATTN_SCALE * 1.4426950408889634
VMEM_LIMIT = 60 * 1024 * 1024
W_CHUNK = 256
IN_SEC = 512
ROW_TILE = 512
FFN_ROW_TILE = 1024
FFN_FF_TILE = 512
NSA_Q_TILE = 256
NSA_SUB_TILES = 2
COMPRESS_BATCH = 2

f32 = jnp.float32
bf16 = jnp.bfloat16


def _dot(a, b):
    return jnp.dot(a, b, preferred_element_type=f32)


def _dot_nt(a, b):
    return lax.dot_general(a, b, (((1,), (1,)), ((), ())), preferred_element_type=f32)


def _rms(x, g):
    return x * lax.rsqrt(jnp.mean(x * x, axis=-1, keepdims=True) + EPS) * g


def _rope(h, cosf, sinf, lane):
    half = ROPE_DIM // 2
    partner = jnp.where(lane < half, pltpu.roll(h, LANES - half, axis=1), pltpu.roll(h, half, axis=1))
    return h * cosf + partner * sinf


def _w_chunk_copy(w_hbm, stage_sc, sem, c):
    return pltpu.make_async_copy(w_hbm.at[pl.ds(c * W_CHUNK, W_CHUNK), :], stage_sc.at[c % 2], sem.at[c % 2])


def _in_proj_kernel(x_ref, g_ref, w_hbm, wgt_ref, cos_ref, sin_ref, ca_ref, cb_ref, cc_ref,
                    u_ref, q_ref, kcvc_ref, kv_ref, gt_ref, ca_out, cb_out, cc_out,
                    w_ref, stage_sc, sem):
    @pl.when(pl.program_id(0) == 0)
    def _():
        n_chunk = w_ref.shape[0] // W_CHUNK
        _w_chunk_copy(w_hbm, stage_sc, sem, 0).start()
        for c in range(n_chunk):
            if c + 1 < n_chunk:
                _w_chunk_copy(w_hbm, stage_sc, sem, c + 1).start()
            _w_chunk_copy(w_hbm, stage_sc, sem, c).wait()
            w_ref[c * W_CHUNK:(c + 1) * W_CHUNK, :] = stage_sc[c % 2].astype(bf16)

    def section(s):
        return w_ref[s * IN_SEC:(s + 1) * IN_SEC, :]

    ca_out[...] = ca_ref[...].astype(bf16)
    cb_out[...] = cb_ref[...].astype(bf16)
    cc_out[...] = cc_ref[...].astype(bf16)
    a = _rms(x_ref[...], g_ref[...]).astype(bf16)
    cosf = cos_ref[...]
    sinf = sin_ref[...]
    lane = lax.broadcasted_iota(jnp.int32, cosf.shape, 1)
    heads = IN_SEC // HEAD_DIM
    s = 0
    for c in range(0, u_ref.shape[1], IN_SEC):
        u_ref[:, c:c + IN_SEC] = _dot_nt(a, section(s)).astype(bf16)
        s += 1
    for c in range(0, q_ref.shape[1], IN_SEC):
        acc = _dot_nt(a, section(s))
        s += 1
        for j in range(heads):
            h = _rope(acc[:, j * HEAD_DIM:(j + 1) * HEAD_DIM], cosf, sinf, lane) * Q_SCALE
            q_ref[:, c + j * HEAD_DIM:c + (j + 1) * HEAD_DIM] = h.astype(bf16)
    for dst, base, dt in ((kcvc_ref, 0, f32), (kv_ref, 0, bf16), (kv_ref, IN_SEC, bf16)):
        acc = _dot_nt(a, section(s))
        s += 1
        for j in range(heads):
            h = acc[:, j * HEAD_DIM:(j + 1) * HEAD_DIM]
            if j < N_KV_GROUPS:
                h = _rope(h, cosf, sinf, lane)
            dst[:, base + j * HEAD_DIM:base + (j + 1) * HEAD_DIM] = h.astype(dt)
    gt_ref[...] = _dot_nt(a, wgt_ref[...])


def _in_proj(x2, g, w_p, w_gt, cosf, sinf, cast_a, cast_b, cast_c, *, T, tm, n_pool, n_q):
    n_rows, d = x2.shape
    assert n_pool % IN_SEC == 0 and n_q % IN_SEC == 0 and 2 * N_KV_GROUPS * HEAD_DIM == IN_SEC
    n_w = (w_p.shape[0] // W_CHUNK) * W_CHUNK
    assert w_p.dtype == f32 and n_w + w_gt.shape[0] >= w_p.shape[0] and n_w == n_pool + n_q + 3 * IN_SEC
    tiles_per_seq = T // tm
    n_steps = n_rows // tm
    row = lambda i: (i, 0)
    cast_spec = lambda w: pl.BlockSpec((w.shape[0] // n_steps, w.shape[1]), row)
    const = lambda i: (0, 0)
    tab = lambda i: (i % tiles_per_seq, 0)
    return pl.pallas_call(
        _in_proj_kernel,
        grid=(n_rows // tm,),
        in_specs=[
            pl.BlockSpec((tm, d), row),
            pl.BlockSpec((1, d), const),
            pl.BlockSpec(memory_space=pl.ANY),
            pl.BlockSpec((LANES, d), const),
            pl.BlockSpec((tm, LANES), tab),
            pl.BlockSpec((tm, LANES), tab),
            cast_spec(cast_a), cast_spec(cast_b), cast_spec(cast_c),
        ],
        out_specs=[
            pl.BlockSpec((tm, n_pool), row),
            pl.BlockSpec((tm, n_q), row),
            pl.BlockSpec((tm, IN_SEC), row),
            pl.BlockSpec((tm, 2 * IN_SEC), row),
            pl.BlockSpec((tm, LANES), row),
            cast_spec(cast_a), cast_spec(cast_b), cast_spec(cast_c),
        ],
        out_shape=[
            jax.ShapeDtypeStruct((n_rows, n_pool), bf16),
            jax.ShapeDtypeStruct((n_rows, n_q), bf16),
            jax.ShapeDtypeStruct((n_rows, IN_SEC), f32),
            jax.ShapeDtypeStruct((n_rows, 2 * IN_SEC), bf16),
            jax.ShapeDtypeStruct((n_rows, LANES), f32),
            jax.ShapeDtypeStruct(cast_a.shape, bf16),
            jax.ShapeDtypeStruct(cast_b.shape, bf16),
            jax.ShapeDtypeStruct(cast_c.shape, bf16),
        ],
        scratch_shapes=[
            pltpu.VMEM((n_w, d), bf16),
            pltpu.VMEM((2, W_CHUNK, d), f32),
            pltpu.SemaphoreType.DMA((2,)),
        ],
        compiler_params=pltpu.CompilerParams(
            dimension_semantics=("arbitrary",), vmem_limit_bytes=VMEM_LIMIT),
        name="in_proj",
    )(x2, g, w_p, w_gt, cosf, sinf, cast_a, cast_b, cast_c)


def _gelu_tanh(x):
    return 0.5 * x * (1.0 + jnp.tanh(0.7978845608028654 * (x + 0.044715 * (x * x * x))))


def _compress_kernel(kc0_ref, kc1_ref, vc0_ref, vc1_ref, pek_ref, w1k_ref, w2k_ref, pev_ref, w1v_ref, w2v_ref,
                     kc_ref, vc_ref):
    bb = kc0_ref.shape[0]
    n_rows = kc0_ref.shape[1] // CMP_STRIDE
    half_k = (CMP_BLOCK // 2) * HEAD_DIM
    n_stack = bb * N_KV_GROUPS * n_rows
    rowi = lax.broadcasted_iota(jnp.int32, (n_stack, HEAD_DIM), 0)
    for srcs, pe_ref, w1_ref, w2_ref, o_ref in (((kc0_ref, kc1_ref), pek_ref, w1k_ref, w2k_ref, kc_ref),
                                                ((vc0_ref, vc1_ref), pev_ref, w1v_ref, w2v_ref, vc_ref)):
        x = jnp.concatenate(
            [jnp.concatenate([src[b, pl.ds(l, n_rows, stride=CMP_STRIDE), :] for l in range(CMP_STRIDE)], axis=1)
             for b in range(bb) for src in srcs], axis=0)
        za = _dot((x + pe_ref[:, :half_k]).astype(bf16), w1_ref[:half_k, :])
        zb = _dot((x + pe_ref[:, half_k:]).astype(bf16), w1_ref[half_k:, :])
        h = za + pltpu.roll(zb, n_stack - 1, axis=0)
        o = _dot(_gelu_tanh(h).astype(bf16), w2_ref[...])
        o = jnp.where((rowi & (n_rows - 1)) < n_rows - 1, o, 0.0).astype(bf16)
        for b in range(bb):
            for g in range(N_KV_GROUPS):
                r0 = (b * N_KV_GROUPS + g) * n_rows
                o_ref[b, g] = o[r0:r0 + n_rows]


def _compress(r3, pek, w1k, w2k, pev, w1v, w2v, *, bb):
    B, T, width = r3.shape
    n_rows = T // CMP_STRIDE
    assert width == 2 * N_KV_GROUPS * HEAD_DIM and N_KV_GROUPS == 2 and B % bb == 0 and n_rows & (n_rows - 1) == 0
    const = lambda b: (0, 0)
    out = jax.ShapeDtypeStruct((B, N_KV_GROUPS, n_rows, HEAD_DIM), bf16)
    ospec = pl.BlockSpec((bb, N_KV_GROUPS, n_rows, HEAD_DIM), lambda b: (b, 0, 0, 0))
    sec = lambda s: pl.BlockSpec((bb, T, HEAD_DIM), lambda b: (b, 0, s))
    return pl.pallas_call(
        _compress_kernel,
        grid=(B // bb,),
        in_specs=[
            sec(0), sec(1), sec(2), sec(3),
            pl.BlockSpec(pek.shape, const), pl.BlockSpec(w1k.shape, const), pl.BlockSpec(w2k.shape, const),
            pl.BlockSpec(pev.shape, const), pl.BlockSpec(w1v.shape, const), pl.BlockSpec(w2v.shape, const),
        ],
        out_specs=[ospec, ospec],
        out_shape=[out, out],
        compiler_params=pltpu.CompilerParams(
            dimension_semantics=("arbitrary",), vmem_limit_bytes=VMEM_LIMIT),
        name="compress",
    )(r3, r3, r3, r3, pek, w1k, w2k, pev, w1v, w2v)


def _nsa_front(qs, t0, kc, vc, kw_ref, vw_ref, *, tq, T):
    R = HEADS_PER_GROUP
    M = R * tq
    n_cmp = kc.shape[0]
    n_sel = T // SEL_BLOCK

    wk = WINDOW + tq
    w0 = pl.multiple_of(jnp.maximum(t0 - WINDOW, 0), tq)
    sw = _dot_nt(qs, kw_ref[pl.ds(w0, wk), :])
    diff = (t0 - w0) + (lax.broadcasted_iota(jnp.int32, (M, wk), 0) & (tq - 1)) \
        - lax.broadcasted_iota(jnp.int32, (M, wk), 1)
    sw = jnp.where((diff & -WINDOW) == 0, sw, NEG)
    ew = jnp.exp2(sw - jnp.max(sw, axis=1, keepdims=True))
    o_win = _dot(ew.astype(bf16), vw_ref[pl.ds(w0, wk), :]) / jnp.sum(ew, axis=1, keepdims=True)

    s = _dot_nt(qs, kc)
    trow = t0 + (lax.broadcasted_iota(jnp.int32, (M, n_cmp), 0) & (tq - 1))
    ncol = lax.broadcasted_iota(jnp.int32, (M, n_cmp), 1)
    valid = (CMP_STRIDE * ncol + (CMP_BLOCK - 1)) <= trow
    s = jnp.where(valid, s, NEG)
    e = jnp.exp2(s - jnp.max(s, axis=1, keepdims=True))
    p = jnp.where(valid, e / jnp.sum(e, axis=1, keepdims=True), 0.0)
    o_cmp = _dot(p.astype(bf16), vc)

    psum = p[0:tq]
    for r in range(1, R):
        psum = psum + p[r * tq:(r + 1) * tq]
    p_hi = psum.astype(bf16)
    p_lo = (psum - p_hi.astype(f32)).astype(bf16)
    sb = lax.broadcasted_iota(jnp.int32, (n_sel, n_cmp), 0) * SEL_BLOCK
    cb = lax.broadcasted_iota(jnp.int32, (n_sel, n_cmp), 1) * CMP_STRIDE
    ov_t = jnp.where((cb < sb + SEL_BLOCK) & (cb + CMP_BLOCK > sb), 1.0, 0.0).astype(bf16)
    imp_t = _dot_nt(ov_t, p_hi) + _dot_nt(ov_t, p_lo)

    jj = lax.broadcasted_iota(jnp.int32, (n_sel, tq), 0)
    cur = jnp.right_shift(t0 + lax.broadcasted_iota(jnp.int32, (n_sel, tq), 1), SEL_BLOCK.bit_length() - 1)
    forced = (jj == 0) | (jj == cur) | (jj == cur - 1)
    v = jnp.where(jj > cur, -jnp.inf, jnp.where(forced, jnp.inf, imp_t))
    sub = 8
    slabs = [v[k * sub:(k + 1) * sub] for k in range(n_sel // sub)]
    ranks = [jnp.zeros((sub, tq), f32) for _ in slabs]
    j_in = lax.broadcasted_iota(jnp.int32, (sub, tq), 0)
    for c in range(n_sel):
        v_c = v[c:c + 1, :]
        for k, v_k in enumerate(slabs):
            if k * sub > c:
                ahead = jnp.where(v_c >= v_k, 1.0, 0.0)
            elif (k + 1) * sub <= c:
                ahead = jnp.where(v_c > v_k, 1.0, 0.0)
            else:
                ahead = jnp.where(j_in > c - k * sub, jnp.where(v_c >= v_k, 1.0, 0.0),
                                  jnp.where(v_c > v_k, 1.0, 0.0))
            ranks[k] = ranks[k] + ahead
    rank = jnp.concatenate(ranks, axis=0)
    unsel_t = jnp.where(rank < float(min(N_SELECT, n_sel)), 0.0, 1.0)
    unsel_t = jnp.concatenate([unsel_t, jnp.zeros((LANES - n_sel, tq), f32)], axis=0)
    return o_win, o_cmp, unsel_t.T.astype(bf16)


def _nsa_kernel(q_ref, kc_ref, vc_ref, ks_ref, vs_ref, kw_ref, vw_ref, eneg_ref, gt_ref, ca_ref, cb_ref,
                o_ref, ca_out, cb_out, s_sc, mx_sc, ls_sc, acc_sc, *, tq, n_sub, T):
    ca_out[...] = ca_ref[...].astype(bf16)
    cb_out[...] = cb_ref[...].astype(bf16)
    R = HEADS_PER_GROUP
    M = R * tq
    tk = n_sub * tq
    i = pl.program_id(2)
    t0 = i * tk
    kc = kc_ref[0, 0]
    vc = vc_ref[0, 0]

    fronts = []
    q_rows = []
    for sb in range(n_sub):
        q = q_ref[sb * tq:(sb + 1) * tq, :]
        qs = jnp.concatenate([q[:, r * HEAD_DIM:(r + 1) * HEAD_DIM] for r in range(R)], axis=0)
        o_win, o_cmp, unsel = _nsa_front(qs, t0 + sb * tq, kc, vc, kw_ref, vw_ref, tq=tq, T=T)
        fronts.append((o_win, o_cmp))
        q_rows.append(jnp.concatenate([qs, jnp.concatenate([unsel] * R, axis=0)], axis=1))
    q_aug = jnp.concatenate(q_rows, axis=0)
    MM = n_sub * M

    n_chunk = tk // LANES
    mx_sc[...] = jnp.full(mx_sc.shape, NEG, f32)

    def score_step(kt, causal):
        k0 = pl.multiple_of(kt * tk, tk)
        k_aug = jnp.concatenate([ks_ref[pl.ds(k0, tk), :], eneg_ref[pl.ds(k0, tk), :]], axis=1)
        sc = _dot_nt(q_aug, k_aug)
        if causal:
            rowi = lax.broadcasted_iota(jnp.int32, (MM, tk), 0)
            qoff = jnp.right_shift(rowi, M.bit_length() - 1) * tq + (rowi & (tq - 1))
            sc = jnp.where(qoff >= lax.broadcasted_iota(jnp.int32, (MM, tk), 1), sc, NEG)
        s_sc[kt] = sc
        part = sc[:, :LANES]
        for c in range(1, n_chunk):
            part = jnp.maximum(part, sc[:, c * LANES:(c + 1) * LANES])
        mx_sc[...] = jnp.maximum(mx_sc[...], part)

    def score_body(kt, carry):
        score_step(kt, False)
        return carry

    lax.fori_loop(0, i, score_body, 0)
    score_step(i, True)
    mx_sc[...] = jnp.broadcast_to(jnp.max(mx_sc[...], axis=1, keepdims=True), mx_sc.shape)
    ls_sc[...] = jnp.zeros(ls_sc.shape, f32)
    acc_sc[...] = jnp.zeros(acc_sc.shape, f32)

    def pv_body(kt, carry):
        k0 = pl.multiple_of(kt * tk, tk)
        sc = s_sc[kt]
        mb = mx_sc[...]
        ps = [jnp.exp2(sc[:, c * LANES:(c + 1) * LANES] - mb) for c in range(n_chunk)]
        ls_sc[...] += functools.reduce(lambda a, b: a + b, ps)
        acc_sc[...] += _dot(jnp.concatenate(ps, axis=1).astype(bf16), vs_ref[pl.ds(k0, tk), :])
        return carry

    lax.fori_loop(0, i + 1, pv_body, 0)
    o_sel = acc_sc[...] / jnp.sum(ls_sc[...], axis=1, keepdims=True)

    gt_all = jax.nn.sigmoid(gt_ref[...])
    gt_all = jnp.where(pl.program_id(1) == 0, gt_all, pltpu.roll(gt_all, LANES - R * N_BRANCH, axis=1))
    for sb, (o_win, o_cmp) in enumerate(fronts):
        gt = gt_all[sb * tq:(sb + 1) * tq]
        for r in range(R):
            rows = slice(r * tq, (r + 1) * tq)
            srows = slice(sb * M + r * tq, sb * M + (r + 1) * tq)
            c = r * N_BRANCH
            o = (gt[:, c:c + 1] * o_cmp[rows] + gt[:, c + 1:c + 2] * o_sel[srows]
                 + gt[:, c + 2:c + 3] * o_win[rows])
            o_ref[sb * tq:(sb + 1) * tq, r * HEAD_DIM:(r + 1) * HEAD_DIM] = o.astype(bf16)


def _nsa(q, kcmp, vcmp, kv, eneg, gates, cast_a, cast_b, *, B, T, tq, n_sub):
    G, R = N_KV_GROUPS, HEADS_PER_GROUP
    tk = n_sub * tq
    nt = T // tk
    n_cmp = kcmp.shape[2]
    assert tq % LANES == 0 and tq & (tq - 1) == 0 and T % tk == 0 and T >= WINDOW + tq
    assert WINDOW % tq == 0 and WINDOW & (WINDOW - 1) == 0 and (R * tq) & (R * tq - 1) == 0
    rowblk = lambda b, g, i: (b * nt + i, g)
    cmp_spec = pl.BlockSpec((1, 1, n_cmp, HEAD_DIM), lambda b, g, i: (b, g, 0, 0))
    kvspec = lambda sec: pl.BlockSpec((T, HEAD_DIM), lambda b, g, i: (b, sec * G + g))
    n_steps = B * G * nt
    cast_spec = lambda w: pl.BlockSpec((w.shape[0] // n_steps, w.shape[1]), lambda b, g, i: ((b * G + g) * nt + i, 0))
    return pl.pallas_call(
        functools.partial(_nsa_kernel, tq=tq, n_sub=n_sub, T=T),
        grid=(B, G, nt),
        in_specs=[
            pl.BlockSpec((tk, R * HEAD_DIM), rowblk),
            cmp_spec, cmp_spec,
            kvspec(0), kvspec(1), kvspec(2), kvspec(3),
            pl.BlockSpec((T, LANES), lambda b, g, i: (0, 0)),
            pl.BlockSpec((tk, LANES), lambda b, g, i: (b * nt + i, 0)),
            cast_spec(cast_a), cast_spec(cast_b),
        ],
        out_specs=[pl.BlockSpec((tk, R * HEAD_DIM), rowblk), cast_spec(cast_a), cast_spec(cast_b)],
        out_shape=[jax.ShapeDtypeStruct((B * T, G * R * HEAD_DIM), bf16), jax.ShapeDtypeStruct(cast_a.shape, bf16),
                   jax.ShapeDtypeStruct(cast_b.shape, bf16)],
        scratch_shapes=[
            pltpu.VMEM((T // tk, n_sub * R * tq, tk), f32),
            pltpu.VMEM((n_sub * R * tq, LANES), f32),
            pltpu.VMEM((n_sub * R * tq, LANES), f32),
            pltpu.VMEM((n_sub * R * tq, HEAD_DIM), f32),
        ],
        compiler_params=pltpu.CompilerParams(
            dimension_semantics=("arbitrary", "arbitrary", "arbitrary"), vmem_limit_bytes=VMEM_LIMIT),
        name="nsa",
    )(q, kcmp, vcmp, kv, kv, kv, kv, eneg, gates, cast_a, cast_b)


def _out_proj_kernel(x_ref, u_ref, uh_ref, y_ref, pw_ref, ps_ref, wo_ref, o_ref, *, tm, T):
    i = pl.program_id(0)
    tiles_per_seq = T // tm
    keep_halo = jnp.where((i % tiles_per_seq) == 0, 0.0, 1.0)
    u = u_ref[...]
    halo = uh_ref[...]
    tt = lax.broadcasted_iota(jnp.int32, (tm, tm), 0)
    ss = lax.broadcasted_iota(jnp.int32, (tm, tm), 1)
    th = lax.broadcasted_iota(jnp.int32, (tm, POOL_HALO), 0)
    rh = lax.broadcasted_iota(jnp.int32, (tm, POOL_HALO), 1)
    tseq = (i % tiles_per_seq) * tm + lax.broadcasted_iota(jnp.int32, (tm, 1), 0)
    n_pool = u.shape[1]
    gd = n_pool // len(POOL_WINDOWS)
    acc = x_ref[...] + _dot(y_ref[...], wo_ref[n_pool:, :])
    for gi, w in enumerate(POOL_WINDOWS):
        cols = slice(gi * gd, (gi + 1) * gd)
        band = jnp.where((tt - ss >= 0) & (tt - ss < w), 1.0, 0.0).astype(bf16)
        band_h = jnp.where(th + POOL_HALO - rh < w, 1.0, 0.0).astype(bf16)
        ug = u[:, cols]
        win_sum = _dot(band, ug) + keep_halo * _dot(band_h, halo[:, cols])
        count = jnp.minimum(tseq + 1, w).astype(f32)
        pooled = win_sum / count - ug.astype(f32)
        yg = _dot(pooled.astype(bf16), pw_ref[gi]) * ps_ref[:, cols]
        acc = acc + _dot(yg.astype(bf16), wo_ref[cols, :])
    o_ref[...] = acc


def _out_proj(x2, u, y_nsa, pool_w, pool_scale, w_out, *, T, tm):
    n_rows, d = x2.shape
    n_pool = u.shape[1]
    hb = tm // POOL_HALO
    row = lambda i: (i, 0)
    const2 = lambda i: (0, 0)
    return pl.pallas_call(
        functools.partial(_out_proj_kernel, tm=tm, T=T),
        grid=(n_rows // tm,),
        in_specs=[
            pl.BlockSpec((tm, d), row),
            pl.BlockSpec((tm, n_pool), row),
            pl.BlockSpec((POOL_HALO, n_pool), lambda i: (jnp.maximum(i * hb - 1, 0), 0)),
            pl.BlockSpec((tm, y_nsa.shape[1]), row),
            pl.BlockSpec(pool_w.shape, lambda i: (0, 0, 0)),
            pl.BlockSpec((1, n_pool), const2),
            pl.BlockSpec(w_out.shape, const2, pipeline_mode=pl.Buffered(1)),
        ],
        out_specs=pl.BlockSpec((tm, d), row),
        out_shape=jax.ShapeDtypeStruct((n_rows, d), f32),
        compiler_params=pltpu.CompilerParams(
            dimension_semantics=("arbitrary",), vmem_limit_bytes=VMEM_LIMIT),
        name="out_proj",
    )(x2, u, u, y_nsa, pool_w, pool_scale, w_out)


def _ffn_kernel(h_ref, g_ref, wg_ref, wu_ref, wd_ref, o_ref, n_sc):
    @pl.when(pl.program_id(1) == 0)
    def _():
        h = h_ref[...]
        n_sc[...] = _rms(h, g_ref[...]).astype(bf16)
        o_ref[...] = h

    n = n_sc[...]
    half = wg_ref.shape[1] // 2
    acc = None
    for c in range(2):
        cols = slice(c * half, (c + 1) * half)
        gate = _dot(n, wg_ref[:, cols])
        up = _dot(n, wu_ref[:, cols])
        a = gate * jax.nn.sigmoid(gate) * up
        d = _dot(a.astype(bf16), wd_ref[cols, :])
        acc = d if acc is None else acc + d
    o_ref[...] += acc


def _ffn(h, g, w_gate, w_up, w_down, *, tm, tf):
    n_rows, d = h.shape
    d_ff = w_gate.shape[1]
    return pl.pallas_call(
        _ffn_kernel,
        grid=(n_rows // tm, d_ff // tf),
        in_specs=[
            pl.BlockSpec((tm, d), lambda i, f: (i, 0)),
            pl.BlockSpec((1, d), lambda i, f: (0, 0)),
            pl.BlockSpec((d, tf), lambda i, f: (0, f)),
            pl.BlockSpec((d, tf), lambda i, f: (0, f)),
            pl.BlockSpec((tf, d), lambda i, f: (f, 0)),
        ],
        out_specs=pl.BlockSpec((tm, d), lambda i, f: (i, 0)),
        out_shape=jax.ShapeDtypeStruct((n_rows, d), f32),
        scratch_shapes=[pltpu.VMEM((tm, d), bf16)],
        compiler_params=pltpu.CompilerParams(
            dimension_semantics=("arbitrary", "arbitrary"), vmem_limit_bytes=VMEM_LIMIT),
        name="ffn",
    )(h, g, w_gate, w_up, w_down)


def _ple_kernel(h_ref, p_ref, g_ref, wg_ref, wp_ref, fg_ref, o_ref):
    h = h_ref[...]
    n = _rms(h, g_ref[...]).astype(bf16)
    gate = jax.nn.sigmoid(_dot(n, wg_ref[...]))
    h = h + _dot(p_ref[...].astype(bf16), wp_ref[...]) * gate
    o_ref[...] = _rms(h, fg_ref[...])


def _ple(h, p2, g, w_gate, w_proj, fg, *, tm):
    n_rows, d = h.shape
    row = lambda i: (i, 0)
    const = lambda i: (0, 0)
    return pl.pallas_call(
        _ple_kernel,
        grid=(n_rows // tm,),
        in_specs=[
            pl.BlockSpec((tm, d), row),
            pl.BlockSpec((tm, p2.shape[1]), row),
            pl.BlockSpec((1, d), const),
            pl.BlockSpec(w_gate.shape, const, pipeline_mode=pl.Buffered(1)),
            pl.BlockSpec(w_proj.shape, const),
            pl.BlockSpec((1, d), const),
        ],
        out_specs=pl.BlockSpec((tm, d), row),
        out_shape=jax.ShapeDtypeStruct((n_rows, d), f32),
        compiler_params=pltpu.CompilerParams(
            dimension_semantics=("arbitrary",), vmem_limit_bytes=VMEM_LIMIT),
        name="ple",
    )(h, p2, g, w_gate, w_proj, fg)


def _rope_tables(T):
    pos = jnp.arange(T, dtype=f32)
    inv_freq = ROPE_THETA ** (-jnp.arange(0, ROPE_DIM, 2, dtype=f32) / ROPE_DIM)
    ang = pos[:, None] * inv_freq[None, :]
    cos, sin = jnp.cos(ang), jnp.sin(ang)
    rest = HEAD_DIM - ROPE_DIM
    cosf = jnp.concatenate([cos, cos, jnp.ones((T, rest), f32)], axis=1)
    sinf = jnp.concatenate([-sin, sin, jnp.zeros((T, rest), f32)], axis=1)
    return cosf, sinf


def _mixer_ffn(h2, in_norm_g, w_in, pool_w, pool_scale, cmp_k_pe, cmp_k_w1, cmp_k_w2,
               cmp_v_pe, cmp_v_w1, cmp_v_w2, w_out, ffn_norm_g, w_gate, w_up, w_down, w_ple_gate, *, B, T):
    n_pool = pool_scale.shape[0]
    n_heads = N_KV_GROUPS * HEADS_PER_GROUP
    nsa_w = n_heads * HEAD_DIM
    kv_w = N_KV_GROUPS * HEAD_DIM
    assert w_in.shape[1] == n_pool + nsa_w + 6 * kv_w + n_heads * N_BRANCH

    o0 = n_pool + nsa_w + 6 * kv_w
    w_p = w_in.T
    w_gt = jnp.pad(w_p[o0:].astype(bf16), ((0, LANES - n_heads * N_BRANCH), (0, 0)))

    cosf, sinf = _rope_tables(T)
    u, q, kcvc, kv, gates, w_out_b, w_gate_b, w_ple_gate_b = _in_proj(
        h2, in_norm_g[None, :], w_p, w_gt, cosf, sinf, w_out, w_gate, w_ple_gate, T=T, tm=ROW_TILE,
        n_pool=n_pool, n_q=nsa_w)

    kcmp, vcmp = _compress(
        kcvc.reshape(B, T, kcvc.shape[1]),
        cmp_k_pe.reshape(1, -1), cmp_k_w1.astype(bf16), cmp_k_w2.astype(bf16),
        cmp_v_pe.reshape(1, -1), cmp_v_w1.astype(bf16), cmp_v_w2.astype(bf16), bb=COMPRESS_BATCH if B % COMPRESS_BATCH == 0 else 1)

    key_blk = jnp.arange(T, dtype=jnp.int32)[:, None] // SEL_BLOCK
    eneg = jnp.where(key_blk == jnp.arange(LANES, dtype=jnp.int32)[None, :], NEG, 0.0).astype(bf16)
    y_nsa, w_up_b, w_down_b = _nsa(q, kcmp, vcmp, kv, eneg, gates, w_up, w_down, B=B, T=T,
                                   tq=NSA_Q_TILE, n_sub=NSA_SUB_TILES)

    h2 = _out_proj(h2, u, y_nsa, pool_w.astype(bf16), pool_scale[None, :], w_out_b, T=T, tm=ROW_TILE)
    return _ffn(h2, ffn_norm_g[None, :], w_gate_b, w_up_b, w_down_b, tm=FFN_ROW_TILE, tf=FFN_FF_TILE), w_ple_gate_b


def kernel(x, p, in_norm_g, w_in, pool_w, pool_scale, cmp_k_pe, cmp_k_w1, cmp_k_w2, cmp_v_pe, cmp_v_w1,
           cmp_v_w2, w_out, ffn_norm_g, w_gate, w_up, w_down, ple_norm_g, w_ple_gate, w_ple_proj, final_norm_g):
    B, T, d = x.shape
    depth = w_in.shape[0]
    assert depth == 1, "the final rmsnorm is fused into the last layer's per-layer-embedding kernel"
    h2 = x.reshape(B * T, d)
    i = 0
    h2, w_ple_gate_b = _mixer_ffn(
        h2, in_norm_g[i], w_in[i], pool_w[i], pool_scale[i], cmp_k_pe[i], cmp_k_w1[i], cmp_k_w2[i],
        cmp_v_pe[i], cmp_v_w1[i], cmp_v_w2[i], w_out[i], ffn_norm_g[i], w_gate[i], w_up[i], w_down[i],
        w_ple_gate[i], B=B, T=T)
    out = _ple(h2, p[i].reshape(B * T, -1), ple_norm_g[i][None, :], w_ple_gate_b,
               w_ple_proj[i].astype(bf16), final_norm_g[None, :], tm=ROW_TILE)
    return out.reshape(B, T, d)
```

```python
import functools

import jax
import jax.numpy as jnp
import numpy as np
from jax import lax
from jax.experimental import pallas as pl
from jax.experimental.pallas import tpu as pltpu

EPS = 1e-6
NEG = -1e30
LANES = 128
POOL_WINDOWS = (2, 4, 8, 16)
POOL_HALO = 16
HEAD_DIM = 128
N_KV_GROUPS = 2
HEADS_PER_GROUP = 4
N_BRANCH = 3
CMP_BLOCK = 32
CMP_STRIDE = 16
SEL_BLOCK = 64
N_SELECT = 16
WINDOW = 512
ROPE_THETA = 500000.0
ROPE_DIM = HEAD_DIM // 4
ATTN_SCALE = HEAD_DIM ** -0.5
Q_SCALE = ATTN_SCALE * 1.4426950408889634
VMEM_LIMIT = 60 * 1024 * 1024
W_CHUNK = 256
IN_SEC = 512
ROW_TILE = 512
FFN_ROW_TILE = 1024
FFN_FF_TILE = 512
NSA_Q_TILE = 256
NSA_SUB_TILES = 2
COMPRESS_BATCH = 2

f32 = jnp.float32
bf16 = jnp.bfloat16


def _dot(a, b):
    return jnp.dot(a, b, preferred_element_type=f32)


def _dot_nt(a, b):
    return lax.dot_general(a, b, (((1,), (1,)), ((), ())), preferred_element_type=f32)


def _rms(x, g):
    return x * lax.rsqrt(jnp.mean(x * x, axis=-1, keepdims=True) + EPS) * g


def _rope(h, cosf, sinf, lane):
    half = ROPE_DIM // 2
    partner = jnp.where(lane < half, pltpu.roll(h, LANES - half, axis=1), pltpu.roll(h, half, axis=1))
    return h * cosf + partner * sinf


def _w_chunk_copy(w_hbm, stage_sc, sem, c):
    return pltpu.make_async_copy(w_hbm.at[pl.ds(c * W_CHUNK, W_CHUNK), :], stage_sc.at[c % 2], sem.at[c % 2])


def _in_proj_kernel(x_ref, g_ref, w_hbm, wgt_ref, cos_ref, sin_ref, ca_ref, cb_ref, cc_ref,
                    u_ref, q_ref, kcvc_ref, kv_ref, gt_ref, ca_out, cb_out, cc_out,
                    w_ref, stage_sc, sem):
    @pl.when(pl.program_id(0) == 0)
    def _():
        n_chunk = w_ref.shape[0] // W_CHUNK
        _w_chunk_copy(w_hbm, stage_sc, sem, 0).start()
        for c in range(n_chunk):
            if c + 1 < n_chunk:
                _w_chunk_copy(w_hbm, stage_sc, sem, c + 1).start()
            _w_chunk_copy(w_hbm, stage_sc, sem, c).wait()
            w_ref[c * W_CHUNK:(c + 1) * W_CHUNK, :] = stage_sc[c % 2].astype(bf16)

    def section(s):
        return w_ref[s * IN_SEC:(s + 1) * IN_SEC, :]

    ca_out[...] = ca_ref[...].astype(bf16)
    cb_out[...] = cb_ref[...].astype(bf16)
    cc_out[...] = cc_ref[...].astype(bf16)
    a = _rms(x_ref[...], g_ref[...]).astype(bf16)
    cosf = cos_ref[...]
    sinf = sin_ref[...]
    lane = lax.broadcasted_iota(jnp.int32, cosf.shape, 1)
    heads = IN_SEC // HEAD_DIM
    s = 0
    for c in range(0, u_ref.shape[1], IN_SEC):
        u_ref[:, c:c + IN_SEC] = _dot_nt(a, section(s)).astype(bf16)
        s += 1
    for c in range(0, q_ref.shape[1], IN_SEC):
        acc = _dot_nt(a, section(s))
        s += 1
        for j in range(heads):
            h = _rope(acc[:, j * HEAD_DIM:(j + 1) * HEAD_DIM], cosf, sinf, lane) * Q_SCALE
            q_ref[:, c + j * HEAD_DIM:c + (j + 1) * HEAD_DIM] = h.astype(bf16)
    for dst, base, dt in ((kcvc_ref, 0, f32), (kv_ref, 0, bf16), (kv_ref, IN_SEC, bf16)):
        acc = _dot_nt(a, section(s))
        s += 1
        for j in range(heads):
            h = acc[:, j * HEAD_DIM:(j + 1) * HEAD_DIM]
            if j < N_KV_GROUPS:
                h = _rope(h, cosf, sinf, lane)
            dst[:, base + j * HEAD_DIM:base + (j + 1) * HEAD_DIM] = h.astype(dt)
    gt_ref[...] = _dot_nt(a, wgt_ref[...])


def _in_proj(x2, g, w_p, w_gt, cosf, sinf, cast_a, cast_b, cast_c, *, T, tm, n_pool, n_q):
    n_rows, d = x2.shape
    assert n_pool % IN_SEC == 0 and n_q % IN_SEC == 0 and 2 * N_KV_GROUPS * HEAD_DIM == IN_SEC
    n_w = (w_p.shape[0] // W_CHUNK) * W_CHUNK
    assert w_p.dtype == f32 and n_w + w_gt.shape[0] >= w_p.shape[0] and n_w == n_pool + n_q + 3 * IN_SEC
    tiles_per_seq = T // tm
    n_steps = n_rows // tm
    row = lambda i: (i, 0)
    cast_spec = lambda w: pl.BlockSpec((w.shape[0] // n_steps, w.shape[1]), row)
    const = lambda i: (0, 0)
    tab = lambda i: (i % tiles_per_seq, 0)
    return pl.pallas_call(
        _in_proj_kernel,
        grid=(n_rows // tm,),
        in_specs=[
            pl.BlockSpec((tm, d), row),
            pl.BlockSpec((1, d), const),
            pl.BlockSpec(memory_space=pl.ANY),
            pl.BlockSpec((LANES, d), const),
            pl.BlockSpec((tm, LANES), tab),
            pl.BlockSpec((tm, LANES), tab),
            cast_spec(cast_a), cast_spec(cast_b), cast_spec(cast_c),
        ],
        out_specs=[
            pl.BlockSpec((tm, n_pool), row),
            pl.BlockSpec((tm, n_q), row),
            pl.BlockSpec((tm, IN_SEC), row),
            pl.BlockSpec((tm, 2 * IN_SEC), row),
            pl.BlockSpec((tm, LANES), row),
            cast_spec(cast_a), cast_spec(cast_b), cast_spec(cast_c),
        ],
        out_shape=[
            jax.ShapeDtypeStruct((n_rows, n_pool), bf16),
            jax.ShapeDtypeStruct((n_rows, n_q), bf16),
            jax.ShapeDtypeStruct((n_rows, IN_SEC), f32),
            jax.ShapeDtypeStruct((n_rows, 2 * IN_SEC), bf16),
            jax.ShapeDtypeStruct((n_rows, LANES), f32),
            jax.ShapeDtypeStruct(cast_a.shape, bf16),
            jax.ShapeDtypeStruct(cast_b.shape, bf16),
            jax.ShapeDtypeStruct(cast_c.shape, bf16),
        ],
        scratch_shapes=[
            pltpu.VMEM((n_w, d), bf16),
            pltpu.VMEM((2, W_CHUNK, d), f32),
            pltpu.SemaphoreType.DMA((2,)),
        ],
        compiler_params=pltpu.CompilerParams(
            dimension_semantics=("arbitrary",), vmem_limit_bytes=VMEM_LIMIT),
        name="in_proj",
    )(x2, g, w_p, w_gt, cosf, sinf, cast_a, cast_b, cast_c)


def _gelu_tanh(x):
    return 0.5 * x * (1.0 + jnp.tanh(0.7978845608028654 * (x + 0.044715 * (x * x * x))))


def _compress_kernel(kc0_ref, kc1_ref, vc0_ref, vc1_ref, pek_ref, w1k_ref, w2k_ref, pev_ref, w1v_ref, w2v_ref,
                     kc_ref, vc_ref):
    bb = kc0_ref.shape[0]
    n_rows = kc0_ref.shape[1] // CMP_STRIDE
    half_k = (CMP_BLOCK // 2) * HEAD_DIM
    n_stack = bb * N_KV_GROUPS * n_rows
    rowi = lax.broadcasted_iota(jnp.int32, (n_stack, HEAD_DIM), 0)
    for srcs, pe_ref, w1_ref, w2_ref, o_ref in (((kc0_ref, kc1_ref), pek_ref, w1k_ref, w2k_ref, kc_ref),
                                                ((vc0_ref, vc1_ref), pev_ref, w1v_ref, w2v_ref, vc_ref)):
        x = jnp.concatenate(
            [jnp.concatenate([src[b, pl.ds(l, n_rows, stride=CMP_STRIDE), :] for l in range(CMP_STRIDE)], axis=1)
             for b in range(bb) for src in srcs], axis=0)
        za = _dot((x + pe_ref[:, :half_k]).astype(bf16), w1_ref[:half_k, :].astype(bf16))
        zb = _dot((x + pe_ref[:, half_k:]).astype(bf16), w1_ref[half_k:, :].astype(bf16))
        h = za + pltpu.roll(zb, n_stack - 1, axis=0)
        o = _dot(_gelu_tanh(h).astype(bf16), w2_ref[...].astype(bf16))
        o = jnp.where((rowi & (n_rows - 1)) < n_rows - 1, o, 0.0).astype(bf16)
        for b in range(bb):
            for g in range(N_KV_GROUPS):
                r0 = (b * N_KV_GROUPS + g) * n_rows
                o_ref[b, g] = o[r0:r0 + n_rows]


def _compress(r3, pek, w1k, w2k, pev, w1v, w2v, *, bb):
    B, T, width = r3.shape
    n_rows = T // CMP_STRIDE
    assert width == 2 * N_KV_GROUPS * HEAD_DIM and N_KV_GROUPS == 2 and B % bb == 0 and n_rows & (n_rows - 1) == 0
    const = lambda b: (0, 0)
    out = jax.ShapeDtypeStruct((B, N_KV_GROUPS, n_rows, HEAD_DIM), bf16)
    ospec = pl.BlockSpec((bb, N_KV_GROUPS, n_rows, HEAD_DIM), lambda b: (b, 0, 0, 0))
    sec = lambda s: pl.BlockSpec((bb, T, HEAD_DIM), lambda b: (b, 0, s))
    return pl.pallas_call(
        _compress_kernel,
        grid=(B // bb,),
        in_specs=[
            sec(0), sec(1), sec(2), sec(3),
            pl.BlockSpec(pek.shape, const), pl.BlockSpec(w1k.shape, const), pl.BlockSpec(w2k.shape, const),
            pl.BlockSpec(pev.shape, const), pl.BlockSpec(w1v.shape, const), pl.BlockSpec(w2v.shape, const),
        ],
        out_specs=[ospec, ospec],
        out_shape=[out, out],
        compiler_params=pltpu.CompilerParams(
            dimension_semantics=("arbitrary",), vmem_limit_bytes=VMEM_LIMIT),
        name="compress",
    )(r3, r3, r3, r3, pek, w1k, w2k, pev, w1v, w2v)


def _nsa_front(qs, t0, kc, vc, kw_ref, vw_ref, *, tq, T):
    R = HEADS_PER_GROUP
    M = R * tq
    n_cmp = kc.shape[0]
    n_sel = T // SEL_BLOCK

    wk = WINDOW + tq
    w0 = pl.multiple_of(jnp.maximum(t0 - WINDOW, 0), tq)
    sw = _dot_nt(qs, kw_ref[pl.ds(w0, wk), :])
    diff = (t0 - w0) + (lax.broadcasted_iota(jnp.int32, (M, wk), 0) & (tq - 1)) \
        - lax.broadcasted_iota(jnp.int32, (M, wk), 1)
    sw = jnp.where((diff & -WINDOW) == 0, sw, NEG)
    ew = jnp.exp2(sw - jnp.max(sw, axis=1, keepdims=True))
    o_win = _dot(ew.astype(bf16), vw_ref[pl.ds(w0, wk), :]) / jnp.sum(ew, axis=1, keepdims=True)

    s = _dot_nt(qs, kc)
    trow = t0 + (lax.broadcasted_iota(jnp.int32, (M, n_cmp), 0) & (tq - 1))
    ncol = lax.broadcasted_iota(jnp.int32, (M, n_cmp), 1)
    valid = (CMP_STRIDE * ncol + (CMP_BLOCK - 1)) <= trow
    s = jnp.where(valid, s, NEG)
    e = jnp.exp2(s - jnp.max(s, axis=1, keepdims=True))
    p = jnp.where(valid, e / jnp.sum(e, axis=1, keepdims=True), 0.0)
    o_cmp = _dot(p.astype(bf16), vc)

    psum = p[0:tq]
    for r in range(1, R):
        psum = psum + p[r * tq:(r + 1) * tq]
    p_hi = psum.astype(bf16)
    p_lo = (psum - p_hi.astype(f32)).astype(bf16)
    sb = lax.broadcasted_iota(jnp.int32, (n_sel, n_cmp), 0) * SEL_BLOCK
    cb = lax.broadcasted_iota(jnp.int32, (n_sel, n_cmp), 1) * CMP_STRIDE
    ov_t = jnp.where((cb < sb + SEL_BLOCK) & (cb + CMP_BLOCK > sb), 1.0, 0.0).astype(bf16)
    imp_t = _dot_nt(ov_t, p_hi) + _dot_nt(ov_t, p_lo)

    jj = lax.broadcasted_iota(jnp.int32, (n_sel, tq), 0)
    cur = jnp.right_shift(t0 + lax.broadcasted_iota(jnp.int32, (n_sel, tq), 1), SEL_BLOCK.bit_length() - 1)
    forced = (jj == 0) | (jj == cur) | (jj == cur - 1)
    v = jnp.where(jj > cur, -jnp.inf, jnp.where(forced, jnp.inf, imp_t))
    sub = 8
    slabs = [v[k * sub:(k + 1) * sub] for k in range(n_sel // sub)]
    ranks = [jnp.zeros((sub, tq), f32) for _ in slabs]
    j_in = lax.broadcasted_iota(jnp.int32, (sub, tq), 0)
    for c in range(n_sel):
        v_c = v[c:c + 1, :]
        for k, v_k in enumerate(slabs):
            if k * sub > c:
                ahead = jnp.where(v_c >= v_k, 1.0, 0.0)
            elif (k + 1) * sub <= c:
                ahead = jnp.where(v_c > v_k, 1.0, 0.0)
            else:
                ahead = jnp.where(j_in > c - k * sub, jnp.where(v_c >= v_k, 1.0, 0.0),
                                  jnp.where(v_c > v_k, 1.0, 0.0))
            ranks[k] = ranks[k] + ahead
    rank = jnp.concatenate(ranks, axis=0)
    unsel_t = jnp.where(rank < float(min(N_SELECT, n_sel)), 0.0, 1.0)
    unsel_t = jnp.concatenate([unsel_t, jnp.zeros((LANES - n_sel, tq), f32)], axis=0)
    return o_win, o_cmp, unsel_t.T.astype(bf16)


def _nsa_kernel(q_ref, kc_ref, vc_ref, ks_ref, vs_ref, kw_ref, vw_ref, eneg_ref, gt_ref, ca_ref, cb_ref,
                o_ref, ca_out, cb_out, s_sc, mx_sc, ls_sc, acc_sc, *, tq, n_sub, T):
    ca_out[...] = ca_ref[...].astype(bf16)
    cb_out[...] = cb_ref[...].astype(bf16)
    R = HEADS_PER_GROUP
    M = R * tq
    tk = n_sub * tq
    i = pl.program_id(2)
    t0 = i * tk
    kc = kc_ref[0, 0]
    vc = vc_ref[0, 0]

    fronts = []
    q_rows = []
    for sb in range(n_sub):
        q = q_ref[sb * tq:(sb + 1) * tq, :]
        qs = jnp.concatenate([q[:, r * HEAD_DIM:(r + 1) * HEAD_DIM] for r in range(R)], axis=0)
        o_win, o_cmp, unsel = _nsa_front(qs, t0 + sb * tq, kc, vc, kw_ref, vw_ref, tq=tq, T=T)
        fronts.append((o_win, o_cmp))
        q_rows.append(jnp.concatenate([qs, jnp.concatenate([unsel] * R, axis=0)], axis=1))
    q_aug = jnp.concatenate(q_rows, axis=0)
    MM = n_sub * M

    n_chunk = tk // LANES
    mx_sc[...] = jnp.full(mx_sc.shape, NEG, f32)

    def score_step(kt, causal):
        k0 = pl.multiple_of(kt * tk, tk)
        k_aug = jnp.concatenate([ks_ref[pl.ds(k0, tk), :], eneg_ref[pl.ds(k0, tk), :]], axis=1)
        sc = _dot_nt(q_aug, k_aug)
        if causal:
            rowi = lax.broadcasted_iota(jnp.int32, (MM, tk), 0)
            qoff = jnp.right_shift(rowi, M.bit_length() - 1) * tq + (rowi & (tq - 1))
            sc = jnp.where(qoff >= lax.broadcasted_iota(jnp.int32, (MM, tk), 1), sc, NEG)
        s_sc[kt] = sc
        part = sc[:, :LANES]
        for c in range(1, n_chunk):
            part = jnp.maximum(part, sc[:, c * LANES:(c + 1) * LANES])
        mx_sc[...] = jnp.maximum(mx_sc[...], part)

    def score_body(kt, carry):
        score_step(kt, False)
        return carry

    lax.fori_loop(0, i, score_body, 0)
    score_step(i, True)
    mx_sc[...] = jnp.broadcast_to(jnp.max(mx_sc[...], axis=1, keepdims=True), mx_sc.shape)
    ls_sc[...] = jnp.zeros(ls_sc.shape, f32)
    acc_sc[...] = jnp.zeros(acc_sc.shape, f32)

    def pv_body(kt, carry):
        k0 = pl.multiple_of(kt * tk, tk)
        sc = s_sc[kt]
        mb = mx_sc[...]
        ps = [jnp.exp2(sc[:, c * LANES:(c + 1) * LANES] - mb) for c in range(n_chunk)]
        ls_sc[...] += functools.reduce(lambda a, b: a + b, ps)
        acc_sc[...] += _dot(jnp.concatenate(ps, axis=1).astype(bf16), vs_ref[pl.ds(k0, tk), :])
        return carry

    lax.fori_loop(0, i + 1, pv_body, 0)
    o_sel = acc_sc[...] / jnp.sum(ls_sc[...], axis=1, keepdims=True)

    gt_all = jax.nn.sigmoid(gt_ref[...])
    gt_all = jnp.where(pl.program_id(1) == 0, gt_all, pltpu.roll(gt_all, LANES - R * N_BRANCH, axis=1))
    for sb, (o_win, o_cmp) in enumerate(fronts):
        gt = gt_all[sb * tq:(sb + 1) * tq]
        for r in range(R):
            rows = slice(r * tq, (r + 1) * tq)
            srows = slice(sb * M + r * tq, sb * M + (r + 1) * tq)
            c = r * N_BRANCH
            o = (gt[:, c:c + 1] * o_cmp[rows] + gt[:, c + 1:c + 2] * o_sel[srows]
                 + gt[:, c + 2:c + 3] * o_win[rows])
            o_ref[sb * tq:(sb + 1) * tq, r * HEAD_DIM:(r + 1) * HEAD_DIM] = o.astype(bf16)


def _nsa(q, kcmp, vcmp, kv, eneg, gates, cast_a, cast_b, *, B, T, tq, n_sub):
    G, R = N_KV_GROUPS, HEADS_PER_GROUP
    tk = n_sub * tq
    nt = T // tk
    n_cmp = kcmp.shape[2]
    assert tq % LANES == 0 and tq & (tq - 1) == 0 and T % tk == 0 and T >= WINDOW + tq
    assert WINDOW % tq == 0 and WINDOW & (WINDOW - 1) == 0 and (R * tq) & (R * tq - 1) == 0
    rowblk = lambda b, g, i: (b * nt + i, g)
    cmp_spec = pl.BlockSpec((1, 1, n_cmp, HEAD_DIM), lambda b, g, i: (b, g, 0, 0))
    kvspec = lambda sec: pl.BlockSpec((T, HEAD_DIM), lambda b, g, i: (b, sec * G + g))
    n_steps = B * G * nt
    cast_spec = lambda w: pl.BlockSpec((w.shape[0] // n_steps, w.shape[1]), lambda b, g, i: ((b * G + g) * nt + i, 0))
    return pl.pallas_call(
        functools.partial(_nsa_kernel, tq=tq, n_sub=n_sub, T=T),
        grid=(B, G, nt),
        in_specs=[
            pl.BlockSpec((tk, R * HEAD_DIM), rowblk),
            cmp_spec, cmp_spec,
            kvspec(0), kvspec(1), kvspec(2), kvspec(3),
            pl.BlockSpec((T, LANES), lambda b, g, i: (0, 0)),
            pl.BlockSpec((tk, LANES), lambda b, g, i: (b * nt + i, 0)),
            cast_spec(cast_a), cast_spec(cast_b),
        ],
        out_specs=[pl.BlockSpec((tk, R * HEAD_DIM), rowblk), cast_spec(cast_a), cast_spec(cast_b)],
        out_shape=[jax.ShapeDtypeStruct((B * T, G * R * HEAD_DIM), bf16), jax.ShapeDtypeStruct(cast_a.shape, bf16),
                   jax.ShapeDtypeStruct(cast_b.shape, bf16)],
        scratch_shapes=[
            pltpu.VMEM((T // tk, n_sub * R * tq, tk), f32),
            pltpu.VMEM((n_sub * R * tq, LANES), f32),
            pltpu.VMEM((n_sub * R * tq, LANES), f32),
            pltpu.VMEM((n_sub * R * tq, HEAD_DIM), f32),
        ],
        compiler_params=pltpu.CompilerParams(
            dimension_semantics=("arbitrary", "arbitrary", "arbitrary"), vmem_limit_bytes=VMEM_LIMIT),
        name="nsa",
    )(q, kcmp, vcmp, kv, kv, kv, kv, eneg, gates, cast_a, cast_b)


def _out_proj_kernel(x_ref, u_ref, uh_ref, y_ref, pw_ref, ps_ref, wo_ref, o_ref, *, tm, T):
    i = pl.program_id(0)
    tiles_per_seq = T // tm
    keep_halo = jnp.where((i % tiles_per_seq) == 0, 0.0, 1.0)
    u = u_ref[...]
    halo = uh_ref[...]
    tt = lax.broadcasted_iota(jnp.int32, (tm, tm), 0)
    ss = lax.broadcasted_iota(jnp.int32, (tm, tm), 1)
    th = lax.broadcasted_iota(jnp.int32, (tm, POOL_HALO), 0)
    rh = lax.broadcasted_iota(jnp.int32, (tm, POOL_HALO), 1)
    tseq = (i % tiles_per_seq) * tm + lax.broadcasted_iota(jnp.int32, (tm, 1), 0)
    n_pool = u.shape[1]
    gd = n_pool // len(POOL_WINDOWS)
    acc = x_ref[...] + _dot(y_ref[...], wo_ref[n_pool:, :])
    for gi, w in enumerate(POOL_WINDOWS):
        cols = slice(gi * gd, (gi + 1) * gd)
        band = jnp.where((tt - ss >= 0) & (tt - ss < w), 1.0, 0.0).astype(bf16)
        band_h = jnp.where(th + POOL_HALO - rh < w, 1.0, 0.0).astype(bf16)
        ug = u[:, cols]
        win_sum = _dot(band, ug) + keep_halo * _dot(band_h, halo[:, cols])
        count = jnp.minimum(tseq + 1, w).astype(f32)
        pooled = win_sum / count - ug.astype(f32)
        yg = _dot(pooled.astype(bf16), pw_ref[gi].astype(bf16)) * ps_ref[:, cols]
        acc = acc + _dot(yg.astype(bf16), wo_ref[cols, :])
    o_ref[...] = acc


def _out_proj(x2, u, y_nsa, pool_w, pool_scale, w_out, *, T, tm):
    n_rows, d = x2.shape
    n_pool = u.shape[1]
    hb = tm // POOL_HALO
    row = lambda i: (i, 0)
    const2 = lambda i: (0, 0)
    return pl.pallas_call(
        functools.partial(_out_proj_kernel, tm=tm, T=T),
        grid=(n_rows // tm,),
        in_specs=[
            pl.BlockSpec((tm, d), row),
            pl.BlockSpec((tm, n_pool), row),
            pl.BlockSpec((POOL_HALO, n_pool), lambda i: (jnp.maximum(i * hb - 1, 0), 0)),
            pl.BlockSpec((tm, y_nsa.shape[1]), row),
            pl.BlockSpec(pool_w.shape, lambda i: (0, 0, 0)),
            pl.BlockSpec((1, n_pool), const2),
            pl.BlockSpec(w_out.shape, const2, pipeline_mode=pl.Buffered(1)),
        ],
        out_specs=pl.BlockSpec((tm, d), row),
        out_shape=jax.ShapeDtypeStruct((n_rows, d), f32),
        compiler_params=pltpu.CompilerParams(
            dimension_semantics=("arbitrary",), vmem_limit_bytes=VMEM_LIMIT),
        name="out_proj",
    )(x2, u, u, y_nsa, pool_w, pool_scale, w_out)


def _ffn_kernel(h_ref, g_ref, wg_ref, wu_ref, wd_ref, o_ref, n_sc):
    @pl.when(pl.program_id(1) == 0)
    def _():
        h = h_ref[...]
        n_sc[...] = _rms(h, g_ref[...]).astype(bf16)
        o_ref[...] = h

    n = n_sc[...]
    half = wg_ref.shape[1] // 2
    acc = None
    for c in range(2):
        cols = slice(c * half, (c + 1) * half)
        gate = _dot(n, wg_ref[:, cols])
        up = _dot(n, wu_ref[:, cols])
        a = gate * jax.nn.sigmoid(gate) * up
        d = _dot(a.astype(bf16), wd_ref[cols, :])
        acc = d if acc is None else acc + d
    o_ref[...] += acc


def _ffn(h, g, w_gate, w_up, w_down, *, tm, tf):
    n_rows, d = h.shape
    d_ff = w_gate.shape[1]
    return pl.pallas_call(
        _ffn_kernel,
        grid=(n_rows // tm, d_ff // tf),
        in_specs=[
            pl.BlockSpec((tm, d), lambda i, f: (i, 0)),
            pl.BlockSpec((1, d), lambda i, f: (0, 0)),
            pl.BlockSpec((d, tf), lambda i, f: (0, f)),
            pl.BlockSpec((d, tf), lambda i, f: (0, f)),
            pl.BlockSpec((tf, d), lambda i, f: (f, 0)),
        ],
        out_specs=pl.BlockSpec((tm, d), lambda i, f: (i, 0)),
        out_shape=jax.ShapeDtypeStruct((n_rows, d), f32),
        scratch_shapes=[pltpu.VMEM((tm, d), bf16)],
        compiler_params=pltpu.CompilerParams(
            dimension_semantics=("arbitrary", "arbitrary"), vmem_limit_bytes=VMEM_LIMIT),
        name="ffn",
    )(h, g, w_gate, w_up, w_down)


def _ple_kernel(h_ref, p_ref, g_ref, wg_ref, wp_ref, fg_ref, o_ref):
    h = h_ref[...]
    n = _rms(h, g_ref[...]).astype(bf16)
    gate = jax.nn.sigmoid(_dot(n, wg_ref[...]))
    h = h + _dot(p_ref[...].astype(bf16), wp_ref[...].astype(bf16)) * gate
    o_ref[...] = _rms(h, fg_ref[...])


def _ple(h, p2, g, w_gate, w_proj, fg, *, tm):
    n_rows, d = h.shape
    row = lambda i: (i, 0)
    const = lambda i: (0, 0)
    return pl.pallas_call(
        _ple_kernel,
        grid=(n_rows // tm,),
        in_specs=[
            pl.BlockSpec((tm, d), row),
            pl.BlockSpec((tm, p2.shape[1]), row),
            pl.BlockSpec((1, d), const),
            pl.BlockSpec(w_gate.shape, const, pipeline_mode=pl.Buffered(1)),
            pl.BlockSpec(w_proj.shape, const),
            pl.BlockSpec((1, d), const),
        ],
        out_specs=pl.BlockSpec((tm, d), row),
        out_shape=jax.ShapeDtypeStruct((n_rows, d), f32),
        compiler_params=pltpu.CompilerParams(
            dimension_semantics=("arbitrary",), vmem_limit_bytes=VMEM_LIMIT),
        name="ple",
    )(h, p2, g, w_gate, w_proj, fg)


def _rope_tables(T):
    pos = np.arange(T, dtype=np.float64)
    inv_freq = ROPE_THETA ** (-np.arange(0, ROPE_DIM, 2, dtype=np.float64) / ROPE_DIM)
    ang = pos[:, None] * inv_freq[None, :]
    cos, sin = np.cos(ang), np.sin(ang)
    rest = HEAD_DIM - ROPE_DIM
    cosf = np.concatenate([cos, cos, np.ones((T, rest))], axis=1)
    sinf = np.concatenate([-sin, sin, np.zeros((T, rest))], axis=1)
    return jnp.asarray(cosf, f32), jnp.asarray(sinf, f32)


def _block_bias_table(T):
    key_blk = np.arange(T)[:, None] // SEL_BLOCK
    return jnp.asarray(np.where(key_blk == np.arange(LANES)[None, :], NEG, 0.0).astype(bf16))


def _mixer_ffn(h2, in_norm_g, w_in, pool_w, pool_scale, cmp_k_pe, cmp_k_w1, cmp_k_w2,
               cmp_v_pe, cmp_v_w1, cmp_v_w2, w_out, ffn_norm_g, w_gate, w_up, w_down, w_ple_gate, *, B, T):
    n_pool = pool_scale.shape[0]
    n_heads = N_KV_GROUPS * HEADS_PER_GROUP
    nsa_w = n_heads * HEAD_DIM
    kv_w = N_KV_GROUPS * HEAD_DIM
    assert w_in.shape[1] == n_pool + nsa_w + 6 * kv_w + n_heads * N_BRANCH

    o0 = n_pool + nsa_w + 6 * kv_w
    w_p = w_in.T
    w_gt = jnp.pad(w_p[o0:].astype(bf16), ((0, LANES - n_heads * N_BRANCH), (0, 0)))

    cosf, sinf = _rope_tables(T)
    u, q, kcvc, kv, gates, w_out_b, w_gate_b, w_ple_gate_b = _in_proj(
        h2, in_norm_g[None, :], w_p, w_gt, cosf, sinf, w_out, w_gate, w_ple_gate, T=T, tm=ROW_TILE,
        n_pool=n_pool, n_q=nsa_w)

    kcmp, vcmp = _compress(
        kcvc.reshape(B, T, kcvc.shape[1]),
        cmp_k_pe.reshape(1, -1), cmp_k_w1, cmp_k_w2, cmp_v_pe.reshape(1, -1), cmp_v_w1, cmp_v_w2,
        bb=COMPRESS_BATCH if B % COMPRESS_BATCH == 0 else 1)

    eneg = _block_bias_table(T)
    y_nsa, w_up_b, w_down_b = _nsa(q, kcmp, vcmp, kv, eneg, gates, w_up, w_down, B=B, T=T,
                                   tq=NSA_Q_TILE, n_sub=NSA_SUB_TILES)

    h2 = _out_proj(h2, u, y_nsa, pool_w, pool_scale[None, :], w_out_b, T=T, tm=ROW_TILE)
    return _ffn(h2, ffn_norm_g[None, :], w_gate_b, w_up_b, w_down_b, tm=FFN_ROW_TILE, tf=FFN_FF_TILE), w_ple_gate_b


def kernel(x, p, in_norm_g, w_in, pool_w, pool_scale, cmp_k_pe, cmp_k_w1, cmp_k_w2, cmp_v_pe, cmp_v_w1,
           cmp_v_w2, w_out, ffn_norm_g, w_gate, w_up, w_down, ple_norm_g, w_ple_gate, w_ple_proj, final_norm_g):
    B, T, d = x.shape
    depth = w_in.shape[0]
    assert depth == 1, "the final rmsnorm is fused into the last layer's per-layer-embedding kernel"
    h2 = x.reshape(B * T, d)
    i = 0
    h2, w_ple_gate_b = _mixer_ffn(
        h2, in_norm_g[i], w_in[i], pool_w[i], pool_scale[i], cmp_k_pe[i], cmp_k_w1[i], cmp_k_w2[i],
        cmp_v_pe[i], cmp_v_w1[i], cmp_v_w2[i], w_out[i], ffn_norm_g[i], w_gate[i], w_up[i], w_down[i],
        w_ple_gate[i], B=B, T=T)
    out = _ple(h2, p[i].reshape(B * T, -1), ple_norm_g[i][None, :], w_ple_gate_b,
               w_ple_proj[i], final_norm_g[None, :], tm=ROW_TILE)
    return out.reshape(B, T, d)
```

```python
import functools

import jax
import jax.numpy as jnp
import numpy as np
from jax import lax
from jax.experimental import pallas as pl
from jax.experimental.pallas import tpu as pltpu

EPS = 1e-6
NEG = -1e30
LANES = 128
POOL_WINDOWS = (2, 4, 8, 16)
POOL_HALO = 16
HEAD_DIM = 128
N_KV_GROUPS = 2
HEADS_PER_GROUP = 4
N_BRANCH = 3
CMP_BLOCK = 32
CMP_STRIDE = 16
SEL_BLOCK = 64
N_SELECT = 16
WINDOW = 512
ROPE_THETA = 500000.0
ROPE_DIM = HEAD_DIM // 4
ATTN_SCALE = HEAD_DIM ** -0.5
Q_SCALE = ATTN_SCALE * 1.4426950408889634
VMEM_LIMIT = 60 * 1024 * 1024
W_CHUNK = 256
IN_SEC = 512
ROW_TILE = 512
FFN_ROW_TILE = 1024
FFN_FF_TILE = 512
NSA_Q_TILE = 256
NSA_SUB_TILES = 2
COMPRESS_BATCH = 2

f32 = jnp.float32
bf16 = jnp.bfloat16


def _dot(a, b):
    return jnp.dot(a, b, preferred_element_type=f32)


def _dot_nt(a, b):
    return lax.dot_general(a, b, (((1,), (1,)), ((), ())), preferred_element_type=f32)


def _rms(x, g):
    return x * lax.rsqrt(jnp.mean(x * x, axis=-1, keepdims=True) + EPS) * g


def _rope(h, cosf, sinf, lane):
    half = ROPE_DIM // 2
    partner = jnp.where(lane < half, pltpu.roll(h, LANES - half, axis=1), pltpu.roll(h, half, axis=1))
    return h * cosf + partner * sinf


def _w_chunk_copy(w_hbm, stage_sc, sem, c):
    return pltpu.make_async_copy(w_hbm.at[pl.ds(c * W_CHUNK, W_CHUNK), :], stage_sc.at[c % 2], sem.at[c % 2])


def _in_proj_kernel(x_ref, g_ref, w_hbm, wgt_ref, cos_ref, sin_ref, ca_ref, cb_ref, cc_ref,
                    u_ref, q_ref, kcvc_ref, kv_ref, gt_ref, ca_out, cb_out, cc_out,
                    w_ref, stage_sc, sem):
    @pl.when(pl.program_id(0) == 0)
    def _():
        n_chunk = w_ref.shape[0] // W_CHUNK
        _w_chunk_copy(w_hbm, stage_sc, sem, 0).start()
        for c in range(n_chunk):
            if c + 1 < n_chunk:
                _w_chunk_copy(w_hbm, stage_sc, sem, c + 1).start()
            _w_chunk_copy(w_hbm, stage_sc, sem, c).wait()
            w_ref[c * W_CHUNK:(c + 1) * W_CHUNK, :] = stage_sc[c % 2].astype(bf16)

    def section(s):
        return w_ref[s * IN_SEC:(s + 1) * IN_SEC, :]

    ca_out[...] = ca_ref[...].astype(bf16)
    cb_out[...] = cb_ref[...].astype(bf16)
    cc_out[...] = cc_ref[...].astype(bf16)
    a = _rms(x_ref[...], g_ref[...]).astype(bf16)
    cosf = cos_ref[...]
    sinf = sin_ref[...]
    lane = lax.broadcasted_iota(jnp.int32, cosf.shape, 1)
    heads = IN_SEC // HEAD_DIM
    s = 0
    for c in range(0, u_ref.shape[1], IN_SEC):
        u_ref[:, c:c + IN_SEC] = _dot_nt(a, section(s)).astype(bf16)
        s += 1
    for c in range(0, q_ref.shape[1], IN_SEC):
        acc = _dot_nt(a, section(s))
        s += 1
        for j in range(heads):
            h = _rope(acc[:, j * HEAD_DIM:(j + 1) * HEAD_DIM], cosf, sinf, lane) * Q_SCALE
            q_ref[:, c + j * HEAD_DIM:c + (j + 1) * HEAD_DIM] = h.astype(bf16)
    for dst, base, dt in ((kcvc_ref, 0, f32), (kv_ref, 0, bf16), (kv_ref, IN_SEC, bf16)):
        acc = _dot_nt(a, section(s))
        s += 1
        for j in range(heads):
            h = acc[:, j * HEAD_DIM:(j + 1) * HEAD_DIM]
            if j < N_KV_GROUPS:
                h = _rope(h, cosf, sinf, lane)
            dst[:, base + j * HEAD_DIM:base + (j + 1) * HEAD_DIM] = h.astype(dt)
    gt_ref[...] = _dot_nt(a, wgt_ref[...])


def _in_proj(x2, g, w_p, w_gt, cosf, sinf, cast_a, cast_b, cast_c, *, T, tm, n_pool, n_q):
    n_rows, d = x2.shape
    assert n_pool % IN_SEC == 0 and n_q % IN_SEC == 0 and 2 * N_KV_GROUPS * HEAD_DIM == IN_SEC
    n_w = (w_p.shape[0] // W_CHUNK) * W_CHUNK
    assert w_p.dtype == f32 and n_w + w_gt.shape[0] >= w_p.shape[0] and n_w == n_pool + n_q + 3 * IN_SEC
    tiles_per_seq = T // tm
    n_steps = n_rows // tm
    row = lambda i: (i, 0)
    cast_spec = lambda w: pl.BlockSpec((w.shape[0] // n_steps, w.shape[1]), row)
    const = lambda i: (0, 0)
    tab = lambda i: (i % tiles_per_seq, 0)
    return pl.pallas_call(
        _in_proj_kernel,
        grid=(n_rows // tm,),
        in_specs=[
            pl.BlockSpec((tm, d), row),
            pl.BlockSpec((1, d), const),
            pl.BlockSpec(memory_space=pl.ANY),
            pl.BlockSpec((LANES, d), const),
            pl.BlockSpec((tm, LANES), tab),
            pl.BlockSpec((tm, LANES), tab),
            cast_spec(cast_a), cast_spec(cast_b), cast_spec(cast_c),
        ],
        out_specs=[
            pl.BlockSpec((tm, n_pool), row),
            pl.BlockSpec((tm, n_q), row),
            pl.BlockSpec((tm, IN_SEC), row),
            pl.BlockSpec((tm, 2 * IN_SEC), row),
            pl.BlockSpec((tm, LANES), row),
            cast_spec(cast_a), cast_spec(cast_b), cast_spec(cast_c),
        ],
        out_shape=[
            jax.ShapeDtypeStruct((n_rows, n_pool), bf16),
            jax.ShapeDtypeStruct((n_rows, n_q), bf16),
            jax.ShapeDtypeStruct((n_rows, IN_SEC), f32),
            jax.ShapeDtypeStruct((n_rows, 2 * IN_SEC), bf16),
            jax.ShapeDtypeStruct((n_rows, LANES), f32),
            jax.ShapeDtypeStruct(cast_a.shape, bf16),
            jax.ShapeDtypeStruct(cast_b.shape, bf16),
            jax.ShapeDtypeStruct(cast_c.shape, bf16),
        ],
        scratch_shapes=[
            pltpu.VMEM((n_w, d), bf16),
            pltpu.VMEM((2, W_CHUNK, d), f32),
            pltpu.SemaphoreType.DMA((2,)),
        ],
        compiler_params=pltpu.CompilerParams(
            dimension_semantics=("arbitrary",), vmem_limit_bytes=VMEM_LIMIT),
        name="in_proj",
    )(x2, g, w_p, w_gt, cosf, sinf, cast_a, cast_b, cast_c)


def _gelu_tanh(x):
    return 0.5 * x * (1.0 + jnp.tanh(0.7978845608028654 * (x + 0.044715 * (x * x * x))))


def _compress_kernel(kc0_ref, kc1_ref, vc0_ref, vc1_ref, pek_ref, w1k_ref, w2k_ref, pev_ref, w1v_ref, w2v_ref,
                     kc_ref, vc_ref):
    bb = kc0_ref.shape[0]
    n_rows = kc0_ref.shape[1] // CMP_STRIDE
    half_k = (CMP_BLOCK // 2) * HEAD_DIM
    n_stack = bb * N_KV_GROUPS * n_rows
    rowi = lax.broadcasted_iota(jnp.int32, (n_stack, HEAD_DIM), 0)
    for srcs, pe_ref, w1_ref, w2_ref, o_ref in (((kc0_ref, kc1_ref), pek_ref, w1k_ref, w2k_ref, kc_ref),
                                                ((vc0_ref, vc1_ref), pev_ref, w1v_ref, w2v_ref, vc_ref)):
        x = jnp.concatenate(
            [jnp.concatenate([src[b, pl.ds(l, n_rows, stride=CMP_STRIDE), :] for l in range(CMP_STRIDE)], axis=1)
             for b in range(bb) for src in srcs], axis=0)
        za = _dot((x + pe_ref[:, :half_k]).astype(bf16), w1_ref[:half_k, :].astype(bf16))
        zb = _dot((x + pe_ref[:, half_k:]).astype(bf16), w1_ref[half_k:, :].astype(bf16))
        h = za + pltpu.roll(zb, n_stack - 1, axis=0)
        o = _dot(_gelu_tanh(h).astype(bf16), w2_ref[...].astype(bf16))
        o = jnp.where((rowi & (n_rows - 1)) < n_rows - 1, o, 0.0).astype(bf16)
        for b in range(bb):
            for g in range(N_KV_GROUPS):
                r0 = (b * N_KV_GROUPS + g) * n_rows
                o_ref[b, g] = o[r0:r0 + n_rows]


def _compress(r3, pek, w1k, w2k, pev, w1v, w2v, *, bb):
    B, T, width = r3.shape
    n_rows = T // CMP_STRIDE
    assert width == 2 * N_KV_GROUPS * HEAD_DIM and N_KV_GROUPS == 2 and B % bb == 0 and n_rows & (n_rows - 1) == 0
    const = lambda b: (0, 0)
    out = jax.ShapeDtypeStruct((B, N_KV_GROUPS, n_rows, HEAD_DIM), bf16)
    ospec = pl.BlockSpec((bb, N_KV_GROUPS, n_rows, HEAD_DIM), lambda b: (b, 0, 0, 0))
    sec = lambda s: pl.BlockSpec((bb, T, HEAD_DIM), lambda b: (b, 0, s))
    return pl.pallas_call(
        _compress_kernel,
        grid=(B // bb,),
        in_specs=[
            sec(0), sec(1), sec(2), sec(3),
            pl.BlockSpec(pek.shape, const), pl.BlockSpec(w1k.shape, const), pl.BlockSpec(w2k.shape, const),
            pl.BlockSpec(pev.shape, const), pl.BlockSpec(w1v.shape, const), pl.BlockSpec(w2v.shape, const),
        ],
        out_specs=[ospec, ospec],
        out_shape=[out, out],
        compiler_params=pltpu.CompilerParams(
            dimension_semantics=("arbitrary",), vmem_limit_bytes=VMEM_LIMIT),
        name="compress",
    )(r3, r3, r3, r3, pek, w1k, w2k, pev, w1v, w2v)


def _nsa_front(qs, t0, kc, vc, kw_ref, vw_ref, *, tq, T):
    R = HEADS_PER_GROUP
    M = R * tq
    n_cmp = kc.shape[0]
    n_sel = T // SEL_BLOCK

    wk = WINDOW + tq
    w0 = pl.multiple_of(jnp.maximum(t0 - WINDOW, 0), tq)
    sw = _dot_nt(qs, kw_ref[pl.ds(w0, wk), :])
    diff = (t0 - w0) + (lax.broadcasted_iota(jnp.int32, (M, wk), 0) & (tq - 1)) \
        - lax.broadcasted_iota(jnp.int32, (M, wk), 1)
    sw = jnp.where((diff & -WINDOW) == 0, sw, NEG)
    ew = jnp.exp2(sw - jnp.max(sw, axis=1, keepdims=True))
    o_win = _dot(ew.astype(bf16), vw_ref[pl.ds(w0, wk), :]) / jnp.sum(ew, axis=1, keepdims=True)

    s = _dot_nt(qs, kc)
    trow = t0 + (lax.broadcasted_iota(jnp.int32, (M, n_cmp), 0) & (tq - 1))
    ncol = lax.broadcasted_iota(jnp.int32, (M, n_cmp), 1)
    valid = (CMP_STRIDE * ncol + (CMP_BLOCK - 1)) <= trow
    s = jnp.where(valid, s, NEG)
    e = jnp.exp2(s - jnp.max(s, axis=1, keepdims=True))
    p = jnp.where(valid, e / jnp.sum(e, axis=1, keepdims=True), 0.0)
    o_cmp = _dot(p.astype(bf16), vc)

    psum = p[0:tq]
    for r in range(1, R):
        psum = psum + p[r * tq:(r + 1) * tq]
    p_hi = psum.astype(bf16)
    p_lo = (psum - p_hi.astype(f32)).astype(bf16)
    sb = lax.broadcasted_iota(jnp.int32, (n_sel, n_cmp), 0) * SEL_BLOCK
    cb = lax.broadcasted_iota(jnp.int32, (n_sel, n_cmp), 1) * CMP_STRIDE
    ov_t = jnp.where((cb < sb + SEL_BLOCK) & (cb + CMP_BLOCK > sb), 1.0, 0.0).astype(bf16)
    imp_t = _dot_nt(ov_t, p_hi) + _dot_nt(ov_t, p_lo)

    jj = lax.broadcasted_iota(jnp.int32, (n_sel, tq), 0)
    cur = jnp.right_shift(t0 + lax.broadcasted_iota(jnp.int32, (n_sel, tq), 1), SEL_BLOCK.bit_length() - 1)
    forced = (jj == 0) | (jj == cur) | (jj == cur - 1)
    v = jnp.where(jj > cur, -jnp.inf, jnp.where(forced, jnp.inf, imp_t))
    sub = 8
    slabs = [v[k * sub:(k + 1) * sub] for k in range(n_sel // sub)]
    ranks = [jnp.zeros((sub, tq), f32) for _ in slabs]
    j_in = lax.broadcasted_iota(jnp.int32, (sub, tq), 0)
    for c in range(n_sel):
        v_c = v[c:c + 1, :]
        for k, v_k in enumerate(slabs):
            if k * sub > c:
                ahead = jnp.where(v_c >= v_k, 1.0, 0.0)
            elif (k + 1) * sub <= c:
                ahead = jnp.where(v_c > v_k, 1.0, 0.0)
            else:
                ahead = jnp.where(j_in > c - k * sub, jnp.where(v_c >= v_k, 1.0, 0.0),
                                  jnp.where(v_c > v_k, 1.0, 0.0))
            ranks[k] = ranks[k] + ahead
    rank = jnp.concatenate(ranks, axis=0)
    unsel_t = jnp.where(rank < float(min(N_SELECT, n_sel)), 0.0, 1.0)
    unsel_t = jnp.concatenate([unsel_t, jnp.zeros((LANES - n_sel, tq), f32)], axis=0)
    return o_win, o_cmp, unsel_t.T.astype(bf16)


def _nsa_kernel(q_ref, kc_ref, vc_ref, ks_ref, vs_ref, kw_ref, vw_ref, eneg_ref, gexp_ref, gt_ref, ca_ref, cb_ref,
                o_ref, ca_out, cb_out, s_sc, mx_sc, ls_sc, acc_sc, *, tq, n_sub, T):
    ca_out[...] = ca_ref[...].astype(bf16)
    cb_out[...] = cb_ref[...].astype(bf16)
    R = HEADS_PER_GROUP
    M = R * tq
    tk = n_sub * tq
    i = pl.program_id(2)
    t0 = i * tk
    kc = kc_ref[0, 0]
    vc = vc_ref[0, 0]

    fronts = []
    q_rows = []
    for sb in range(n_sub):
        q = q_ref[sb * tq:(sb + 1) * tq, :]
        qs = jnp.concatenate([q[:, r * HEAD_DIM:(r + 1) * HEAD_DIM] for r in range(R)], axis=0)
        o_win, o_cmp, unsel = _nsa_front(qs, t0 + sb * tq, kc, vc, kw_ref, vw_ref, tq=tq, T=T)
        fronts.append((o_win, o_cmp))
        q_rows.append(jnp.concatenate([qs, jnp.concatenate([unsel] * R, axis=0)], axis=1))
    q_aug = jnp.concatenate(q_rows, axis=0)
    MM = n_sub * M

    n_chunk = tk // LANES
    mx_sc[...] = jnp.full(mx_sc.shape, NEG, f32)

    def score_step(kt, causal):
        k0 = pl.multiple_of(kt * tk, tk)
        k_aug = jnp.concatenate([ks_ref[pl.ds(k0, tk), :], eneg_ref[pl.ds(k0, tk), :]], axis=1)
        sc = _dot_nt(q_aug, k_aug)
        if causal:
            rowi = lax.broadcasted_iota(jnp.int32, (MM, tk), 0)
            qoff = jnp.right_shift(rowi, M.bit_length() - 1) * tq + (rowi & (tq - 1))
            sc = jnp.where(qoff >= lax.broadcasted_iota(jnp.int32, (MM, tk), 1), sc, NEG)
        s_sc[kt] = sc
        part = sc[:, :LANES]
        for c in range(1, n_chunk):
            part = jnp.maximum(part, sc[:, c * LANES:(c + 1) * LANES])
        mx_sc[...] = jnp.maximum(mx_sc[...], part)

    def score_body(kt, carry):
        score_step(kt, False)
        return carry

    lax.fori_loop(0, i, score_body, 0)
    score_step(i, True)
    mx_sc[...] = jnp.broadcast_to(jnp.max(mx_sc[...], axis=1, keepdims=True), mx_sc.shape)
    ls_sc[...] = jnp.zeros(ls_sc.shape, f32)
    acc_sc[...] = jnp.zeros(acc_sc.shape, f32)

    def pv_body(kt, carry):
        k0 = pl.multiple_of(kt * tk, tk)
        sc = s_sc[kt]
        mb = mx_sc[...]
        ps = [jnp.exp2(sc[:, c * LANES:(c + 1) * LANES] - mb) for c in range(n_chunk)]
        ls_sc[...] += functools.reduce(lambda a, b: a + b, ps)
        acc_sc[...] += _dot(jnp.concatenate(ps, axis=1).astype(bf16), vs_ref[pl.ds(k0, tk), :])
        return carry

    lax.fori_loop(0, i + 1, pv_body, 0)
    o_sel = acc_sc[...] / jnp.sum(ls_sc[...], axis=1, keepdims=True)

    gt_all = jax.nn.sigmoid(gt_ref[...])
    gt_all = jnp.where(pl.program_id(1) == 0, gt_all, pltpu.roll(gt_all, LANES - R * N_BRANCH, axis=1))
    g_hi = gt_all.astype(bf16)
    rest = gt_all - g_hi.astype(f32)
    g_mid = rest.astype(bf16)
    g_lo = (rest - g_mid.astype(f32)).astype(bf16)
    g_exp = _dot(jnp.concatenate([g_hi, g_mid, g_lo], axis=1), gexp_ref[...])
    for sb, (o_win, o_cmp) in enumerate(fronts):
        for r in range(R):
            rows = slice(r * tq, (r + 1) * tq)
            srows = slice(sb * M + r * tq, sb * M + (r + 1) * tq)
            gates = [g_exp[sb * tq:(sb + 1) * tq, (r * N_BRANCH + br) * LANES:(r * N_BRANCH + br + 1) * LANES]
                     for br in range(N_BRANCH)]
            o = gates[0] * o_cmp[rows] + gates[1] * o_sel[srows] + gates[2] * o_win[rows]
            o_ref[sb * tq:(sb + 1) * tq, r * HEAD_DIM:(r + 1) * HEAD_DIM] = o.astype(bf16)


def _nsa(q, kcmp, vcmp, kv, eneg, gexp, gates, cast_a, cast_b, *, B, T, tq, n_sub):
    G, R = N_KV_GROUPS, HEADS_PER_GROUP
    tk = n_sub * tq
    nt = T // tk
    n_cmp = kcmp.shape[2]
    assert tq % LANES == 0 and tq & (tq - 1) == 0 and T % tk == 0 and T >= WINDOW + tq
    assert WINDOW % tq == 0 and WINDOW & (WINDOW - 1) == 0 and (R * tq) & (R * tq - 1) == 0
    rowblk = lambda b, g, i: (b * nt + i, g)
    cmp_spec = pl.BlockSpec((1, 1, n_cmp, HEAD_DIM), lambda b, g, i: (b, g, 0, 0))
    kvspec = lambda sec: pl.BlockSpec((T, HEAD_DIM), lambda b, g, i: (b, sec * G + g))
    n_steps = B * G * nt
    cast_spec = lambda w: pl.BlockSpec((w.shape[0] // n_steps, w.shape[1]), lambda b, g, i: ((b * G + g) * nt + i, 0))
    return pl.pallas_call(
        functools.partial(_nsa_kernel, tq=tq, n_sub=n_sub, T=T),
        grid=(B, G, nt),
        in_specs=[
            pl.BlockSpec((tk, R * HEAD_DIM), rowblk),
            cmp_spec, cmp_spec,
            kvspec(0), kvspec(1), kvspec(2), kvspec(3),
            pl.BlockSpec((T, LANES), lambda b, g, i: (0, 0)),
            pl.BlockSpec(gexp.shape, lambda b, g, i: (0, 0)),
            pl.BlockSpec((tk, LANES), lambda b, g, i: (b * nt + i, 0)),
            cast_spec(cast_a), cast_spec(cast_b),
        ],
        out_specs=[pl.BlockSpec((tk, R * HEAD_DIM), rowblk), cast_spec(cast_a), cast_spec(cast_b)],
        out_shape=[jax.ShapeDtypeStruct((B * T, G * R * HEAD_DIM), bf16), jax.ShapeDtypeStruct(cast_a.shape, bf16),
                   jax.ShapeDtypeStruct(cast_b.shape, bf16)],
        scratch_shapes=[
            pltpu.VMEM((T // tk, n_sub * R * tq, tk), f32),
            pltpu.VMEM((n_sub * R * tq, LANES), f32),
            pltpu.VMEM((n_sub * R * tq, LANES), f32),
            pltpu.VMEM((n_sub * R * tq, HEAD_DIM), f32),
        ],
        compiler_params=pltpu.CompilerParams(
            dimension_semantics=("arbitrary", "arbitrary", "arbitrary"), vmem_limit_bytes=VMEM_LIMIT),
        name="nsa",
    )(q, kcmp, vcmp, kv, kv, kv, kv, eneg, gexp, gates, cast_a, cast_b)


def _out_proj_kernel(x_ref, u_ref, uh_ref, y_ref, pw_ref, ps_ref, wo_ref, o_ref, *, tm, T):
    i = pl.program_id(0)
    tiles_per_seq = T // tm
    keep_halo = jnp.where((i % tiles_per_seq) == 0, 0.0, 1.0)
    u = u_ref[...]
    halo = uh_ref[...]
    tt = lax.broadcasted_iota(jnp.int32, (tm, tm), 0)
    ss = lax.broadcasted_iota(jnp.int32, (tm, tm), 1)
    th = lax.broadcasted_iota(jnp.int32, (tm, POOL_HALO), 0)
    rh = lax.broadcasted_iota(jnp.int32, (tm, POOL_HALO), 1)
    tseq = (i % tiles_per_seq) * tm + lax.broadcasted_iota(jnp.int32, (tm, 1), 0)
    n_pool = u.shape[1]
    gd = n_pool // len(POOL_WINDOWS)
    acc = x_ref[...] + _dot(y_ref[...], wo_ref[n_pool:, :])
    for gi, w in enumerate(POOL_WINDOWS):
        cols = slice(gi * gd, (gi + 1) * gd)
        band = jnp.where((tt - ss >= 0) & (tt - ss < w), 1.0, 0.0).astype(bf16)
        band_h = jnp.where(th + POOL_HALO - rh < w, 1.0, 0.0).astype(bf16)
        ug = u[:, cols]
        win_sum = _dot(band, ug) + keep_halo * _dot(band_h, halo[:, cols])
        count = jnp.minimum(tseq + 1, w).astype(f32)
        pooled = win_sum / count - ug.astype(f32)
        yg = _dot(pooled.astype(bf16), pw_ref[gi].astype(bf16)) * ps_ref[:, cols]
        acc = acc + _dot(yg.astype(bf16), wo_ref[cols, :])
    o_ref[...] = acc


def _out_proj(x2, u, y_nsa, pool_w, pool_scale, w_out, *, T, tm):
    n_rows, d = x2.shape
    n_pool = u.shape[1]
    hb = tm // POOL_HALO
    row = lambda i: (i, 0)
    const2 = lambda i: (0, 0)
    return pl.pallas_call(
        functools.partial(_out_proj_kernel, tm=tm, T=T),
        grid=(n_rows // tm,),
        in_specs=[
            pl.BlockSpec((tm, d), row),
            pl.BlockSpec((tm, n_pool), row),
            pl.BlockSpec((POOL_HALO, n_pool), lambda i: (jnp.maximum(i * hb - 1, 0), 0)),
            pl.BlockSpec((tm, y_nsa.shape[1]), row),
            pl.BlockSpec(pool_w.shape, lambda i: (0, 0, 0)),
            pl.BlockSpec((1, n_pool), const2),
            pl.BlockSpec(w_out.shape, const2, pipeline_mode=pl.Buffered(1)),
        ],
        out_specs=pl.BlockSpec((tm, d), row),
        out_shape=jax.ShapeDtypeStruct((n_rows, d), f32),
        compiler_params=pltpu.CompilerParams(
            dimension_semantics=("arbitrary",), vmem_limit_bytes=VMEM_LIMIT),
        name="out_proj",
    )(x2, u, u, y_nsa, pool_w, pool_scale, w_out)


def _ffn_kernel(h_ref, g_ref, wg_ref, wu_ref, wd_ref, o_ref, n_sc):
    @pl.when(pl.program_id(1) == 0)
    def _():
        h = h_ref[...]
        n_sc[...] = _rms(h, g_ref[...]).astype(bf16)
        o_ref[...] = h

    n = n_sc[...]
    half = wg_ref.shape[1] // 2
    acc = None
    for c in range(2):
        cols = slice(c * half, (c + 1) * half)
        gate = _dot(n, wg_ref[:, cols])
        up = _dot(n, wu_ref[:, cols])
        a = gate * jax.nn.sigmoid(gate) * up
        d = _dot(a.astype(bf16), wd_ref[cols, :])
        acc = d if acc is None else acc + d
    o_ref[...] += acc


def _ffn(h, g, w_gate, w_up, w_down, *, tm, tf):
    n_rows, d = h.shape
    d_ff = w_gate.shape[1]
    return pl.pallas_call(
        _ffn_kernel,
        grid=(n_rows // tm, d_ff // tf),
        in_specs=[
            pl.BlockSpec((tm, d), lambda i, f: (i, 0)),
            pl.BlockSpec((1, d), lambda i, f: (0, 0)),
            pl.BlockSpec((d, tf), lambda i, f: (0, f)),
            pl.BlockSpec((d, tf), lambda i, f: (0, f)),
            pl.BlockSpec((tf, d), lambda i, f: (f, 0)),
        ],
        out_specs=pl.BlockSpec((tm, d), lambda i, f: (i, 0)),
        out_shape=jax.ShapeDtypeStruct((n_rows, d), f32),
        scratch_shapes=[pltpu.VMEM((tm, d), bf16)],
        compiler_params=pltpu.CompilerParams(
            dimension_semantics=("arbitrary", "arbitrary"), vmem_limit_bytes=VMEM_LIMIT),
        name="ffn",
    )(h, g, w_gate, w_up, w_down)


def _ple_kernel(h_ref, p_ref, g_ref, wg_ref, wp_ref, fg_ref, o_ref):
    h = h_ref[...]
    n = _rms(h, g_ref[...]).astype(bf16)
    gate = jax.nn.sigmoid(_dot(n, wg_ref[...]))
    h = h + _dot(p_ref[...].astype(bf16), wp_ref[...].astype(bf16)) * gate
    o_ref[...] = _rms(h, fg_ref[...])


def _ple(h, p2, g, w_gate, w_proj, fg, *, tm):
    n_rows, d = h.shape
    row = lambda i: (i, 0)
    const = lambda i: (0, 0)
    return pl.pallas_call(
        _ple_kernel,
        grid=(n_rows // tm,),
        in_specs=[
            pl.BlockSpec((tm, d), row),
            pl.BlockSpec((tm, p2.shape[1]), row),
            pl.BlockSpec((1, d), const),
            pl.BlockSpec(w_gate.shape, const, pipeline_mode=pl.Buffered(1)),
            pl.BlockSpec(w_proj.shape, const),
            pl.BlockSpec((1, d), const),
        ],
        out_specs=pl.BlockSpec((tm, d), row),
        out_shape=jax.ShapeDtypeStruct((n_rows, d), f32),
        compiler_params=pltpu.CompilerParams(
            dimension_semantics=("arbitrary",), vmem_limit_bytes=VMEM_LIMIT),
        name="ple",
    )(h, p2, g, w_gate, w_proj, fg)


def _rope_tables(T):
    pos = np.arange(T, dtype=np.float64)
    inv_freq = ROPE_THETA ** (-np.arange(0, ROPE_DIM, 2, dtype=np.float64) / ROPE_DIM)
    ang = pos[:, None] * inv_freq[None, :]
    cos, sin = np.cos(ang), np.sin(ang)
    rest = HEAD_DIM - ROPE_DIM
    cosf = np.concatenate([cos, cos, np.ones((T, rest))], axis=1)
    sinf = np.concatenate([-sin, sin, np.zeros((T, rest))], axis=1)
    return jnp.asarray(cosf, f32), jnp.asarray(sinf, f32)


def _block_bias_table(T):
    key_blk = np.arange(T)[:, None] // SEL_BLOCK
    return jnp.asarray(np.where(key_blk == np.arange(LANES)[None, :], NEG, 0.0).astype(bf16))


def _gate_expansion_table():
    n_col = HEADS_PER_GROUP * N_BRANCH
    e = np.zeros((3, LANES, n_col, LANES), np.float32)
    for c in range(n_col):
        e[:, c, c, :] = 1.0
    return jnp.asarray(e.reshape(3 * LANES, n_col * LANES).astype(bf16))


def _mixer_ffn(h2, in_norm_g, w_in, pool_w, pool_scale, cmp_k_pe, cmp_k_w1, cmp_k_w2,
               cmp_v_pe, cmp_v_w1, cmp_v_w2, w_out, ffn_norm_g, w_gate, w_up, w_down, w_ple_gate, *, B, T):
    n_pool = pool_scale.shape[0]
    n_heads = N_KV_GROUPS * HEADS_PER_GROUP
    nsa_w = n_heads * HEAD_DIM
    kv_w = N_KV_GROUPS * HEAD_DIM
    assert w_in.shape[1] == n_pool + nsa_w + 6 * kv_w + n_heads * N_BRANCH

    o0 = n_pool + nsa_w + 6 * kv_w
    w_p = w_in.T
    w_gt = jnp.pad(w_p[o0:].astype(bf16), ((0, LANES - n_heads * N_BRANCH), (0, 0)))

    cosf, sinf = _rope_tables(T)
    u, q, kcvc, kv, gates, w_out_b, w_gate_b, w_ple_gate_b = _in_proj(
        h2, in_norm_g[None, :], w_p, w_gt, cosf, sinf, w_out, w_gate, w_ple_gate, T=T, tm=ROW_TILE,
        n_pool=n_pool, n_q=nsa_w)

    kcmp, vcmp = _compress(
        kcvc.reshape(B, T, kcvc.shape[1]),
        cmp_k_pe.reshape(1, -1), cmp_k_w1, cmp_k_w2, cmp_v_pe.reshape(1, -1), cmp_v_w1, cmp_v_w2,
        bb=COMPRESS_BATCH if B % COMPRESS_BATCH == 0 else 1)

    eneg = _block_bias_table(T)
    y_nsa, w_up_b, w_down_b = _nsa(q, kcmp, vcmp, kv, eneg, _gate_expansion_table(), gates, w_up, w_down, B=B, T=T,
                                   tq=NSA_Q_TILE, n_sub=NSA_SUB_TILES)

    h2 = _out_proj(h2, u, y_nsa, pool_w, pool_scale[None, :], w_out_b, T=T, tm=ROW_TILE)
    return _ffn(h2, ffn_norm_g[None, :], w_gate_b, w_up_b, w_down_b, tm=FFN_ROW_TILE, tf=FFN_FF_TILE), w_ple_gate_b


def kernel(x, p, in_norm_g, w_in, pool_w, pool_scale, cmp_k_pe, cmp_k_w1, cmp_k_w2, cmp_v_pe, cmp_v_w1,
           cmp_v_w2, w_out, ffn_norm_g, w_gate, w_up, w_down, ple_norm_g, w_ple_gate, w_ple_proj, final_norm_g):
    B, T, d = x.shape
    depth = w_in.shape[0]
    assert depth == 1, "the final rmsnorm is fused into the last layer's per-layer-embedding kernel"
    h2 = x.reshape(B * T, d)
    i = 0
    h2, w_ple_gate_b = _mixer_ffn(
        h2, in_norm_g[i], w_in[i], pool_w[i], pool_scale[i], cmp_k_pe[i], cmp_k_w1[i], cmp_k_w2[i],
        cmp_v_pe[i], cmp_v_w1[i], cmp_v_w2[i], w_out[i], ffn_norm_g[i], w_gate[i], w_up[i], w_down[i],
        w_ple_gate[i], B=B, T=T)
    out = _ple(h2, p[i].reshape(B * T, -1), ple_norm_g[i][None, :], w_ple_gate_b,
               w_ple_proj[i], final_norm_g[None, :], tm=ROW_TILE)
    return out.reshape(B, T, d)
```

```python
import functools

import jax
import jax.numpy as jnp
import numpy as np
from jax import lax
from jax.experimental import pallas as pl
from jax.experimental.pallas import tpu as pltpu

EPS = 1e-6
NEG = -1e30
LANES = 128
POOL_WINDOWS = (2, 4, 8, 16)
POOL_HALO = 16
HEAD_DIM = 128
N_KV_GROUPS = 2
HEADS_PER_GROUP = 4
N_BRANCH = 3
CMP_BLOCK = 32
CMP_STRIDE = 16
SEL_BLOCK = 64
N_SELECT = 16
WINDOW = 512
ROPE_THETA = 500000.0
ROPE_DIM = HEAD_DIM // 4
ATTN_SCALE = HEAD_DIM ** -0.5
Q_SCALE = ATTN_SCALE * 1.4426950408889634
VMEM_LIMIT = 60 * 1024 * 1024
W_CHUNK = 256
IN_SEC = 512
ROW_TILE = 512
FFN_ROW_TILE = 1024
FFN_FF_TILE = 512
NSA_Q_TILE = 256
NSA_SUB_TILES = 2
COMPRESS_BATCH = 2

f32 = jnp.float32
bf16 = jnp.bfloat16


def _dot(a, b):
    return jnp.dot(a, b, preferred_element_type=f32)


def _dot_nt(a, b):
    return lax.dot_general(a, b, (((1,), (1,)), ((), ())), preferred_element_type=f32)


def _rms(x, g):
    return x * lax.rsqrt(jnp.mean(x * x, axis=-1, keepdims=True) + EPS) * g


def _rope(h, cosf, sinf, lane):
    half = ROPE_DIM // 2
    partner = jnp.where(lane < half, pltpu.roll(h, LANES - half, axis=1), pltpu.roll(h, half, axis=1))
    return h * cosf + partner * sinf


def _w_chunk_copy(w_hbm, stage_sc, sem, c):
    return pltpu.make_async_copy(w_hbm.at[pl.ds(c * W_CHUNK, W_CHUNK), :], stage_sc.at[c % 2], sem.at[c % 2])


def _in_proj_kernel(x_ref, g_ref, w_hbm, wgt_ref, cos_ref, sin_ref, ca_ref, cb_ref, cc_ref,
                    u_ref, q_ref, kcvc_ref, kv_ref, gt_ref, ca_out, cb_out, cc_out,
                    w_ref, stage_sc, sem):
    @pl.when(pl.program_id(0) == 0)
    def _():
        n_chunk = w_ref.shape[0] // W_CHUNK
        _w_chunk_copy(w_hbm, stage_sc, sem, 0).start()
        for c in range(n_chunk):
            if c + 1 < n_chunk:
                _w_chunk_copy(w_hbm, stage_sc, sem, c + 1).start()
            _w_chunk_copy(w_hbm, stage_sc, sem, c).wait()
            w_ref[c * W_CHUNK:(c + 1) * W_CHUNK, :] = stage_sc[c % 2].astype(bf16)

    def section(s):
        return w_ref[s * IN_SEC:(s + 1) * IN_SEC, :]

    ca_out[...] = ca_ref[...].astype(bf16)
    cb_out[...] = cb_ref[...].astype(bf16)
    cc_out[...] = cc_ref[...].astype(bf16)
    a = _rms(x_ref[...], g_ref[...]).astype(bf16)
    cosf = cos_ref[...]
    sinf = sin_ref[...]
    lane = lax.broadcasted_iota(jnp.int32, cosf.shape, 1)
    heads = IN_SEC // HEAD_DIM
    s = 0
    for c in range(0, u_ref.shape[1], IN_SEC):
        u_ref[:, c:c + IN_SEC] = _dot_nt(a, section(s)).astype(bf16)
        s += 1
    for c in range(0, q_ref.shape[1], IN_SEC):
        acc = _dot_nt(a, section(s))
        s += 1
        for j in range(heads):
            h = _rope(acc[:, j * HEAD_DIM:(j + 1) * HEAD_DIM], cosf, sinf, lane) * Q_SCALE
            q_ref[:, c + j * HEAD_DIM:c + (j + 1) * HEAD_DIM] = h.astype(bf16)
    for dst, base, dt in ((kcvc_ref, 0, f32), (kv_ref, 0, bf16), (kv_ref, IN_SEC, bf16)):
        acc = _dot_nt(a, section(s))
        s += 1
        for j in range(heads):
            h = acc[:, j * HEAD_DIM:(j + 1) * HEAD_DIM]
            if j < N_KV_GROUPS:
                h = _rope(h, cosf, sinf, lane)
            dst[:, base + j * HEAD_DIM:base + (j + 1) * HEAD_DIM] = h.astype(dt)
    gt_ref[...] = _dot_nt(a, wgt_ref[...])


def _in_proj(x2, g, w_p, w_gt, cosf, sinf, cast_a, cast_b, cast_c, *, T, tm, n_pool, n_q):
    n_rows, d = x2.shape
    assert n_pool % IN_SEC == 0 and n_q % IN_SEC == 0 and 2 * N_KV_GROUPS * HEAD_DIM == IN_SEC
    n_w = (w_p.shape[0] // W_CHUNK) * W_CHUNK
    assert w_p.dtype == f32 and n_w + w_gt.shape[0] >= w_p.shape[0] and n_w == n_pool + n_q + 3 * IN_SEC
    tiles_per_seq = T // tm
    n_steps = n_rows // tm
    row = lambda i: (i, 0)
    cast_spec = lambda w: pl.BlockSpec((w.shape[0] // n_steps, w.shape[1]), row)
    const = lambda i: (0, 0)
    tab = lambda i: (i % tiles_per_seq, 0)
    return pl.pallas_call(
        _in_proj_kernel,
        grid=(n_rows // tm,),
        in_specs=[
            pl.BlockSpec((tm, d), row),
            pl.BlockSpec((1, d), const),
            pl.BlockSpec(memory_space=pl.ANY),
            pl.BlockSpec((LANES, d), const),
            pl.BlockSpec((tm, LANES), tab),
            pl.BlockSpec((tm, LANES), tab),
            cast_spec(cast_a), cast_spec(cast_b), cast_spec(cast_c),
        ],
        out_specs=[
            pl.BlockSpec((tm, n_pool), row),
            pl.BlockSpec((tm, n_q), row),
            pl.BlockSpec((tm, IN_SEC), row),
            pl.BlockSpec((tm, 2 * IN_SEC), row),
            pl.BlockSpec((tm, LANES), row),
            cast_spec(cast_a), cast_spec(cast_b), cast_spec(cast_c),
        ],
        out_shape=[
            jax.ShapeDtypeStruct((n_rows, n_pool), bf16),
            jax.ShapeDtypeStruct((n_rows, n_q), bf16),
            jax.ShapeDtypeStruct((n_rows, IN_SEC), f32),
            jax.ShapeDtypeStruct((n_rows, 2 * IN_SEC), bf16),
            jax.ShapeDtypeStruct((n_rows, LANES), f32),
            jax.ShapeDtypeStruct(cast_a.shape, bf16),
            jax.ShapeDtypeStruct(cast_b.shape, bf16),
            jax.ShapeDtypeStruct(cast_c.shape, bf16),
        ],
        scratch_shapes=[
            pltpu.VMEM((n_w, d), bf16),
            pltpu.VMEM((2, W_CHUNK, d), f32),
            pltpu.SemaphoreType.DMA((2,)),
        ],
        compiler_params=pltpu.CompilerParams(
            dimension_semantics=("arbitrary",), vmem_limit_bytes=VMEM_LIMIT),
        name="in_proj",
    )(x2, g, w_p, w_gt, cosf, sinf, cast_a, cast_b, cast_c)


def _gelu_tanh(x):
    return 0.5 * x * (1.0 + jnp.tanh(0.7978845608028654 * (x + 0.044715 * (x * x * x))))


def _compress_kernel(kc0_ref, kc1_ref, vc0_ref, vc1_ref, pek_ref, w1k_ref, w2k_ref, pev_ref, w1v_ref, w2v_ref,
                     kc_ref, vc_ref):
    bb = kc0_ref.shape[0]
    n_rows = kc0_ref.shape[1] // CMP_STRIDE
    half_k = (CMP_BLOCK // 2) * HEAD_DIM
    n_stack = bb * N_KV_GROUPS * n_rows
    rowi = lax.broadcasted_iota(jnp.int32, (n_stack, HEAD_DIM), 0)
    for srcs, pe_ref, w1_ref, w2_ref, o_ref in (((kc0_ref, kc1_ref), pek_ref, w1k_ref, w2k_ref, kc_ref),
                                                ((vc0_ref, vc1_ref), pev_ref, w1v_ref, w2v_ref, vc_ref)):
        x = jnp.concatenate(
            [jnp.concatenate([src[b, pl.ds(l, n_rows, stride=CMP_STRIDE), :] for l in range(CMP_STRIDE)], axis=1)
             for b in range(bb) for src in srcs], axis=0)
        za = _dot((x + pe_ref[:, :half_k]).astype(bf16), w1_ref[:half_k, :].astype(bf16))
        zb = _dot((x + pe_ref[:, half_k:]).astype(bf16), w1_ref[half_k:, :].astype(bf16))
        h = za + pltpu.roll(zb, n_stack - 1, axis=0)
        o = _dot(_gelu_tanh(h).astype(bf16), w2_ref[...].astype(bf16))
        o = jnp.where((rowi & (n_rows - 1)) < n_rows - 1, o, 0.0).astype(bf16)
        for b in range(bb):
            for g in range(N_KV_GROUPS):
                r0 = (b * N_KV_GROUPS + g) * n_rows
                o_ref[b, g] = o[r0:r0 + n_rows]


def _compress(r3, pek, w1k, w2k, pev, w1v, w2v, *, bb):
    B, T, width = r3.shape
    n_rows = T // CMP_STRIDE
    assert width == 2 * N_KV_GROUPS * HEAD_DIM and N_KV_GROUPS == 2 and B % bb == 0 and n_rows & (n_rows - 1) == 0
    const = lambda b: (0, 0)
    out = jax.ShapeDtypeStruct((B, N_KV_GROUPS, n_rows, HEAD_DIM), bf16)
    ospec = pl.BlockSpec((bb, N_KV_GROUPS, n_rows, HEAD_DIM), lambda b: (b, 0, 0, 0))
    sec = lambda s: pl.BlockSpec((bb, T, HEAD_DIM), lambda b: (b, 0, s))
    return pl.pallas_call(
        _compress_kernel,
        grid=(B // bb,),
        in_specs=[
            sec(0), sec(1), sec(2), sec(3),
            pl.BlockSpec(pek.shape, const), pl.BlockSpec(w1k.shape, const), pl.BlockSpec(w2k.shape, const),
            pl.BlockSpec(pev.shape, const), pl.BlockSpec(w1v.shape, const), pl.BlockSpec(w2v.shape, const),
        ],
        out_specs=[ospec, ospec],
        out_shape=[out, out],
        compiler_params=pltpu.CompilerParams(
            dimension_semantics=("arbitrary",), vmem_limit_bytes=VMEM_LIMIT),
        name="compress",
    )(r3, r3, r3, r3, pek, w1k, w2k, pev, w1v, w2v)


def _nsa_front(qs, t0, kc, vc, kw_ref, vw_ref, *, tq, T):
    R = HEADS_PER_GROUP
    M = R * tq
    n_cmp = kc.shape[0]
    n_sel = T // SEL_BLOCK

    wk = WINDOW + tq
    w0 = pl.multiple_of(jnp.maximum(t0 - WINDOW, 0), tq)
    sw = _dot_nt(qs, kw_ref[pl.ds(w0, wk), :])
    diff = (t0 - w0) + lax.broadcasted_iota(jnp.int32, (tq, wk), 0) - lax.broadcasted_iota(jnp.int32, (tq, wk), 1)
    wbias = jnp.where((diff & -WINDOW) == 0, 0.0, NEG)
    sw = jnp.concatenate([sw[r * tq:(r + 1) * tq] + wbias for r in range(R)], axis=0)
    ew = jnp.exp2(sw - jnp.max(sw, axis=1, keepdims=True))
    o_win = _dot(ew.astype(bf16), vw_ref[pl.ds(w0, wk), :]) / jnp.sum(ew, axis=1, keepdims=True)

    s = _dot_nt(qs, kc)
    valid = (CMP_STRIDE * lax.broadcasted_iota(jnp.int32, (tq, n_cmp), 1) + (CMP_BLOCK - 1)
             <= t0 + lax.broadcasted_iota(jnp.int32, (tq, n_cmp), 0))
    s = jnp.concatenate([jnp.where(valid, s[r * tq:(r + 1) * tq], NEG) for r in range(R)], axis=0)
    e = jnp.exp2(s - jnp.max(s, axis=1, keepdims=True))
    p = e / jnp.sum(e, axis=1, keepdims=True)
    p = jnp.concatenate([jnp.where(valid, p[r * tq:(r + 1) * tq], 0.0) for r in range(R)], axis=0)
    o_cmp = _dot(p.astype(bf16), vc)

    psum = p[0:tq]
    for r in range(1, R):
        psum = psum + p[r * tq:(r + 1) * tq]
    p_hi = psum.astype(bf16)
    p_lo = (psum - p_hi.astype(f32)).astype(bf16)
    sb = lax.broadcasted_iota(jnp.int32, (n_sel, n_cmp), 0) * SEL_BLOCK
    cb = lax.broadcasted_iota(jnp.int32, (n_sel, n_cmp), 1) * CMP_STRIDE
    ov_t = jnp.where((cb < sb + SEL_BLOCK) & (cb + CMP_BLOCK > sb), 1.0, 0.0).astype(bf16)
    imp_t = _dot_nt(ov_t, p_hi) + _dot_nt(ov_t, p_lo)

    jj = lax.broadcasted_iota(jnp.int32, (n_sel, tq), 0)
    cur = jnp.right_shift(t0 + lax.broadcasted_iota(jnp.int32, (n_sel, tq), 1), SEL_BLOCK.bit_length() - 1)
    forced = (jj == 0) | (jj == cur) | (jj == cur - 1)
    v = jnp.where(jj > cur, -jnp.inf, jnp.where(forced, jnp.inf, imp_t))
    sub = 8
    slabs = [v[k * sub:(k + 1) * sub] for k in range(n_sel // sub)]
    ranks = [jnp.zeros((sub, tq), f32) for _ in slabs]
    j_in = lax.broadcasted_iota(jnp.int32, (sub, tq), 0)
    for c in range(n_sel):
        v_c = v[c:c + 1, :]
        for k, v_k in enumerate(slabs):
            if k * sub > c:
                ahead = jnp.where(v_c >= v_k, 1.0, 0.0)
            elif (k + 1) * sub <= c:
                ahead = jnp.where(v_c > v_k, 1.0, 0.0)
            else:
                ahead = jnp.where(j_in > c - k * sub, jnp.where(v_c >= v_k, 1.0, 0.0),
                                  jnp.where(v_c > v_k, 1.0, 0.0))
            ranks[k] = ranks[k] + ahead
    rank = jnp.concatenate(ranks, axis=0)
    unsel_t = jnp.where(rank < float(min(N_SELECT, n_sel)), 0.0, 1.0)
    unsel_t = jnp.concatenate([unsel_t, jnp.zeros((LANES - n_sel, tq), f32)], axis=0)
    return o_win, o_cmp, unsel_t.T.astype(bf16)


def _nsa_kernel(q_ref, kc_ref, vc_ref, ks_ref, vs_ref, kw_ref, vw_ref, eneg_ref, gexp_ref, gt_ref, ca_ref, cb_ref,
                o_ref, ca_out, cb_out, s_sc, mx_sc, ls_sc, acc_sc, *, tq, n_sub, T):
    ca_out[...] = ca_ref[...].astype(bf16)
    cb_out[...] = cb_ref[...].astype(bf16)
    R = HEADS_PER_GROUP
    M = R * tq
    tk = n_sub * tq
    i = pl.program_id(2)
    t0 = i * tk
    kc = kc_ref[0, 0]
    vc = vc_ref[0, 0]

    fronts = []
    q_rows = []
    for sb in range(n_sub):
        q = q_ref[sb * tq:(sb + 1) * tq, :]
        qs = jnp.concatenate([q[:, r * HEAD_DIM:(r + 1) * HEAD_DIM] for r in range(R)], axis=0)
        o_win, o_cmp, unsel = _nsa_front(qs, t0 + sb * tq, kc, vc, kw_ref, vw_ref, tq=tq, T=T)
        fronts.append((o_win, o_cmp))
        q_rows.append(jnp.concatenate([qs, jnp.concatenate([unsel] * R, axis=0)], axis=1))
    q_aug = jnp.concatenate(q_rows, axis=0)
    MM = n_sub * M

    n_chunk = tk // LANES
    mx_sc[...] = jnp.full(mx_sc.shape, NEG, f32)

    def score_step(kt, causal):
        k0 = pl.multiple_of(kt * tk, tk)
        k_aug = jnp.concatenate([ks_ref[pl.ds(k0, tk), :], eneg_ref[pl.ds(k0, tk), :]], axis=1)
        sc = _dot_nt(q_aug, k_aug)
        if causal:
            cbias = jnp.where(lax.broadcasted_iota(jnp.int32, (tk, tk), 0)
                              >= lax.broadcasted_iota(jnp.int32, (tk, tk), 1), 0.0, NEG)
            sc = jnp.concatenate([sc[sb * M + r * tq:sb * M + (r + 1) * tq] + cbias[sb * tq:(sb + 1) * tq]
                                  for sb in range(n_sub) for r in range(R)], axis=0)
        s_sc[kt] = sc
        part = sc[:, :LANES]
        for c in range(1, n_chunk):
            part = jnp.maximum(part, sc[:, c * LANES:(c + 1) * LANES])
        mx_sc[...] = jnp.maximum(mx_sc[...], part)

    def score_body(kt, carry):
        score_step(kt, False)
        return carry

    lax.fori_loop(0, i, score_body, 0)
    score_step(i, True)
    mx_sc[...] = jnp.broadcast_to(jnp.max(mx_sc[...], axis=1, keepdims=True), mx_sc.shape)
    ls_sc[...] = jnp.zeros(ls_sc.shape, f32)
    acc_sc[...] = jnp.zeros(acc_sc.shape, f32)

    def pv_body(kt, carry):
        k0 = pl.multiple_of(kt * tk, tk)
        sc = s_sc[kt]
        mb = mx_sc[...]
        ps = [jnp.exp2(sc[:, c * LANES:(c + 1) * LANES] - mb) for c in range(n_chunk)]
        ls_sc[...] += functools.reduce(lambda a, b: a + b, ps)
        acc_sc[...] += _dot(jnp.concatenate(ps, axis=1).astype(bf16), vs_ref[pl.ds(k0, tk), :])
        return carry

    lax.fori_loop(0, i + 1, pv_body, 0)
    o_sel = acc_sc[...] / jnp.sum(ls_sc[...], axis=1, keepdims=True)

    gt_all = jax.nn.sigmoid(gt_ref[...])
    gt_all = jnp.where(pl.program_id(1) == 0, gt_all, pltpu.roll(gt_all, LANES - R * N_BRANCH, axis=1))
    g_hi = gt_all.astype(bf16)
    rest = gt_all - g_hi.astype(f32)
    g_mid = rest.astype(bf16)
    g_lo = (rest - g_mid.astype(f32)).astype(bf16)
    g_exp = _dot(jnp.concatenate([g_hi, g_mid, g_lo], axis=1), gexp_ref[...])
    for sb, (o_win, o_cmp) in enumerate(fronts):
        for r in range(R):
            rows = slice(r * tq, (r + 1) * tq)
            srows = slice(sb * M + r * tq, sb * M + (r + 1) * tq)
            gates = [g_exp[sb * tq:(sb + 1) * tq, (r * N_BRANCH + br) * LANES:(r * N_BRANCH + br + 1) * LANES]
                     for br in range(N_BRANCH)]
            o = gates[0] * o_cmp[rows] + gates[1] * o_sel[srows] + gates[2] * o_win[rows]
            o_ref[sb * tq:(sb + 1) * tq, r * HEAD_DIM:(r + 1) * HEAD_DIM] = o.astype(bf16)


def _nsa(q, kcmp, vcmp, kv, eneg, gexp, gates, cast_a, cast_b, *, B, T, tq, n_sub):
    G, R = N_KV_GROUPS, HEADS_PER_GROUP
    tk = n_sub * tq
    nt = T // tk
    n_cmp = kcmp.shape[2]
    assert tq % LANES == 0 and tq & (tq - 1) == 0 and T % tk == 0 and T >= WINDOW + tq
    assert WINDOW % tq == 0 and WINDOW & (WINDOW - 1) == 0 and (R * tq) & (R * tq - 1) == 0
    rowblk = lambda b, g, i: (b * nt + i, g)
    cmp_spec = pl.BlockSpec((1, 1, n_cmp, HEAD_DIM), lambda b, g, i: (b, g, 0, 0))
    kvspec = lambda sec: pl.BlockSpec((T, HEAD_DIM), lambda b, g, i: (b, sec * G + g))
    n_steps = B * G * nt
    cast_spec = lambda w: pl.BlockSpec((w.shape[0] // n_steps, w.shape[1]), lambda b, g, i: ((b * G + g) * nt + i, 0))
    return pl.pallas_call(
        functools.partial(_nsa_kernel, tq=tq, n_sub=n_sub, T=T),
        grid=(B, G, nt),
        in_specs=[
            pl.BlockSpec((tk, R * HEAD_DIM), rowblk),
            cmp_spec, cmp_spec,
            kvspec(0), kvspec(1), kvspec(2), kvspec(3),
            pl.BlockSpec((T, LANES), lambda b, g, i: (0, 0)),
            pl.BlockSpec(gexp.shape, lambda b, g, i: (0, 0)),
            pl.BlockSpec((tk, LANES), lambda b, g, i: (b * nt + i, 0)),
            cast_spec(cast_a), cast_spec(cast_b),
        ],
        out_specs=[pl.BlockSpec((tk, R * HEAD_DIM), rowblk), cast_spec(cast_a), cast_spec(cast_b)],
        out_shape=[jax.ShapeDtypeStruct((B * T, G * R * HEAD_DIM), bf16), jax.ShapeDtypeStruct(cast_a.shape, bf16),
                   jax.ShapeDtypeStruct(cast_b.shape, bf16)],
        scratch_shapes=[
            pltpu.VMEM((T // tk, n_sub * R * tq, tk), f32),
            pltpu.VMEM((n_sub * R * tq, LANES), f32),
            pltpu.VMEM((n_sub * R * tq, LANES), f32),
            pltpu.VMEM((n_sub * R * tq, HEAD_DIM), f32),
        ],
        compiler_params=pltpu.CompilerParams(
            dimension_semantics=("arbitrary", "arbitrary", "arbitrary"), vmem_limit_bytes=VMEM_LIMIT),
        name="nsa",
    )(q, kcmp, vcmp, kv, kv, kv, kv, eneg, gexp, gates, cast_a, cast_b)


def _out_proj_kernel(x_ref, u_ref, uh_ref, y_ref, pw_ref, ps_ref, wo_ref, o_ref, *, tm, T):
    i = pl.program_id(0)
    tiles_per_seq = T // tm
    keep_halo = jnp.where((i % tiles_per_seq) == 0, 0.0, 1.0)
    u = u_ref[...]
    halo = uh_ref[...]
    tt = lax.broadcasted_iota(jnp.int32, (tm, tm), 0)
    ss = lax.broadcasted_iota(jnp.int32, (tm, tm), 1)
    th = lax.broadcasted_iota(jnp.int32, (tm, POOL_HALO), 0)
    rh = lax.broadcasted_iota(jnp.int32, (tm, POOL_HALO), 1)
    tseq = (i % tiles_per_seq) * tm + lax.broadcasted_iota(jnp.int32, (tm, 1), 0)
    n_pool = u.shape[1]
    gd = n_pool // len(POOL_WINDOWS)
    acc = x_ref[...] + _dot(y_ref[...], wo_ref[n_pool:, :])
    for gi, w in enumerate(POOL_WINDOWS):
        cols = slice(gi * gd, (gi + 1) * gd)
        band = jnp.where((tt - ss >= 0) & (tt - ss < w), 1.0, 0.0).astype(bf16)
        band_h = jnp.where(th + POOL_HALO - rh < w, 1.0, 0.0).astype(bf16)
        ug = u[:, cols]
        win_sum = _dot(band, ug) + keep_halo * _dot(band_h, halo[:, cols])
        count = jnp.minimum(tseq + 1, w).astype(f32)
        pooled = win_sum / count - ug.astype(f32)
        yg = _dot(pooled.astype(bf16), pw_ref[gi].astype(bf16)) * ps_ref[:, cols]
        acc = acc + _dot(yg.astype(bf16), wo_ref[cols, :])
    o_ref[...] = acc


def _out_proj(x2, u, y_nsa, pool_w, pool_scale, w_out, *, T, tm):
    n_rows, d = x2.shape
    n_pool = u.shape[1]
    hb = tm // POOL_HALO
    row = lambda i: (i, 0)
    const2 = lambda i: (0, 0)
    return pl.pallas_call(
        functools.partial(_out_proj_kernel, tm=tm, T=T),
        grid=(n_rows // tm,),
        in_specs=[
            pl.BlockSpec((tm, d), row),
            pl.BlockSpec((tm, n_pool), row),
            pl.BlockSpec((POOL_HALO, n_pool), lambda i: (jnp.maximum(i * hb - 1, 0), 0)),
            pl.BlockSpec((tm, y_nsa.shape[1]), row),
            pl.BlockSpec(pool_w.shape, lambda i: (0, 0, 0)),
            pl.BlockSpec((1, n_pool), const2),
            pl.BlockSpec(w_out.shape, const2, pipeline_mode=pl.Buffered(1)),
        ],
        out_specs=pl.BlockSpec((tm, d), row),
        out_shape=jax.ShapeDtypeStruct((n_rows, d), f32),
        compiler_params=pltpu.CompilerParams(
            dimension_semantics=("arbitrary",), vmem_limit_bytes=VMEM_LIMIT),
        name="out_proj",
    )(x2, u, u, y_nsa, pool_w, pool_scale, w_out)


def _ffn_kernel(h_ref, g_ref, wg_ref, wu_ref, wd_ref, o_ref, n_sc):
    @pl.when(pl.program_id(1) == 0)
    def _():
        h = h_ref[...]
        n_sc[...] = _rms(h, g_ref[...]).astype(bf16)
        o_ref[...] = h

    n = n_sc[...]
    half = wg_ref.shape[1] // 2
    acc = None
    for c in range(2):
        cols = slice(c * half, (c + 1) * half)
        gate = _dot(n, wg_ref[:, cols])
        up = _dot(n, wu_ref[:, cols])
        a = gate * jax.nn.sigmoid(gate) * up
        d = _dot(a.astype(bf16), wd_ref[cols, :])
        acc = d if acc is None else acc + d
    o_ref[...] += acc


def _ffn(h, g, w_gate, w_up, w_down, *, tm, tf):
    n_rows, d = h.shape
    d_ff = w_gate.shape[1]
    return pl.pallas_call(
        _ffn_kernel,
        grid=(n_rows // tm, d_ff // tf),
        in_specs=[
            pl.BlockSpec((tm, d), lambda i, f: (i, 0)),
            pl.BlockSpec((1, d), lambda i, f: (0, 0)),
            pl.BlockSpec((d, tf), lambda i, f: (0, f)),
            pl.BlockSpec((d, tf), lambda i, f: (0, f)),
            pl.BlockSpec((tf, d), lambda i, f: (f, 0)),
        ],
        out_specs=pl.BlockSpec((tm, d), lambda i, f: (i, 0)),
        out_shape=jax.ShapeDtypeStruct((n_rows, d), f32),
        scratch_shapes=[pltpu.VMEM((tm, d), bf16)],
        compiler_params=pltpu.CompilerParams(
            dimension_semantics=("arbitrary", "arbitrary"), vmem_limit_bytes=VMEM_LIMIT),
        name="ffn",
    )(h, g, w_gate, w_up, w_down)


def _ple_kernel(h_ref, p_ref, g_ref, wg_ref, wp_ref, fg_ref, o_ref):
    h = h_ref[...]
    n = _rms(h, g_ref[...]).astype(bf16)
    gate = jax.nn.sigmoid(_dot(n, wg_ref[...]))
    h = h + _dot(p_ref[...].astype(bf16), wp_ref[...].astype(bf16)) * gate
    o_ref[...] = _rms(h, fg_ref[...])


def _ple(h, p2, g, w_gate, w_proj, fg, *, tm):
    n_rows, d = h.shape
    row = lambda i: (i, 0)
    const = lambda i: (0, 0)
    return pl.pallas_call(
        _ple_kernel,
        grid=(n_rows // tm,),
        in_specs=[
            pl.BlockSpec((tm, d), row),
            pl.BlockSpec((tm, p2.shape[1]), row),
            pl.BlockSpec((1, d), const),
            pl.BlockSpec(w_gate.shape, const, pipeline_mode=pl.Buffered(1)),
            pl.BlockSpec(w_proj.shape, const),
            pl.BlockSpec((1, d), const),
        ],
        out_specs=pl.BlockSpec((tm, d), row),
        out_shape=jax.ShapeDtypeStruct((n_rows, d), f32),
        compiler_params=pltpu.CompilerParams(
            dimension_semantics=("arbitrary",), vmem_limit_bytes=VMEM_LIMIT),
        name="ple",
    )(h, p2, g, w_gate, w_proj, fg)


def _rope_tables(T):
    pos = np.arange(T, dtype=np.float64)
    inv_freq = ROPE_THETA ** (-np.arange(0, ROPE_DIM, 2, dtype=np.float64) / ROPE_DIM)
    ang = pos[:, None] * inv_freq[None, :]
    cos, sin = np.cos(ang), np.sin(ang)
    rest = HEAD_DIM - ROPE_DIM
    cosf = np.concatenate([cos, cos, np.ones((T, rest))], axis=1)
    sinf = np.concatenate([-sin, sin, np.zeros((T, rest))], axis=1)
    return jnp.asarray(cosf, f32), jnp.asarray(sinf, f32)


def _block_bias_table(T):
    key_blk = np.arange(T)[:, None] // SEL_BLOCK
    return jnp.asarray(np.where(key_blk == np.arange(LANES)[None, :], NEG, 0.0).astype(bf16))


def _gate_expansion_table():
    n_col = HEADS_PER_GROUP * N_BRANCH
    e = np.zeros((3, LANES, n_col, LANES), np.float32)
    for c in range(n_col):
        e[:, c, c, :] = 1.0
    return jnp.asarray(e.reshape(3 * LANES, n_col * LANES).astype(bf16))


def _mixer_ffn(h2, in_norm_g, w_in, pool_w, pool_scale, cmp_k_pe, cmp_k_w1, cmp_k_w2,
               cmp_v_pe, cmp_v_w1, cmp_v_w2, w_out, ffn_norm_g, w_gate, w_up, w_down, w_ple_gate, *, B, T):
    n_pool = pool_scale.shape[0]
    n_heads = N_KV_GROUPS * HEADS_PER_GROUP
    nsa_w = n_heads * HEAD_DIM
    kv_w = N_KV_GROUPS * HEAD_DIM
    assert w_in.shape[1] == n_pool + nsa_w + 6 * kv_w + n_heads * N_BRANCH

    o0 = n_pool + nsa_w + 6 * kv_w
    w_p = w_in.T
    w_gt = jnp.pad(w_p[o0:].astype(bf16), ((0, LANES - n_heads * N_BRANCH), (0, 0)))

    cosf, sinf = _rope_tables(T)
    u, q, kcvc, kv, gates, w_out_b, w_gate_b, w_ple_gate_b = _in_proj(
        h2, in_norm_g[None, :], w_p, w_gt, cosf, sinf, w_out, w_gate, w_ple_gate, T=T, tm=ROW_TILE,
        n_pool=n_pool, n_q=nsa_w)

    kcmp, vcmp = _compress(
        kcvc.reshape(B, T, kcvc.shape[1]),
        cmp_k_pe.reshape(1, -1), cmp_k_w1, cmp_k_w2, cmp_v_pe.reshape(1, -1), cmp_v_w1, cmp_v_w2,
        bb=COMPRESS_BATCH if B % COMPRESS_BATCH == 0 else 1)

    eneg = _block_bias_table(T)
    y_nsa, w_up_b, w_down_b = _nsa(q, kcmp, vcmp, kv, eneg, _gate_expansion_table(), gates, w_up, w_down, B=B, T=T,
                                   tq=NSA_Q_TILE, n_sub=NSA_SUB_TILES)

    h2 = _out_proj(h2, u, y_nsa, pool_w, pool_scale[None, :], w_out_b, T=T, tm=ROW_TILE)
    return _ffn(h2, ffn_norm_g[None, :], w_gate_b, w_up_b, w_down_b, tm=FFN_ROW_TILE, tf=FFN_FF_TILE), w_ple_gate_b


def kernel(x, p, in_norm_g, w_in, pool_w, pool_scale, cmp_k_pe, cmp_k_w1, cmp_k_w2, cmp_v_pe, cmp_v_w1,
           cmp_v_w2, w_out, ffn_norm_g, w_gate, w_up, w_down, ple_norm_g, w_ple_gate, w_ple_proj, final_norm_g):
    B, T, d = x.shape
    depth = w_in.shape[0]
    assert depth == 1, "the final rmsnorm is fused into the last layer's per-layer-embedding kernel"
    h2 = x.reshape(B * T, d)
    i = 0
    h2, w_ple_gate_b = _mixer_ffn(
        h2, in_norm_g[i], w_in[i], pool_w[i], pool_scale[i], cmp_k_pe[i], cmp_k_w1[i], cmp_k_w2[i],
        cmp_v_pe[i], cmp_v_w1[i], cmp_v_w2[i], w_out[i], ffn_norm_g[i], w_gate[i], w_up[i], w_down[i],
        w_ple_gate[i], B=B, T=T)
    out = _ple(h2, p[i].reshape(B * T, -1), ple_norm_g[i][None, :], w_ple_gate_b,
               w_ple_proj[i], final_norm_g[None, :], tm=ROW_TILE)
    return out.reshape(B, T, d)
```

```python
import functools

import jax
import jax.numpy as jnp
import numpy as np
from jax import lax
from jax.experimental import pallas as pl
from jax.experimental.pallas import tpu as pltpu

EPS = 1e-6
NEG = -1e30
LANES = 128
POOL_WINDOWS = (2, 4, 8, 16)
POOL_HALO = 16
HEAD_DIM = 128
N_KV_GROUPS = 2
HEADS_PER_GROUP = 4
N_BRANCH = 3
CMP_BLOCK = 32
CMP_STRIDE = 16
SEL_BLOCK = 64
N_SELECT = 16
WINDOW = 512
ROPE_THETA = 500000.0
ROPE_DIM = HEAD_DIM // 4
ATTN_SCALE = HEAD_DIM ** -0.5
Q_SCALE = ATTN_SCALE * 1.4426950408889634
VMEM_LIMIT = 60 * 1024 * 1024
W_CHUNK = 256
IN_SEC = 512
ROW_TILE = 512
FFN_ROW_TILE = 1024
FFN_FF_TILE = 512
NSA_Q_TILE = 256
NSA_SUB_TILES = 2
COMPRESS_BATCH = 2

f32 = jnp.float32
bf16 = jnp.bfloat16


def _dot(a, b):
    return jnp.dot(a, b, preferred_element_type=f32)


def _dot_nt(a, b):
    return lax.dot_general(a, b, (((1,), (1,)), ((), ())), preferred_element_type=f32)


def _rms(x, g):
    return x * lax.rsqrt(jnp.mean(x * x, axis=-1, keepdims=True) + EPS) * g


def _rope(h, cosf, sinf, lane):
    half = ROPE_DIM // 2
    partner = jnp.where(lane < half, pltpu.roll(h, LANES - half, axis=1), pltpu.roll(h, half, axis=1))
    return h * cosf + partner * sinf


def _w_chunk_copy(w_hbm, stage_sc, sem, c):
    return pltpu.make_async_copy(w_hbm.at[pl.ds(c * W_CHUNK, W_CHUNK), :], stage_sc.at[c % 2], sem.at[c % 2])


def _in_proj_kernel(x_ref, g_ref, w_hbm, wgt_ref, cos_ref, sin_ref, ca_ref, cb_ref, cc_ref,
                    u_ref, q_ref, kcvc_ref, kv_ref, gt_ref, ca_out, cb_out, cc_out,
                    w_ref, stage_sc, sem):
    @pl.when(pl.program_id(0) == 0)
    def _():
        n_chunk = w_ref.shape[0] // W_CHUNK
        _w_chunk_copy(w_hbm, stage_sc, sem, 0).start()
        for c in range(n_chunk):
            if c + 1 < n_chunk:
                _w_chunk_copy(w_hbm, stage_sc, sem, c + 1).start()
            _w_chunk_copy(w_hbm, stage_sc, sem, c).wait()
            w_ref[c * W_CHUNK:(c + 1) * W_CHUNK, :] = stage_sc[c % 2].astype(bf16)

    def section(s):
        return w_ref[s * IN_SEC:(s + 1) * IN_SEC, :]

    ca_out[...] = ca_ref[...].astype(bf16)
    cb_out[...] = cb_ref[...].astype(bf16)
    cc_out[...] = cc_ref[...].astype(bf16)
    a = _rms(x_ref[...], g_ref[...]).astype(bf16)
    cosf = cos_ref[...]
    sinf = sin_ref[...]
    lane = lax.broadcasted_iota(jnp.int32, cosf.shape, 1)
    heads = IN_SEC // HEAD_DIM
    s = 0
    for c in range(0, u_ref.shape[1], IN_SEC):
        u_ref[:, c:c + IN_SEC] = _dot_nt(a, section(s)).astype(bf16)
        s += 1
    for c in range(0, q_ref.shape[1], IN_SEC):
        acc = _dot_nt(a, section(s))
        s += 1
        for j in range(heads):
            h = _rope(acc[:, j * HEAD_DIM:(j + 1) * HEAD_DIM], cosf, sinf, lane) * Q_SCALE
            q_ref[:, c + j * HEAD_DIM:c + (j + 1) * HEAD_DIM] = h.astype(bf16)
    for dst, base, dt in ((kcvc_ref, 0, f32), (kv_ref, 0, bf16), (kv_ref, IN_SEC, bf16)):
        acc = _dot_nt(a, section(s))
        s += 1
        for j in range(heads):
            h = acc[:, j * HEAD_DIM:(j + 1) * HEAD_DIM]
            if j < N_KV_GROUPS:
                h = _rope(h, cosf, sinf, lane)
            dst[:, base + j * HEAD_DIM:base + (j + 1) * HEAD_DIM] = h.astype(dt)
    gt_ref[...] = _dot_nt(a, wgt_ref[...])


def _in_proj(x2, g, w_p, w_gt, cosf, sinf, cast_a, cast_b, cast_c, *, T, tm, n_pool, n_q):
    n_rows, d = x2.shape
    assert n_pool % IN_SEC == 0 and n_q % IN_SEC == 0 and 2 * N_KV_GROUPS * HEAD_DIM == IN_SEC
    n_w = (w_p.shape[0] // W_CHUNK) * W_CHUNK
    assert w_p.dtype == f32 and n_w + w_gt.shape[0] >= w_p.shape[0] and n_w == n_pool + n_q + 3 * IN_SEC
    tiles_per_seq = T // tm
    n_steps = n_rows // tm
    row = lambda i: (i, 0)
    cast_spec = lambda w: pl.BlockSpec((w.shape[0] // n_steps, w.shape[1]), row)
    const = lambda i: (0, 0)
    tab = lambda i: (i % tiles_per_seq, 0)
    return pl.pallas_call(
        _in_proj_kernel,
        grid=(n_rows // tm,),
        in_specs=[
            pl.BlockSpec((tm, d), row),
            pl.BlockSpec((1, d), const),
            pl.BlockSpec(memory_space=pl.ANY),
            pl.BlockSpec((LANES, d), const),
            pl.BlockSpec((tm, LANES), tab),
            pl.BlockSpec((tm, LANES), tab),
            cast_spec(cast_a), cast_spec(cast_b), cast_spec(cast_c),
        ],
        out_specs=[
            pl.BlockSpec((tm, n_pool), row),
            pl.BlockSpec((tm, n_q), row),
            pl.BlockSpec((tm, IN_SEC), row),
            pl.BlockSpec((tm, 2 * IN_SEC), row),
            pl.BlockSpec((tm, LANES), row),
            cast_spec(cast_a), cast_spec(cast_b), cast_spec(cast_c),
        ],
        out_shape=[
            jax.ShapeDtypeStruct((n_rows, n_pool), bf16),
            jax.ShapeDtypeStruct((n_rows, n_q), bf16),
            jax.ShapeDtypeStruct((n_rows, IN_SEC), f32),
            jax.ShapeDtypeStruct((n_rows, 2 * IN_SEC), bf16),
            jax.ShapeDtypeStruct((n_rows, LANES), f32),
            jax.ShapeDtypeStruct(cast_a.shape, bf16),
            jax.ShapeDtypeStruct(cast_b.shape, bf16),
            jax.ShapeDtypeStruct(cast_c.shape, bf16),
        ],
        scratch_shapes=[
            pltpu.VMEM((n_w, d), bf16),
            pltpu.VMEM((2, W_CHUNK, d), f32),
            pltpu.SemaphoreType.DMA((2,)),
        ],
        compiler_params=pltpu.CompilerParams(
            dimension_semantics=("arbitrary",), vmem_limit_bytes=VMEM_LIMIT),
        name="in_proj",
    )(x2, g, w_p, w_gt, cosf, sinf, cast_a, cast_b, cast_c)


def _gelu_tanh(x):
    return 0.5 * x * (1.0 + jnp.tanh(0.7978845608028654 * (x + 0.044715 * (x * x * x))))


def _compress_kernel(kc0_ref, kc1_ref, vc0_ref, vc1_ref, pek_ref, w1k_ref, w2k_ref, pev_ref, w1v_ref, w2v_ref,
                     kc_ref, vc_ref):
    bb = kc0_ref.shape[0]
    n_rows = kc0_ref.shape[1] // CMP_STRIDE
    half_k = (CMP_BLOCK // 2) * HEAD_DIM
    n_stack = bb * N_KV_GROUPS * n_rows
    rowi = lax.broadcasted_iota(jnp.int32, (n_stack, HEAD_DIM), 0)
    for srcs, pe_ref, w1_ref, w2_ref, o_ref in (((kc0_ref, kc1_ref), pek_ref, w1k_ref, w2k_ref, kc_ref),
                                                ((vc0_ref, vc1_ref), pev_ref, w1v_ref, w2v_ref, vc_ref)):
        x = jnp.concatenate(
            [jnp.concatenate([src[b, pl.ds(l, n_rows, stride=CMP_STRIDE), :] for l in range(CMP_STRIDE)], axis=1)
             for b in range(bb) for src in srcs], axis=0)
        za = _dot((x + pe_ref[:, :half_k]).astype(bf16), w1_ref[:half_k, :].astype(bf16))
        zb = _dot((x + pe_ref[:, half_k:]).astype(bf16), w1_ref[half_k:, :].astype(bf16))
        h = za + pltpu.roll(zb, n_stack - 1, axis=0)
        o = _dot(_gelu_tanh(h).astype(bf16), w2_ref[...].astype(bf16))
        o = jnp.where((rowi & (n_rows - 1)) < n_rows - 1, o, 0.0).astype(bf16)
        for b in range(bb):
            for g in range(N_KV_GROUPS):
                r0 = (b * N_KV_GROUPS + g) * n_rows
                o_ref[b, g] = o[r0:r0 + n_rows]


def _compress(r3, pek, w1k, w2k, pev, w1v, w2v, *, bb):
    B, T, width = r3.shape
    n_rows = T // CMP_STRIDE
    assert width == 2 * N_KV_GROUPS * HEAD_DIM and N_KV_GROUPS == 2 and B % bb == 0 and n_rows & (n_rows - 1) == 0
    const = lambda b: (0, 0)
    out = jax.ShapeDtypeStruct((B, N_KV_GROUPS, n_rows, HEAD_DIM), bf16)
    ospec = pl.BlockSpec((bb, N_KV_GROUPS, n_rows, HEAD_DIM), lambda b: (b, 0, 0, 0))
    sec = lambda s: pl.BlockSpec((bb, T, HEAD_DIM), lambda b: (b, 0, s))
    return pl.pallas_call(
        _compress_kernel,
        grid=(B // bb,),
        in_specs=[
            sec(0), sec(1), sec(2), sec(3),
            pl.BlockSpec(pek.shape, const), pl.BlockSpec(w1k.shape, const), pl.BlockSpec(w2k.shape, const),
            pl.BlockSpec(pev.shape, const), pl.BlockSpec(w1v.shape, const), pl.BlockSpec(w2v.shape, const),
        ],
        out_specs=[ospec, ospec],
        out_shape=[out, out],
        compiler_params=pltpu.CompilerParams(
            dimension_semantics=("arbitrary",), vmem_limit_bytes=VMEM_LIMIT),
        name="compress",
    )(r3, r3, r3, r3, pek, w1k, w2k, pev, w1v, w2v)


def _nsa_front(qs, t0, kc, vc, kw_ref, vw_ref, *, tq, T):
    R = HEADS_PER_GROUP
    M = R * tq
    n_cmp = kc.shape[0]
    n_sel = T // SEL_BLOCK

    wk = WINDOW + tq
    w0 = pl.multiple_of(jnp.maximum(t0 - WINDOW, 0), tq)
    sw = _dot_nt(qs, kw_ref[pl.ds(w0, wk), :])
    diff = (t0 - w0) + lax.broadcasted_iota(jnp.int32, (tq, wk), 0) - lax.broadcasted_iota(jnp.int32, (tq, wk), 1)
    wbias = jnp.where((diff & -WINDOW) == 0, 0.0, NEG)
    sw = jnp.concatenate([sw[r * tq:(r + 1) * tq] + wbias for r in range(R)], axis=0)
    ew = jnp.exp2(sw - jnp.max(sw, axis=1, keepdims=True))
    o_win = _dot(ew.astype(bf16), vw_ref[pl.ds(w0, wk), :]) / jnp.sum(ew, axis=1, keepdims=True)

    s = _dot_nt(qs, kc)
    valid = (CMP_STRIDE * lax.broadcasted_iota(jnp.int32, (tq, n_cmp), 1) + (CMP_BLOCK - 1)
             <= t0 + lax.broadcasted_iota(jnp.int32, (tq, n_cmp), 0))
    s = jnp.concatenate([jnp.where(valid, s[r * tq:(r + 1) * tq], NEG) for r in range(R)], axis=0)
    e = jnp.exp2(s - jnp.max(s, axis=1, keepdims=True))
    p = e / jnp.sum(e, axis=1, keepdims=True)
    p = jnp.concatenate([jnp.where(valid, p[r * tq:(r + 1) * tq], 0.0) for r in range(R)], axis=0)
    o_cmp = _dot(p.astype(bf16), vc)

    psum = p[0:tq]
    for r in range(1, R):
        psum = psum + p[r * tq:(r + 1) * tq]
    p_hi = psum.astype(bf16)
    p_lo = (psum - p_hi.astype(f32)).astype(bf16)
    sb = lax.broadcasted_iota(jnp.int32, (n_sel, n_cmp), 0) * SEL_BLOCK
    cb = lax.broadcasted_iota(jnp.int32, (n_sel, n_cmp), 1) * CMP_STRIDE
    ov_t = jnp.where((cb < sb + SEL_BLOCK) & (cb + CMP_BLOCK > sb), 1.0, 0.0).astype(bf16)
    imp_t = _dot_nt(ov_t, p_hi) + _dot_nt(ov_t, p_lo)

    jj = lax.broadcasted_iota(jnp.int32, (n_sel, tq), 0)
    cur = jnp.right_shift(t0 + lax.broadcasted_iota(jnp.int32, (n_sel, tq), 1), SEL_BLOCK.bit_length() - 1)
    forced = (jj == 0) | (jj == cur) | (jj == cur - 1)
    v = jnp.where(jj > cur, -jnp.inf, jnp.where(forced, jnp.inf, imp_t))
    sub = 8
    slabs = [v[k * sub:(k + 1) * sub] for k in range(n_sel // sub)]
    ranks = [jnp.zeros((sub, tq), f32) for _ in slabs]
    j_in = lax.broadcasted_iota(jnp.int32, (sub, tq), 0)
    for c in range(n_sel):
        v_c = v[c:c + 1, :]
        for k, v_k in enumerate(slabs):
            if k * sub > c:
                ahead = jnp.where(v_c >= v_k, 1.0, 0.0)
            elif (k + 1) * sub <= c:
                ahead = jnp.where(v_c > v_k, 1.0, 0.0)
            else:
                ahead = jnp.where(j_in > c - k * sub, jnp.where(v_c >= v_k, 1.0, 0.0),
                                  jnp.where(v_c > v_k, 1.0, 0.0))
            ranks[k] = ranks[k] + ahead
    rank = jnp.concatenate(ranks, axis=0)
    unsel_t = jnp.where(rank < float(min(N_SELECT, n_sel)), 0.0, 1.0)
    unsel_t = jnp.concatenate([unsel_t, jnp.zeros((LANES - n_sel, tq), f32)], axis=0)
    return o_win, o_cmp, unsel_t.T.astype(bf16)


def _nsa_kernel(q_ref, kc_ref, vc_ref, ks_ref, vs_ref, kw_ref, vw_ref, eneg_ref, gexp_ref, gt_ref, ca_ref, cb_ref,
                o_ref, ca_out, cb_out, s_sc, mx_sc, ls_sc, acc_sc, *, tq, n_sub, T):
    ca_out[...] = ca_ref[...].astype(bf16)
    cb_out[...] = cb_ref[...].astype(bf16)
    R = HEADS_PER_GROUP
    M = R * tq
    tk = n_sub * tq
    i = pl.program_id(2)
    t0 = i * tk
    kc = kc_ref[0, 0]
    vc = vc_ref[0, 0]

    fronts = []
    q_rows = []
    for sb in range(n_sub):
        q = q_ref[sb * tq:(sb + 1) * tq, :]
        qs = jnp.concatenate([q[:, r * HEAD_DIM:(r + 1) * HEAD_DIM] for r in range(R)], axis=0)
        o_win, o_cmp, unsel = _nsa_front(qs, t0 + sb * tq, kc, vc, kw_ref, vw_ref, tq=tq, T=T)
        fronts.append((o_win, o_cmp))
        q_rows.append(jnp.concatenate([qs, jnp.concatenate([unsel] * R, axis=0)], axis=1))
    q_aug = jnp.concatenate(q_rows, axis=0)
    MM = n_sub * M

    n_chunk = tk // LANES
    mx_sc[...] = jnp.full(mx_sc.shape, NEG, f32)

    def score_step(kt, causal):
        k0 = pl.multiple_of(kt * tk, tk)
        k_aug = jnp.concatenate([ks_ref[pl.ds(k0, tk), :], eneg_ref[pl.ds(k0, tk), :]], axis=1)
        sc = _dot_nt(q_aug, k_aug)
        if causal:
            cbias = jnp.where(lax.broadcasted_iota(jnp.int32, (tk, tk), 0)
                              >= lax.broadcasted_iota(jnp.int32, (tk, tk), 1), 0.0, NEG)
            sc = jnp.concatenate([sc[sb * M + r * tq:sb * M + (r + 1) * tq] + cbias[sb * tq:(sb + 1) * tq]
                                  for sb in range(n_sub) for r in range(R)], axis=0)
        s_sc[kt] = sc
        part = sc[:, :LANES]
        for c in range(1, n_chunk):
            part = jnp.maximum(part, sc[:, c * LANES:(c + 1) * LANES])
        mx_sc[...] = jnp.maximum(mx_sc[...], part)

    def score_body(kt, carry):
        score_step(kt, False)
        return carry

    lax.fori_loop(0, i, score_body, 0)
    score_step(i, True)
    mx_sc[...] = jnp.broadcast_to(jnp.max(mx_sc[...], axis=1, keepdims=True), mx_sc.shape)
    ls_sc[...] = jnp.zeros(ls_sc.shape, f32)
    acc_sc[...] = jnp.zeros(acc_sc.shape, f32)

    def pv_body(kt, carry):
        k0 = pl.multiple_of(kt * tk, tk)
        sc = s_sc[kt]
        mb = mx_sc[...]
        ps = [jnp.exp2(sc[:, c * LANES:(c + 1) * LANES] - mb) for c in range(n_chunk)]
        ls_sc[...] += functools.reduce(lambda a, b: a + b, ps)
        acc_sc[...] += _dot(jnp.concatenate(ps, axis=1).astype(bf16), vs_ref[pl.ds(k0, tk), :])
        return carry

    lax.fori_loop(0, i + 1, pv_body, 0)
    o_sel = acc_sc[...] / jnp.sum(ls_sc[...], axis=1, keepdims=True)

    gt_all = jax.nn.sigmoid(gt_ref[...])
    gt_all = jnp.where(pl.program_id(1) == 0, gt_all, pltpu.roll(gt_all, LANES - R * N_BRANCH, axis=1))
    g_hi = gt_all.astype(bf16)
    rest = gt_all - g_hi.astype(f32)
    g_mid = rest.astype(bf16)
    g_lo = (rest - g_mid.astype(f32)).astype(bf16)
    g_exp = _dot(jnp.concatenate([g_hi, g_mid, g_lo], axis=1), gexp_ref[...])
    for sb, (o_win, o_cmp) in enumerate(fronts):
        for r in range(R):
            rows = slice(r * tq, (r + 1) * tq)
            srows = slice(sb * M + r * tq, sb * M + (r + 1) * tq)
            gates = [g_exp[sb * tq:(sb + 1) * tq, (r * N_BRANCH + br) * LANES:(r * N_BRANCH + br + 1) * LANES]
                     for br in range(N_BRANCH)]
            o = gates[0] * o_cmp[rows] + gates[1] * o_sel[srows] + gates[2] * o_win[rows]
            o_ref[sb * tq:(sb + 1) * tq, r * HEAD_DIM:(r + 1) * HEAD_DIM] = o.astype(bf16)


def _nsa(q, kcmp, vcmp, kv, eneg, gexp, gates, cast_a, cast_b, *, B, T, tq, n_sub):
    G, R = N_KV_GROUPS, HEADS_PER_GROUP
    tk = n_sub * tq
    nt = T // tk
    n_cmp = kcmp.shape[2]
    assert tq % LANES == 0 and tq & (tq - 1) == 0 and T % tk == 0 and T >= WINDOW + tq
    assert WINDOW % tq == 0 and WINDOW & (WINDOW - 1) == 0 and (R * tq) & (R * tq - 1) == 0
    rowblk = lambda b, g, i: (b * nt + i, g)
    cmp_spec = pl.BlockSpec((1, 1, n_cmp, HEAD_DIM), lambda b, g, i: (b, g, 0, 0))
    kvspec = lambda sec: pl.BlockSpec((T, HEAD_DIM), lambda b, g, i: (b, sec * G + g))
    n_steps = B * G * nt
    cast_spec = lambda w: pl.BlockSpec((w.shape[0] // n_steps, w.shape[1]), lambda b, g, i: ((b * G + g) * nt + i, 0))
    return pl.pallas_call(
        functools.partial(_nsa_kernel, tq=tq, n_sub=n_sub, T=T),
        grid=(B, G, nt),
        in_specs=[
            pl.BlockSpec((tk, R * HEAD_DIM), rowblk),
            cmp_spec, cmp_spec,
            kvspec(0), kvspec(1), kvspec(2), kvspec(3),
            pl.BlockSpec((T, LANES), lambda b, g, i: (0, 0)),
            pl.BlockSpec(gexp.shape, lambda b, g, i: (0, 0)),
            pl.BlockSpec((tk, LANES), lambda b, g, i: (b * nt + i, 0)),
            cast_spec(cast_a), cast_spec(cast_b),
        ],
        out_specs=[pl.BlockSpec((tk, R * HEAD_DIM), rowblk), cast_spec(cast_a), cast_spec(cast_b)],
        out_shape=[jax.ShapeDtypeStruct((B * T, G * R * HEAD_DIM), bf16), jax.ShapeDtypeStruct(cast_a.shape, bf16),
                   jax.ShapeDtypeStruct(cast_b.shape, bf16)],
        scratch_shapes=[
            pltpu.VMEM((T // tk, n_sub * R * tq, tk), f32),
            pltpu.VMEM((n_sub * R * tq, LANES), f32),
            pltpu.VMEM((n_sub * R * tq, LANES), f32),
            pltpu.VMEM((n_sub * R * tq, HEAD_DIM), f32),
        ],
        compiler_params=pltpu.CompilerParams(
            dimension_semantics=("arbitrary", "arbitrary", "arbitrary"), vmem_limit_bytes=VMEM_LIMIT),
        name="nsa",
    )(q, kcmp, vcmp, kv, kv, kv, kv, eneg, gexp, gates, cast_a, cast_b)


def _out_proj_kernel(x_ref, u_ref, uh_ref, y_ref, pw_ref, ps_ref, wo_ref, o_ref, *, tm, T):
    i = pl.program_id(0)
    tiles_per_seq = T // tm
    keep_halo = jnp.where((i % tiles_per_seq) == 0, 0.0, 1.0)
    u = u_ref[...]
    halo = uh_ref[...].astype(f32) * keep_halo
    tseq = (i % tiles_per_seq) * tm + lax.broadcasted_iota(jnp.int32, (tm, 1), 0)
    n_pool = u.shape[1]
    gd = n_pool // len(POOL_WINDOWS)
    acc = x_ref[...] + _dot(y_ref[...], wo_ref[n_pool:, :])
    for gi, w in enumerate(POOL_WINDOWS):
        assert w <= POOL_HALO and w & (w - 1) == 0
        cols = slice(gi * gd, (gi + 1) * gd)
        ug = u[:, cols].astype(f32)
        ws = jnp.concatenate([halo[:, cols], ug], axis=0)
        k = 1
        while k < w:
            ws = ws + pltpu.roll(ws, k, axis=0)
            k *= 2
        win_sum = ws[POOL_HALO:]
        count = jnp.minimum(tseq + 1, w).astype(f32)
        pooled = win_sum / count - ug
        yg = _dot(pooled.astype(bf16), pw_ref[gi].astype(bf16)) * ps_ref[:, cols]
        acc = acc + _dot(yg.astype(bf16), wo_ref[cols, :])
    o_ref[...] = acc


def _out_proj(x2, u, y_nsa, pool_w, pool_scale, w_out, *, T, tm):
    n_rows, d = x2.shape
    n_pool = u.shape[1]
    hb = tm // POOL_HALO
    row = lambda i: (i, 0)
    const2 = lambda i: (0, 0)
    return pl.pallas_call(
        functools.partial(_out_proj_kernel, tm=tm, T=T),
        grid=(n_rows // tm,),
        in_specs=[
            pl.BlockSpec((tm, d), row),
            pl.BlockSpec((tm, n_pool), row),
            pl.BlockSpec((POOL_HALO, n_pool), lambda i: (jnp.maximum(i * hb - 1, 0), 0)),
            pl.BlockSpec((tm, y_nsa.shape[1]), row),
            pl.BlockSpec(pool_w.shape, lambda i: (0, 0, 0)),
            pl.BlockSpec((1, n_pool), const2),
            pl.BlockSpec(w_out.shape, const2, pipeline_mode=pl.Buffered(1)),
        ],
        out_specs=pl.BlockSpec((tm, d), row),
        out_shape=jax.ShapeDtypeStruct((n_rows, d), f32),
        compiler_params=pltpu.CompilerParams(
            dimension_semantics=("arbitrary",), vmem_limit_bytes=VMEM_LIMIT),
        name="out_proj",
    )(x2, u, u, y_nsa, pool_w, pool_scale, w_out)


def _ffn_kernel(h_ref, g_ref, wg_ref, wu_ref, wd_ref, o_ref, n_sc):
    @pl.when(pl.program_id(1) == 0)
    def _():
        h = h_ref[...]
        n_sc[...] = _rms(h, g_ref[...]).astype(bf16)
        o_ref[...] = h

    n = n_sc[...]
    half = wg_ref.shape[1] // 2
    acc = None
    for c in range(2):
        cols = slice(c * half, (c + 1) * half)
        gate = _dot(n, wg_ref[:, cols])
        up = _dot(n, wu_ref[:, cols])
        a = gate * jax.nn.sigmoid(gate) * up
        d = _dot(a.astype(bf16), wd_ref[cols, :])
        acc = d if acc is None else acc + d
    o_ref[...] += acc


def _ffn(h, g, w_gate, w_up, w_down, *, tm, tf):
    n_rows, d = h.shape
    d_ff = w_gate.shape[1]
    return pl.pallas_call(
        _ffn_kernel,
        grid=(n_rows // tm, d_ff // tf),
        in_specs=[
            pl.BlockSpec((tm, d), lambda i, f: (i, 0)),
            pl.BlockSpec((1, d), lambda i, f: (0, 0)),
            pl.BlockSpec((d, tf), lambda i, f: (0, f)),
            pl.BlockSpec((d, tf), lambda i, f: (0, f)),
            pl.BlockSpec((tf, d), lambda i, f: (f, 0)),
        ],
        out_specs=pl.BlockSpec((tm, d), lambda i, f: (i, 0)),
        out_shape=jax.ShapeDtypeStruct((n_rows, d), f32),
        scratch_shapes=[pltpu.VMEM((tm, d), bf16)],
        compiler_params=pltpu.CompilerParams(
            dimension_semantics=("arbitrary", "arbitrary"), vmem_limit_bytes=VMEM_LIMIT),
        name="ffn",
    )(h, g, w_gate, w_up, w_down)


def _ple_kernel(h_ref, p_ref, g_ref, wg_ref, wp_ref, fg_ref, o_ref):
    h = h_ref[...]
    n = _rms(h, g_ref[...]).astype(bf16)
    gate = jax.nn.sigmoid(_dot(n, wg_ref[...]))
    h = h + _dot(p_ref[...].astype(bf16), wp_ref[...].astype(bf16)) * gate
    o_ref[...] = _rms(h, fg_ref[...])


def _ple(h, p2, g, w_gate, w_proj, fg, *, tm):
    n_rows, d = h.shape
    row = lambda i: (i, 0)
    const = lambda i: (0, 0)
    return pl.pallas_call(
        _ple_kernel,
        grid=(n_rows // tm,),
        in_specs=[
            pl.BlockSpec((tm, d), row),
            pl.BlockSpec((tm, p2.shape[1]), row),
            pl.BlockSpec((1, d), const),
            pl.BlockSpec(w_gate.shape, const, pipeline_mode=pl.Buffered(1)),
            pl.BlockSpec(w_proj.shape, const),
            pl.BlockSpec((1, d), const),
        ],
        out_specs=pl.BlockSpec((tm, d), row),
        out_shape=jax.ShapeDtypeStruct((n_rows, d), f32),
        compiler_params=pltpu.CompilerParams(
            dimension_semantics=("arbitrary",), vmem_limit_bytes=VMEM_LIMIT),
        name="ple",
    )(h, p2, g, w_gate, w_proj, fg)


def _rope_tables(T):
    pos = np.arange(T, dtype=np.float64)
    inv_freq = ROPE_THETA ** (-np.arange(0, ROPE_DIM, 2, dtype=np.float64) / ROPE_DIM)
    ang = pos[:, None] * inv_freq[None, :]
    cos, sin = np.cos(ang), np.sin(ang)
    rest = HEAD_DIM - ROPE_DIM
    cosf = np.concatenate([cos, cos, np.ones((T, rest))], axis=1)
    sinf = np.concatenate([-sin, sin, np.zeros((T, rest))], axis=1)
    return jnp.asarray(cosf, f32), jnp.asarray(sinf, f32)


def _block_bias_table(T):
    key_blk = np.arange(T)[:, None] // SEL_BLOCK
    return jnp.asarray(np.where(key_blk == np.arange(LANES)[None, :], NEG, 0.0).astype(bf16))


def _gate_expansion_table():
    n_col = HEADS_PER_GROUP * N_BRANCH
    e = np.zeros((3, LANES, n_col, LANES), np.float32)
    for c in range(n_col):
        e[:, c, c, :] = 1.0
    return jnp.asarray(e.reshape(3 * LANES, n_col * LANES).astype(bf16))


def _mixer_ffn(h2, in_norm_g, w_in, pool_w, pool_scale, cmp_k_pe, cmp_k_w1, cmp_k_w2,
               cmp_v_pe, cmp_v_w1, cmp_v_w2, w_out, ffn_norm_g, w_gate, w_up, w_down, w_ple_gate, *, B, T):
    n_pool = pool_scale.shape[0]
    n_heads = N_KV_GROUPS * HEADS_PER_GROUP
    nsa_w = n_heads * HEAD_DIM
    kv_w = N_KV_GROUPS * HEAD_DIM
    assert w_in.shape[1] == n_pool + nsa_w + 6 * kv_w + n_heads * N_BRANCH

    o0 = n_pool + nsa_w + 6 * kv_w
    w_p = w_in.T
    w_gt = jnp.pad(w_p[o0:].astype(bf16), ((0, LANES - n_heads * N_BRANCH), (0, 0)))

    cosf, sinf = _rope_tables(T)
    u, q, kcvc, kv, gates, w_out_b, w_gate_b, w_ple_gate_b = _in_proj(
        h2, in_norm_g[None, :], w_p, w_gt, cosf, sinf, w_out, w_gate, w_ple_gate, T=T, tm=ROW_TILE,
        n_pool=n_pool, n_q=nsa_w)

    kcmp, vcmp = _compress(
        kcvc.reshape(B, T, kcvc.shape[1]),
        cmp_k_pe.reshape(1, -1), cmp_k_w1, cmp_k_w2, cmp_v_pe.reshape(1, -1), cmp_v_w1, cmp_v_w2,
        bb=COMPRESS_BATCH if B % COMPRESS_BATCH == 0 else 1)

    eneg = _block_bias_table(T)
    y_nsa, w_up_b, w_down_b = _nsa(q, kcmp, vcmp, kv, eneg, _gate_expansion_table(), gates, w_up, w_down, B=B, T=T,
                                   tq=NSA_Q_TILE, n_sub=NSA_SUB_TILES)

    h2 = _out_proj(h2, u, y_nsa, pool_w, pool_scale[None, :], w_out_b, T=T, tm=ROW_TILE)
    return _ffn(h2, ffn_norm_g[None, :], w_gate_b, w_up_b, w_down_b, tm=FFN_ROW_TILE, tf=FFN_FF_TILE), w_ple_gate_b


def kernel(x, p, in_norm_g, w_in, pool_w, pool_scale, cmp_k_pe, cmp_k_w1, cmp_k_w2, cmp_v_pe, cmp_v_w1,
           cmp_v_w2, w_out, ffn_norm_g, w_gate, w_up, w_down, ple_norm_g, w_ple_gate, w_ple_proj, final_norm_g):
    B, T, d = x.shape
    depth = w_in.shape[0]
    assert depth == 1, "the final rmsnorm is fused into the last layer's per-layer-embedding kernel"
    h2 = x.reshape(B * T, d)
    i = 0
    h2, w_ple_gate_b = _mixer_ffn(
        h2, in_norm_g[i], w_in[i], pool_w[i], pool_scale[i], cmp_k_pe[i], cmp_k_w1[i], cmp_k_w2[i],
        cmp_v_pe[i], cmp_v_w1[i], cmp_v_w2[i], w_out[i], ffn_norm_g[i], w_gate[i], w_up[i], w_down[i],
        w_ple_gate[i], B=B, T=T)
    out = _ple(h2, p[i].reshape(B * T, -1), ple_norm_g[i][None, :], w_ple_gate_b,
               w_ple_proj[i], final_norm_g[None, :], tm=ROW_TILE)
    return out.reshape(B, T, d)
```

```python
import functools

import jax
import jax.numpy as jnp
import numpy as np
from jax import lax
from jax.experimental import pallas as pl
from jax.experimental.pallas import tpu as pltpu

EPS = 1e-6
NEG = -1e30
LANES = 128
POOL_WINDOWS = (2, 4, 8, 16)
POOL_HALO = 16
HEAD_DIM = 128
N_KV_GROUPS = 2
HEADS_PER_GROUP = 4
N_BRANCH = 3
CMP_BLOCK = 32
CMP_STRIDE = 16
SEL_BLOCK = 64
N_SELECT = 16
WINDOW = 512
ROPE_THETA = 500000.0
ROPE_DIM = HEAD_DIM // 4
ATTN_SCALE = HEAD_DIM ** -0.5
Q_SCALE = ATTN_SCALE * 1.4426950408889634
VMEM_LIMIT = 60 * 1024 * 1024
W_CHUNK = 128
W_SLOTS = 4
IN_SEC = 512
ROW_TILE = 512
FFN_ROW_TILE = 1024
FFN_FF_TILE = 512
NSA_Q_TILE = 256
NSA_SUB_TILES = 2
COMPRESS_BATCH = 2

f32 = jnp.float32
bf16 = jnp.bfloat16


def _dot(a, b):
    return jnp.dot(a, b, preferred_element_type=f32)


def _dot_nt(a, b):
    return lax.dot_general(a, b, (((1,), (1,)), ((), ())), preferred_element_type=f32)


def _rms(x, g):
    return x * lax.rsqrt(jnp.mean(x * x, axis=-1, keepdims=True) + EPS) * g


def _rope(h, cosf, sinf, lane):
    half = ROPE_DIM // 2
    partner = jnp.where(lane < half, pltpu.roll(h, LANES - half, axis=1), pltpu.roll(h, half, axis=1))
    return h * cosf + partner * sinf


def _w_chunk_copy(w_hbm, stage_sc, sem, c):
    slot = c % W_SLOTS
    return pltpu.make_async_copy(w_hbm.at[pl.ds(c * W_CHUNK, W_CHUNK), :], stage_sc.at[slot], sem.at[slot])


def _in_proj_kernel(x_ref, g_ref, w_hbm, wgt_ref, cos_ref, sin_ref, ca_ref, cb_ref, cc_ref,
                    u_ref, q_ref, kcvc_ref, kv_ref, gt_ref, ca_out, cb_out, cc_out,
                    w_ref, stage_sc, sem):
    @pl.when(pl.program_id(0) == 0)
    def _():
        n_chunk = w_ref.shape[0] // W_CHUNK
        for c in range(min(W_SLOTS - 1, n_chunk)):
            _w_chunk_copy(w_hbm, stage_sc, sem, c).start()
        for c in range(n_chunk):
            if c + W_SLOTS - 1 < n_chunk:
                _w_chunk_copy(w_hbm, stage_sc, sem, c + W_SLOTS - 1).start()
            _w_chunk_copy(w_hbm, stage_sc, sem, c).wait()
            w_ref[c * W_CHUNK:(c + 1) * W_CHUNK, :] = stage_sc[c % W_SLOTS].astype(bf16)

    def section(s):
        return w_ref[s * IN_SEC:(s + 1) * IN_SEC, :]

    ca_out[...] = ca_ref[...].astype(bf16)
    cb_out[...] = cb_ref[...].astype(bf16)
    cc_out[...] = cc_ref[...].astype(bf16)
    a = _rms(x_ref[...], g_ref[...]).astype(bf16)
    cosf = cos_ref[...]
    sinf = sin_ref[...]
    lane = lax.broadcasted_iota(jnp.int32, cosf.shape, 1)
    heads = IN_SEC // HEAD_DIM
    s = 0
    for c in range(0, u_ref.shape[1], IN_SEC):
        u_ref[:, c:c + IN_SEC] = _dot_nt(a, section(s)).astype(bf16)
        s += 1
    for c in range(0, q_ref.shape[1], IN_SEC):
        acc = _dot_nt(a, section(s))
        s += 1
        for j in range(heads):
            h = _rope(acc[:, j * HEAD_DIM:(j + 1) * HEAD_DIM], cosf, sinf, lane) * Q_SCALE
            q_ref[:, c + j * HEAD_DIM:c + (j + 1) * HEAD_DIM] = h.astype(bf16)
    for dst, base, dt in ((kcvc_ref, 0, f32), (kv_ref, 0, bf16), (kv_ref, IN_SEC, bf16)):
        acc = _dot_nt(a, section(s))
        s += 1
        for j in range(heads):
            h = acc[:, j * HEAD_DIM:(j + 1) * HEAD_DIM]
            if j < N_KV_GROUPS:
                h = _rope(h, cosf, sinf, lane)
            dst[:, base + j * HEAD_DIM:base + (j + 1) * HEAD_DIM] = h.astype(dt)
    gt_ref[...] = _dot_nt(a, wgt_ref[...])


def _in_proj(x2, g, w_p, w_gt, cosf, sinf, cast_a, cast_b, cast_c, *, T, tm, n_pool, n_q):
    n_rows, d = x2.shape
    assert n_pool % IN_SEC == 0 and n_q % IN_SEC == 0 and 2 * N_KV_GROUPS * HEAD_DIM == IN_SEC
    n_w = (w_p.shape[0] // W_CHUNK) * W_CHUNK
    assert w_p.dtype == f32 and n_w + w_gt.shape[0] >= w_p.shape[0] and n_w == n_pool + n_q + 3 * IN_SEC
    tiles_per_seq = T // tm
    n_steps = n_rows // tm
    row = lambda i: (i, 0)
    cast_spec = lambda w: pl.BlockSpec((w.shape[0] // n_steps, w.shape[1]), row)
    const = lambda i: (0, 0)
    tab = lambda i: (i % tiles_per_seq, 0)
    return pl.pallas_call(
        _in_proj_kernel,
        grid=(n_rows // tm,),
        in_specs=[
            pl.BlockSpec((tm, d), row),
            pl.BlockSpec((1, d), const),
            pl.BlockSpec(memory_space=pl.ANY),
            pl.BlockSpec((LANES, d), const),
            pl.BlockSpec((tm, LANES), tab),
            pl.BlockSpec((tm, LANES), tab),
            cast_spec(cast_a), cast_spec(cast_b), cast_spec(cast_c),
        ],
        out_specs=[
            pl.BlockSpec((tm, n_pool), row),
            pl.BlockSpec((tm, n_q), row),
            pl.BlockSpec((tm, IN_SEC), row),
            pl.BlockSpec((tm, 2 * IN_SEC), row),
            pl.BlockSpec((tm, LANES), row),
            cast_spec(cast_a), cast_spec(cast_b), cast_spec(cast_c),
        ],
        out_shape=[
            jax.ShapeDtypeStruct((n_rows, n_pool), bf16),
            jax.ShapeDtypeStruct((n_rows, n_q), bf16),
            jax.ShapeDtypeStruct((n_rows, IN_SEC), f32),
            jax.ShapeDtypeStruct((n_rows, 2 * IN_SEC), bf16),
            jax.ShapeDtypeStruct((n_rows, LANES), f32),
            jax.ShapeDtypeStruct(cast_a.shape, bf16),
            jax.ShapeDtypeStruct(cast_b.shape, bf16),
            jax.ShapeDtypeStruct(cast_c.shape, bf16),
        ],
        scratch_shapes=[
            pltpu.VMEM((n_w, d), bf16),
            pltpu.VMEM((W_SLOTS, W_CHUNK, d), f32),
            pltpu.SemaphoreType.DMA((W_SLOTS,)),
        ],
        compiler_params=pltpu.CompilerParams(
            dimension_semantics=("arbitrary",), vmem_limit_bytes=VMEM_LIMIT),
        name="in_proj",
    )(x2, g, w_p, w_gt, cosf, sinf, cast_a, cast_b, cast_c)


def _gelu_tanh(x):
    return 0.5 * x * (1.0 + jnp.tanh(0.7978845608028654 * (x + 0.044715 * (x * x * x))))


def _compress_kernel(kc0_ref, kc1_ref, vc0_ref, vc1_ref, pek_ref, w1k_ref, w2k_ref, pev_ref, w1v_ref, w2v_ref,
                     kc_ref, vc_ref):
    bb = kc0_ref.shape[0]
    n_rows = kc0_ref.shape[1] // CMP_STRIDE
    half_k = (CMP_BLOCK // 2) * HEAD_DIM
    n_stack = bb * N_KV_GROUPS * n_rows
    rowi = lax.broadcasted_iota(jnp.int32, (n_stack, HEAD_DIM), 0)
    for srcs, pe_ref, w1_ref, w2_ref, o_ref in (((kc0_ref, kc1_ref), pek_ref, w1k_ref, w2k_ref, kc_ref),
                                                ((vc0_ref, vc1_ref), pev_ref, w1v_ref, w2v_ref, vc_ref)):
        x = jnp.concatenate(
            [jnp.concatenate([src[b, pl.ds(l, n_rows, stride=CMP_STRIDE), :] for l in range(CMP_STRIDE)], axis=1)
             for b in range(bb) for src in srcs], axis=0)
        za = _dot((x + pe_ref[:, :half_k]).astype(bf16), w1_ref[:half_k, :].astype(bf16))
        zb = _dot((x + pe_ref[:, half_k:]).astype(bf16), w1_ref[half_k:, :].astype(bf16))
        h = za + pltpu.roll(zb, n_stack - 1, axis=0)
        o = _dot(_gelu_tanh(h).astype(bf16), w2_ref[...].astype(bf16))
        o = jnp.where((rowi & (n_rows - 1)) < n_rows - 1, o, 0.0).astype(bf16)
        for b in range(bb):
            for g in range(N_KV_GROUPS):
                r0 = (b * N_KV_GROUPS + g) * n_rows
                o_ref[b, g] = o[r0:r0 + n_rows]


def _compress(r3, pek, w1k, w2k, pev, w1v, w2v, *, bb):
    B, T, width = r3.shape
    n_rows = T // CMP_STRIDE
    assert width == 2 * N_KV_GROUPS * HEAD_DIM and N_KV_GROUPS == 2 and B % bb == 0 and n_rows & (n_rows - 1) == 0
    const = lambda b: (0, 0)
    out = jax.ShapeDtypeStruct((B, N_KV_GROUPS, n_rows, HEAD_DIM), bf16)
    ospec = pl.BlockSpec((bb, N_KV_GROUPS, n_rows, HEAD_DIM), lambda b: (b, 0, 0, 0))
    sec = lambda s: pl.BlockSpec((bb, T, HEAD_DIM), lambda b: (b, 0, s))
    return pl.pallas_call(
        _compress_kernel,
        grid=(B // bb,),
        in_specs=[
            sec(0), sec(1), sec(2), sec(3),
            pl.BlockSpec(pek.shape, const), pl.BlockSpec(w1k.shape, const), pl.BlockSpec(w2k.shape, const),
            pl.BlockSpec(pev.shape, const), pl.BlockSpec(w1v.shape, const), pl.BlockSpec(w2v.shape, const),
        ],
        out_specs=[ospec, ospec],
        out_shape=[out, out],
        compiler_params=pltpu.CompilerParams(
            dimension_semantics=("arbitrary",), vmem_limit_bytes=VMEM_LIMIT),
        name="compress",
    )(r3, r3, r3, r3, pek, w1k, w2k, pev, w1v, w2v)


def _nsa_front(qs, t0, kc, vc, kw_ref, vw_ref, *, tq, T):
    R = HEADS_PER_GROUP
    n_cmp = kc.shape[0]
    n_sel = T // SEL_BLOCK

    wk = WINDOW + tq
    w0 = pl.multiple_of(jnp.maximum(t0 - WINDOW, 0), tq)
    sw = _dot_nt(qs, kw_ref[pl.ds(w0, wk), :])
    diff = (t0 - w0) + lax.broadcasted_iota(jnp.int32, (tq, wk), 0) - lax.broadcasted_iota(jnp.int32, (tq, wk), 1)
    wbias = jnp.where((diff & -WINDOW) == 0, 0.0, NEG)
    sw = jnp.concatenate([sw[r * tq:(r + 1) * tq] + wbias for r in range(R)], axis=0)
    ew = jnp.exp2(sw - jnp.max(sw, axis=1, keepdims=True))
    o_win = _dot(ew.astype(bf16), vw_ref[pl.ds(w0, wk), :]) / jnp.sum(ew, axis=1, keepdims=True)

    s = _dot_nt(qs, kc)
    valid = (CMP_STRIDE * lax.broadcasted_iota(jnp.int32, (tq, n_cmp), 1) + (CMP_BLOCK - 1)
             <= t0 + lax.broadcasted_iota(jnp.int32, (tq, n_cmp), 0))
    s = jnp.concatenate([jnp.where(valid, s[r * tq:(r + 1) * tq], NEG) for r in range(R)], axis=0)
    e = jnp.exp2(s - jnp.max(s, axis=1, keepdims=True))
    p = e / jnp.sum(e, axis=1, keepdims=True)
    p = jnp.concatenate([jnp.where(valid, p[r * tq:(r + 1) * tq], 0.0) for r in range(R)], axis=0)
    o_cmp = _dot(p.astype(bf16), vc)

    psum = p[0:tq]
    for r in range(1, R):
        psum = psum + p[r * tq:(r + 1) * tq]
    p_hi = psum.astype(bf16)
    p_lo = (psum - p_hi.astype(f32)).astype(bf16)
    sb = lax.broadcasted_iota(jnp.int32, (n_sel, n_cmp), 0) * SEL_BLOCK
    cb = lax.broadcasted_iota(jnp.int32, (n_sel, n_cmp), 1) * CMP_STRIDE
    ov_t = jnp.where((cb < sb + SEL_BLOCK) & (cb + CMP_BLOCK > sb), 1.0, 0.0).astype(bf16)
    imp_t = _dot_nt(ov_t, p_hi) + _dot_nt(ov_t, p_lo)

    jj = lax.broadcasted_iota(jnp.int32, (n_sel, tq), 0)
    cur = jnp.right_shift(t0 + lax.broadcasted_iota(jnp.int32, (n_sel, tq), 1), SEL_BLOCK.bit_length() - 1)
    forced = (jj == 0) | (jj == cur) | (jj == cur - 1)
    v = jnp.where(jj > cur, -jnp.inf, jnp.where(forced, jnp.inf, imp_t))
    sub = 8
    slabs = [v[k * sub:(k + 1) * sub] for k in range(n_sel // sub)]
    ranks = [jnp.zeros((sub, tq), f32) for _ in slabs]
    j_in = lax.broadcasted_iota(jnp.int32, (sub, tq), 0)
    for c in range(n_sel):
        v_c = v[c:c + 1, :]
        for k, v_k in enumerate(slabs):
            if k * sub > c:
                ahead = jnp.where(v_c >= v_k, 1.0, 0.0)
            elif (k + 1) * sub <= c:
                ahead = jnp.where(v_c > v_k, 1.0, 0.0)
            else:
                ahead = jnp.where(j_in > c - k * sub, jnp.where(v_c >= v_k, 1.0, 0.0),
                                  jnp.where(v_c > v_k, 1.0, 0.0))
            ranks[k] = ranks[k] + ahead
    rank = jnp.concatenate(ranks, axis=0)
    unsel_t = jnp.where(rank < float(min(N_SELECT, n_sel)), 0.0, 1.0)
    unsel_t = jnp.concatenate([unsel_t, jnp.zeros((LANES - n_sel, tq), f32)], axis=0)
    return o_win, o_cmp, unsel_t.T.astype(bf16)


def _nsa_kernel(q_ref, kc_ref, vc_ref, ks_ref, vs_ref, kw_ref, vw_ref, eneg_ref, gexp_ref, gt_ref, ca_ref, cb_ref,
                o_ref, ca_out, cb_out, s_sc, mx_sc, ls_sc, acc_sc, *, tq, n_sub, T):
    ca_out[...] = ca_ref[...].astype(bf16)
    cb_out[...] = cb_ref[...].astype(bf16)
    R = HEADS_PER_GROUP
    M = R * tq
    tk = n_sub * tq
    i = pl.program_id(2)
    t0 = i * tk
    kc = kc_ref[0, 0]
    vc = vc_ref[0, 0]

    fronts = []
    q_rows = []
    for sb in range(n_sub):
        q = q_ref[sb * tq:(sb + 1) * tq, :]
        qs = jnp.concatenate([q[:, r * HEAD_DIM:(r + 1) * HEAD_DIM] for r in range(R)], axis=0)
        o_win, o_cmp, unsel = _nsa_front(qs, t0 + sb * tq, kc, vc, kw_ref, vw_ref, tq=tq, T=T)
        fronts.append((o_win, o_cmp))
        q_rows.append(jnp.concatenate([qs, jnp.concatenate([unsel] * R, axis=0)], axis=1))
    q_aug = jnp.concatenate(q_rows, axis=0)

    n_chunk = tk // LANES
    mx_sc[...] = jnp.full(mx_sc.shape, NEG, f32)

    def score_step(kt, causal):
        k0 = pl.multiple_of(kt * tk, tk)
        k_aug = jnp.concatenate([ks_ref[pl.ds(k0, tk), :], eneg_ref[pl.ds(k0, tk), :]], axis=1)
        sc = _dot_nt(q_aug, k_aug)
        if causal:
            cbias = jnp.where(lax.broadcasted_iota(jnp.int32, (tk, tk), 0)
                              >= lax.broadcasted_iota(jnp.int32, (tk, tk), 1), 0.0, NEG)
            sc = jnp.concatenate([sc[sb * M + r * tq:sb * M + (r + 1) * tq] + cbias[sb * tq:(sb + 1) * tq]
                                  for sb in range(n_sub) for r in range(R)], axis=0)
        s_sc[kt] = sc
        part = sc[:, :LANES]
        for c in range(1, n_chunk):
            part = jnp.maximum(part, sc[:, c * LANES:(c + 1) * LANES])
        mx_sc[...] = jnp.maximum(mx_sc[...], part)

    def score_body(kt, carry):
        score_step(kt, False)
        return carry

    lax.fori_loop(0, i, score_body, 0)
    score_step(i, True)
    mx_sc[...] = jnp.broadcast_to(jnp.max(mx_sc[...], axis=1, keepdims=True), mx_sc.shape)
    ls_sc[...] = jnp.zeros(ls_sc.shape, f32)
    acc_sc[...] = jnp.zeros(acc_sc.shape, f32)

    def pv_body(kt, carry):
        k0 = pl.multiple_of(kt * tk, tk)
        sc = s_sc[kt]
        mb = mx_sc[...]
        ps = [jnp.exp2(sc[:, c * LANES:(c + 1) * LANES] - mb) for c in range(n_chunk)]
        ls_sc[...] += functools.reduce(lambda a, b: a + b, ps)
        acc_sc[...] += _dot(jnp.concatenate(ps, axis=1).astype(bf16), vs_ref[pl.ds(k0, tk), :])
        return carry

    lax.fori_loop(0, i + 1, pv_body, 0)
    o_sel = acc_sc[...] / jnp.sum(ls_sc[...], axis=1, keepdims=True)

    gt_all = jax.nn.sigmoid(gt_ref[...])
    gt_all = jnp.where(pl.program_id(1) == 0, gt_all, pltpu.roll(gt_all, LANES - R * N_BRANCH, axis=1))
    g_hi = gt_all.astype(bf16)
    rest = gt_all - g_hi.astype(f32)
    g_mid = rest.astype(bf16)
    g_lo = (rest - g_mid.astype(f32)).astype(bf16)
    g_exp = _dot(jnp.concatenate([g_hi, g_mid, g_lo], axis=1), gexp_ref[...])
    for sb, (o_win, o_cmp) in enumerate(fronts):
        for r in range(R):
            rows = slice(r * tq, (r + 1) * tq)
            srows = slice(sb * M + r * tq, sb * M + (r + 1) * tq)
            gates = [g_exp[sb * tq:(sb + 1) * tq, (r * N_BRANCH + br) * LANES:(r * N_BRANCH + br + 1) * LANES]
                     for br in range(N_BRANCH)]
            o = gates[0] * o_cmp[rows] + gates[1] * o_sel[srows] + gates[2] * o_win[rows]
            o_ref[sb * tq:(sb + 1) * tq, r * HEAD_DIM:(r + 1) * HEAD_DIM] = o.astype(bf16)


def _nsa(q, kcmp, vcmp, kv, eneg, gexp, gates, cast_a, cast_b, *, B, T, tq, n_sub):
    G, R = N_KV_GROUPS, HEADS_PER_GROUP
    tk = n_sub * tq
    nt = T // tk
    n_cmp = kcmp.shape[2]
    assert tq % LANES == 0 and tq & (tq - 1) == 0 and T % tk == 0 and T >= WINDOW + tq
    assert WINDOW % tq == 0 and WINDOW & (WINDOW - 1) == 0 and (R * tq) & (R * tq - 1) == 0
    rowblk = lambda b, g, i: (b * nt + i, g)
    cmp_spec = pl.BlockSpec((1, 1, n_cmp, HEAD_DIM), lambda b, g, i: (b, g, 0, 0))
    kvspec = lambda sec: pl.BlockSpec((T, HEAD_DIM), lambda b, g, i: (b, sec * G + g))
    n_steps = B * G * nt
    cast_spec = lambda w: pl.BlockSpec((w.shape[0] // n_steps, w.shape[1]), lambda b, g, i: ((b * G + g) * nt + i, 0))
    return pl.pallas_call(
        functools.partial(_nsa_kernel, tq=tq, n_sub=n_sub, T=T),
        grid=(B, G, nt),
        in_specs=[
            pl.BlockSpec((tk, R * HEAD_DIM), rowblk),
            cmp_spec, cmp_spec,
            kvspec(0), kvspec(1), kvspec(2), kvspec(3),
            pl.BlockSpec((T, LANES), lambda b, g, i: (0, 0)),
            pl.BlockSpec(gexp.shape, lambda b, g, i: (0, 0)),
            pl.BlockSpec((tk, LANES), lambda b, g, i: (b * nt + i, 0)),
            cast_spec(cast_a), cast_spec(cast_b),
        ],
        out_specs=[pl.BlockSpec((tk, R * HEAD_DIM), rowblk), cast_spec(cast_a), cast_spec(cast_b)],
        out_shape=[jax.ShapeDtypeStruct((B * T, G * R * HEAD_DIM), bf16), jax.ShapeDtypeStruct(cast_a.shape, bf16),
                   jax.ShapeDtypeStruct(cast_b.shape, bf16)],
        scratch_shapes=[
            pltpu.VMEM((T // tk, n_sub * R * tq, tk), f32),
            pltpu.VMEM((n_sub * R * tq, LANES), f32),
            pltpu.VMEM((n_sub * R * tq, LANES), f32),
            pltpu.VMEM((n_sub * R * tq, HEAD_DIM), f32),
        ],
        compiler_params=pltpu.CompilerParams(
            dimension_semantics=("arbitrary", "arbitrary", "arbitrary"), vmem_limit_bytes=VMEM_LIMIT),
        name="nsa",
    )(q, kcmp, vcmp, kv, kv, kv, kv, eneg, gexp, gates, cast_a, cast_b)


def _out_proj_kernel(x_ref, u_ref, uh_ref, y_ref, pw_ref, ps_ref, wo_ref, o_ref, *, tm, T):
    i = pl.program_id(0)
    tiles_per_seq = T // tm
    keep_halo = jnp.where((i % tiles_per_seq) == 0, 0.0, 1.0)
    u = u_ref[...]
    halo = uh_ref[...].astype(f32) * keep_halo
    tseq = (i % tiles_per_seq) * tm + lax.broadcasted_iota(jnp.int32, (tm, 1), 0)
    n_pool = u.shape[1]
    gd = n_pool // len(POOL_WINDOWS)
    acc = x_ref[...] + _dot(y_ref[...], wo_ref[n_pool:, :])
    for gi, w in enumerate(POOL_WINDOWS):
        assert w <= POOL_HALO and w & (w - 1) == 0
        cols = slice(gi * gd, (gi + 1) * gd)
        ug = u[:, cols].astype(f32)
        ws = jnp.concatenate([halo[:, cols], ug], axis=0)
        k = 1
        while k < w:
            ws = ws + pltpu.roll(ws, k, axis=0)
            k *= 2
        win_sum = ws[POOL_HALO:]
        count = jnp.minimum(tseq + 1, w).astype(f32)
        pooled = win_sum / count - ug
        yg = _dot(pooled.astype(bf16), pw_ref[gi].astype(bf16)) * ps_ref[:, cols]
        acc = acc + _dot(yg.astype(bf16), wo_ref[cols, :])
    o_ref[...] = acc


def _out_proj(x2, u, y_nsa, pool_w, pool_scale, w_out, *, T, tm):
    n_rows, d = x2.shape
    n_pool = u.shape[1]
    hb = tm // POOL_HALO
    row = lambda i: (i, 0)
    const2 = lambda i: (0, 0)
    return pl.pallas_call(
        functools.partial(_out_proj_kernel, tm=tm, T=T),
        grid=(n_rows // tm,),
        in_specs=[
            pl.BlockSpec((tm, d), row),
            pl.BlockSpec((tm, n_pool), row),
            pl.BlockSpec((POOL_HALO, n_pool), lambda i: (jnp.maximum(i * hb - 1, 0), 0)),
            pl.BlockSpec((tm, y_nsa.shape[1]), row),
            pl.BlockSpec(pool_w.shape, lambda i: (0, 0, 0)),
            pl.BlockSpec((1, n_pool), const2),
            pl.BlockSpec(w_out.shape, const2, pipeline_mode=pl.Buffered(1)),
        ],
        out_specs=pl.BlockSpec((tm, d), row),
        out_shape=jax.ShapeDtypeStruct((n_rows, d), f32),
        compiler_params=pltpu.CompilerParams(
            dimension_semantics=("arbitrary",), vmem_limit_bytes=VMEM_LIMIT),
        name="out_proj",
    )(x2, u, u, y_nsa, pool_w, pool_scale, w_out)


def _ffn_kernel(h_ref, g_ref, wg_ref, wu_ref, wd_ref, o_ref, n_sc):
    @pl.when(pl.program_id(1) == 0)
    def _():
        h = h_ref[...]
        n_sc[...] = _rms(h, g_ref[...]).astype(bf16)
        o_ref[...] = h

    n = n_sc[...]
    half = wg_ref.shape[1] // 2
    acc = None
    for c in range(2):
        cols = slice(c * half, (c + 1) * half)
        gate = _dot(n, wg_ref[:, cols])
        up = _dot(n, wu_ref[:, cols])
        a = gate * jax.nn.sigmoid(gate) * up
        d = _dot(a.astype(bf16), wd_ref[cols, :])
        acc = d if acc is None else acc + d
    o_ref[...] += acc


def _ffn(h, g, w_gate, w_up, w_down, *, tm, tf):
    n_rows, d = h.shape
    d_ff = w_gate.shape[1]
    return pl.pallas_call(
        _ffn_kernel,
        grid=(n_rows // tm, d_ff // tf),
        in_specs=[
            pl.BlockSpec((tm, d), lambda i, f: (i, 0)),
            pl.BlockSpec((1, d), lambda i, f: (0, 0)),
            pl.BlockSpec((d, tf), lambda i, f: (0, f)),
            pl.BlockSpec((d, tf), lambda i, f: (0, f)),
            pl.BlockSpec((tf, d), lambda i, f: (f, 0)),
        ],
        out_specs=pl.BlockSpec((tm, d), lambda i, f: (i, 0)),
        out_shape=jax.ShapeDtypeStruct((n_rows, d), f32),
        scratch_shapes=[pltpu.VMEM((tm, d), bf16)],
        compiler_params=pltpu.CompilerParams(
            dimension_semantics=("arbitrary", "arbitrary"), vmem_limit_bytes=VMEM_LIMIT),
        name="ffn",
    )(h, g, w_gate, w_up, w_down)


def _ple_kernel(h_ref, p_ref, g_ref, wg_ref, wp_ref, fg_ref, o_ref):
    h = h_ref[...]
    n = _rms(h, g_ref[...]).astype(bf16)
    gate = jax.nn.sigmoid(_dot(n, wg_ref[...]))
    h = h + _dot(p_ref[...].astype(bf16), wp_ref[...].astype(bf16)) * gate
    o_ref[...] = _rms(h, fg_ref[...])


def _ple(h, p2, g, w_gate, w_proj, fg, *, tm):
    n_rows, d = h.shape
    row = lambda i: (i, 0)
    const = lambda i: (0, 0)
    return pl.pallas_call(
        _ple_kernel,
        grid=(n_rows // tm,),
        in_specs=[
            pl.BlockSpec((tm, d), row),
            pl.BlockSpec((tm, p2.shape[1]), row),
            pl.BlockSpec((1, d), const),
            pl.BlockSpec(w_gate.shape, const, pipeline_mode=pl.Buffered(1)),
            pl.BlockSpec(w_proj.shape, const),
            pl.BlockSpec((1, d), const),
        ],
        out_specs=pl.BlockSpec((tm, d), row),
        out_shape=jax.ShapeDtypeStruct((n_rows, d), f32),
        compiler_params=pltpu.CompilerParams(
            dimension_semantics=("arbitrary",), vmem_limit_bytes=VMEM_LIMIT),
        name="ple",
    )(h, p2, g, w_gate, w_proj, fg)


def _rope_tables(T):
    pos = np.arange(T, dtype=np.float64)
    inv_freq = ROPE_THETA ** (-np.arange(0, ROPE_DIM, 2, dtype=np.float64) / ROPE_DIM)
    ang = pos[:, None] * inv_freq[None, :]
    cos, sin = np.cos(ang), np.sin(ang)
    rest = HEAD_DIM - ROPE_DIM
    cosf = np.concatenate([cos, cos, np.ones((T, rest))], axis=1)
    sinf = np.concatenate([-sin, sin, np.zeros((T, rest))], axis=1)
    return jnp.asarray(cosf, f32), jnp.asarray(sinf, f32)


def _block_bias_table(T):
    key_blk = np.arange(T)[:, None] // SEL_BLOCK
    return jnp.asarray(np.where(key_blk == np.arange(LANES)[None, :], NEG, 0.0).astype(bf16))


def _gate_expansion_table():
    n_col = HEADS_PER_GROUP * N_BRANCH
    e = np.zeros((3, LANES, n_col, LANES), np.float32)
    for c in range(n_col):
        e[:, c, c, :] = 1.0
    return jnp.asarray(e.reshape(3 * LANES, n_col * LANES).astype(bf16))


def _mixer_ffn(h2, in_norm_g, w_in, pool_w, pool_scale, cmp_k_pe, cmp_k_w1, cmp_k_w2,
               cmp_v_pe, cmp_v_w1, cmp_v_w2, w_out, ffn_norm_g, w_gate, w_up, w_down, w_ple_gate, *, B, T):
    n_pool = pool_scale.shape[0]
    n_heads = N_KV_GROUPS * HEADS_PER_GROUP
    nsa_w = n_heads * HEAD_DIM
    kv_w = N_KV_GROUPS * HEAD_DIM
    assert w_in.shape[1] == n_pool + nsa_w + 6 * kv_w + n_heads * N_BRANCH

    o0 = n_pool + nsa_w + 6 * kv_w
    w_p = w_in.T
    w_gt = jnp.pad(w_p[o0:].astype(bf16), ((0, LANES - n_heads * N_BRANCH), (0, 0)))

    cosf, sinf = _rope_tables(T)
    u, q, kcvc, kv, gates, w_out_b, w_gate_b, w_ple_gate_b = _in_proj(
        h2, in_norm_g[None, :], w_p, w_gt, cosf, sinf, w_out, w_gate, w_ple_gate, T=T, tm=ROW_TILE,
        n_pool=n_pool, n_q=nsa_w)

    kcmp, vcmp = _compress(
        kcvc.reshape(B, T, kcvc.shape[1]),
        cmp_k_pe.reshape(1, -1), cmp_k_w1, cmp_k_w2, cmp_v_pe.reshape(1, -1), cmp_v_w1, cmp_v_w2,
        bb=COMPRESS_BATCH if B % COMPRESS_BATCH == 0 else 1)

    eneg = _block_bias_table(T)
    y_nsa, w_up_b, w_down_b = _nsa(q, kcmp, vcmp, kv, eneg, _gate_expansion_table(), gates, w_up, w_down, B=B, T=T,
                                   tq=NSA_Q_TILE, n_sub=NSA_SUB_TILES)

    h2 = _out_proj(h2, u, y_nsa, pool_w, pool_scale[None, :], w_out_b, T=T, tm=ROW_TILE)
    return _ffn(h2, ffn_norm_g[None, :], w_gate_b, w_up_b, w_down_b, tm=FFN_ROW_TILE, tf=FFN_FF_TILE), w_ple_gate_b


def kernel(x, p, in_norm_g, w_in, pool_w, pool_scale, cmp_k_pe, cmp_k_w1, cmp_k_w2, cmp_v_pe, cmp_v_w1,
           cmp_v_w2, w_out, ffn_norm_g, w_gate, w_up, w_down, ple_norm_g, w_ple_gate, w_ple_proj, final_norm_g):
    B, T, d = x.shape
    depth = w_in.shape[0]
    assert depth == 1, "the final rmsnorm is fused into the last layer's per-layer-embedding kernel"
    h2 = x.reshape(B * T, d)
    i = 0
    h2, w_ple_gate_b = _mixer_ffn(
        h2, in_norm_g[i], w_in[i], pool_w[i], pool_scale[i], cmp_k_pe[i], cmp_k_w1[i], cmp_k_w2[i],
        cmp_v_pe[i], cmp_v_w1[i], cmp_v_w2[i], w_out[i], ffn_norm_g[i], w_gate[i], w_up[i], w_down[i],
        w_ple_gate[i], B=B, T=T)
    out = _ple(h2, p[i].reshape(B * T, -1), ple_norm_g[i][None, :], w_ple_gate_b,
               w_ple_proj[i], final_norm_g[None, :], tm=ROW_TILE)
    return out.reshape(B, T, d)
```

```python
import functools

import jax
import jax.numpy as jnp
import numpy as np
from jax import lax
from jax.experimental import pallas as pl
from jax.experimental.pallas import tpu as pltpu

EPS = 1e-6
NEG = -1e30
LANES = 128
POOL_WINDOWS = (2, 4, 8, 16)
POOL_HALO = 16
HEAD_DIM = 128
N_KV_GROUPS = 2
HEADS_PER_GROUP = 4
N_BRANCH = 3
CMP_BLOCK = 32
CMP_STRIDE = 16
SEL_BLOCK = 64
N_SELECT = 16
WINDOW = 512
ROPE_THETA = 500000.0
ROPE_DIM = HEAD_DIM // 4
ATTN_SCALE = HEAD_DIM ** -0.5
Q_SCALE = ATTN_SCALE * 1.4426950408889634
VMEM_LIMIT = 60 * 1024 * 1024
W_CHUNK = 128
W_SLOTS = 4
IN_SEC = 512
ROW_TILE = 512
FFN_ROW_TILE = 1024
FFN_FF_TILE = 512
NSA_Q_TILE = 256
NSA_SUB_TILES = 2
COMPRESS_BATCH = 2

f32 = jnp.float32
bf16 = jnp.bfloat16


def _dot(a, b):
    return jnp.dot(a, b, preferred_element_type=f32)


def _dot_nt(a, b):
    return lax.dot_general(a, b, (((1,), (1,)), ((), ())), preferred_element_type=f32)


def _rms(x, g):
    return x * lax.rsqrt(jnp.mean(x * x, axis=-1, keepdims=True) + EPS) * g


def _rope(h, cosf, sinf, lane):
    half = ROPE_DIM // 2
    partner = jnp.where(lane < half, pltpu.roll(h, LANES - half, axis=1), pltpu.roll(h, half, axis=1))
    return h * cosf + partner * sinf


def _w_chunk_copy(w_hbm, stage_sc, sem, c):
    slot = c % W_SLOTS
    return pltpu.make_async_copy(w_hbm.at[pl.ds(c * W_CHUNK, W_CHUNK), :], stage_sc.at[slot], sem.at[slot])


def _in_proj_kernel(x_ref, g_ref, w_hbm, wgt_ref, cos_ref, sin_ref, ca_ref, cb_ref, cc_ref,
                    u_ref, q_ref, kcvc_ref, kv_ref, gt_ref, ca_out, cb_out, cc_out,
                    w_ref, stage_sc, sem):
    @pl.when(pl.program_id(0) == 0)
    def _():
        n_chunk = w_ref.shape[0] // W_CHUNK
        for c in range(min(W_SLOTS - 1, n_chunk)):
            _w_chunk_copy(w_hbm, stage_sc, sem, c).start()
        for c in range(n_chunk):
            if c + W_SLOTS - 1 < n_chunk:
                _w_chunk_copy(w_hbm, stage_sc, sem, c + W_SLOTS - 1).start()
            _w_chunk_copy(w_hbm, stage_sc, sem, c).wait()
            w_ref[c * W_CHUNK:(c + 1) * W_CHUNK, :] = stage_sc[c % W_SLOTS].astype(bf16)

    def section(s):
        return w_ref[s * IN_SEC:(s + 1) * IN_SEC, :]

    ca_out[...] = ca_ref[...].astype(bf16)
    cb_out[...] = cb_ref[...].astype(bf16)
    cc_out[...] = cc_ref[...].astype(bf16)
    a = _rms(x_ref[...], g_ref[...]).astype(bf16)
    cosf = cos_ref[...]
    sinf = sin_ref[...]
    lane = lax.broadcasted_iota(jnp.int32, cosf.shape, 1)
    heads = IN_SEC // HEAD_DIM
    s = 0
    for c in range(0, u_ref.shape[1], IN_SEC):
        u_ref[:, c:c + IN_SEC] = _dot_nt(a, section(s)).astype(bf16)
        s += 1
    for c in range(0, q_ref.shape[1], IN_SEC):
        acc = _dot_nt(a, section(s))
        s += 1
        for j in range(heads):
            h = _rope(acc[:, j * HEAD_DIM:(j + 1) * HEAD_DIM], cosf, sinf, lane) * Q_SCALE
            q_ref[:, c + j * HEAD_DIM:c + (j + 1) * HEAD_DIM] = h.astype(bf16)
    for dst, base, dt in ((kcvc_ref, 0, f32), (kv_ref, 0, bf16), (kv_ref, IN_SEC, bf16)):
        acc = _dot_nt(a, section(s))
        s += 1
        for j in range(heads):
            h = acc[:, j * HEAD_DIM:(j + 1) * HEAD_DIM]
            if j < N_KV_GROUPS:
                h = _rope(h, cosf, sinf, lane)
            dst[:, base + j * HEAD_DIM:base + (j + 1) * HEAD_DIM] = h.astype(dt)
    gt_ref[...] = _dot_nt(a, wgt_ref[...])


def _in_proj(x2, g, w_p, w_gt, cosf, sinf, cast_a, cast_b, cast_c, *, T, tm, n_pool, n_q):
    n_rows, d = x2.shape
    assert n_pool % IN_SEC == 0 and n_q % IN_SEC == 0 and 2 * N_KV_GROUPS * HEAD_DIM == IN_SEC
    n_w = (w_p.shape[0] // W_CHUNK) * W_CHUNK
    assert w_p.dtype == f32 and n_w + w_gt.shape[0] >= w_p.shape[0] and n_w == n_pool + n_q + 3 * IN_SEC
    tiles_per_seq = T // tm
    n_steps = n_rows // tm
    row = lambda i: (i, 0)
    cast_spec = lambda w: pl.BlockSpec((w.shape[0] // n_steps, w.shape[1]), row)
    const = lambda i: (0, 0)
    tab = lambda i: (i % tiles_per_seq, 0)
    return pl.pallas_call(
        _in_proj_kernel,
        grid=(n_rows // tm,),
        in_specs=[
            pl.BlockSpec((tm, d), row),
            pl.BlockSpec((1, d), const),
            pl.BlockSpec(memory_space=pl.ANY),
            pl.BlockSpec((LANES, d), const),
            pl.BlockSpec((tm, LANES), tab),
            pl.BlockSpec((tm, LANES), tab),
            cast_spec(cast_a), cast_spec(cast_b), cast_spec(cast_c),
        ],
        out_specs=[
            pl.BlockSpec((tm, n_pool), row),
            pl.BlockSpec((tm, n_q), row),
            pl.BlockSpec((tm, IN_SEC), row),
            pl.BlockSpec((tm, 2 * IN_SEC), row),
            pl.BlockSpec((tm, LANES), row),
            cast_spec(cast_a), cast_spec(cast_b), cast_spec(cast_c),
        ],
        out_shape=[
            jax.ShapeDtypeStruct((n_rows, n_pool), bf16),
            jax.ShapeDtypeStruct((n_rows, n_q), bf16),
            jax.ShapeDtypeStruct((n_rows, IN_SEC), f32),
            jax.ShapeDtypeStruct((n_rows, 2 * IN_SEC), bf16),
            jax.ShapeDtypeStruct((n_rows, LANES), f32),
            jax.ShapeDtypeStruct(cast_a.shape, bf16),
            jax.ShapeDtypeStruct(cast_b.shape, bf16),
            jax.ShapeDtypeStruct(cast_c.shape, bf16),
        ],
        scratch_shapes=[
            pltpu.VMEM((n_w, d), bf16),
            pltpu.VMEM((W_SLOTS, W_CHUNK, d), f32),
            pltpu.SemaphoreType.DMA((W_SLOTS,)),
        ],
        compiler_params=pltpu.CompilerParams(
            dimension_semantics=("arbitrary",), vmem_limit_bytes=VMEM_LIMIT),
        name="in_proj",
    )(x2, g, w_p, w_gt, cosf, sinf, cast_a, cast_b, cast_c)


def _gelu_tanh(x):
    return 0.5 * x * (1.0 + jnp.tanh(0.7978845608028654 * (x + 0.044715 * (x * x * x))))


def _compress_kernel(kc0_ref, kc1_ref, vc0_ref, vc1_ref, pek_ref, w1k_ref, w2k_ref, pev_ref, w1v_ref, w2v_ref,
                     kc_ref, vc_ref):
    bb = kc0_ref.shape[0]
    n_rows = kc0_ref.shape[1] // CMP_STRIDE
    half_k = (CMP_BLOCK // 2) * HEAD_DIM
    n_stack = bb * N_KV_GROUPS * n_rows
    rowi = lax.broadcasted_iota(jnp.int32, (n_stack, HEAD_DIM), 0)
    for srcs, pe_ref, w1_ref, w2_ref, o_ref in (((kc0_ref, kc1_ref), pek_ref, w1k_ref, w2k_ref, kc_ref),
                                                ((vc0_ref, vc1_ref), pev_ref, w1v_ref, w2v_ref, vc_ref)):
        x = jnp.concatenate(
            [jnp.concatenate([src[b, pl.ds(l, n_rows, stride=CMP_STRIDE), :] for l in range(CMP_STRIDE)], axis=1)
             for b in range(bb) for src in srcs], axis=0)
        za = _dot((x + pe_ref[:, :half_k]).astype(bf16), w1_ref[:half_k, :].astype(bf16))
        zb = _dot((x + pe_ref[:, half_k:]).astype(bf16), w1_ref[half_k:, :].astype(bf16))
        h = za + pltpu.roll(zb, n_stack - 1, axis=0)
        o = _dot(_gelu_tanh(h).astype(bf16), w2_ref[...].astype(bf16))
        o = jnp.where((rowi & (n_rows - 1)) < n_rows - 1, o, 0.0).astype(bf16)
        for b in range(bb):
            for g in range(N_KV_GROUPS):
                r0 = (b * N_KV_GROUPS + g) * n_rows
                o_ref[b, g] = o[r0:r0 + n_rows]


def _compress(r3, pek, w1k, w2k, pev, w1v, w2v, *, bb):
    B, T, width = r3.shape
    n_rows = T // CMP_STRIDE
    assert width == 2 * N_KV_GROUPS * HEAD_DIM and N_KV_GROUPS == 2 and B % bb == 0 and n_rows & (n_rows - 1) == 0
    const = lambda b: (0, 0)
    out = jax.ShapeDtypeStruct((B, N_KV_GROUPS, n_rows, HEAD_DIM), bf16)
    ospec = pl.BlockSpec((bb, N_KV_GROUPS, n_rows, HEAD_DIM), lambda b: (b, 0, 0, 0))
    sec = lambda s: pl.BlockSpec((bb, T, HEAD_DIM), lambda b: (b, 0, s))
    return pl.pallas_call(
        _compress_kernel,
        grid=(B // bb,),
        in_specs=[
            sec(0), sec(1), sec(2), sec(3),
            pl.BlockSpec(pek.shape, const), pl.BlockSpec(w1k.shape, const), pl.BlockSpec(w2k.shape, const),
            pl.BlockSpec(pev.shape, const), pl.BlockSpec(w1v.shape, const), pl.BlockSpec(w2v.shape, const),
        ],
        out_specs=[ospec, ospec],
        out_shape=[out, out],
        compiler_params=pltpu.CompilerParams(
            dimension_semantics=("arbitrary",), vmem_limit_bytes=VMEM_LIMIT),
        name="compress",
    )(r3, r3, r3, r3, pek, w1k, w2k, pev, w1v, w2v)


def _nsa_front(qs, t0, kc, vc, kw_ref, vw_ref, *, tq, T):
    R = HEADS_PER_GROUP
    n_cmp = kc.shape[0]
    n_sel = T // SEL_BLOCK

    wk = WINDOW + tq
    w0 = pl.multiple_of(jnp.maximum(t0 - WINDOW, 0), tq)
    sw = _dot_nt(qs, kw_ref[pl.ds(w0, wk), :])
    diff = (t0 - w0) + lax.broadcasted_iota(jnp.int32, (tq, wk), 0) - lax.broadcasted_iota(jnp.int32, (tq, wk), 1)
    wbias = jnp.where((diff & -WINDOW) == 0, 0.0, NEG)
    sw = jnp.concatenate([sw[r * tq:(r + 1) * tq] + wbias for r in range(R)], axis=0)
    ew = jnp.exp2(sw - jnp.max(sw, axis=1, keepdims=True))
    o_win = _dot(ew.astype(bf16), vw_ref[pl.ds(w0, wk), :]) / jnp.sum(ew, axis=1, keepdims=True)

    s = _dot_nt(qs, kc)
    valid = (CMP_STRIDE * lax.broadcasted_iota(jnp.int32, (tq, n_cmp), 1) + (CMP_BLOCK - 1)
             <= t0 + lax.broadcasted_iota(jnp.int32, (tq, n_cmp), 0))
    s = jnp.concatenate([jnp.where(valid, s[r * tq:(r + 1) * tq], NEG) for r in range(R)], axis=0)
    e = jnp.exp2(s - jnp.max(s, axis=1, keepdims=True))
    p = e / jnp.sum(e, axis=1, keepdims=True)
    p = jnp.concatenate([jnp.where(valid, p[r * tq:(r + 1) * tq], 0.0) for r in range(R)], axis=0)
    o_cmp = _dot(p.astype(bf16), vc)

    psum = p[0:tq]
    for r in range(1, R):
        psum = psum + p[r * tq:(r + 1) * tq]
    p_hi = psum.astype(bf16)
    p_lo = (psum - p_hi.astype(f32)).astype(bf16)
    sb = lax.broadcasted_iota(jnp.int32, (n_sel, n_cmp), 0) * SEL_BLOCK
    cb = lax.broadcasted_iota(jnp.int32, (n_sel, n_cmp), 1) * CMP_STRIDE
    ov_t = jnp.where((cb < sb + SEL_BLOCK) & (cb + CMP_BLOCK > sb), 1.0, 0.0).astype(bf16)
    imp_t = _dot_nt(ov_t, p_hi) + _dot_nt(ov_t, p_lo)

    jj = lax.broadcasted_iota(jnp.int32, (n_sel, tq), 0)
    cur = jnp.right_shift(t0 + lax.broadcasted_iota(jnp.int32, (n_sel, tq), 1), SEL_BLOCK.bit_length() - 1)
    forced = (jj == 0) | (jj == cur) | (jj == cur - 1)
    v = jnp.where(jj > cur, -jnp.inf, jnp.where(forced, jnp.inf, imp_t))
    sub = 8
    slabs = [v[k * sub:(k + 1) * sub] for k in range(n_sel // sub)]
    ranks = [jnp.zeros((sub, tq), f32) for _ in slabs]
    j_in = lax.broadcasted_iota(jnp.int32, (sub, tq), 0)
    for c in range(n_sel):
        v_c = v[c:c + 1, :]
        for k, v_k in enumerate(slabs):
            if k * sub > c:
                ahead = jnp.where(v_c >= v_k, 1.0, 0.0)
            elif (k + 1) * sub <= c:
                ahead = jnp.where(v_c > v_k, 1.0, 0.0)
            else:
                ahead = jnp.where(j_in > c - k * sub, jnp.where(v_c >= v_k, 1.0, 0.0),
                                  jnp.where(v_c > v_k, 1.0, 0.0))
            ranks[k] = ranks[k] + ahead
    rank = jnp.concatenate(ranks, axis=0)
    unsel_t = jnp.where(rank < float(min(N_SELECT, n_sel)), 0.0, 1.0)
    unsel_t = jnp.concatenate([unsel_t, jnp.zeros((LANES - n_sel, tq), f32)], axis=0)
    return o_win, o_cmp, unsel_t.T.astype(bf16)


def _nsa_kernel(q_ref, kc_ref, vc_ref, ks_ref, vs_ref, kw_ref, vw_ref, eneg_ref, gexp_ref, gt_ref, ca_ref, cb_ref,
                o_ref, ca_out, cb_out, s_sc, mx_sc, ls_sc, acc_sc, *, tq, n_sub, T):
    ca_out[...] = ca_ref[...].astype(bf16)
    cb_out[...] = cb_ref[...].astype(bf16)
    R = HEADS_PER_GROUP
    M = R * tq
    tk = n_sub * tq
    i = pl.program_id(2)
    t0 = i * tk
    kc = kc_ref[0, 0]
    vc = vc_ref[0, 0]

    fronts = []
    q_rows = []
    for sb in range(n_sub):
        q = q_ref[sb * tq:(sb + 1) * tq, :]
        qs = jnp.concatenate([q[:, r * HEAD_DIM:(r + 1) * HEAD_DIM] for r in range(R)], axis=0)
        o_win, o_cmp, unsel = _nsa_front(qs, t0 + sb * tq, kc, vc, kw_ref, vw_ref, tq=tq, T=T)
        fronts.append((o_win, o_cmp))
        q_rows.append(jnp.concatenate([qs, jnp.concatenate([unsel] * R, axis=0)], axis=1))
    q_aug = jnp.concatenate(q_rows, axis=0)

    n_chunk = tk // LANES
    mx_sc[...] = jnp.full(mx_sc.shape, NEG, f32)

    def lane_max(sc):
        part = sc[:, :LANES]
        for c in range(1, sc.shape[1] // LANES):
            part = jnp.maximum(part, sc[:, c * LANES:(c + 1) * LANES])
        return part

    def score_body(kt, carry):
        k0 = pl.multiple_of(kt * tk, tk)
        k_aug = jnp.concatenate([ks_ref[pl.ds(k0, tk), :], eneg_ref[pl.ds(k0, tk), :]], axis=1)
        sc = _dot_nt(q_aug, k_aug)
        s_sc[kt] = sc
        mx_sc[...] = jnp.maximum(mx_sc[...], lane_max(sc))
        return carry

    lax.fori_loop(0, i, score_body, 0)

    kd = pl.multiple_of(i * tk, tk)
    tri = jnp.where(lax.broadcasted_iota(jnp.int32, (tq, tq), 0) >= lax.broadcasted_iota(jnp.int32, (tq, tq), 1),
                    0.0, NEG)
    for sb in range(n_sub):
        width = (sb + 1) * tq
        rows = slice(sb * M, (sb + 1) * M)
        k_aug = jnp.concatenate([ks_ref[pl.ds(kd, width), :], eneg_ref[pl.ds(kd, width), :]], axis=1)
        sc = _dot_nt(q_aug[rows], k_aug)
        last = jnp.concatenate([sc[r * tq:(r + 1) * tq, sb * tq:] + tri for r in range(R)], axis=0)
        sc = last if sb == 0 else jnp.concatenate([sc[:, :sb * tq], last], axis=1)
        s_sc[i, rows, :width] = sc
        mx_sc[rows, :] = jnp.maximum(mx_sc[rows, :], lane_max(sc))
    mx_sc[...] = jnp.broadcast_to(jnp.max(mx_sc[...], axis=1, keepdims=True), mx_sc.shape)
    ls_sc[...] = jnp.zeros(ls_sc.shape, f32)
    acc_sc[...] = jnp.zeros(acc_sc.shape, f32)

    def pv_body(kt, carry):
        k0 = pl.multiple_of(kt * tk, tk)
        sc = s_sc[kt]
        mb = mx_sc[...]
        ps = [jnp.exp2(sc[:, c * LANES:(c + 1) * LANES] - mb) for c in range(n_chunk)]
        ls_sc[...] += functools.reduce(lambda a, b: a + b, ps)
        acc_sc[...] += _dot(jnp.concatenate(ps, axis=1).astype(bf16), vs_ref[pl.ds(k0, tk), :])
        return carry

    lax.fori_loop(0, i, pv_body, 0)
    for sb in range(n_sub):
        width = (sb + 1) * tq
        rows = slice(sb * M, (sb + 1) * M)
        sc = s_sc[i, rows, :width]
        mb = mx_sc[rows, :]
        ps = [jnp.exp2(sc[:, c * LANES:(c + 1) * LANES] - mb) for c in range(width // LANES)]
        ls_sc[rows, :] += functools.reduce(lambda a, b: a + b, ps)
        acc_sc[rows, :] += _dot(jnp.concatenate(ps, axis=1).astype(bf16), vs_ref[pl.ds(kd, width), :])
    o_sel = acc_sc[...] / jnp.sum(ls_sc[...], axis=1, keepdims=True)

    gt_all = jax.nn.sigmoid(gt_ref[...])
    gt_all = jnp.where(pl.program_id(1) == 0, gt_all, pltpu.roll(gt_all, LANES - R * N_BRANCH, axis=1))
    g_hi = gt_all.astype(bf16)
    rest = gt_all - g_hi.astype(f32)
    g_mid = rest.astype(bf16)
    g_lo = (rest - g_mid.astype(f32)).astype(bf16)
    g_exp = _dot(jnp.concatenate([g_hi, g_mid, g_lo], axis=1), gexp_ref[...])
    for sb, (o_win, o_cmp) in enumerate(fronts):
        for r in range(R):
            rows = slice(r * tq, (r + 1) * tq)
            srows = slice(sb * M + r * tq, sb * M + (r + 1) * tq)
            gates = [g_exp[sb * tq:(sb + 1) * tq, (r * N_BRANCH + br) * LANES:(r * N_BRANCH + br + 1) * LANES]
                     for br in range(N_BRANCH)]
            o = gates[0] * o_cmp[rows] + gates[1] * o_sel[srows] + gates[2] * o_win[rows]
            o_ref[sb * tq:(sb + 1) * tq, r * HEAD_DIM:(r + 1) * HEAD_DIM] = o.astype(bf16)


def _nsa(q, kcmp, vcmp, kv, eneg, gexp, gates, cast_a, cast_b, *, B, T, tq, n_sub):
    G, R = N_KV_GROUPS, HEADS_PER_GROUP
    tk = n_sub * tq
    nt = T // tk
    n_cmp = kcmp.shape[2]
    assert tq % LANES == 0 and tq & (tq - 1) == 0 and T % tk == 0 and T >= WINDOW + tq
    assert WINDOW % tq == 0 and WINDOW & (WINDOW - 1) == 0 and (R * tq) & (R * tq - 1) == 0
    rowblk = lambda b, g, i: (b * nt + i, g)
    cmp_spec = pl.BlockSpec((1, 1, n_cmp, HEAD_DIM), lambda b, g, i: (b, g, 0, 0))
    kvspec = lambda sec: pl.BlockSpec((T, HEAD_DIM), lambda b, g, i: (b, sec * G + g))
    n_steps = B * G * nt
    cast_spec = lambda w: pl.BlockSpec((w.shape[0] // n_steps, w.shape[1]), lambda b, g, i: ((b * G + g) * nt + i, 0))
    return pl.pallas_call(
        functools.partial(_nsa_kernel, tq=tq, n_sub=n_sub, T=T),
        grid=(B, G, nt),
        in_specs=[
            pl.BlockSpec((tk, R * HEAD_DIM), rowblk),
            cmp_spec, cmp_spec,
            kvspec(0), kvspec(1), kvspec(2), kvspec(3),
            pl.BlockSpec((T, LANES), lambda b, g, i: (0, 0)),
            pl.BlockSpec(gexp.shape, lambda b, g, i: (0, 0)),
            pl.BlockSpec((tk, LANES), lambda b, g, i: (b * nt + i, 0)),
            cast_spec(cast_a), cast_spec(cast_b),
        ],
        out_specs=[pl.BlockSpec((tk, R * HEAD_DIM), rowblk), cast_spec(cast_a), cast_spec(cast_b)],
        out_shape=[jax.ShapeDtypeStruct((B * T, G * R * HEAD_DIM), bf16), jax.ShapeDtypeStruct(cast_a.shape, bf16),
                   jax.ShapeDtypeStruct(cast_b.shape, bf16)],
        scratch_shapes=[
            pltpu.VMEM((T // tk, n_sub * R * tq, tk), f32),
            pltpu.VMEM((n_sub * R * tq, LANES), f32),
            pltpu.VMEM((n_sub * R * tq, LANES), f32),
            pltpu.VMEM((n_sub * R * tq, HEAD_DIM), f32),
        ],
        compiler_params=pltpu.CompilerParams(
            dimension_semantics=("arbitrary", "arbitrary", "arbitrary"), vmem_limit_bytes=VMEM_LIMIT),
        name="nsa",
    )(q, kcmp, vcmp, kv, kv, kv, kv, eneg, gexp, gates, cast_a, cast_b)


def _out_proj_kernel(x_ref, u_ref, uh_ref, y_ref, pw_ref, ps_ref, wo_ref, o_ref, *, tm, T):
    i = pl.program_id(0)
    tiles_per_seq = T // tm
    keep_halo = jnp.where((i % tiles_per_seq) == 0, 0.0, 1.0)
    u = u_ref[...]
    halo = uh_ref[...].astype(f32) * keep_halo
    tseq = (i % tiles_per_seq) * tm + lax.broadcasted_iota(jnp.int32, (tm, 1), 0)
    n_pool = u.shape[1]
    gd = n_pool // len(POOL_WINDOWS)
    acc = x_ref[...] + _dot(y_ref[...], wo_ref[n_pool:, :])
    for gi, w in enumerate(POOL_WINDOWS):
        assert w <= POOL_HALO and w & (w - 1) == 0
        cols = slice(gi * gd, (gi + 1) * gd)
        ug = u[:, cols].astype(f32)
        ws = jnp.concatenate([halo[:, cols], ug], axis=0)
        k = 1
        while k < w:
            ws = ws + pltpu.roll(ws, k, axis=0)
            k *= 2
        win_sum = ws[POOL_HALO:]
        count = jnp.minimum(tseq + 1, w).astype(f32)
        pooled = win_sum / count - ug
        yg = _dot(pooled.astype(bf16), pw_ref[gi].astype(bf16)) * ps_ref[:, cols]
        acc = acc + _dot(yg.astype(bf16), wo_ref[cols, :])
    o_ref[...] = acc


def _out_proj(x2, u, y_nsa, pool_w, pool_scale, w_out, *, T, tm):
    n_rows, d = x2.shape
    n_pool = u.shape[1]
    hb = tm // POOL_HALO
    row = lambda i: (i, 0)
    const2 = lambda i: (0, 0)
    return pl.pallas_call(
        functools.partial(_out_proj_kernel, tm=tm, T=T),
        grid=(n_rows // tm,),
        in_specs=[
            pl.BlockSpec((tm, d), row),
            pl.BlockSpec((tm, n_pool), row),
            pl.BlockSpec((POOL_HALO, n_pool), lambda i: (jnp.maximum(i * hb - 1, 0), 0)),
            pl.BlockSpec((tm, y_nsa.shape[1]), row),
            pl.BlockSpec(pool_w.shape, lambda i: (0, 0, 0)),
            pl.BlockSpec((1, n_pool), const2),
            pl.BlockSpec(w_out.shape, const2, pipeline_mode=pl.Buffered(1)),
        ],
        out_specs=pl.BlockSpec((tm, d), row),
        out_shape=jax.ShapeDtypeStruct((n_rows, d), f32),
        compiler_params=pltpu.CompilerParams(
            dimension_semantics=("arbitrary",), vmem_limit_bytes=VMEM_LIMIT),
        name="out_proj",
    )(x2, u, u, y_nsa, pool_w, pool_scale, w_out)


def _ffn_kernel(h_ref, g_ref, wg_ref, wu_ref, wd_ref, o_ref, n_sc):
    @pl.when(pl.program_id(1) == 0)
    def _():
        h = h_ref[...]
        n_sc[...] = _rms(h, g_ref[...]).astype(bf16)
        o_ref[...] = h

    n = n_sc[...]
    half = wg_ref.shape[1] // 2
    acc = None
    for c in range(2):
        cols = slice(c * half, (c + 1) * half)
        gate = _dot(n, wg_ref[:, cols])
        up = _dot(n, wu_ref[:, cols])
        a = gate * jax.nn.sigmoid(gate) * up
        d = _dot(a.astype(bf16), wd_ref[cols, :])
        acc = d if acc is None else acc + d
    o_ref[...] += acc


def _ffn(h, g, w_gate, w_up, w_down, *, tm, tf):
    n_rows, d = h.shape
    d_ff = w_gate.shape[1]
    return pl.pallas_call(
        _ffn_kernel,
        grid=(n_rows // tm, d_ff // tf),
        in_specs=[
            pl.BlockSpec((tm, d), lambda i, f: (i, 0)),
            pl.BlockSpec((1, d), lambda i, f: (0, 0)),
            pl.BlockSpec((d, tf), lambda i, f: (0, f)),
            pl.BlockSpec((d, tf), lambda i, f: (0, f)),
            pl.BlockSpec((tf, d), lambda i, f: (f, 0)),
        ],
        out_specs=pl.BlockSpec((tm, d), lambda i, f: (i, 0)),
        out_shape=jax.ShapeDtypeStruct((n_rows, d), f32),
        scratch_shapes=[pltpu.VMEM((tm, d), bf16)],
        compiler_params=pltpu.CompilerParams(
            dimension_semantics=("arbitrary", "arbitrary"), vmem_limit_bytes=VMEM_LIMIT),
        name="ffn",
    )(h, g, w_gate, w_up, w_down)


def _ple_kernel(h_ref, p_ref, g_ref, wg_ref, wp_ref, fg_ref, o_ref):
    h = h_ref[...]
    n = _rms(h, g_ref[...]).astype(bf16)
    gate = jax.nn.sigmoid(_dot(n, wg_ref[...]))
    h = h + _dot(p_ref[...].astype(bf16), wp_ref[...].astype(bf16)) * gate
    o_ref[...] = _rms(h, fg_ref[...])


def _ple(h, p2, g, w_gate, w_proj, fg, *, tm):
    n_rows, d = h.shape
    row = lambda i: (i, 0)
    const = lambda i: (0, 0)
    return pl.pallas_call(
        _ple_kernel,
        grid=(n_rows // tm,),
        in_specs=[
            pl.BlockSpec((tm, d), row),
            pl.BlockSpec((tm, p2.shape[1]), row),
            pl.BlockSpec((1, d), const),
            pl.BlockSpec(w_gate.shape, const, pipeline_mode=pl.Buffered(1)),
            pl.BlockSpec(w_proj.shape, const),
            pl.BlockSpec((1, d), const),
        ],
        out_specs=pl.BlockSpec((tm, d), row),
        out_shape=jax.ShapeDtypeStruct((n_rows, d), f32),
        compiler_params=pltpu.CompilerParams(
            dimension_semantics=("arbitrary",), vmem_limit_bytes=VMEM_LIMIT),
        name="ple",
    )(h, p2, g, w_gate, w_proj, fg)


def _rope_tables(T):
    pos = np.arange(T, dtype=np.float64)
    inv_freq = ROPE_THETA ** (-np.arange(0, ROPE_DIM, 2, dtype=np.float64) / ROPE_DIM)
    ang = pos[:, None] * inv_freq[None, :]
    cos, sin = np.cos(ang), np.sin(ang)
    rest = HEAD_DIM - ROPE_DIM
    cosf = np.concatenate([cos, cos, np.ones((T, rest))], axis=1)
    sinf = np.concatenate([-sin, sin, np.zeros((T, rest))], axis=1)
    return jnp.asarray(cosf, f32), jnp.asarray(sinf, f32)


def _block_bias_table(T):
    key_blk = np.arange(T)[:, None] // SEL_BLOCK
    return jnp.asarray(np.where(key_blk == np.arange(LANES)[None, :], NEG, 0.0).astype(bf16))


def _gate_expansion_table():
    n_col = HEADS_PER_GROUP * N_BRANCH
    e = np.zeros((3, LANES, n_col, LANES), np.float32)
    for c in range(n_col):
        e[:, c, c, :] = 1.0
    return jnp.asarray(e.reshape(3 * LANES, n_col * LANES).astype(bf16))


def _mixer_ffn(h2, in_norm_g, w_in, pool_w, pool_scale, cmp_k_pe, cmp_k_w1, cmp_k_w2,
               cmp_v_pe, cmp_v_w1, cmp_v_w2, w_out, ffn_norm_g, w_gate, w_up, w_down, w_ple_gate, *, B, T):
    n_pool = pool_scale.shape[0]
    n_heads = N_KV_GROUPS * HEADS_PER_GROUP
    nsa_w = n_heads * HEAD_DIM
    kv_w = N_KV_GROUPS * HEAD_DIM
    assert w_in.shape[1] == n_pool + nsa_w + 6 * kv_w + n_heads * N_BRANCH

    o0 = n_pool + nsa_w + 6 * kv_w
    w_p = w_in.T
    w_gt = jnp.pad(w_p[o0:].astype(bf16), ((0, LANES - n_heads * N_BRANCH), (0, 0)))

    cosf, sinf = _rope_tables(T)
    u, q, kcvc, kv, gates, w_out_b, w_gate_b, w_ple_gate_b = _in_proj(
        h2, in_norm_g[None, :], w_p, w_gt, cosf, sinf, w_out, w_gate, w_ple_gate, T=T, tm=ROW_TILE,
        n_pool=n_pool, n_q=nsa_w)

    kcmp, vcmp = _compress(
        kcvc.reshape(B, T, kcvc.shape[1]),
        cmp_k_pe.reshape(1, -1), cmp_k_w1, cmp_k_w2, cmp_v_pe.reshape(1, -1), cmp_v_w1, cmp_v_w2,
        bb=COMPRESS_BATCH if B % COMPRESS_BATCH == 0 else 1)

    eneg = _block_bias_table(T)
    y_nsa, w_up_b, w_down_b = _nsa(q, kcmp, vcmp, kv, eneg, _gate_expansion_table(), gates, w_up, w_down, B=B, T=T,
                                   tq=NSA_Q_TILE, n_sub=NSA_SUB_TILES)

    h2 = _out_proj(h2, u, y_nsa, pool_w, pool_scale[None, :], w_out_b, T=T, tm=ROW_TILE)
    return _ffn(h2, ffn_norm_g[None, :], w_gate_b, w_up_b, w_down_b, tm=FFN_ROW_TILE, tf=FFN_FF_TILE), w_ple_gate_b


def kernel(x, p, in_norm_g, w_in, pool_w, pool_scale, cmp_k_pe, cmp_k_w1, cmp_k_w2, cmp_v_pe, cmp_v_w1,
           cmp_v_w2, w_out, ffn_norm_g, w_gate, w_up, w_down, ple_norm_g, w_ple_gate, w_ple_proj, final_norm_g):
    B, T, d = x.shape
    depth = w_in.shape[0]
    assert depth == 1, "the final rmsnorm is fused into the last layer's per-layer-embedding kernel"
    h2 = x.reshape(B * T, d)
    i = 0
    h2, w_ple_gate_b = _mixer_ffn(
        h2, in_norm_g[i], w_in[i], pool_w[i], pool_scale[i], cmp_k_pe[i], cmp_k_w1[i], cmp_k_w2[i],
        cmp_v_pe[i], cmp_v_w1[i], cmp_v_w2[i], w_out[i], ffn_norm_g[i], w_gate[i], w_up[i], w_down[i],
        w_ple_gate[i], B=B, T=T)
    out = _ple(h2, p[i].reshape(B * T, -1), ple_norm_g[i][None, :], w_ple_gate_b,
               w_ple_proj[i], final_norm_g[None, :], tm=ROW_TILE)
    return out.reshape(B, T, d)
```

```python
import functools

import jax
import jax.numpy as jnp
import numpy as np
from jax import lax
from jax.experimental import pallas as pl
from jax.experimental.pallas import tpu as pltpu

EPS = 1e-6
NEG = -1e30
LANES = 128
POOL_WINDOWS = (2, 4, 8, 16)
POOL_HALO = 16
HEAD_DIM = 128
N_KV_GROUPS = 2
HEADS_PER_GROUP = 4
N_BRANCH = 3
CMP_BLOCK = 32
CMP_STRIDE = 16
SEL_BLOCK = 64
N_SELECT = 16
WINDOW = 512
ROPE_THETA = 500000.0
ROPE_DIM = HEAD_DIM // 4
ATTN_SCALE = HEAD_DIM ** -0.5
Q_SCALE = ATTN_SCALE * 1.4426950408889634
VMEM_LIMIT = 60 * 1024 * 1024
W_CHUNK = 64
W_SLOTS = 8
IN_SEC = 512
ROW_TILE = 512
FFN_ROW_TILE = 1024
FFN_FF_TILE = 512
NSA_Q_TILE = 256
NSA_SUB_TILES = 2
COMPRESS_BATCH = 2

f32 = jnp.float32
bf16 = jnp.bfloat16


def _dot(a, b):
    return jnp.dot(a, b, preferred_element_type=f32)


def _dot_nt(a, b):
    return lax.dot_general(a, b, (((1,), (1,)), ((), ())), preferred_element_type=f32)


def _rms(x, g):
    return x * lax.rsqrt(jnp.mean(x * x, axis=-1, keepdims=True) + EPS) * g


def _rope(h, cosf, sinf, lane):
    half = ROPE_DIM // 2
    partner = jnp.where(lane < half, pltpu.roll(h, LANES - half, axis=1), pltpu.roll(h, half, axis=1))
    return h * cosf + partner * sinf


def _w_chunk_copy(w_hbm, stage_sc, sem, c):
    slot = c % W_SLOTS
    return pltpu.make_async_copy(w_hbm.at[pl.ds(c * W_CHUNK, W_CHUNK), :], stage_sc.at[slot], sem.at[slot])


def _in_proj_kernel(x_ref, g_ref, w_hbm, wgt_ref, cos_ref, sin_ref, ca_ref, cb_ref, cc_ref,
                    u_ref, q_ref, kcvc_ref, kv_ref, gt_ref, ca_out, cb_out, cc_out,
                    w_ref, stage_sc, sem):
    @pl.when(pl.program_id(0) == 0)
    def _():
        n_chunk = w_ref.shape[0] // W_CHUNK
        for c in range(min(W_SLOTS - 1, n_chunk)):
            _w_chunk_copy(w_hbm, stage_sc, sem, c).start()
        for c in range(n_chunk):
            if c + W_SLOTS - 1 < n_chunk:
                _w_chunk_copy(w_hbm, stage_sc, sem, c + W_SLOTS - 1).start()
            _w_chunk_copy(w_hbm, stage_sc, sem, c).wait()
            w_ref[c * W_CHUNK:(c + 1) * W_CHUNK, :] = stage_sc[c % W_SLOTS].astype(bf16)

    def section(s):
        return w_ref[s * IN_SEC:(s + 1) * IN_SEC, :]

    ca_out[...] = ca_ref[...].astype(bf16)
    cb_out[...] = cb_ref[...].astype(bf16)
    cc_out[...] = cc_ref[...].astype(bf16)
    a = _rms(x_ref[...], g_ref[...]).astype(bf16)
    cosf = cos_ref[...]
    sinf = sin_ref[...]
    lane = lax.broadcasted_iota(jnp.int32, cosf.shape, 1)
    heads = IN_SEC // HEAD_DIM
    s = 0
    for c in range(0, u_ref.shape[1], IN_SEC):
        u_ref[:, c:c + IN_SEC] = _dot_nt(a, section(s)).astype(bf16)
        s += 1
    for c in range(0, q_ref.shape[1], IN_SEC):
        acc = _dot_nt(a, section(s))
        s += 1
        for j in range(heads):
            h = _rope(acc[:, j * HEAD_DIM:(j + 1) * HEAD_DIM], cosf, sinf, lane) * Q_SCALE
            q_ref[:, c + j * HEAD_DIM:c + (j + 1) * HEAD_DIM] = h.astype(bf16)
    for dst, base, dt in ((kcvc_ref, 0, f32), (kv_ref, 0, bf16), (kv_ref, IN_SEC, bf16)):
        acc = _dot_nt(a, section(s))
        s += 1
        for j in range(heads):
            h = acc[:, j * HEAD_DIM:(j + 1) * HEAD_DIM]
            if j < N_KV_GROUPS:
                h = _rope(h, cosf, sinf, lane)
            dst[:, base + j * HEAD_DIM:base + (j + 1) * HEAD_DIM] = h.astype(dt)
    gt_ref[...] = _dot_nt(a, wgt_ref[...])


def _in_proj(x2, g, w_p, w_gt, cosf, sinf, cast_a, cast_b, cast_c, *, T, tm, n_pool, n_q):
    n_rows, d = x2.shape
    assert n_pool % IN_SEC == 0 and n_q % IN_SEC == 0 and 2 * N_KV_GROUPS * HEAD_DIM == IN_SEC
    n_w = (w_p.shape[0] // W_CHUNK) * W_CHUNK
    assert w_p.dtype == f32 and n_w + w_gt.shape[0] >= w_p.shape[0] and n_w == n_pool + n_q + 3 * IN_SEC
    tiles_per_seq = T // tm
    n_steps = n_rows // tm
    row = lambda i: (i, 0)
    cast_spec = lambda w: pl.BlockSpec((w.shape[0] // n_steps, w.shape[1]), row)
    const = lambda i: (0, 0)
    tab = lambda i: (i % tiles_per_seq, 0)
    return pl.pallas_call(
        _in_proj_kernel,
        grid=(n_rows // tm,),
        in_specs=[
            pl.BlockSpec((tm, d), row),
            pl.BlockSpec((1, d), const),
            pl.BlockSpec(memory_space=pl.ANY),
            pl.BlockSpec((LANES, d), const),
            pl.BlockSpec((tm, LANES), tab),
            pl.BlockSpec((tm, LANES), tab),
            cast_spec(cast_a), cast_spec(cast_b), cast_spec(cast_c),
        ],
        out_specs=[
            pl.BlockSpec((tm, n_pool), row),
            pl.BlockSpec((tm, n_q), row),
            pl.BlockSpec((tm, IN_SEC), row),
            pl.BlockSpec((tm, 2 * IN_SEC), row),
            pl.BlockSpec((tm, LANES), row),
            cast_spec(cast_a), cast_spec(cast_b), cast_spec(cast_c),
        ],
        out_shape=[
            jax.ShapeDtypeStruct((n_rows, n_pool), bf16),
            jax.ShapeDtypeStruct((n_rows, n_q), bf16),
            jax.ShapeDtypeStruct((n_rows, IN_SEC), f32),
            jax.ShapeDtypeStruct((n_rows, 2 * IN_SEC), bf16),
            jax.ShapeDtypeStruct((n_rows, LANES), f32),
            jax.ShapeDtypeStruct(cast_a.shape, bf16),
            jax.ShapeDtypeStruct(cast_b.shape, bf16),
            jax.ShapeDtypeStruct(cast_c.shape, bf16),
        ],
        scratch_shapes=[
            pltpu.VMEM((n_w, d), bf16),
            pltpu.VMEM((W_SLOTS, W_CHUNK, d), f32),
            pltpu.SemaphoreType.DMA((W_SLOTS,)),
        ],
        compiler_params=pltpu.CompilerParams(
            dimension_semantics=("arbitrary",), vmem_limit_bytes=VMEM_LIMIT),
        name="in_proj",
    )(x2, g, w_p, w_gt, cosf, sinf, cast_a, cast_b, cast_c)


def _gelu_tanh(x):
    return 0.5 * x * (1.0 + jnp.tanh(0.7978845608028654 * (x + 0.044715 * (x * x * x))))


def _compress_kernel(kc0_ref, kc1_ref, vc0_ref, vc1_ref, pek_ref, w1k_ref, w2k_ref, pev_ref, w1v_ref, w2v_ref,
                     kc_ref, vc_ref):
    bb = kc0_ref.shape[0]
    n_rows = kc0_ref.shape[1] // CMP_STRIDE
    half_k = (CMP_BLOCK // 2) * HEAD_DIM
    n_stack = bb * N_KV_GROUPS * n_rows
    rowi = lax.broadcasted_iota(jnp.int32, (n_stack, HEAD_DIM), 0)
    for srcs, pe_ref, w1_ref, w2_ref, o_ref in (((kc0_ref, kc1_ref), pek_ref, w1k_ref, w2k_ref, kc_ref),
                                                ((vc0_ref, vc1_ref), pev_ref, w1v_ref, w2v_ref, vc_ref)):
        x = jnp.concatenate(
            [jnp.concatenate([src[b, pl.ds(l, n_rows, stride=CMP_STRIDE), :] for l in range(CMP_STRIDE)], axis=1)
             for b in range(bb) for src in srcs], axis=0)
        za = _dot((x + pe_ref[:, :half_k]).astype(bf16), w1_ref[:half_k, :].astype(bf16))
        zb = _dot((x + pe_ref[:, half_k:]).astype(bf16), w1_ref[half_k:, :].astype(bf16))
        h = za + pltpu.roll(zb, n_stack - 1, axis=0)
        o = _dot(_gelu_tanh(h).astype(bf16), w2_ref[...].astype(bf16))
        o = jnp.where((rowi & (n_rows - 1)) < n_rows - 1, o, 0.0).astype(bf16)
        for b in range(bb):
            for g in range(N_KV_GROUPS):
                r0 = (b * N_KV_GROUPS + g) * n_rows
                o_ref[b, g] = o[r0:r0 + n_rows]


def _compress(r3, pek, w1k, w2k, pev, w1v, w2v, *, bb):
    B, T, width = r3.shape
    n_rows = T // CMP_STRIDE
    assert width == 2 * N_KV_GROUPS * HEAD_DIM and N_KV_GROUPS == 2 and B % bb == 0 and n_rows & (n_rows - 1) == 0
    const = lambda b: (0, 0)
    out = jax.ShapeDtypeStruct((B, N_KV_GROUPS, n_rows, HEAD_DIM), bf16)
    ospec = pl.BlockSpec((bb, N_KV_GROUPS, n_rows, HEAD_DIM), lambda b: (b, 0, 0, 0))
    sec = lambda s: pl.BlockSpec((bb, T, HEAD_DIM), lambda b: (b, 0, s))
    return pl.pallas_call(
        _compress_kernel,
        grid=(B // bb,),
        in_specs=[
            sec(0), sec(1), sec(2), sec(3),
            pl.BlockSpec(pek.shape, const), pl.BlockSpec(w1k.shape, const), pl.BlockSpec(w2k.shape, const),
            pl.BlockSpec(pev.shape, const), pl.BlockSpec(w1v.shape, const), pl.BlockSpec(w2v.shape, const),
        ],
        out_specs=[ospec, ospec],
        out_shape=[out, out],
        compiler_params=pltpu.CompilerParams(
            dimension_semantics=("arbitrary",), vmem_limit_bytes=VMEM_LIMIT),
        name="compress",
    )(r3, r3, r3, r3, pek, w1k, w2k, pev, w1v, w2v)


def _nsa_front(qs, t0, kc, vc, kw_ref, vw_ref, *, tq, T):
    R = HEADS_PER_GROUP
    n_cmp = kc.shape[0]
    n_sel = T // SEL_BLOCK

    wk = WINDOW + tq
    w0 = pl.multiple_of(jnp.maximum(t0 - WINDOW, 0), tq)
    sw = _dot_nt(qs, kw_ref[pl.ds(w0, wk), :])
    diff = (t0 - w0) + lax.broadcasted_iota(jnp.int32, (tq, wk), 0) - lax.broadcasted_iota(jnp.int32, (tq, wk), 1)
    wbias = jnp.where((diff & -WINDOW) == 0, 0.0, NEG)
    sw = jnp.concatenate([sw[r * tq:(r + 1) * tq] + wbias for r in range(R)], axis=0)
    ew = jnp.exp2(sw - jnp.max(sw, axis=1, keepdims=True))
    o_win = _dot(ew.astype(bf16), vw_ref[pl.ds(w0, wk), :]) / jnp.sum(ew, axis=1, keepdims=True)

    s = _dot_nt(qs, kc)
    valid = (CMP_STRIDE * lax.broadcasted_iota(jnp.int32, (tq, n_cmp), 1) + (CMP_BLOCK - 1)
             <= t0 + lax.broadcasted_iota(jnp.int32, (tq, n_cmp), 0))
    s = jnp.concatenate([jnp.where(valid, s[r * tq:(r + 1) * tq], NEG) for r in range(R)], axis=0)
    e = jnp.exp2(s - jnp.max(s, axis=1, keepdims=True))
    p = e / jnp.sum(e, axis=1, keepdims=True)
    p = jnp.concatenate([jnp.where(valid, p[r * tq:(r + 1) * tq], 0.0) for r in range(R)], axis=0)
    o_cmp = _dot(p.astype(bf16), vc)

    psum = p[0:tq]
    for r in range(1, R):
        psum = psum + p[r * tq:(r + 1) * tq]
    p_hi = psum.astype(bf16)
    p_lo = (psum - p_hi.astype(f32)).astype(bf16)
    sb = lax.broadcasted_iota(jnp.int32, (n_sel, n_cmp), 0) * SEL_BLOCK
    cb = lax.broadcasted_iota(jnp.int32, (n_sel, n_cmp), 1) * CMP_STRIDE
    ov_t = jnp.where((cb < sb + SEL_BLOCK) & (cb + CMP_BLOCK > sb), 1.0, 0.0).astype(bf16)
    imp_t = _dot_nt(ov_t, p_hi) + _dot_nt(ov_t, p_lo)

    jj = lax.broadcasted_iota(jnp.int32, (n_sel, tq), 0)
    cur = jnp.right_shift(t0 + lax.broadcasted_iota(jnp.int32, (n_sel, tq), 1), SEL_BLOCK.bit_length() - 1)
    forced = (jj == 0) | (jj == cur) | (jj == cur - 1)
    v = jnp.where(jj > cur, -jnp.inf, jnp.where(forced, jnp.inf, imp_t))
    sub = 8
    slabs = [v[k * sub:(k + 1) * sub] for k in range(n_sel // sub)]
    ranks = [jnp.zeros((sub, tq), f32) for _ in slabs]
    j_in = lax.broadcasted_iota(jnp.int32, (sub, tq), 0)
    for c in range(n_sel):
        v_c = v[c:c + 1, :]
        for k, v_k in enumerate(slabs):
            if k * sub > c:
                ahead = jnp.where(v_c >= v_k, 1.0, 0.0)
            elif (k + 1) * sub <= c:
                ahead = jnp.where(v_c > v_k, 1.0, 0.0)
            else:
                ahead = jnp.where(j_in > c - k * sub, jnp.where(v_c >= v_k, 1.0, 0.0),
                                  jnp.where(v_c > v_k, 1.0, 0.0))
            ranks[k] = ranks[k] + ahead
    rank = jnp.concatenate(ranks, axis=0)
    unsel_t = jnp.where(rank < float(min(N_SELECT, n_sel)), 0.0, 1.0)
    unsel_t = jnp.concatenate([unsel_t, jnp.zeros((LANES - n_sel, tq), f32)], axis=0)
    return o_win, o_cmp, unsel_t.T.astype(bf16)


def _nsa_kernel(q_ref, kc_ref, vc_ref, ks_ref, vs_ref, kw_ref, vw_ref, eneg_ref, gexp_ref, gt_ref, ca_ref, cb_ref,
                o_ref, ca_out, cb_out, s_sc, mx_sc, ls_sc, acc_sc, *, tq, n_sub, T):
    ca_out[...] = ca_ref[...].astype(bf16)
    cb_out[...] = cb_ref[...].astype(bf16)
    R = HEADS_PER_GROUP
    M = R * tq
    tk = n_sub * tq
    i = pl.program_id(2)
    t0 = i * tk
    kc = kc_ref[0, 0]
    vc = vc_ref[0, 0]

    fronts = []
    q_rows = []
    for sb in range(n_sub):
        q = q_ref[sb * tq:(sb + 1) * tq, :]
        qs = jnp.concatenate([q[:, r * HEAD_DIM:(r + 1) * HEAD_DIM] for r in range(R)], axis=0)
        o_win, o_cmp, unsel = _nsa_front(qs, t0 + sb * tq, kc, vc, kw_ref, vw_ref, tq=tq, T=T)
        fronts.append((o_win, o_cmp))
        q_rows.append(jnp.concatenate([qs, jnp.concatenate([unsel] * R, axis=0)], axis=1))
    q_aug = jnp.concatenate(q_rows, axis=0)

    n_chunk = tk // LANES
    mx_sc[...] = jnp.full(mx_sc.shape, NEG, f32)

    def lane_max(sc):
        part = sc[:, :LANES]
        for c in range(1, sc.shape[1] // LANES):
            part = jnp.maximum(part, sc[:, c * LANES:(c + 1) * LANES])
        return part

    def score_tiles(kt, n_tiles):
        k0 = pl.multiple_of(kt * tk, tk)
        width = n_tiles * tk
        k_aug = jnp.concatenate([ks_ref[pl.ds(k0, width), :], eneg_ref[pl.ds(k0, width), :]], axis=1)
        sc = _dot_nt(q_aug, k_aug)
        for j in range(n_tiles):
            s_sc[kt + j] = sc[:, j * tk:(j + 1) * tk]
        mx_sc[...] = jnp.maximum(mx_sc[...], lane_max(sc))

    def score_pair(kp, carry):
        score_tiles(2 * kp, 2)
        return carry

    lax.fori_loop(0, i // 2, score_pair, 0)

    @pl.when(i % 2 == 1)
    def _():
        score_tiles(i - 1, 1)


    kd = pl.multiple_of(i * tk, tk)
    tri = jnp.where(lax.broadcasted_iota(jnp.int32, (tq, tq), 0) >= lax.broadcasted_iota(jnp.int32, (tq, tq), 1),
                    0.0, NEG)
    for sb in range(n_sub):
        width = (sb + 1) * tq
        rows = slice(sb * M, (sb + 1) * M)
        k_aug = jnp.concatenate([ks_ref[pl.ds(kd, width), :], eneg_ref[pl.ds(kd, width), :]], axis=1)
        sc = _dot_nt(q_aug[rows], k_aug)
        last = jnp.concatenate([sc[r * tq:(r + 1) * tq, sb * tq:] + tri for r in range(R)], axis=0)
        sc = last if sb == 0 else jnp.concatenate([sc[:, :sb * tq], last], axis=1)
        s_sc[i, rows, :width] = sc
        mx_sc[rows, :] = jnp.maximum(mx_sc[rows, :], lane_max(sc))
    mx_sc[...] = jnp.broadcast_to(jnp.max(mx_sc[...], axis=1, keepdims=True), mx_sc.shape)
    ls_sc[...] = jnp.zeros(ls_sc.shape, f32)
    acc_sc[...] = jnp.zeros(acc_sc.shape, f32)

    def pv_body(kt, carry):
        k0 = pl.multiple_of(kt * tk, tk)
        sc = s_sc[kt]
        mb = mx_sc[...]
        ps = [jnp.exp2(sc[:, c * LANES:(c + 1) * LANES] - mb) for c in range(n_chunk)]
        ls_sc[...] += functools.reduce(lambda a, b: a + b, ps)
        acc_sc[...] += _dot(jnp.concatenate(ps, axis=1).astype(bf16), vs_ref[pl.ds(k0, tk), :])
        return carry

    lax.fori_loop(0, i, pv_body, 0)
    for sb in range(n_sub):
        width = (sb + 1) * tq
        rows = slice(sb * M, (sb + 1) * M)
        sc = s_sc[i, rows, :width]
        mb = mx_sc[rows, :]
        ps = [jnp.exp2(sc[:, c * LANES:(c + 1) * LANES] - mb) for c in range(width // LANES)]
        ls_sc[rows, :] += functools.reduce(lambda a, b: a + b, ps)
        acc_sc[rows, :] += _dot(jnp.concatenate(ps, axis=1).astype(bf16), vs_ref[pl.ds(kd, width), :])
    o_sel = acc_sc[...] / jnp.sum(ls_sc[...], axis=1, keepdims=True)

    gt_all = jax.nn.sigmoid(gt_ref[...])
    gt_all = jnp.where(pl.program_id(1) == 0, gt_all, pltpu.roll(gt_all, LANES - R * N_BRANCH, axis=1))
    g_hi = gt_all.astype(bf16)
    rest = gt_all - g_hi.astype(f32)
    g_mid = rest.astype(bf16)
    g_lo = (rest - g_mid.astype(f32)).astype(bf16)
    g_exp = _dot(jnp.concatenate([g_hi, g_mid, g_lo], axis=1), gexp_ref[...])
    for sb, (o_win, o_cmp) in enumerate(fronts):
        for r in range(R):
            rows = slice(r * tq, (r + 1) * tq)
            srows = slice(sb * M + r * tq, sb * M + (r + 1) * tq)
            gates = [g_exp[sb * tq:(sb + 1) * tq, (r * N_BRANCH + br) * LANES:(r * N_BRANCH + br + 1) * LANES]
                     for br in range(N_BRANCH)]
            o = gates[0] * o_cmp[rows] + gates[1] * o_sel[srows] + gates[2] * o_win[rows]
            o_ref[sb * tq:(sb + 1) * tq, r * HEAD_DIM:(r + 1) * HEAD_DIM] = o.astype(bf16)


def _nsa(q, kcmp, vcmp, kv, eneg, gexp, gates, cast_a, cast_b, *, B, T, tq, n_sub):
    G, R = N_KV_GROUPS, HEADS_PER_GROUP
    tk = n_sub * tq
    nt = T // tk
    n_cmp = kcmp.shape[2]
    assert tq % LANES == 0 and tq & (tq - 1) == 0 and T % tk == 0 and T >= WINDOW + tq
    assert WINDOW % tq == 0 and WINDOW & (WINDOW - 1) == 0 and (R * tq) & (R * tq - 1) == 0
    rowblk = lambda b, g, i: (b * nt + i, g)
    cmp_spec = pl.BlockSpec((1, 1, n_cmp, HEAD_DIM), lambda b, g, i: (b, g, 0, 0))
    kvspec = lambda sec: pl.BlockSpec((T, HEAD_DIM), lambda b, g, i: (b, sec * G + g))
    n_steps = B * G * nt
    cast_spec = lambda w: pl.BlockSpec((w.shape[0] // n_steps, w.shape[1]), lambda b, g, i: ((b * G + g) * nt + i, 0))
    return pl.pallas_call(
        functools.partial(_nsa_kernel, tq=tq, n_sub=n_sub, T=T),
        grid=(B, G, nt),
        in_specs=[
            pl.BlockSpec((tk, R * HEAD_DIM), rowblk),
            cmp_spec, cmp_spec,
            kvspec(0), kvspec(1), kvspec(2), kvspec(3),
            pl.BlockSpec((T, LANES), lambda b, g, i: (0, 0)),
            pl.BlockSpec(gexp.shape, lambda b, g, i: (0, 0)),
            pl.BlockSpec((tk, LANES), lambda b, g, i: (b * nt + i, 0)),
            cast_spec(cast_a), cast_spec(cast_b),
        ],
        out_specs=[pl.BlockSpec((tk, R * HEAD_DIM), rowblk), cast_spec(cast_a), cast_spec(cast_b)],
        out_shape=[jax.ShapeDtypeStruct((B * T, G * R * HEAD_DIM), bf16), jax.ShapeDtypeStruct(cast_a.shape, bf16),
                   jax.ShapeDtypeStruct(cast_b.shape, bf16)],
        scratch_shapes=[
            pltpu.VMEM((T // tk, n_sub * R * tq, tk), f32),
            pltpu.VMEM((n_sub * R * tq, LANES), f32),
            pltpu.VMEM((n_sub * R * tq, LANES), f32),
            pltpu.VMEM((n_sub * R * tq, HEAD_DIM), f32),
        ],
        compiler_params=pltpu.CompilerParams(
            dimension_semantics=("arbitrary", "arbitrary", "arbitrary"), vmem_limit_bytes=VMEM_LIMIT),
        name="nsa",
    )(q, kcmp, vcmp, kv, kv, kv, kv, eneg, gexp, gates, cast_a, cast_b)


def _out_proj_kernel(x_ref, u_ref, uh_ref, y_ref, pw_ref, ps_ref, wo_ref, o_ref, *, tm, T):
    i = pl.program_id(0)
    tiles_per_seq = T // tm
    keep_halo = jnp.where((i % tiles_per_seq) == 0, 0.0, 1.0)
    u = u_ref[...]
    halo = uh_ref[...].astype(f32) * keep_halo
    tseq = (i % tiles_per_seq) * tm + lax.broadcasted_iota(jnp.int32, (tm, 1), 0)
    n_pool = u.shape[1]
    gd = n_pool // len(POOL_WINDOWS)
    acc = x_ref[...] + _dot(y_ref[...], wo_ref[n_pool:, :])
    for gi, w in enumerate(POOL_WINDOWS):
        assert w <= POOL_HALO and w & (w - 1) == 0
        cols = slice(gi * gd, (gi + 1) * gd)
        ug = u[:, cols].astype(f32)
        ws = jnp.concatenate([halo[:, cols], ug], axis=0)
        k = 1
        while k < w:
            ws = ws + pltpu.roll(ws, k, axis=0)
            k *= 2
        win_sum = ws[POOL_HALO:]
        count = jnp.minimum(tseq + 1, w).astype(f32)
        pooled = win_sum / count - ug
        yg = _dot(pooled.astype(bf16), pw_ref[gi].astype(bf16)) * ps_ref[:, cols]
        acc = acc + _dot(yg.astype(bf16), wo_ref[cols, :])
    o_ref[...] = acc


def _out_proj(x2, u, y_nsa, pool_w, pool_scale, w_out, *, T, tm):
    n_rows, d = x2.shape
    n_pool = u.shape[1]
    hb = tm // POOL_HALO
    row = lambda i: (i, 0)
    const2 = lambda i: (0, 0)
    return pl.pallas_call(
        functools.partial(_out_proj_kernel, tm=tm, T=T),
        grid=(n_rows // tm,),
        in_specs=[
            pl.BlockSpec((tm, d), row),
            pl.BlockSpec((tm, n_pool), row),
            pl.BlockSpec((POOL_HALO, n_pool), lambda i: (jnp.maximum(i * hb - 1, 0), 0)),
            pl.BlockSpec((tm, y_nsa.shape[1]), row),
            pl.BlockSpec(pool_w.shape, lambda i: (0, 0, 0)),
            pl.BlockSpec((1, n_pool), const2),
            pl.BlockSpec(w_out.shape, const2, pipeline_mode=pl.Buffered(1)),
        ],
        out_specs=pl.BlockSpec((tm, d), row),
        out_shape=jax.ShapeDtypeStruct((n_rows, d), f32),
        compiler_params=pltpu.CompilerParams(
            dimension_semantics=("arbitrary",), vmem_limit_bytes=VMEM_LIMIT),
        name="out_proj",
    )(x2, u, u, y_nsa, pool_w, pool_scale, w_out)


def _ffn_kernel(h_ref, g_ref, wg_ref, wu_ref, wd_ref, o_ref, n_sc):
    @pl.when(pl.program_id(1) == 0)
    def _():
        h = h_ref[...]
        n_sc[...] = _rms(h, g_ref[...]).astype(bf16)
        o_ref[...] = h

    n = n_sc[...]
    half = wg_ref.shape[1] // 2
    acc = None
    for c in range(2):
        cols = slice(c * half, (c + 1) * half)
        gate = _dot(n, wg_ref[:, cols])
        up = _dot(n, wu_ref[:, cols])
        a = gate * jax.nn.sigmoid(gate) * up
        d = _dot(a.astype(bf16), wd_ref[cols, :])
        acc = d if acc is None else acc + d
    o_ref[...] += acc


def _ffn(h, g, w_gate, w_up, w_down, *, tm, tf):
    n_rows, d = h.shape
    d_ff = w_gate.shape[1]
    return pl.pallas_call(
        _ffn_kernel,
        grid=(n_rows // tm, d_ff // tf),
        in_specs=[
            pl.BlockSpec((tm, d), lambda i, f: (i, 0)),
            pl.BlockSpec((1, d), lambda i, f: (0, 0)),
            pl.BlockSpec((d, tf), lambda i, f: (0, f)),
            pl.BlockSpec((d, tf), lambda i, f: (0, f)),
            pl.BlockSpec((tf, d), lambda i, f: (f, 0)),
        ],
        out_specs=pl.BlockSpec((tm, d), lambda i, f: (i, 0)),
        out_shape=jax.ShapeDtypeStruct((n_rows, d), f32),
        scratch_shapes=[pltpu.VMEM((tm, d), bf16)],
        compiler_params=pltpu.CompilerParams(
            dimension_semantics=("arbitrary", "arbitrary"), vmem_limit_bytes=VMEM_LIMIT),
        name="ffn",
    )(h, g, w_gate, w_up, w_down)


def _ple_kernel(h_ref, p_ref, g_ref, wg_ref, wp_ref, fg_ref, o_ref):
    h = h_ref[...]
    n = _rms(h, g_ref[...]).astype(bf16)
    gate = jax.nn.sigmoid(_dot(n, wg_ref[...]))
    h = h + _dot(p_ref[...].astype(bf16), wp_ref[...].astype(bf16)) * gate
    o_ref[...] = _rms(h, fg_ref[...])


def _ple_outer(h_hbm, p_hbm, g_ref, wg_ref, wp_ref, fg_ref, o_hbm, *, tm):
    n_rows, d = h_hbm.shape
    row = lambda i: (i, 0)

    def body(h_ref, p_ref, o_ref):
        _ple_kernel(h_ref, p_ref, g_ref, wg_ref, wp_ref, fg_ref, o_ref)

    pltpu.emit_pipeline(
        body,
        grid=(n_rows // tm,),
        in_specs=[pl.BlockSpec((tm, d), row, pipeline_mode=pl.Buffered(3)),
                  pl.BlockSpec((tm, p_hbm.shape[1]), row)],
        out_specs=[pl.BlockSpec((tm, d), row)],
    )(h_hbm, p_hbm, o_hbm)


def _ple(h, p2, g, w_gate, w_proj, fg, *, tm):
    n_rows, d = h.shape
    whole = pl.BlockSpec(memory_space=pltpu.VMEM)
    return pl.pallas_call(
        functools.partial(_ple_outer, tm=tm),
        in_specs=[pl.BlockSpec(memory_space=pl.ANY), pl.BlockSpec(memory_space=pl.ANY), whole, whole, whole, whole],
        out_specs=pl.BlockSpec(memory_space=pl.ANY),
        out_shape=jax.ShapeDtypeStruct((n_rows, d), f32),
        compiler_params=pltpu.CompilerParams(vmem_limit_bytes=VMEM_LIMIT),
        name="ple",
    )(h, p2, g, w_gate, w_proj, fg)


def _rope_tables(T):
    pos = np.arange(T, dtype=np.float64)
    inv_freq = ROPE_THETA ** (-np.arange(0, ROPE_DIM, 2, dtype=np.float64) / ROPE_DIM)
    ang = pos[:, None] * inv_freq[None, :]
    cos, sin = np.cos(ang), np.sin(ang)
    rest = HEAD_DIM - ROPE_DIM
    cosf = np.concatenate([cos, cos, np.ones((T, rest))], axis=1)
    sinf = np.concatenate([-sin, sin, np.zeros((T, rest))], axis=1)
    return jnp.asarray(cosf, f32), jnp.asarray(sinf, f32)


def _block_bias_table(T):
    key_blk = np.arange(T)[:, None] // SEL_BLOCK
    return jnp.asarray(np.where(key_blk == np.arange(LANES)[None, :], NEG, 0.0).astype(bf16))


def _gate_expansion_table():
    n_col = HEADS_PER_GROUP * N_BRANCH
    e = np.zeros((3, LANES, n_col, LANES), np.float32)
    for c in range(n_col):
        e[:, c, c, :] = 1.0
    return jnp.asarray(e.reshape(3 * LANES, n_col * LANES).astype(bf16))


def _mixer_ffn(h2, in_norm_g, w_in, pool_w, pool_scale, cmp_k_pe, cmp_k_w1, cmp_k_w2,
               cmp_v_pe, cmp_v_w1, cmp_v_w2, w_out, ffn_norm_g, w_gate, w_up, w_down, w_ple_gate, *, B, T):
    n_pool = pool_scale.shape[0]
    n_heads = N_KV_GROUPS * HEADS_PER_GROUP
    nsa_w = n_heads * HEAD_DIM
    kv_w = N_KV_GROUPS * HEAD_DIM
    assert w_in.shape[1] == n_pool + nsa_w + 6 * kv_w + n_heads * N_BRANCH

    o0 = n_pool + nsa_w + 6 * kv_w
    w_p = w_in.T
    w_gt = jnp.pad(w_p[o0:].astype(bf16), ((0, LANES - n_heads * N_BRANCH), (0, 0)))

    cosf, sinf = _rope_tables(T)
    u, q, kcvc, kv, gates, w_out_b, w_gate_b, w_ple_gate_b = _in_proj(
        h2, in_norm_g[None, :], w_p, w_gt, cosf, sinf, w_out, w_gate, w_ple_gate, T=T, tm=ROW_TILE,
        n_pool=n_pool, n_q=nsa_w)

    kcmp, vcmp = _compress(
        kcvc.reshape(B, T, kcvc.shape[1]),
        cmp_k_pe.reshape(1, -1), cmp_k_w1, cmp_k_w2, cmp_v_pe.reshape(1, -1), cmp_v_w1, cmp_v_w2,
        bb=COMPRESS_BATCH if B % COMPRESS_BATCH == 0 else 1)

    eneg = _block_bias_table(T)
    y_nsa, w_up_b, w_down_b = _nsa(q, kcmp, vcmp, kv, eneg, _gate_expansion_table(), gates, w_up, w_down, B=B, T=T,
                                   tq=NSA_Q_TILE, n_sub=NSA_SUB_TILES)

    h2 = _out_proj(h2, u, y_nsa, pool_w, pool_scale[None, :], w_out_b, T=T, tm=ROW_TILE)
    return _ffn(h2, ffn_norm_g[None, :], w_gate_b, w_up_b, w_down_b, tm=FFN_ROW_TILE, tf=FFN_FF_TILE), w_ple_gate_b


def kernel(x, p, in_norm_g, w_in, pool_w, pool_scale, cmp_k_pe, cmp_k_w1, cmp_k_w2, cmp_v_pe, cmp_v_w1,
           cmp_v_w2, w_out, ffn_norm_g, w_gate, w_up, w_down, ple_norm_g, w_ple_gate, w_ple_proj, final_norm_g):
    B, T, d = x.shape
    depth = w_in.shape[0]
    assert depth == 1, "the final rmsnorm is fused into the last layer's per-layer-embedding kernel"
    h2 = x.reshape(B * T, d)
    i = 0
    h2, w_ple_gate_b = _mixer_ffn(
        h2, in_norm_g[i], w_in[i], pool_w[i], pool_scale[i], cmp_k_pe[i], cmp_k_w1[i], cmp_k_w2[i],
        cmp_v_pe[i], cmp_v_w1[i], cmp_v_w2[i], w_out[i], ffn_norm_g[i], w_gate[i], w_up[i], w_down[i],
        w_ple_gate[i], B=B, T=T)
    out = _ple(h2, p[i].reshape(B * T, -1), ple_norm_g[i][None, :], w_ple_gate_b,
               w_ple_proj[i], final_norm_g[None, :], tm=ROW_TILE)
    return out.reshape(B, T, d)
```

```python
import functools

import jax
import jax.numpy as jnp
import numpy as np
from jax import lax
from jax.experimental import pallas as pl
from jax.experimental.pallas import tpu as pltpu

EPS = 1e-6
NEG = -1e30
LANES = 128
POOL_WINDOWS = (2, 4, 8, 16)
POOL_HALO = 16
HEAD_DIM = 128
N_KV_GROUPS = 2
HEADS_PER_GROUP = 4
N_BRANCH = 3
CMP_BLOCK = 32
CMP_STRIDE = 16
SEL_BLOCK = 64
N_SELECT = 16
WINDOW = 512
ROPE_THETA = 500000.0
ROPE_DIM = HEAD_DIM // 4
ATTN_SCALE = HEAD_DIM ** -0.5
Q_SCALE = ATTN_SCALE * 1.4426950408889634
VMEM_LIMIT = 60 * 1024 * 1024
W_CHUNK = 64
W_SLOTS = 8
IN_SEC = 512
ROW_TILE = 512
FFN_ROW_TILE = 1024
FFN_FF_TILE = 512
NSA_Q_TILE = 256
NSA_SUB_TILES = 2
NSA_FRONT_SUB_TILES = 4
COMPRESS_BATCH = 2

f32 = jnp.float32
bf16 = jnp.bfloat16


def _dot(a, b):
    return jnp.dot(a, b, preferred_element_type=f32)


def _dot_nt(a, b):
    return lax.dot_general(a, b, (((1,), (1,)), ((), ())), preferred_element_type=f32)


def _rms(x, g):
    return x * lax.rsqrt(jnp.mean(x * x, axis=-1, keepdims=True) + EPS) * g


def _rope(h, cosf, sinf, lane):
    half = ROPE_DIM // 2
    partner = jnp.where(lane < half, pltpu.roll(h, LANES - half, axis=1), pltpu.roll(h, half, axis=1))
    return h * cosf + partner * sinf


def _w_chunk_copy(w_hbm, stage_sc, sem, c):
    slot = c % W_SLOTS
    return pltpu.make_async_copy(w_hbm.at[pl.ds(c * W_CHUNK, W_CHUNK), :], stage_sc.at[slot], sem.at[slot])


def _in_proj_kernel(x_ref, g_ref, w_hbm, wgt_ref, cos_ref, sin_ref, ca_ref, cb_ref, cc_ref,
                    u_ref, q_ref, kcvc_ref, kv_ref, gt_ref, ca_out, cb_out, cc_out,
                    w_ref, stage_sc, sem):
    @pl.when(pl.program_id(0) == 0)
    def _():
        n_chunk = w_ref.shape[0] // W_CHUNK
        for c in range(min(W_SLOTS - 1, n_chunk)):
            _w_chunk_copy(w_hbm, stage_sc, sem, c).start()
        for c in range(n_chunk):
            if c + W_SLOTS - 1 < n_chunk:
                _w_chunk_copy(w_hbm, stage_sc, sem, c + W_SLOTS - 1).start()
            _w_chunk_copy(w_hbm, stage_sc, sem, c).wait()
            w_ref[c * W_CHUNK:(c + 1) * W_CHUNK, :] = stage_sc[c % W_SLOTS].astype(bf16)

    def section(s):
        return w_ref[s * IN_SEC:(s + 1) * IN_SEC, :]

    ca_out[...] = ca_ref[...].astype(bf16)
    cb_out[...] = cb_ref[...].astype(bf16)
    cc_out[...] = cc_ref[...].astype(bf16)
    a = _rms(x_ref[...], g_ref[...]).astype(bf16)
    cosf = cos_ref[...]
    sinf = sin_ref[...]
    lane = lax.broadcasted_iota(jnp.int32, cosf.shape, 1)
    heads = IN_SEC // HEAD_DIM
    s = 0
    for c in range(0, u_ref.shape[1], IN_SEC):
        u_ref[:, c:c + IN_SEC] = _dot_nt(a, section(s)).astype(bf16)
        s += 1
    for c in range(0, q_ref.shape[1], IN_SEC):
        acc = _dot_nt(a, section(s))
        s += 1
        for j in range(heads):
            h = _rope(acc[:, j * HEAD_DIM:(j + 1) * HEAD_DIM], cosf, sinf, lane) * Q_SCALE
            q_ref[:, c + j * HEAD_DIM:c + (j + 1) * HEAD_DIM] = h.astype(bf16)
    for dst, base, dt in ((kcvc_ref, 0, f32), (kv_ref, 0, bf16), (kv_ref, IN_SEC, bf16)):
        acc = _dot_nt(a, section(s))
        s += 1
        for j in range(heads):
            h = acc[:, j * HEAD_DIM:(j + 1) * HEAD_DIM]
            if j < N_KV_GROUPS:
                h = _rope(h, cosf, sinf, lane)
            dst[:, base + j * HEAD_DIM:base + (j + 1) * HEAD_DIM] = h.astype(dt)
    gt_ref[...] = _dot_nt(a, wgt_ref[...])


def _in_proj(x2, g, w_p, w_gt, cosf, sinf, cast_a, cast_b, cast_c, *, T, tm, n_pool, n_q):
    n_rows, d = x2.shape
    assert n_pool % IN_SEC == 0 and n_q % IN_SEC == 0 and 2 * N_KV_GROUPS * HEAD_DIM == IN_SEC
    n_w = (w_p.shape[0] // W_CHUNK) * W_CHUNK
    assert w_p.dtype == f32 and n_w + w_gt.shape[0] >= w_p.shape[0] and n_w == n_pool + n_q + 3 * IN_SEC
    tiles_per_seq = T // tm
    n_steps = n_rows // tm
    row = lambda i: (i, 0)
    cast_spec = lambda w: pl.BlockSpec((w.shape[0] // n_steps, w.shape[1]), row)
    const = lambda i: (0, 0)
    tab = lambda i: (i % tiles_per_seq, 0)
    return pl.pallas_call(
        _in_proj_kernel,
        grid=(n_rows // tm,),
        in_specs=[
            pl.BlockSpec((tm, d), row),
            pl.BlockSpec((1, d), const),
            pl.BlockSpec(memory_space=pl.ANY),
            pl.BlockSpec((LANES, d), const),
            pl.BlockSpec((tm, LANES), tab),
            pl.BlockSpec((tm, LANES), tab),
            cast_spec(cast_a), cast_spec(cast_b), cast_spec(cast_c),
        ],
        out_specs=[
            pl.BlockSpec((tm, n_pool), row),
            pl.BlockSpec((tm, n_q), row),
            pl.BlockSpec((tm, IN_SEC), row),
            pl.BlockSpec((tm, 2 * IN_SEC), row),
            pl.BlockSpec((tm, LANES), row),
            cast_spec(cast_a), cast_spec(cast_b), cast_spec(cast_c),
        ],
        out_shape=[
            jax.ShapeDtypeStruct((n_rows, n_pool), bf16),
            jax.ShapeDtypeStruct((n_rows, n_q), bf16),
            jax.ShapeDtypeStruct((n_rows, IN_SEC), f32),
            jax.ShapeDtypeStruct((n_rows, 2 * IN_SEC), bf16),
            jax.ShapeDtypeStruct((n_rows, LANES), f32),
            jax.ShapeDtypeStruct(cast_a.shape, bf16),
            jax.ShapeDtypeStruct(cast_b.shape, bf16),
            jax.ShapeDtypeStruct(cast_c.shape, bf16),
        ],
        scratch_shapes=[
            pltpu.VMEM((n_w, d), bf16),
            pltpu.VMEM((W_SLOTS, W_CHUNK, d), f32),
            pltpu.SemaphoreType.DMA((W_SLOTS,)),
        ],
        compiler_params=pltpu.CompilerParams(
            dimension_semantics=("arbitrary",), vmem_limit_bytes=VMEM_LIMIT),
        name="in_proj",
    )(x2, g, w_p, w_gt, cosf, sinf, cast_a, cast_b, cast_c)


def _gelu_tanh(x):
    return 0.5 * x * (1.0 + jnp.tanh(0.7978845608028654 * (x + 0.044715 * (x * x * x))))


def _compress_kernel(kc0_ref, kc1_ref, vc0_ref, vc1_ref, pek_ref, w1k_ref, w2k_ref, pev_ref, w1v_ref, w2v_ref,
                     kc_ref, vc_ref):
    bb = kc0_ref.shape[0]
    n_rows = kc0_ref.shape[1] // CMP_STRIDE
    half_k = (CMP_BLOCK // 2) * HEAD_DIM
    n_stack = bb * N_KV_GROUPS * n_rows
    rowi = lax.broadcasted_iota(jnp.int32, (n_stack, HEAD_DIM), 0)
    for srcs, pe_ref, w1_ref, w2_ref, o_ref in (((kc0_ref, kc1_ref), pek_ref, w1k_ref, w2k_ref, kc_ref),
                                                ((vc0_ref, vc1_ref), pev_ref, w1v_ref, w2v_ref, vc_ref)):
        x = jnp.concatenate(
            [jnp.concatenate([src[b, pl.ds(l, n_rows, stride=CMP_STRIDE), :] for l in range(CMP_STRIDE)], axis=1)
             for b in range(bb) for src in srcs], axis=0)
        za = _dot((x + pe_ref[:, :half_k]).astype(bf16), w1_ref[:half_k, :].astype(bf16))
        zb = _dot((x + pe_ref[:, half_k:]).astype(bf16), w1_ref[half_k:, :].astype(bf16))
        h = za + pltpu.roll(zb, n_stack - 1, axis=0)
        o = _dot(_gelu_tanh(h).astype(bf16), w2_ref[...].astype(bf16))
        o = jnp.where((rowi & (n_rows - 1)) < n_rows - 1, o, 0.0).astype(bf16)
        for b in range(bb):
            for g in range(N_KV_GROUPS):
                r0 = (b * N_KV_GROUPS + g) * n_rows
                o_ref[b, g] = o[r0:r0 + n_rows]


def _compress(r3, pek, w1k, w2k, pev, w1v, w2v, *, bb):
    B, T, width = r3.shape
    n_rows = T // CMP_STRIDE
    assert width == 2 * N_KV_GROUPS * HEAD_DIM and N_KV_GROUPS == 2 and B % bb == 0 and n_rows & (n_rows - 1) == 0
    const = lambda b: (0, 0)
    out = jax.ShapeDtypeStruct((B, N_KV_GROUPS, n_rows, HEAD_DIM), bf16)
    ospec = pl.BlockSpec((bb, N_KV_GROUPS, n_rows, HEAD_DIM), lambda b: (b, 0, 0, 0))
    sec = lambda s: pl.BlockSpec((bb, T, HEAD_DIM), lambda b: (b, 0, s))
    return pl.pallas_call(
        _compress_kernel,
        grid=(B // bb,),
        in_specs=[
            sec(0), sec(1), sec(2), sec(3),
            pl.BlockSpec(pek.shape, const), pl.BlockSpec(w1k.shape, const), pl.BlockSpec(w2k.shape, const),
            pl.BlockSpec(pev.shape, const), pl.BlockSpec(w1v.shape, const), pl.BlockSpec(w2v.shape, const),
        ],
        out_specs=[ospec, ospec],
        out_shape=[out, out],
        compiler_params=pltpu.CompilerParams(
            dimension_semantics=("arbitrary",), vmem_limit_bytes=VMEM_LIMIT),
        name="compress",
    )(r3, r3, r3, r3, pek, w1k, w2k, pev, w1v, w2v)


def _nsa_front(qs, t0, kc, vc, kw_ref, vw_ref, *, tq, T):
    R = HEADS_PER_GROUP
    n_cmp = kc.shape[0]
    n_sel = T // SEL_BLOCK

    wk = WINDOW + tq
    w0 = pl.multiple_of(jnp.maximum(t0 - WINDOW, 0), tq)
    sw = _dot_nt(qs, kw_ref[pl.ds(w0, wk), :])
    diff = (t0 - w0) + lax.broadcasted_iota(jnp.int32, (tq, wk), 0) - lax.broadcasted_iota(jnp.int32, (tq, wk), 1)
    wbias = jnp.where((diff & -WINDOW) == 0, 0.0, NEG)
    sw = jnp.concatenate([sw[r * tq:(r + 1) * tq] + wbias for r in range(R)], axis=0)
    ew = jnp.exp2(sw - jnp.max(sw, axis=1, keepdims=True))
    o_win = _dot(ew.astype(bf16), vw_ref[pl.ds(w0, wk), :]) / jnp.sum(ew, axis=1, keepdims=True)

    s = _dot_nt(qs, kc)
    valid = (CMP_STRIDE * lax.broadcasted_iota(jnp.int32, (tq, n_cmp), 1) + (CMP_BLOCK - 1)
             <= t0 + lax.broadcasted_iota(jnp.int32, (tq, n_cmp), 0))
    s = jnp.concatenate([jnp.where(valid, s[r * tq:(r + 1) * tq], NEG) for r in range(R)], axis=0)
    e = jnp.exp2(s - jnp.max(s, axis=1, keepdims=True))
    p = e / jnp.sum(e, axis=1, keepdims=True)
    p = jnp.concatenate([jnp.where(valid, p[r * tq:(r + 1) * tq], 0.0) for r in range(R)], axis=0)
    o_cmp = _dot(p.astype(bf16), vc)

    psum = p[0:tq]
    for r in range(1, R):
        psum = psum + p[r * tq:(r + 1) * tq]
    p_hi = psum.astype(bf16)
    p_lo = (psum - p_hi.astype(f32)).astype(bf16)
    sb = lax.broadcasted_iota(jnp.int32, (n_sel, n_cmp), 0) * SEL_BLOCK
    cb = lax.broadcasted_iota(jnp.int32, (n_sel, n_cmp), 1) * CMP_STRIDE
    ov_t = jnp.where((cb < sb + SEL_BLOCK) & (cb + CMP_BLOCK > sb), 1.0, 0.0).astype(bf16)
    imp_t = _dot_nt(ov_t, p_hi) + _dot_nt(ov_t, p_lo)

    jj = lax.broadcasted_iota(jnp.int32, (n_sel, tq), 0)
    cur = jnp.right_shift(t0 + lax.broadcasted_iota(jnp.int32, (n_sel, tq), 1), SEL_BLOCK.bit_length() - 1)
    forced = (jj == 0) | (jj == cur) | (jj == cur - 1)
    v = jnp.where(jj > cur, -jnp.inf, jnp.where(forced, jnp.inf, imp_t))
    sub = 8
    slabs = [v[k * sub:(k + 1) * sub] for k in range(n_sel // sub)]
    ranks = [jnp.zeros((sub, tq), f32) for _ in slabs]
    j_in = lax.broadcasted_iota(jnp.int32, (sub, tq), 0)
    for c in range(n_sel):
        v_c = v[c:c + 1, :]
        for k, v_k in enumerate(slabs):
            if k * sub > c:
                ahead = jnp.where(v_c >= v_k, 1.0, 0.0)
            elif (k + 1) * sub <= c:
                ahead = jnp.where(v_c > v_k, 1.0, 0.0)
            else:
                ahead = jnp.where(j_in > c - k * sub, jnp.where(v_c >= v_k, 1.0, 0.0),
                                  jnp.where(v_c > v_k, 1.0, 0.0))
            ranks[k] = ranks[k] + ahead
    rank = jnp.concatenate(ranks, axis=0)
    unsel_t = jnp.where(rank < float(min(N_SELECT, n_sel)), 0.0, 1.0)
    unsel_t = jnp.concatenate([unsel_t, jnp.zeros((LANES - n_sel, tq), f32)], axis=0)
    return o_win, o_cmp, unsel_t.T.astype(bf16)


def _gate_tiles(gt_ref, gexp_ref, g_id):
    R = HEADS_PER_GROUP
    gt_all = jax.nn.sigmoid(gt_ref[...])
    gt_all = jnp.where(g_id == 0, gt_all, pltpu.roll(gt_all, LANES - R * N_BRANCH, axis=1))
    g_hi = gt_all.astype(bf16)
    rest = gt_all - g_hi.astype(f32)
    g_mid = rest.astype(bf16)
    g_lo = (rest - g_mid.astype(f32)).astype(bf16)
    return _dot(jnp.concatenate([g_hi, g_mid, g_lo], axis=1), gexp_ref[...])


def _nsa_front_kernel(q_ref, kc_ref, vc_ref, kw_ref, vw_ref, gexp_ref, gt_ref, part_ref, unsel_ref, *, tq, n_sub, T):
    R = HEADS_PER_GROUP
    tk = n_sub * tq
    t0 = pl.program_id(2) * tk
    kc = kc_ref[0, 0]
    vc = vc_ref[0, 0]
    g_exp = _gate_tiles(gt_ref, gexp_ref, pl.program_id(1))
    for sb in range(n_sub):
        q = q_ref[sb * tq:(sb + 1) * tq, :]
        qs = jnp.concatenate([q[:, r * HEAD_DIM:(r + 1) * HEAD_DIM] for r in range(R)], axis=0)
        o_win, o_cmp, unsel = _nsa_front(qs, t0 + sb * tq, kc, vc, kw_ref, vw_ref, tq=tq, T=T)
        unsel_ref[sb * tq:(sb + 1) * tq, :] = unsel
        for r in range(R):
            rows = slice(r * tq, (r + 1) * tq)
            gw = [g_exp[sb * tq:(sb + 1) * tq, (r * N_BRANCH + br) * LANES:(r * N_BRANCH + br + 1) * LANES]
                  for br in (0, 2)]
            part_ref[sb * tq:(sb + 1) * tq, r * HEAD_DIM:(r + 1) * HEAD_DIM] = gw[0] * o_cmp[rows] + gw[1] * o_win[rows]


def _nsa_sel_kernel(q_ref, unsel_ref, ks_ref, vs_ref, eneg_ref, gexp_ref, gt_ref, part_ref, ca_ref, cb_ref,
                    o_ref, ca_out, cb_out, s_sc, mx_sc, ls_sc, acc_sc, *, tq, n_sub, T):
    ca_out[...] = ca_ref[...].astype(bf16)
    cb_out[...] = cb_ref[...].astype(bf16)
    R = HEADS_PER_GROUP
    M = R * tq
    tk = n_sub * tq
    i = pl.program_id(2)
    q_rows = []
    for sb in range(n_sub):
        q = q_ref[sb * tq:(sb + 1) * tq, :]
        qs = jnp.concatenate([q[:, r * HEAD_DIM:(r + 1) * HEAD_DIM] for r in range(R)], axis=0)
        unsel = unsel_ref[sb * tq:(sb + 1) * tq, :]
        q_rows.append(jnp.concatenate([qs, jnp.concatenate([unsel] * R, axis=0)], axis=1))
    q_aug = jnp.concatenate(q_rows, axis=0)

    n_chunk = tk // LANES
    mx_sc[...] = jnp.full(mx_sc.shape, NEG, f32)

    def lane_max(sc):
        part = sc[:, :LANES]
        for c in range(1, sc.shape[1] // LANES):
            part = jnp.maximum(part, sc[:, c * LANES:(c + 1) * LANES])
        return part

    def score_tiles(kt, n_tiles):
        k0 = pl.multiple_of(kt * tk, tk)
        width = n_tiles * tk
        k_aug = jnp.concatenate([ks_ref[pl.ds(k0, width), :], eneg_ref[pl.ds(k0, width), :]], axis=1)
        sc = _dot_nt(q_aug, k_aug)
        for j in range(n_tiles):
            s_sc[kt + j] = sc[:, j * tk:(j + 1) * tk]
        mx_sc[...] = jnp.maximum(mx_sc[...], lane_max(sc))

    def score_pair(kp, carry):
        score_tiles(2 * kp, 2)
        return carry

    lax.fori_loop(0, i // 2, score_pair, 0)

    @pl.when(i % 2 == 1)
    def _():
        score_tiles(i - 1, 1)


    kd = pl.multiple_of(i * tk, tk)
    tri = jnp.where(lax.broadcasted_iota(jnp.int32, (tq, tq), 0) >= lax.broadcasted_iota(jnp.int32, (tq, tq), 1),
                    0.0, NEG)
    for sb in range(n_sub):
        width = (sb + 1) * tq
        rows = slice(sb * M, (sb + 1) * M)
        k_aug = jnp.concatenate([ks_ref[pl.ds(kd, width), :], eneg_ref[pl.ds(kd, width), :]], axis=1)
        sc = _dot_nt(q_aug[rows], k_aug)
        last = jnp.concatenate([sc[r * tq:(r + 1) * tq, sb * tq:] + tri for r in range(R)], axis=0)
        sc = last if sb == 0 else jnp.concatenate([sc[:, :sb * tq], last], axis=1)
        s_sc[i, rows, :width] = sc
        mx_sc[rows, :] = jnp.maximum(mx_sc[rows, :], lane_max(sc))
    mx_sc[...] = jnp.broadcast_to(jnp.max(mx_sc[...], axis=1, keepdims=True), mx_sc.shape)
    ls_sc[...] = jnp.zeros(ls_sc.shape, f32)
    acc_sc[...] = jnp.zeros(acc_sc.shape, f32)

    def pv_body(kt, carry):
        k0 = pl.multiple_of(kt * tk, tk)
        sc = s_sc[kt]
        mb = mx_sc[...]
        ps = [jnp.exp2(sc[:, c * LANES:(c + 1) * LANES] - mb) for c in range(n_chunk)]
        ls_sc[...] += functools.reduce(lambda a, b: a + b, ps)
        acc_sc[...] += _dot(jnp.concatenate(ps, axis=1).astype(bf16), vs_ref[pl.ds(k0, tk), :])
        return carry

    lax.fori_loop(0, i, pv_body, 0)
    for sb in range(n_sub):
        width = (sb + 1) * tq
        rows = slice(sb * M, (sb + 1) * M)
        sc = s_sc[i, rows, :width]
        mb = mx_sc[rows, :]
        ps = [jnp.exp2(sc[:, c * LANES:(c + 1) * LANES] - mb) for c in range(width // LANES)]
        ls_sc[rows, :] += functools.reduce(lambda a, b: a + b, ps)
        acc_sc[rows, :] += _dot(jnp.concatenate(ps, axis=1).astype(bf16), vs_ref[pl.ds(kd, width), :])
    o_sel = acc_sc[...] / jnp.sum(ls_sc[...], axis=1, keepdims=True)

    g_exp = _gate_tiles(gt_ref, gexp_ref, pl.program_id(1))
    for sb in range(n_sub):
        for r in range(R):
            srows = slice(sb * M + r * tq, sb * M + (r + 1) * tq)
            g_sel = g_exp[sb * tq:(sb + 1) * tq, (r * N_BRANCH + 1) * LANES:(r * N_BRANCH + 2) * LANES]
            cols = slice(r * HEAD_DIM, (r + 1) * HEAD_DIM)
            o = part_ref[sb * tq:(sb + 1) * tq, cols] + g_sel * o_sel[srows]
            o_ref[sb * tq:(sb + 1) * tq, cols] = o.astype(bf16)


def _nsa(q, kcmp, vcmp, kv, eneg, gexp, gates, cast_a, cast_b, *, B, T, tq, n_sub, n_sub_front):
    G, R = N_KV_GROUPS, HEADS_PER_GROUP
    n_cmp = kcmp.shape[2]
    assert tq % LANES == 0 and tq & (tq - 1) == 0 and T >= WINDOW + tq
    assert WINDOW % tq == 0 and WINDOW & (WINDOW - 1) == 0 and (R * tq) & (R * tq - 1) == 0
    cmp_spec = pl.BlockSpec((1, 1, n_cmp, HEAD_DIM), lambda b, g, i: (b, g, 0, 0))
    kvspec = lambda sec: pl.BlockSpec((T, HEAD_DIM), lambda b, g, i: (b, sec * G + g))
    const2 = lambda b, g, i: (0, 0)

    tf_ = n_sub_front * tq
    ntf = T // tf_
    assert T % tf_ == 0
    rowf = lambda b, g, i: (b * ntf + i, g)
    part, unsel = pl.pallas_call(
        functools.partial(_nsa_front_kernel, tq=tq, n_sub=n_sub_front, T=T),
        grid=(B, G, ntf),
        in_specs=[
            pl.BlockSpec((tf_, R * HEAD_DIM), rowf),
            cmp_spec, cmp_spec, kvspec(2), kvspec(3),
            pl.BlockSpec(gexp.shape, const2),
            pl.BlockSpec((tf_, LANES), lambda b, g, i: (b * ntf + i, 0)),
        ],
        out_specs=[pl.BlockSpec((tf_, R * HEAD_DIM), rowf), pl.BlockSpec((tf_, LANES), rowf)],
        out_shape=[jax.ShapeDtypeStruct((B * T, G * R * HEAD_DIM), f32),
                   jax.ShapeDtypeStruct((B * T, G * LANES), bf16)],
        compiler_params=pltpu.CompilerParams(
            dimension_semantics=("arbitrary", "arbitrary", "arbitrary"), vmem_limit_bytes=VMEM_LIMIT),
        name="nsa_front",
    )(q, kcmp, vcmp, kv, kv, gexp, gates)

    tk = n_sub * tq
    nt = T // tk
    assert T % tk == 0
    rowblk = lambda b, g, i: (b * nt + i, g)
    n_steps = B * G * nt
    cast_spec = lambda w: pl.BlockSpec((w.shape[0] // n_steps, w.shape[1]), lambda b, g, i: ((b * G + g) * nt + i, 0))
    return pl.pallas_call(
        functools.partial(_nsa_sel_kernel, tq=tq, n_sub=n_sub, T=T),
        grid=(B, G, nt),
        in_specs=[
            pl.BlockSpec((tk, R * HEAD_DIM), rowblk),
            pl.BlockSpec((tk, LANES), rowblk),
            kvspec(0), kvspec(1),
            pl.BlockSpec((T, LANES), const2),
            pl.BlockSpec(gexp.shape, const2),
            pl.BlockSpec((tk, LANES), lambda b, g, i: (b * nt + i, 0)),
            pl.BlockSpec((tk, R * HEAD_DIM), rowblk),
            cast_spec(cast_a), cast_spec(cast_b),
        ],
        out_specs=[pl.BlockSpec((tk, R * HEAD_DIM), rowblk), cast_spec(cast_a), cast_spec(cast_b)],
        out_shape=[jax.ShapeDtypeStruct((B * T, G * R * HEAD_DIM), bf16), jax.ShapeDtypeStruct(cast_a.shape, bf16),
                   jax.ShapeDtypeStruct(cast_b.shape, bf16)],
        scratch_shapes=[
            pltpu.VMEM((T // tk, n_sub * R * tq, tk), f32),
            pltpu.VMEM((n_sub * R * tq, LANES), f32),
            pltpu.VMEM((n_sub * R * tq, LANES), f32),
            pltpu.VMEM((n_sub * R * tq, HEAD_DIM), f32),
        ],
        compiler_params=pltpu.CompilerParams(
            dimension_semantics=("arbitrary", "arbitrary", "arbitrary"), vmem_limit_bytes=VMEM_LIMIT),
        name="nsa_sel",
    )(q, unsel, kv, kv, eneg, gexp, gates, part, cast_a, cast_b)


def _out_proj_kernel(x_ref, u_ref, uh_ref, y_ref, pw_ref, ps_ref, wo_ref, o_ref, *, tm, T):
    i = pl.program_id(0)
    tiles_per_seq = T // tm
    keep_halo = jnp.where((i % tiles_per_seq) == 0, 0.0, 1.0)
    u = u_ref[...]
    halo = uh_ref[...].astype(f32) * keep_halo
    tseq = (i % tiles_per_seq) * tm + lax.broadcasted_iota(jnp.int32, (tm, 1), 0)
    n_pool = u.shape[1]
    gd = n_pool // len(POOL_WINDOWS)
    acc = x_ref[...] + _dot(y_ref[...], wo_ref[n_pool:, :])
    for gi, w in enumerate(POOL_WINDOWS):
        assert w <= POOL_HALO and w & (w - 1) == 0
        cols = slice(gi * gd, (gi + 1) * gd)
        ug = u[:, cols].astype(f32)
        ws = jnp.concatenate([halo[:, cols], ug], axis=0)
        k = 1
        while k < w:
            ws = ws + pltpu.roll(ws, k, axis=0)
            k *= 2
        win_sum = ws[POOL_HALO:]
        count = jnp.minimum(tseq + 1, w).astype(f32)
        pooled = win_sum / count - ug
        yg = _dot(pooled.astype(bf16), pw_ref[gi].astype(bf16)) * ps_ref[:, cols]
        acc = acc + _dot(yg.astype(bf16), wo_ref[cols, :])
    o_ref[...] = acc


def _out_proj(x2, u, y_nsa, pool_w, pool_scale, w_out, *, T, tm):
    n_rows, d = x2.shape
    n_pool = u.shape[1]
    hb = tm // POOL_HALO
    row = lambda i: (i, 0)
    const2 = lambda i: (0, 0)
    return pl.pallas_call(
        functools.partial(_out_proj_kernel, tm=tm, T=T),
        grid=(n_rows // tm,),
        in_specs=[
            pl.BlockSpec((tm, d), row),
            pl.BlockSpec((tm, n_pool), row),
            pl.BlockSpec((POOL_HALO, n_pool), lambda i: (jnp.maximum(i * hb - 1, 0), 0)),
            pl.BlockSpec((tm, y_nsa.shape[1]), row),
            pl.BlockSpec(pool_w.shape, lambda i: (0, 0, 0)),
            pl.BlockSpec((1, n_pool), const2),
            pl.BlockSpec(w_out.shape, const2, pipeline_mode=pl.Buffered(1)),
        ],
        out_specs=pl.BlockSpec((tm, d), row),
        out_shape=jax.ShapeDtypeStruct((n_rows, d), f32),
        compiler_params=pltpu.CompilerParams(
            dimension_semantics=("arbitrary",), vmem_limit_bytes=VMEM_LIMIT),
        name="out_proj",
    )(x2, u, u, y_nsa, pool_w, pool_scale, w_out)


def _ffn_kernel(h_ref, g_ref, wg_ref, wu_ref, wd_ref, o_ref, n_sc):
    @pl.when(pl.program_id(1) == 0)
    def _():
        h = h_ref[...]
        n_sc[...] = _rms(h, g_ref[...]).astype(bf16)
        o_ref[...] = h

    n = n_sc[...]
    half = wg_ref.shape[1] // 2
    acc = None
    for c in range(2):
        cols = slice(c * half, (c + 1) * half)
        gate = _dot(n, wg_ref[:, cols])
        up = _dot(n, wu_ref[:, cols])
        a = gate * jax.nn.sigmoid(gate) * up
        d = _dot(a.astype(bf16), wd_ref[cols, :])
        acc = d if acc is None else acc + d
    o_ref[...] += acc


def _ffn(h, g, w_gate, w_up, w_down, *, tm, tf):
    n_rows, d = h.shape
    d_ff = w_gate.shape[1]
    return pl.pallas_call(
        _ffn_kernel,
        grid=(n_rows // tm, d_ff // tf),
        in_specs=[
            pl.BlockSpec((tm, d), lambda i, f: (i, 0)),
            pl.BlockSpec((1, d), lambda i, f: (0, 0)),
            pl.BlockSpec((d, tf), lambda i, f: (0, f)),
            pl.BlockSpec((d, tf), lambda i, f: (0, f)),
            pl.BlockSpec((tf, d), lambda i, f: (f, 0)),
        ],
        out_specs=pl.BlockSpec((tm, d), lambda i, f: (i, 0)),
        out_shape=jax.ShapeDtypeStruct((n_rows, d), f32),
        scratch_shapes=[pltpu.VMEM((tm, d), bf16)],
        compiler_params=pltpu.CompilerParams(
            dimension_semantics=("arbitrary", "arbitrary"), vmem_limit_bytes=VMEM_LIMIT),
        name="ffn",
    )(h, g, w_gate, w_up, w_down)


def _ple_kernel(h_ref, p_ref, g_ref, wg_ref, wp_ref, fg_ref, o_ref):
    h = h_ref[...]
    n = _rms(h, g_ref[...]).astype(bf16)
    gate = jax.nn.sigmoid(_dot(n, wg_ref[...]))
    h = h + _dot(p_ref[...].astype(bf16), wp_ref[...].astype(bf16)) * gate
    o_ref[...] = _rms(h, fg_ref[...])


def _ple(h, p2, g, w_gate, w_proj, fg, *, tm):
    n_rows, d = h.shape
    row = lambda i: (i, 0)
    const = lambda i: (0, 0)
    return pl.pallas_call(
        _ple_kernel,
        grid=(n_rows // tm,),
        in_specs=[
            pl.BlockSpec((tm, d), row),
            pl.BlockSpec((tm, p2.shape[1]), row),
            pl.BlockSpec((1, d), const),
            pl.BlockSpec(w_gate.shape, const, pipeline_mode=pl.Buffered(1)),
            pl.BlockSpec(w_proj.shape, const),
            pl.BlockSpec((1, d), const),
        ],
        out_specs=pl.BlockSpec((tm, d), row),
        out_shape=jax.ShapeDtypeStruct((n_rows, d), f32),
        compiler_params=pltpu.CompilerParams(
            dimension_semantics=("arbitrary",), vmem_limit_bytes=VMEM_LIMIT),
        name="ple",
    )(h, p2, g, w_gate, w_proj, fg)


def _rope_tables(T):
    pos = np.arange(T, dtype=np.float64)
    inv_freq = ROPE_THETA ** (-np.arange(0, ROPE_DIM, 2, dtype=np.float64) / ROPE_DIM)
    ang = pos[:, None] * inv_freq[None, :]
    cos, sin = np.cos(ang), np.sin(ang)
    rest = HEAD_DIM - ROPE_DIM
    cosf = np.concatenate([cos, cos, np.ones((T, rest))], axis=1)
    sinf = np.concatenate([-sin, sin, np.zeros((T, rest))], axis=1)
    return jnp.asarray(cosf, f32), jnp.asarray(sinf, f32)


def _block_bias_table(T):
    key_blk = np.arange(T)[:, None] // SEL_BLOCK
    return jnp.asarray(np.where(key_blk == np.arange(LANES)[None, :], NEG, 0.0).astype(bf16))


def _gate_expansion_table():
    n_col = HEADS_PER_GROUP * N_BRANCH
    e = np.zeros((3, LANES, n_col, LANES), np.float32)
    for c in range(n_col):
        e[:, c, c, :] = 1.0
    return jnp.asarray(e.reshape(3 * LANES, n_col * LANES).astype(bf16))


def _mixer_ffn(h2, in_norm_g, w_in, pool_w, pool_scale, cmp_k_pe, cmp_k_w1, cmp_k_w2,
               cmp_v_pe, cmp_v_w1, cmp_v_w2, w_out, ffn_norm_g, w_gate, w_up, w_down, w_ple_gate, *, B, T):
    n_pool = pool_scale.shape[0]
    n_heads = N_KV_GROUPS * HEADS_PER_GROUP
    nsa_w = n_heads * HEAD_DIM
    kv_w = N_KV_GROUPS * HEAD_DIM
    assert w_in.shape[1] == n_pool + nsa_w + 6 * kv_w + n_heads * N_BRANCH

    o0 = n_pool + nsa_w + 6 * kv_w
    w_p = w_in.T
    w_gt = jnp.pad(w_p[o0:].astype(bf16), ((0, LANES - n_heads * N_BRANCH), (0, 0)))

    cosf, sinf = _rope_tables(T)
    u, q, kcvc, kv, gates, w_out_b, w_gate_b, w_ple_gate_b = _in_proj(
        h2, in_norm_g[None, :], w_p, w_gt, cosf, sinf, w_out, w_gate, w_ple_gate, T=T, tm=ROW_TILE,
        n_pool=n_pool, n_q=nsa_w)

    kcmp, vcmp = _compress(
        kcvc.reshape(B, T, kcvc.shape[1]),
        cmp_k_pe.reshape(1, -1), cmp_k_w1, cmp_k_w2, cmp_v_pe.reshape(1, -1), cmp_v_w1, cmp_v_w2,
        bb=COMPRESS_BATCH if B % COMPRESS_BATCH == 0 else 1)

    eneg = _block_bias_table(T)
    y_nsa, w_up_b, w_down_b = _nsa(q, kcmp, vcmp, kv, eneg, _gate_expansion_table(), gates, w_up, w_down, B=B, T=T,
                                   tq=NSA_Q_TILE, n_sub=NSA_SUB_TILES, n_sub_front=NSA_FRONT_SUB_TILES)

    h2 = _out_proj(h2, u, y_nsa, pool_w, pool_scale[None, :], w_out_b, T=T, tm=ROW_TILE)
    return _ffn(h2, ffn_norm_g[None, :], w_gate_b, w_up_b, w_down_b, tm=FFN_ROW_TILE, tf=FFN_FF_TILE), w_ple_gate_b


def kernel(x, p, in_norm_g, w_in, pool_w, pool_scale, cmp_k_pe, cmp_k_w1, cmp_k_w2, cmp_v_pe, cmp_v_w1,
           cmp_v_w2, w_out, ffn_norm_g, w_gate, w_up, w_down, ple_norm_g, w_ple_gate, w_ple_proj, final_norm_g):
    B, T, d = x.shape
    depth = w_in.shape[0]
    assert depth == 1, "the final rmsnorm is fused into the last layer's per-layer-embedding kernel"
    h2 = x.reshape(B * T, d)
    i = 0
    h2, w_ple_gate_b = _mixer_ffn(
        h2, in_norm_g[i], w_in[i], pool_w[i], pool_scale[i], cmp_k_pe[i], cmp_k_w1[i], cmp_k_w2[i],
        cmp_v_pe[i], cmp_v_w1[i], cmp_v_w2[i], w_out[i], ffn_norm_g[i], w_gate[i], w_up[i], w_down[i],
        w_ple_gate[i], B=B, T=T)
    out = _ple(h2, p[i].reshape(B * T, -1), ple_norm_g[i][None, :], w_ple_gate_b,
               w_ple_proj[i], final_norm_g[None, :], tm=ROW_TILE)
    return out.reshape(B, T, d)
```

```python
import functools

import jax
import jax.numpy as jnp
import numpy as np
from jax import lax
from jax.experimental import pallas as pl
from jax.experimental.pallas import tpu as pltpu

EPS = 1e-6
NEG = -1e30
LANES = 128
POOL_WINDOWS = (2, 4, 8, 16)
POOL_HALO = 16
HEAD_DIM = 128
N_KV_GROUPS = 2
HEADS_PER_GROUP = 4
N_BRANCH = 3
CMP_BLOCK = 32
CMP_STRIDE = 16
SEL_BLOCK = 64
N_SELECT = 16
WINDOW = 512
ROPE_THETA = 500000.0
ROPE_DIM = HEAD_DIM // 4
ATTN_SCALE = HEAD_DIM ** -0.5
Q_SCALE = ATTN_SCALE * 1.4426950408889634
VMEM_LIMIT = 60 * 1024 * 1024
W_CHUNK = 64
W_SLOTS = 8
IN_SEC = 512
ROW_TILE = 512
FFN_ROW_TILE = 1024
FFN_FF_TILE = 512
NSA_Q_TILE = 256
NSA_SUB_TILES = 2
COMPRESS_BATCH = 2

f32 = jnp.float32
bf16 = jnp.bfloat16


def _dot(a, b):
    return jnp.dot(a, b, preferred_element_type=f32)


def _dot_nt(a, b):
    return lax.dot_general(a, b, (((1,), (1,)), ((), ())), preferred_element_type=f32)


def _rms(x, g):
    return x * lax.rsqrt(jnp.mean(x * x, axis=-1, keepdims=True) + EPS) * g


def _rope(h, cosf, sinf, lane):
    half = ROPE_DIM // 2
    partner = jnp.where(lane < half, pltpu.roll(h, LANES - half, axis=1), pltpu.roll(h, half, axis=1))
    return h * cosf + partner * sinf


def _w_chunk_copy(w_hbm, stage_sc, sem, c):
    slot = c % W_SLOTS
    return pltpu.make_async_copy(w_hbm.at[pl.ds(c * W_CHUNK, W_CHUNK), :], stage_sc.at[slot], sem.at[slot])


def _in_proj_kernel(x_ref, g_ref, w_hbm, wgt_ref, cos_ref, sin_ref, ca_ref, cb_ref, cc_ref,
                    u_ref, q_ref, kcvc_ref, kv_ref, gt_ref, ca_out, cb_out, cc_out,
                    w_ref, stage_sc, sem):
    @pl.when(pl.program_id(0) == 0)
    def _():
        n_chunk = w_ref.shape[0] // W_CHUNK
        for c in range(min(W_SLOTS - 1, n_chunk)):
            _w_chunk_copy(w_hbm, stage_sc, sem, c).start()
        for c in range(n_chunk):
            if c + W_SLOTS - 1 < n_chunk:
                _w_chunk_copy(w_hbm, stage_sc, sem, c + W_SLOTS - 1).start()
            _w_chunk_copy(w_hbm, stage_sc, sem, c).wait()
            w_ref[c * W_CHUNK:(c + 1) * W_CHUNK, :] = stage_sc[c % W_SLOTS].astype(bf16)

    def section(s):
        return w_ref[s * IN_SEC:(s + 1) * IN_SEC, :]

    ca_out[...] = ca_ref[...].astype(bf16)
    cb_out[...] = cb_ref[...].astype(bf16)
    cc_out[...] = cc_ref[...].astype(bf16)
    a = _rms(x_ref[...], g_ref[...]).astype(bf16)
    cosf = cos_ref[...]
    sinf = sin_ref[...]
    lane = lax.broadcasted_iota(jnp.int32, cosf.shape, 1)
    heads = IN_SEC // HEAD_DIM
    s = 0
    for c in range(0, u_ref.shape[1], IN_SEC):
        u_ref[:, c:c + IN_SEC] = _dot_nt(a, section(s)).astype(bf16)
        s += 1
    for c in range(0, q_ref.shape[1], IN_SEC):
        acc = _dot_nt(a, section(s))
        s += 1
        for j in range(heads):
            h = _rope(acc[:, j * HEAD_DIM:(j + 1) * HEAD_DIM], cosf, sinf, lane) * Q_SCALE
            q_ref[:, c + j * HEAD_DIM:c + (j + 1) * HEAD_DIM] = h.astype(bf16)
    for dst, base, dt in ((kcvc_ref, 0, f32), (kv_ref, 0, bf16), (kv_ref, IN_SEC, bf16)):
        acc = _dot_nt(a, section(s))
        s += 1
        for j in range(heads):
            h = acc[:, j * HEAD_DIM:(j + 1) * HEAD_DIM]
            if j < N_KV_GROUPS:
                h = _rope(h, cosf, sinf, lane)
            dst[:, base + j * HEAD_DIM:base + (j + 1) * HEAD_DIM] = h.astype(dt)
    gt_ref[...] = _dot_nt(a, wgt_ref[...])


def _in_proj(x2, g, w_p, w_gt, cosf, sinf, cast_a, cast_b, cast_c, *, T, tm, n_pool, n_q):
    n_rows, d = x2.shape
    assert n_pool % IN_SEC == 0 and n_q % IN_SEC == 0 and 2 * N_KV_GROUPS * HEAD_DIM == IN_SEC
    n_w = (w_p.shape[0] // W_CHUNK) * W_CHUNK
    assert w_p.dtype == f32 and n_w + w_gt.shape[0] >= w_p.shape[0] and n_w == n_pool + n_q + 3 * IN_SEC
    tiles_per_seq = T // tm
    n_steps = n_rows // tm
    row = lambda i: (i, 0)
    cast_spec = lambda w: pl.BlockSpec((w.shape[0] // n_steps, w.shape[1]), row)
    const = lambda i: (0, 0)
    tab = lambda i: (i % tiles_per_seq, 0)
    return pl.pallas_call(
        _in_proj_kernel,
        grid=(n_rows // tm,),
        in_specs=[
            pl.BlockSpec((tm, d), row),
            pl.BlockSpec((1, d), const),
            pl.BlockSpec(memory_space=pl.ANY),
            pl.BlockSpec((LANES, d), const),
            pl.BlockSpec((tm, LANES), tab),
            pl.BlockSpec((tm, LANES), tab),
            cast_spec(cast_a), cast_spec(cast_b), cast_spec(cast_c),
        ],
        out_specs=[
            pl.BlockSpec((tm, n_pool), row),
            pl.BlockSpec((tm, n_q), row),
            pl.BlockSpec((tm, IN_SEC), row),
            pl.BlockSpec((tm, 2 * IN_SEC), row),
            pl.BlockSpec((tm, LANES), row),
            cast_spec(cast_a), cast_spec(cast_b), cast_spec(cast_c),
        ],
        out_shape=[
            jax.ShapeDtypeStruct((n_rows, n_pool), bf16),
            jax.ShapeDtypeStruct((n_rows, n_q), bf16),
            jax.ShapeDtypeStruct((n_rows, IN_SEC), f32),
            jax.ShapeDtypeStruct((n_rows, 2 * IN_SEC), bf16),
            jax.ShapeDtypeStruct((n_rows, LANES), f32),
            jax.ShapeDtypeStruct(cast_a.shape, bf16),
            jax.ShapeDtypeStruct(cast_b.shape, bf16),
            jax.ShapeDtypeStruct(cast_c.shape, bf16),
        ],
        scratch_shapes=[
            pltpu.VMEM((n_w, d), bf16),
            pltpu.VMEM((W_SLOTS, W_CHUNK, d), f32),
            pltpu.SemaphoreType.DMA((W_SLOTS,)),
        ],
        compiler_params=pltpu.CompilerParams(
            dimension_semantics=("arbitrary",), vmem_limit_bytes=VMEM_LIMIT),
        name="in_proj",
    )(x2, g, w_p, w_gt, cosf, sinf, cast_a, cast_b, cast_c)


def _gelu_tanh(x):
    return 0.5 * x * (1.0 + jnp.tanh(0.7978845608028654 * (x + 0.044715 * (x * x * x))))


def _compress_kernel(kc0_ref, kc1_ref, vc0_ref, vc1_ref, pek_ref, w1k_ref, w2k_ref, pev_ref, w1v_ref, w2v_ref,
                     kc_ref, vc_ref):
    bb = kc0_ref.shape[0]
    n_rows = kc0_ref.shape[1] // CMP_STRIDE
    half_k = (CMP_BLOCK // 2) * HEAD_DIM
    n_stack = bb * N_KV_GROUPS * n_rows
    rowi = lax.broadcasted_iota(jnp.int32, (n_stack, HEAD_DIM), 0)
    for srcs, pe_ref, w1_ref, w2_ref, o_ref in (((kc0_ref, kc1_ref), pek_ref, w1k_ref, w2k_ref, kc_ref),
                                                ((vc0_ref, vc1_ref), pev_ref, w1v_ref, w2v_ref, vc_ref)):
        x = jnp.concatenate(
            [jnp.concatenate([src[b, pl.ds(l, n_rows, stride=CMP_STRIDE), :] for l in range(CMP_STRIDE)], axis=1)
             for b in range(bb) for src in srcs], axis=0)
        za = _dot((x + pe_ref[:, :half_k]).astype(bf16), w1_ref[:half_k, :].astype(bf16))
        zb = _dot((x + pe_ref[:, half_k:]).astype(bf16), w1_ref[half_k:, :].astype(bf16))
        h = za + pltpu.roll(zb, n_stack - 1, axis=0)
        o = _dot(_gelu_tanh(h).astype(bf16), w2_ref[...].astype(bf16))
        o = jnp.where((rowi & (n_rows - 1)) < n_rows - 1, o, 0.0).astype(bf16)
        for b in range(bb):
            for g in range(N_KV_GROUPS):
                r0 = (b * N_KV_GROUPS + g) * n_rows
                o_ref[b, g] = o[r0:r0 + n_rows]


def _compress(r3, pek, w1k, w2k, pev, w1v, w2v, *, bb):
    B, T, width = r3.shape
    n_rows = T // CMP_STRIDE
    assert width == 2 * N_KV_GROUPS * HEAD_DIM and N_KV_GROUPS == 2 and B % bb == 0 and n_rows & (n_rows - 1) == 0
    const = lambda b: (0, 0)
    out = jax.ShapeDtypeStruct((B, N_KV_GROUPS, n_rows, HEAD_DIM), bf16)
    ospec = pl.BlockSpec((bb, N_KV_GROUPS, n_rows, HEAD_DIM), lambda b: (b, 0, 0, 0))
    sec = lambda s: pl.BlockSpec((bb, T, HEAD_DIM), lambda b: (b, 0, s))
    return pl.pallas_call(
        _compress_kernel,
        grid=(B // bb,),
        in_specs=[
            sec(0), sec(1), sec(2), sec(3),
            pl.BlockSpec(pek.shape, const), pl.BlockSpec(w1k.shape, const), pl.BlockSpec(w2k.shape, const),
            pl.BlockSpec(pev.shape, const), pl.BlockSpec(w1v.shape, const), pl.BlockSpec(w2v.shape, const),
        ],
        out_specs=[ospec, ospec],
        out_shape=[out, out],
        compiler_params=pltpu.CompilerParams(
            dimension_semantics=("arbitrary",), vmem_limit_bytes=VMEM_LIMIT),
        name="compress",
    )(r3, r3, r3, r3, pek, w1k, w2k, pev, w1v, w2v)


def _nsa_front(qs, t0, kc, vc, kw_ref, vw_ref, *, tq, T):
    R = HEADS_PER_GROUP
    n_cmp = kc.shape[0]
    n_sel = T // SEL_BLOCK

    wk = WINDOW + tq
    w0 = pl.multiple_of(jnp.maximum(t0 - WINDOW, 0), tq)
    sw = _dot_nt(qs, kw_ref[pl.ds(w0, wk), :])
    diff = (t0 - w0) + lax.broadcasted_iota(jnp.int32, (tq, wk), 0) - lax.broadcasted_iota(jnp.int32, (tq, wk), 1)
    wbias = jnp.where((diff & -WINDOW) == 0, 0.0, NEG)
    sw = jnp.concatenate([sw[r * tq:(r + 1) * tq] + wbias for r in range(R)], axis=0)
    ew = jnp.exp2(sw - jnp.max(sw, axis=1, keepdims=True))
    o_win = _dot(ew.astype(bf16), vw_ref[pl.ds(w0, wk), :]) / jnp.sum(ew, axis=1, keepdims=True)

    s = _dot_nt(qs, kc)
    valid = (CMP_STRIDE * lax.broadcasted_iota(jnp.int32, (tq, n_cmp), 1) + (CMP_BLOCK - 1)
             <= t0 + lax.broadcasted_iota(jnp.int32, (tq, n_cmp), 0))
    s = jnp.concatenate([jnp.where(valid, s[r * tq:(r + 1) * tq], NEG) for r in range(R)], axis=0)
    e = jnp.exp2(s - jnp.max(s, axis=1, keepdims=True))
    p = e / jnp.sum(e, axis=1, keepdims=True)
    p = jnp.concatenate([jnp.where(valid, p[r * tq:(r + 1) * tq], 0.0) for r in range(R)], axis=0)
    o_cmp = _dot(p.astype(bf16), vc)

    psum = p[0:tq]
    for r in range(1, R):
        psum = psum + p[r * tq:(r + 1) * tq]
    p_hi = psum.astype(bf16)
    p_lo = (psum - p_hi.astype(f32)).astype(bf16)
    sb = lax.broadcasted_iota(jnp.int32, (n_sel, n_cmp), 0) * SEL_BLOCK
    cb = lax.broadcasted_iota(jnp.int32, (n_sel, n_cmp), 1) * CMP_STRIDE
    ov_t = jnp.where((cb < sb + SEL_BLOCK) & (cb + CMP_BLOCK > sb), 1.0, 0.0).astype(bf16)
    imp_t = _dot_nt(ov_t, p_hi) + _dot_nt(ov_t, p_lo)

    jj = lax.broadcasted_iota(jnp.int32, (n_sel, tq), 0)
    cur = jnp.right_shift(t0 + lax.broadcasted_iota(jnp.int32, (n_sel, tq), 1), SEL_BLOCK.bit_length() - 1)
    forced = (jj == 0) | (jj == cur) | (jj == cur - 1)
    v = jnp.where(jj > cur, -jnp.inf, jnp.where(forced, jnp.inf, imp_t))
    sub = 8
    slabs = [v[k * sub:(k + 1) * sub] for k in range(n_sel // sub)]
    ranks = [jnp.zeros((sub, tq), f32) for _ in slabs]
    j_in = lax.broadcasted_iota(jnp.int32, (sub, tq), 0)
    for c in range(n_sel):
        v_c = v[c:c + 1, :]
        for k, v_k in enumerate(slabs):
            if k * sub > c:
                ahead = jnp.where(v_c >= v_k, 1.0, 0.0)
            elif (k + 1) * sub <= c:
                ahead = jnp.where(v_c > v_k, 1.0, 0.0)
            else:
                ahead = jnp.where(j_in > c - k * sub, jnp.where(v_c >= v_k, 1.0, 0.0),
                                  jnp.where(v_c > v_k, 1.0, 0.0))
            ranks[k] = ranks[k] + ahead
    rank = jnp.concatenate(ranks, axis=0)
    unsel_t = jnp.where(rank < float(min(N_SELECT, n_sel)), 0.0, 1.0)
    unsel_t = jnp.concatenate([unsel_t, jnp.zeros((LANES - n_sel, tq), f32)], axis=0)
    return o_win, o_cmp, unsel_t.T.astype(bf16)


def _nsa_kernel(q_ref, kc_ref, vc_ref, ks_ref, vs_ref, kw_ref, vw_ref, eneg_ref, gexp_ref, gt_ref, ca_ref, cb_ref,
                o_ref, ca_out, cb_out, s_sc, mx_sc, ls_sc, acc_sc, *, tq, n_sub, T):
    ca_out[...] = ca_ref[...].astype(bf16)
    cb_out[...] = cb_ref[...].astype(bf16)
    R = HEADS_PER_GROUP
    M = R * tq
    tk = n_sub * tq
    i = pl.program_id(2)
    t0 = i * tk
    kc = kc_ref[0, 0]
    vc = vc_ref[0, 0]

    fronts = []
    q_rows = []
    for sb in range(n_sub):
        q = q_ref[sb * tq:(sb + 1) * tq, :]
        qs = jnp.concatenate([q[:, r * HEAD_DIM:(r + 1) * HEAD_DIM] for r in range(R)], axis=0)
        o_win, o_cmp, unsel = _nsa_front(qs, t0 + sb * tq, kc, vc, kw_ref, vw_ref, tq=tq, T=T)
        fronts.append((o_win, o_cmp))
        q_rows.append(jnp.concatenate([qs, jnp.concatenate([unsel] * R, axis=0)], axis=1))
    q_aug = jnp.concatenate(q_rows, axis=0)

    n_chunk = tk // LANES
    mx_sc[...] = jnp.full(mx_sc.shape, NEG, f32)

    def lane_max(sc):
        part = sc[:, :LANES]
        for c in range(1, sc.shape[1] // LANES):
            part = jnp.maximum(part, sc[:, c * LANES:(c + 1) * LANES])
        return part

    def score_tiles(kt, n_tiles):
        k0 = pl.multiple_of(kt * tk, tk)
        width = n_tiles * tk
        k_aug = jnp.concatenate([ks_ref[pl.ds(k0, width), :], eneg_ref[pl.ds(k0, width), :]], axis=1)
        sc = _dot_nt(q_aug, k_aug)
        for j in range(n_tiles):
            s_sc[kt + j] = sc[:, j * tk:(j + 1) * tk]
        mx_sc[...] = jnp.maximum(mx_sc[...], lane_max(sc))

    def score_pair(kp, carry):
        score_tiles(2 * kp, 2)
        return carry

    lax.fori_loop(0, i // 2, score_pair, 0)

    @pl.when(i % 2 == 1)
    def _():
        score_tiles(i - 1, 1)


    kd = pl.multiple_of(i * tk, tk)
    tri = jnp.where(lax.broadcasted_iota(jnp.int32, (tq, tq), 0) >= lax.broadcasted_iota(jnp.int32, (tq, tq), 1),
                    0.0, NEG)
    for sb in range(n_sub):
        width = (sb + 1) * tq
        rows = slice(sb * M, (sb + 1) * M)
        k_aug = jnp.concatenate([ks_ref[pl.ds(kd, width), :], eneg_ref[pl.ds(kd, width), :]], axis=1)
        sc = _dot_nt(q_aug[rows], k_aug)
        last = jnp.concatenate([sc[r * tq:(r + 1) * tq, sb * tq:] + tri for r in range(R)], axis=0)
        sc = last if sb == 0 else jnp.concatenate([sc[:, :sb * tq], last], axis=1)
        s_sc[i, rows, :width] = sc
        mx_sc[rows, :] = jnp.maximum(mx_sc[rows, :], lane_max(sc))
    mx_sc[...] = jnp.broadcast_to(jnp.max(mx_sc[...], axis=1, keepdims=True), mx_sc.shape)
    ls_sc[...] = jnp.zeros(ls_sc.shape, f32)
    acc_sc[...] = jnp.zeros(acc_sc.shape, f32)

    def pv_body(kt, carry):
        k0 = pl.multiple_of(kt * tk, tk)
        sc = s_sc[kt]
        mb = mx_sc[...]
        ps = [jnp.exp2(sc[:, c * LANES:(c + 1) * LANES] - mb) for c in range(n_chunk)]
        ls_sc[...] += functools.reduce(lambda a, b: a + b, ps)
        acc_sc[...] += _dot(jnp.concatenate(ps, axis=1).astype(bf16), vs_ref[pl.ds(k0, tk), :])
        return carry

    lax.fori_loop(0, i, pv_body, 0)
    for sb in range(n_sub):
        width = (sb + 1) * tq
        rows = slice(sb * M, (sb + 1) * M)
        sc = s_sc[i, rows, :width]
        mb = mx_sc[rows, :]
        ps = [jnp.exp2(sc[:, c * LANES:(c + 1) * LANES] - mb) for c in range(width // LANES)]
        ls_sc[rows, :] += functools.reduce(lambda a, b: a + b, ps)
        acc_sc[rows, :] += _dot(jnp.concatenate(ps, axis=1).astype(bf16), vs_ref[pl.ds(kd, width), :])
    o_sel = acc_sc[...] / jnp.sum(ls_sc[...], axis=1, keepdims=True)

    gt_all = jax.nn.sigmoid(gt_ref[...])
    gt_all = jnp.where(pl.program_id(1) == 0, gt_all, pltpu.roll(gt_all, LANES - R * N_BRANCH, axis=1))
    g_hi = gt_all.astype(bf16)
    rest = gt_all - g_hi.astype(f32)
    g_mid = rest.astype(bf16)
    g_lo = (rest - g_mid.astype(f32)).astype(bf16)
    g_exp = _dot(jnp.concatenate([g_hi, g_mid, g_lo], axis=1), gexp_ref[...])
    for sb, (o_win, o_cmp) in enumerate(fronts):
        for r in range(R):
            rows = slice(r * tq, (r + 1) * tq)
            srows = slice(sb * M + r * tq, sb * M + (r + 1) * tq)
            gates = [g_exp[sb * tq:(sb + 1) * tq, (r * N_BRANCH + br) * LANES:(r * N_BRANCH + br + 1) * LANES]
                     for br in range(N_BRANCH)]
            o = gates[0] * o_cmp[rows] + gates[1] * o_sel[srows] + gates[2] * o_win[rows]
            o_ref[sb * tq:(sb + 1) * tq, r * HEAD_DIM:(r + 1) * HEAD_DIM] = o.astype(bf16)


def _nsa(q, kcmp, vcmp, kv, eneg, gexp, gates, cast_a, cast_b, *, B, T, tq, n_sub):
    G, R = N_KV_GROUPS, HEADS_PER_GROUP
    tk = n_sub * tq
    nt = T // tk
    n_cmp = kcmp.shape[2]
    assert tq % LANES == 0 and tq & (tq - 1) == 0 and T % tk == 0 and T >= WINDOW + tq
    assert WINDOW % tq == 0 and WINDOW & (WINDOW - 1) == 0 and (R * tq) & (R * tq - 1) == 0
    rowblk = lambda b, g, i: (b * nt + i, g)
    cmp_spec = pl.BlockSpec((1, 1, n_cmp, HEAD_DIM), lambda b, g, i: (b, g, 0, 0))
    kvspec = lambda sec: pl.BlockSpec((T, HEAD_DIM), lambda b, g, i: (b, sec * G + g))
    n_steps = B * G * nt
    cast_spec = lambda w: pl.BlockSpec((w.shape[0] // n_steps, w.shape[1]), lambda b, g, i: ((b * G + g) * nt + i, 0))
    return pl.pallas_call(
        functools.partial(_nsa_kernel, tq=tq, n_sub=n_sub, T=T),
        grid=(B, G, nt),
        in_specs=[
            pl.BlockSpec((tk, R * HEAD_DIM), rowblk),
            cmp_spec, cmp_spec,
            kvspec(0), kvspec(1), kvspec(2), kvspec(3),
            pl.BlockSpec((T, LANES), lambda b, g, i: (0, 0)),
            pl.BlockSpec(gexp.shape, lambda b, g, i: (0, 0)),
            pl.BlockSpec((tk, LANES), lambda b, g, i: (b * nt + i, 0)),
            cast_spec(cast_a), cast_spec(cast_b),
        ],
        out_specs=[pl.BlockSpec((tk, R * HEAD_DIM), rowblk), cast_spec(cast_a), cast_spec(cast_b)],
        out_shape=[jax.ShapeDtypeStruct((B * T, G * R * HEAD_DIM), bf16), jax.ShapeDtypeStruct(cast_a.shape, bf16),
                   jax.ShapeDtypeStruct(cast_b.shape, bf16)],
        scratch_shapes=[
            pltpu.VMEM((T // tk, n_sub * R * tq, tk), f32),
            pltpu.VMEM((n_sub * R * tq, LANES), f32),
            pltpu.VMEM((n_sub * R * tq, LANES), f32),
            pltpu.VMEM((n_sub * R * tq, HEAD_DIM), f32),
        ],
        compiler_params=pltpu.CompilerParams(
            dimension_semantics=("arbitrary", "arbitrary", "arbitrary"), vmem_limit_bytes=VMEM_LIMIT),
        name="nsa",
    )(q, kcmp, vcmp, kv, kv, kv, kv, eneg, gexp, gates, cast_a, cast_b)


def _out_proj_kernel(x_ref, u_ref, uh_ref, y_ref, pw_ref, ps_ref, wo_ref, g_ref, o_ref, n_ref, *, tm, T):
    i = pl.program_id(0)
    tiles_per_seq = T // tm
    keep_halo = jnp.where((i % tiles_per_seq) == 0, 0.0, 1.0)
    u = u_ref[...]
    halo = uh_ref[...].astype(f32) * keep_halo
    tseq = (i % tiles_per_seq) * tm + lax.broadcasted_iota(jnp.int32, (tm, 1), 0)
    n_pool = u.shape[1]
    gd = n_pool // len(POOL_WINDOWS)
    acc = x_ref[...] + _dot(y_ref[...], wo_ref[n_pool:, :])
    for gi, w in enumerate(POOL_WINDOWS):
        assert w <= POOL_HALO and w & (w - 1) == 0
        cols = slice(gi * gd, (gi + 1) * gd)
        ug = u[:, cols].astype(f32)
        ws = jnp.concatenate([halo[:, cols], ug], axis=0)
        k = 1
        while k < w:
            ws = ws + pltpu.roll(ws, k, axis=0)
            k *= 2
        win_sum = ws[POOL_HALO:]
        count = jnp.minimum(tseq + 1, w).astype(f32)
        pooled = win_sum / count - ug
        yg = _dot(pooled.astype(bf16), pw_ref[gi].astype(bf16)) * ps_ref[:, cols]
        acc = acc + _dot(yg.astype(bf16), wo_ref[cols, :])
    o_ref[...] = acc
    n_ref[...] = _rms(acc, g_ref[...]).astype(bf16)


def _out_proj(x2, u, y_nsa, pool_w, pool_scale, w_out, g_next, *, T, tm):
    n_rows, d = x2.shape
    n_pool = u.shape[1]
    hb = tm // POOL_HALO
    row = lambda i: (i, 0)
    const2 = lambda i: (0, 0)
    return pl.pallas_call(
        functools.partial(_out_proj_kernel, tm=tm, T=T),
        grid=(n_rows // tm,),
        in_specs=[
            pl.BlockSpec((tm, d), row),
            pl.BlockSpec((tm, n_pool), row),
            pl.BlockSpec((POOL_HALO, n_pool), lambda i: (jnp.maximum(i * hb - 1, 0), 0)),
            pl.BlockSpec((tm, y_nsa.shape[1]), row),
            pl.BlockSpec(pool_w.shape, lambda i: (0, 0, 0)),
            pl.BlockSpec((1, n_pool), const2),
            pl.BlockSpec(w_out.shape, const2, pipeline_mode=pl.Buffered(1)),
            pl.BlockSpec((1, d), const2),
        ],
        out_specs=[pl.BlockSpec((tm, d), row), pl.BlockSpec((tm, d), row)],
        out_shape=[jax.ShapeDtypeStruct((n_rows, d), f32), jax.ShapeDtypeStruct((n_rows, d), bf16)],
        compiler_params=pltpu.CompilerParams(
            dimension_semantics=("arbitrary",), vmem_limit_bytes=VMEM_LIMIT),
        name="out_proj",
    )(x2, u, u, y_nsa, pool_w, pool_scale, w_out, g_next)


def _ffn_kernel(h_hbm, n_ref, wg_ref, wu_ref, wd_ref, o_hbm, acc_sc, sem, *, tm):
    i = pl.program_id(0)
    f = pl.program_id(1)
    n_i = pl.num_programs(0)
    n_f = pl.num_programs(1)

    def rows(t):
        return pl.ds(pl.multiple_of(t * tm, tm), tm)

    def load(t):
        return pltpu.make_async_copy(h_hbm.at[rows(t), :], acc_sc.at[t % 2], sem.at[0])

    def store(t):
        return pltpu.make_async_copy(acc_sc.at[t % 2], o_hbm.at[rows(t), :], sem.at[1])

    @pl.when((i == 0) & (f == 0))
    def _():
        load(i).start()

    @pl.when(f == 0)
    def _():
        load(i).wait()

    @pl.when((f == 0) & (i > 0))
    def _():
        store(i - 1).start()

    @pl.when((f == 1) & (i > 0))
    def _():
        store(i - 1).wait()

    @pl.when((f == 1) & (i + 1 < n_i))
    def _():
        load(i + 1).start()

    n = n_ref[...]
    half = wg_ref.shape[1] // 2
    acc = None
    for c in range(2):
        cols = slice(c * half, (c + 1) * half)
        gate = _dot(n, wg_ref[:, cols])
        up = _dot(n, wu_ref[:, cols])
        a = gate * jax.nn.sigmoid(gate) * up
        d = _dot(a.astype(bf16), wd_ref[cols, :])
        acc = d if acc is None else acc + d
    acc_sc[i % 2] += acc

    @pl.when((i == n_i - 1) & (f == n_f - 1))
    def _():
        store(i).start()
        store(i).wait()


def _ffn(h, n, w_gate, w_up, w_down, *, tm, tf):
    n_rows, d = h.shape
    d_ff = w_gate.shape[1]
    assert n_rows % tm == 0 and d_ff // tf >= 2
    return pl.pallas_call(
        functools.partial(_ffn_kernel, tm=tm),
        grid=(n_rows // tm, d_ff // tf),
        in_specs=[
            pl.BlockSpec(memory_space=pl.ANY),
            pl.BlockSpec((tm, d), lambda i, f: (i, 0)),
            pl.BlockSpec((d, tf), lambda i, f: (0, f)),
            pl.BlockSpec((d, tf), lambda i, f: (0, f)),
            pl.BlockSpec((tf, d), lambda i, f: (f, 0)),
        ],
        out_specs=pl.BlockSpec(memory_space=pl.ANY),
        out_shape=jax.ShapeDtypeStruct((n_rows, d), f32),
        scratch_shapes=[pltpu.VMEM((2, tm, d), f32), pltpu.SemaphoreType.DMA((2,))],
        compiler_params=pltpu.CompilerParams(
            dimension_semantics=("arbitrary", "arbitrary"), vmem_limit_bytes=VMEM_LIMIT),
        name="ffn",
    )(h, n, w_gate, w_up, w_down)


def _ple_kernel(h_ref, p_ref, g_ref, wg_ref, wp_ref, fg_ref, o_ref):
    h = h_ref[...]
    n = _rms(h, g_ref[...]).astype(bf16)
    gate = jax.nn.sigmoid(_dot(n, wg_ref[...]))
    h = h + _dot(p_ref[...].astype(bf16), wp_ref[...].astype(bf16)) * gate
    o_ref[...] = _rms(h, fg_ref[...])


def _ple(h, p2, g, w_gate, w_proj, fg, *, tm):
    n_rows, d = h.shape
    row = lambda i: (i, 0)
    const = lambda i: (0, 0)
    return pl.pallas_call(
        _ple_kernel,
        grid=(n_rows // tm,),
        in_specs=[
            pl.BlockSpec((tm, d), row),
            pl.BlockSpec((tm, p2.shape[1]), row),
            pl.BlockSpec((1, d), const),
            pl.BlockSpec(w_gate.shape, const, pipeline_mode=pl.Buffered(1)),
            pl.BlockSpec(w_proj.shape, const),
            pl.BlockSpec((1, d), const),
        ],
        out_specs=pl.BlockSpec((tm, d), row),
        out_shape=jax.ShapeDtypeStruct((n_rows, d), f32),
        compiler_params=pltpu.CompilerParams(
            dimension_semantics=("arbitrary",), vmem_limit_bytes=VMEM_LIMIT),
        name="ple",
    )(h, p2, g, w_gate, w_proj, fg)


def _rope_tables(T):
    pos = np.arange(T, dtype=np.float64)
    inv_freq = ROPE_THETA ** (-np.arange(0, ROPE_DIM, 2, dtype=np.float64) / ROPE_DIM)
    ang = pos[:, None] * inv_freq[None, :]
    cos, sin = np.cos(ang), np.sin(ang)
    rest = HEAD_DIM - ROPE_DIM
    cosf = np.concatenate([cos, cos, np.ones((T, rest))], axis=1)
    sinf = np.concatenate([-sin, sin, np.zeros((T, rest))], axis=1)
    return jnp.asarray(cosf, f32), jnp.asarray(sinf, f32)


def _block_bias_table(T):
    key_blk = np.arange(T)[:, None] // SEL_BLOCK
    return jnp.asarray(np.where(key_blk == np.arange(LANES)[None, :], NEG, 0.0).astype(bf16))


def _gate_expansion_table():
    n_col = HEADS_PER_GROUP * N_BRANCH
    e = np.zeros((3, LANES, n_col, LANES), np.float32)
    for c in range(n_col):
        e[:, c, c, :] = 1.0
    return jnp.asarray(e.reshape(3 * LANES, n_col * LANES).astype(bf16))


def _mixer_ffn(h2, in_norm_g, w_in, pool_w, pool_scale, cmp_k_pe, cmp_k_w1, cmp_k_w2,
               cmp_v_pe, cmp_v_w1, cmp_v_w2, w_out, ffn_norm_g, w_gate, w_up, w_down, w_ple_gate, *, B, T):
    n_pool = pool_scale.shape[0]
    n_heads = N_KV_GROUPS * HEADS_PER_GROUP
    nsa_w = n_heads * HEAD_DIM
    kv_w = N_KV_GROUPS * HEAD_DIM
    assert w_in.shape[1] == n_pool + nsa_w + 6 * kv_w + n_heads * N_BRANCH

    o0 = n_pool + nsa_w + 6 * kv_w
    w_p = w_in.T
    w_gt = jnp.pad(w_p[o0:].astype(bf16), ((0, LANES - n_heads * N_BRANCH), (0, 0)))

    cosf, sinf = _rope_tables(T)
    u, q, kcvc, kv, gates, w_out_b, w_gate_b, w_ple_gate_b = _in_proj(
        h2, in_norm_g[None, :], w_p, w_gt, cosf, sinf, w_out, w_gate, w_ple_gate, T=T, tm=ROW_TILE,
        n_pool=n_pool, n_q=nsa_w)

    kcmp, vcmp = _compress(
        kcvc.reshape(B, T, kcvc.shape[1]),
        cmp_k_pe.reshape(1, -1), cmp_k_w1, cmp_k_w2, cmp_v_pe.reshape(1, -1), cmp_v_w1, cmp_v_w2,
        bb=COMPRESS_BATCH if B % COMPRESS_BATCH == 0 else 1)

    eneg = _block_bias_table(T)
    y_nsa, w_up_b, w_down_b = _nsa(q, kcmp, vcmp, kv, eneg, _gate_expansion_table(), gates, w_up, w_down, B=B, T=T,
                                   tq=NSA_Q_TILE, n_sub=NSA_SUB_TILES)

    h2, n2 = _out_proj(h2, u, y_nsa, pool_w, pool_scale[None, :], w_out_b, ffn_norm_g[None, :], T=T, tm=ROW_TILE)
    return _ffn(h2, n2, w_gate_b, w_up_b, w_down_b, tm=FFN_ROW_TILE, tf=FFN_FF_TILE), w_ple_gate_b


def kernel(x, p, in_norm_g, w_in, pool_w, pool_scale, cmp_k_pe, cmp_k_w1, cmp_k_w2, cmp_v_pe, cmp_v_w1,
           cmp_v_w2, w_out, ffn_norm_g, w_gate, w_up, w_down, ple_norm_g, w_ple_gate, w_ple_proj, final_norm_g):
    B, T, d = x.shape
    depth = w_in.shape[0]
    assert depth == 1, "the final rmsnorm is fused into the last layer's per-layer-embedding kernel"
    h2 = x.reshape(B * T, d)
    i = 0
    h2, w_ple_gate_b = _mixer_ffn(
        h2, in_norm_g[i], w_in[i], pool_w[i], pool_scale[i], cmp_k_pe[i], cmp_k_w1[i], cmp_k_w2[i],
        cmp_v_pe[i], cmp_v_w1[i], cmp_v_w2[i], w_out[i], ffn_norm_g[i], w_gate[i], w_up[i], w_down[i],
        w_ple_gate[i], B=B, T=T)
    out = _ple(h2, p[i].reshape(B * T, -1), ple_norm_g[i][None, :], w_ple_gate_b,
               w_ple_proj[i], final_norm_g[None, :], tm=ROW_TILE)
    return out.reshape(B, T, d)
```

```python
import functools

import jax
import jax.numpy as jnp
import numpy as np
from jax import lax
from jax.experimental import pallas as pl
from jax.experimental.pallas import tpu as pltpu

EPS = 1e-6
NEG = -1e30
LANES = 128
POOL_WINDOWS = (2, 4, 8, 16)
POOL_HALO = 16
HEAD_DIM = 128
N_KV_GROUPS = 2
HEADS_PER_GROUP = 4
N_BRANCH = 3
CMP_BLOCK = 32
CMP_STRIDE = 16
SEL_BLOCK = 64
N_SELECT = 16
WINDOW = 512
ROPE_THETA = 500000.0
ROPE_DIM = HEAD_DIM // 4
ATTN_SCALE = HEAD_DIM ** -0.5
Q_SCALE = ATTN_SCALE * 1.4426950408889634
VMEM_LIMIT = 60 * 1024 * 1024
FFN_VMEM_LIMIT = 63 * 1024 * 1024
W_CHUNK = 64
W_SLOTS = 8
IN_SEC = 512
ROW_TILE = 512
FFN_ROW_TILE = 1024
FFN_FF_TILE = 512
NSA_Q_TILE = 256
NSA_SUB_TILES = 2
COMPRESS_BATCH = 2

f32 = jnp.float32
bf16 = jnp.bfloat16


def _dot(a, b):
    return jnp.dot(a, b, preferred_element_type=f32)


def _dot_nt(a, b):
    return lax.dot_general(a, b, (((1,), (1,)), ((), ())), preferred_element_type=f32)


def _rms(x, g):
    return x * lax.rsqrt(jnp.mean(x * x, axis=-1, keepdims=True) + EPS) * g


def _rope(h, cosf, sinf, lane):
    half = ROPE_DIM // 2
    partner = jnp.where(lane < half, pltpu.roll(h, LANES - half, axis=1), pltpu.roll(h, half, axis=1))
    return h * cosf + partner * sinf


def _w_chunk_copy(w_hbm, stage_sc, sem, c):
    slot = c % W_SLOTS
    return pltpu.make_async_copy(w_hbm.at[pl.ds(c * W_CHUNK, W_CHUNK), :], stage_sc.at[slot], sem.at[slot])


def _in_proj_kernel(x_ref, g_ref, w_hbm, wgt_ref, cos_ref, sin_ref, ca_ref, cb_ref, cc_ref,
                    u_ref, q_ref, kcvc_ref, kv_ref, gt_ref, ca_out, cb_out, cc_out,
                    w_ref, stage_sc, sem):
    @pl.when(pl.program_id(0) == 0)
    def _():
        n_chunk = w_ref.shape[0] // W_CHUNK
        for c in range(min(W_SLOTS - 1, n_chunk)):
            _w_chunk_copy(w_hbm, stage_sc, sem, c).start()
        for c in range(n_chunk):
            if c + W_SLOTS - 1 < n_chunk:
                _w_chunk_copy(w_hbm, stage_sc, sem, c + W_SLOTS - 1).start()
            _w_chunk_copy(w_hbm, stage_sc, sem, c).wait()
            w_ref[c * W_CHUNK:(c + 1) * W_CHUNK, :] = stage_sc[c % W_SLOTS].astype(bf16)

    def section(s):
        return w_ref[s * IN_SEC:(s + 1) * IN_SEC, :]

    ca_out[...] = ca_ref[...].astype(bf16)
    cb_out[...] = cb_ref[...].astype(bf16)
    cc_out[...] = cc_ref[...].astype(bf16)
    a = _rms(x_ref[...], g_ref[...]).astype(bf16)
    cosf = cos_ref[...]
    sinf = sin_ref[...]
    lane = lax.broadcasted_iota(jnp.int32, cosf.shape, 1)
    heads = IN_SEC // HEAD_DIM
    s = 0
    for c in range(0, u_ref.shape[1], IN_SEC):
        u_ref[:, c:c + IN_SEC] = _dot_nt(a, section(s)).astype(bf16)
        s += 1
    for c in range(0, q_ref.shape[1], IN_SEC):
        acc = _dot_nt(a, section(s))
        s += 1
        for j in range(heads):
            h = _rope(acc[:, j * HEAD_DIM:(j + 1) * HEAD_DIM], cosf, sinf, lane) * Q_SCALE
            q_ref[:, c + j * HEAD_DIM:c + (j + 1) * HEAD_DIM] = h.astype(bf16)
    for dst, base, dt in ((kcvc_ref, 0, f32), (kv_ref, 0, bf16), (kv_ref, IN_SEC, bf16)):
        acc = _dot_nt(a, section(s))
        s += 1
        for j in range(heads):
            h = acc[:, j * HEAD_DIM:(j + 1) * HEAD_DIM]
            if j < N_KV_GROUPS:
                h = _rope(h, cosf, sinf, lane)
            dst[:, base + j * HEAD_DIM:base + (j + 1) * HEAD_DIM] = h.astype(dt)
    gt_ref[...] = _dot_nt(a, wgt_ref[...])


def _in_proj(x2, g, w_p, w_gt, cosf, sinf, cast_a, cast_b, cast_c, *, T, tm, n_pool, n_q):
    n_rows, d = x2.shape
    assert n_pool % IN_SEC == 0 and n_q % IN_SEC == 0 and 2 * N_KV_GROUPS * HEAD_DIM == IN_SEC
    n_w = (w_p.shape[0] // W_CHUNK) * W_CHUNK
    assert w_p.dtype == f32 and n_w + w_gt.shape[0] >= w_p.shape[0] and n_w == n_pool + n_q + 3 * IN_SEC
    tiles_per_seq = T // tm
    n_steps = n_rows // tm
    row = lambda i: (i, 0)
    cast_spec = lambda w: pl.BlockSpec((w.shape[0] // n_steps, w.shape[1]), row)
    const = lambda i: (0, 0)
    tab = lambda i: (i % tiles_per_seq, 0)
    return pl.pallas_call(
        _in_proj_kernel,
        grid=(n_rows // tm,),
        in_specs=[
            pl.BlockSpec((tm, d), row),
            pl.BlockSpec((1, d), const),
            pl.BlockSpec(memory_space=pl.ANY),
            pl.BlockSpec((LANES, d), const),
            pl.BlockSpec((tm, LANES), tab),
            pl.BlockSpec((tm, LANES), tab),
            cast_spec(cast_a), cast_spec(cast_b), cast_spec(cast_c),
        ],
        out_specs=[
            pl.BlockSpec((tm, n_pool), row),
            pl.BlockSpec((tm, n_q), row),
            pl.BlockSpec((tm, IN_SEC), row),
            pl.BlockSpec((tm, 2 * IN_SEC), row),
            pl.BlockSpec((tm, LANES), row),
            cast_spec(cast_a), cast_spec(cast_b), cast_spec(cast_c),
        ],
        out_shape=[
            jax.ShapeDtypeStruct((n_rows, n_pool), bf16),
            jax.ShapeDtypeStruct((n_rows, n_q), bf16),
            jax.ShapeDtypeStruct((n_rows, IN_SEC), f32),
            jax.ShapeDtypeStruct((n_rows, 2 * IN_SEC), bf16),
            jax.ShapeDtypeStruct((n_rows, LANES), f32),
            jax.ShapeDtypeStruct(cast_a.shape, bf16),
            jax.ShapeDtypeStruct(cast_b.shape, bf16),
            jax.ShapeDtypeStruct(cast_c.shape, bf16),
        ],
        scratch_shapes=[
            pltpu.VMEM((n_w, d), bf16),
            pltpu.VMEM((W_SLOTS, W_CHUNK, d), f32),
            pltpu.SemaphoreType.DMA((W_SLOTS,)),
        ],
        compiler_params=pltpu.CompilerParams(
            dimension_semantics=("arbitrary",), vmem_limit_bytes=VMEM_LIMIT),
        name="in_proj",
    )(x2, g, w_p, w_gt, cosf, sinf, cast_a, cast_b, cast_c)


def _gelu_tanh(x):
    return 0.5 * x * (1.0 + jnp.tanh(0.7978845608028654 * (x + 0.044715 * (x * x * x))))


def _compress_kernel(kc0_ref, kc1_ref, vc0_ref, vc1_ref, pek_ref, w1k_ref, w2k_ref, pev_ref, w1v_ref, w2v_ref,
                     kc_ref, vc_ref):
    bb = kc0_ref.shape[0]
    n_rows = kc0_ref.shape[1] // CMP_STRIDE
    half_k = (CMP_BLOCK // 2) * HEAD_DIM
    n_stack = bb * N_KV_GROUPS * n_rows
    rowi = lax.broadcasted_iota(jnp.int32, (n_stack, HEAD_DIM), 0)
    for srcs, pe_ref, w1_ref, w2_ref, o_ref in (((kc0_ref, kc1_ref), pek_ref, w1k_ref, w2k_ref, kc_ref),
                                                ((vc0_ref, vc1_ref), pev_ref, w1v_ref, w2v_ref, vc_ref)):
        x = jnp.concatenate(
            [jnp.concatenate([src[b, pl.ds(l, n_rows, stride=CMP_STRIDE), :] for l in range(CMP_STRIDE)], axis=1)
             for b in range(bb) for src in srcs], axis=0)
        za = _dot((x + pe_ref[:, :half_k]).astype(bf16), w1_ref[:half_k, :].astype(bf16))
        zb = _dot((x + pe_ref[:, half_k:]).astype(bf16), w1_ref[half_k:, :].astype(bf16))
        h = za + pltpu.roll(zb, n_stack - 1, axis=0)
        o = _dot(_gelu_tanh(h).astype(bf16), w2_ref[...].astype(bf16))
        o = jnp.where((rowi & (n_rows - 1)) < n_rows - 1, o, 0.0).astype(bf16)
        for b in range(bb):
            for g in range(N_KV_GROUPS):
                r0 = (b * N_KV_GROUPS + g) * n_rows
                o_ref[b, g] = o[r0:r0 + n_rows]


def _compress(r3, pek, w1k, w2k, pev, w1v, w2v, *, bb):
    B, T, width = r3.shape
    n_rows = T // CMP_STRIDE
    assert width == 2 * N_KV_GROUPS * HEAD_DIM and N_KV_GROUPS == 2 and B % bb == 0 and n_rows & (n_rows - 1) == 0
    const = lambda b: (0, 0)
    out = jax.ShapeDtypeStruct((B, N_KV_GROUPS, n_rows, HEAD_DIM), bf16)
    ospec = pl.BlockSpec((bb, N_KV_GROUPS, n_rows, HEAD_DIM), lambda b: (b, 0, 0, 0))
    sec = lambda s: pl.BlockSpec((bb, T, HEAD_DIM), lambda b: (b, 0, s))
    return pl.pallas_call(
        _compress_kernel,
        grid=(B // bb,),
        in_specs=[
            sec(0), sec(1), sec(2), sec(3),
            pl.BlockSpec(pek.shape, const), pl.BlockSpec(w1k.shape, const), pl.BlockSpec(w2k.shape, const),
            pl.BlockSpec(pev.shape, const), pl.BlockSpec(w1v.shape, const), pl.BlockSpec(w2v.shape, const),
        ],
        out_specs=[ospec, ospec],
        out_shape=[out, out],
        compiler_params=pltpu.CompilerParams(
            dimension_semantics=("arbitrary",), vmem_limit_bytes=VMEM_LIMIT),
        name="compress",
    )(r3, r3, r3, r3, pek, w1k, w2k, pev, w1v, w2v)


def _nsa_front(qs, t0, kc, vc, kw_ref, vw_ref, *, tq, T):
    R = HEADS_PER_GROUP
    n_cmp = kc.shape[0]
    n_sel = T // SEL_BLOCK

    wk = WINDOW + tq
    w0 = pl.multiple_of(jnp.maximum(t0 - WINDOW, 0), tq)
    sw = _dot_nt(qs, kw_ref[pl.ds(w0, wk), :])
    diff = (t0 - w0) + lax.broadcasted_iota(jnp.int32, (tq, wk), 0) - lax.broadcasted_iota(jnp.int32, (tq, wk), 1)
    wbias = jnp.where((diff & -WINDOW) == 0, 0.0, NEG)
    sw = jnp.concatenate([sw[r * tq:(r + 1) * tq] + wbias for r in range(R)], axis=0)
    ew = jnp.exp2(sw - jnp.max(sw, axis=1, keepdims=True))
    o_win = _dot(ew.astype(bf16), vw_ref[pl.ds(w0, wk), :]) / jnp.sum(ew, axis=1, keepdims=True)

    s = _dot_nt(qs, kc)
    valid = (CMP_STRIDE * lax.broadcasted_iota(jnp.int32, (tq, n_cmp), 1) + (CMP_BLOCK - 1)
             <= t0 + lax.broadcasted_iota(jnp.int32, (tq, n_cmp), 0))
    s = jnp.concatenate([jnp.where(valid, s[r * tq:(r + 1) * tq], NEG) for r in range(R)], axis=0)
    e = jnp.exp2(s - jnp.max(s, axis=1, keepdims=True))
    p = e / jnp.sum(e, axis=1, keepdims=True)
    p = jnp.concatenate([jnp.where(valid, p[r * tq:(r + 1) * tq], 0.0) for r in range(R)], axis=0)
    o_cmp = _dot(p.astype(bf16), vc)

    psum = p[0:tq]
    for r in range(1, R):
        psum = psum + p[r * tq:(r + 1) * tq]
    p_hi = psum.astype(bf16)
    p_lo = (psum - p_hi.astype(f32)).astype(bf16)
    sb = lax.broadcasted_iota(jnp.int32, (n_sel, n_cmp), 0) * SEL_BLOCK
    cb = lax.broadcasted_iota(jnp.int32, (n_sel, n_cmp), 1) * CMP_STRIDE
    ov_t = jnp.where((cb < sb + SEL_BLOCK) & (cb + CMP_BLOCK > sb), 1.0, 0.0).astype(bf16)
    imp_t = _dot_nt(ov_t, p_hi) + _dot_nt(ov_t, p_lo)

    jj = lax.broadcasted_iota(jnp.int32, (n_sel, tq), 0)
    cur = jnp.right_shift(t0 + lax.broadcasted_iota(jnp.int32, (n_sel, tq), 1), SEL_BLOCK.bit_length() - 1)
    forced = (jj == 0) | (jj == cur) | (jj == cur - 1)
    v = jnp.where(jj > cur, -jnp.inf, jnp.where(forced, jnp.inf, imp_t))
    sub = 8
    slabs = [v[k * sub:(k + 1) * sub] for k in range(n_sel // sub)]
    ranks = [jnp.zeros((sub, tq), f32) for _ in slabs]
    j_in = lax.broadcasted_iota(jnp.int32, (sub, tq), 0)
    for c in range(n_sel):
        v_c = v[c:c + 1, :]
        for k, v_k in enumerate(slabs):
            if k * sub > c:
                ahead = jnp.where(v_c >= v_k, 1.0, 0.0)
            elif (k + 1) * sub <= c:
                ahead = jnp.where(v_c > v_k, 1.0, 0.0)
            else:
                ahead = jnp.where(j_in > c - k * sub, jnp.where(v_c >= v_k, 1.0, 0.0),
                                  jnp.where(v_c > v_k, 1.0, 0.0))
            ranks[k] = ranks[k] + ahead
    rank = jnp.concatenate(ranks, axis=0)
    unsel_t = jnp.where(rank < float(min(N_SELECT, n_sel)), 0.0, 1.0)
    unsel_t = jnp.concatenate([unsel_t, jnp.zeros((LANES - n_sel, tq), f32)], axis=0)
    return o_win, o_cmp, unsel_t.T.astype(bf16)


def _nsa_kernel(q_ref, kc_ref, vc_ref, ks_ref, vs_ref, kw_ref, vw_ref, eneg_ref, gexp_ref, gt_ref, ca_ref, cb_ref,
                o_ref, ca_out, cb_out, s_sc, mx_sc, ls_sc, acc_sc, *, tq, n_sub, T):
    ca_out[...] = ca_ref[...].astype(bf16)
    cb_out[...] = cb_ref[...].astype(bf16)
    R = HEADS_PER_GROUP
    M = R * tq
    tk = n_sub * tq
    i = pl.program_id(2)
    t0 = i * tk
    kc = kc_ref[0, 0]
    vc = vc_ref[0, 0]

    fronts = []
    q_rows = []
    for sb in range(n_sub):
        q = q_ref[sb * tq:(sb + 1) * tq, :]
        qs = jnp.concatenate([q[:, r * HEAD_DIM:(r + 1) * HEAD_DIM] for r in range(R)], axis=0)
        o_win, o_cmp, unsel = _nsa_front(qs, t0 + sb * tq, kc, vc, kw_ref, vw_ref, tq=tq, T=T)
        fronts.append((o_win, o_cmp))
        q_rows.append(jnp.concatenate([qs, jnp.concatenate([unsel] * R, axis=0)], axis=1))
    q_aug = jnp.concatenate(q_rows, axis=0)

    n_chunk = tk // LANES
    mx_sc[...] = jnp.full(mx_sc.shape, NEG, f32)

    def lane_max(sc):
        part = sc[:, :LANES]
        for c in range(1, sc.shape[1] // LANES):
            part = jnp.maximum(part, sc[:, c * LANES:(c + 1) * LANES])
        return part

    def score_tiles(kt, n_tiles):
        k0 = pl.multiple_of(kt * tk, tk)
        width = n_tiles * tk
        k_aug = jnp.concatenate([ks_ref[pl.ds(k0, width), :], eneg_ref[pl.ds(k0, width), :]], axis=1)
        sc = _dot_nt(q_aug, k_aug)
        for j in range(n_tiles):
            s_sc[kt + j] = sc[:, j * tk:(j + 1) * tk]
        mx_sc[...] = jnp.maximum(mx_sc[...], lane_max(sc))

    def score_pair(kp, carry):
        score_tiles(2 * kp, 2)
        return carry

    lax.fori_loop(0, i // 2, score_pair, 0)

    @pl.when(i % 2 == 1)
    def _():
        score_tiles(i - 1, 1)


    kd = pl.multiple_of(i * tk, tk)
    tri = jnp.where(lax.broadcasted_iota(jnp.int32, (tq, tq), 0) >= lax.broadcasted_iota(jnp.int32, (tq, tq), 1),
                    0.0, NEG)
    for sb in range(n_sub):
        width = (sb + 1) * tq
        rows = slice(sb * M, (sb + 1) * M)
        k_aug = jnp.concatenate([ks_ref[pl.ds(kd, width), :], eneg_ref[pl.ds(kd, width), :]], axis=1)
        sc = _dot_nt(q_aug[rows], k_aug)
        last = jnp.concatenate([sc[r * tq:(r + 1) * tq, sb * tq:] + tri for r in range(R)], axis=0)
        sc = last if sb == 0 else jnp.concatenate([sc[:, :sb * tq], last], axis=1)
        s_sc[i, rows, :width] = sc
        mx_sc[rows, :] = jnp.maximum(mx_sc[rows, :], lane_max(sc))
    mx_sc[...] = jnp.broadcast_to(jnp.max(mx_sc[...], axis=1, keepdims=True), mx_sc.shape)
    ls_sc[...] = jnp.zeros(ls_sc.shape, f32)
    acc_sc[...] = jnp.zeros(acc_sc.shape, f32)

    def pv_body(kt, carry):
        k0 = pl.multiple_of(kt * tk, tk)
        sc = s_sc[kt]
        mb = mx_sc[...]
        ps = [jnp.exp2(sc[:, c * LANES:(c + 1) * LANES] - mb) for c in range(n_chunk)]
        ls_sc[...] += functools.reduce(lambda a, b: a + b, ps)
        acc_sc[...] += _dot(jnp.concatenate(ps, axis=1).astype(bf16), vs_ref[pl.ds(k0, tk), :])
        return carry

    lax.fori_loop(0, i, pv_body, 0)
    for sb in range(n_sub):
        width = (sb + 1) * tq
        rows = slice(sb * M, (sb + 1) * M)
        sc = s_sc[i, rows, :width]
        mb = mx_sc[rows, :]
        ps = [jnp.exp2(sc[:, c * LANES:(c + 1) * LANES] - mb) for c in range(width // LANES)]
        ls_sc[rows, :] += functools.reduce(lambda a, b: a + b, ps)
        acc_sc[rows, :] += _dot(jnp.concatenate(ps, axis=1).astype(bf16), vs_ref[pl.ds(kd, width), :])
    o_sel = acc_sc[...] / jnp.sum(ls_sc[...], axis=1, keepdims=True)

    gt_all = jax.nn.sigmoid(gt_ref[...])
    gt_all = jnp.where(pl.program_id(1) == 0, gt_all, pltpu.roll(gt_all, LANES - R * N_BRANCH, axis=1))
    g_hi = gt_all.astype(bf16)
    rest = gt_all - g_hi.astype(f32)
    g_mid = rest.astype(bf16)
    g_lo = (rest - g_mid.astype(f32)).astype(bf16)
    g_exp = _dot(jnp.concatenate([g_hi, g_mid, g_lo], axis=1), gexp_ref[...])
    for sb, (o_win, o_cmp) in enumerate(fronts):
        for r in range(R):
            rows = slice(r * tq, (r + 1) * tq)
            srows = slice(sb * M + r * tq, sb * M + (r + 1) * tq)
            gates = [g_exp[sb * tq:(sb + 1) * tq, (r * N_BRANCH + br) * LANES:(r * N_BRANCH + br + 1) * LANES]
                     for br in range(N_BRANCH)]
            o = gates[0] * o_cmp[rows] + gates[1] * o_sel[srows] + gates[2] * o_win[rows]
            o_ref[sb * tq:(sb + 1) * tq, r * HEAD_DIM:(r + 1) * HEAD_DIM] = o.astype(bf16)


def _nsa(q, kcmp, vcmp, kv, eneg, gexp, gates, cast_a, cast_b, *, B, T, tq, n_sub):
    G, R = N_KV_GROUPS, HEADS_PER_GROUP
    tk = n_sub * tq
    nt = T // tk
    n_cmp = kcmp.shape[2]
    assert tq % LANES == 0 and tq & (tq - 1) == 0 and T % tk == 0 and T >= WINDOW + tq
    assert WINDOW % tq == 0 and WINDOW & (WINDOW - 1) == 0 and (R * tq) & (R * tq - 1) == 0
    rowblk = lambda b, g, i: (b * nt + i, g)
    cmp_spec = pl.BlockSpec((1, 1, n_cmp, HEAD_DIM), lambda b, g, i: (b, g, 0, 0))
    kvspec = lambda sec: pl.BlockSpec((T, HEAD_DIM), lambda b, g, i: (b, sec * G + g))
    n_steps = B * G * nt
    cast_spec = lambda w: pl.BlockSpec((w.shape[0] // n_steps, w.shape[1]), lambda b, g, i: ((b * G + g) * nt + i, 0))
    return pl.pallas_call(
        functools.partial(_nsa_kernel, tq=tq, n_sub=n_sub, T=T),
        grid=(B, G, nt),
        in_specs=[
            pl.BlockSpec((tk, R * HEAD_DIM), rowblk),
            cmp_spec, cmp_spec,
            kvspec(0), kvspec(1), kvspec(2), kvspec(3),
            pl.BlockSpec((T, LANES), lambda b, g, i: (0, 0)),
            pl.BlockSpec(gexp.shape, lambda b, g, i: (0, 0)),
            pl.BlockSpec((tk, LANES), lambda b, g, i: (b * nt + i, 0)),
            cast_spec(cast_a), cast_spec(cast_b),
        ],
        out_specs=[pl.BlockSpec((tk, R * HEAD_DIM), rowblk), cast_spec(cast_a), cast_spec(cast_b)],
        out_shape=[jax.ShapeDtypeStruct((B * T, G * R * HEAD_DIM), bf16), jax.ShapeDtypeStruct(cast_a.shape, bf16),
                   jax.ShapeDtypeStruct(cast_b.shape, bf16)],
        scratch_shapes=[
            pltpu.VMEM((T // tk, n_sub * R * tq, tk), f32),
            pltpu.VMEM((n_sub * R * tq, LANES), f32),
            pltpu.VMEM((n_sub * R * tq, LANES), f32),
            pltpu.VMEM((n_sub * R * tq, HEAD_DIM), f32),
        ],
        compiler_params=pltpu.CompilerParams(
            dimension_semantics=("arbitrary", "arbitrary", "arbitrary"), vmem_limit_bytes=VMEM_LIMIT),
        name="nsa",
    )(q, kcmp, vcmp, kv, kv, kv, kv, eneg, gexp, gates, cast_a, cast_b)


def _out_proj_kernel(x_ref, u_ref, uh_ref, y_ref, pw_ref, ps_ref, wo_ref, g_ref, o_ref, n_ref, *, tm, T):
    i = pl.program_id(0)
    tiles_per_seq = T // tm
    keep_halo = jnp.where((i % tiles_per_seq) == 0, 0.0, 1.0)
    u = u_ref[...]
    halo = uh_ref[...].astype(f32) * keep_halo
    tseq = (i % tiles_per_seq) * tm + lax.broadcasted_iota(jnp.int32, (tm, 1), 0)
    n_pool = u.shape[1]
    gd = n_pool // len(POOL_WINDOWS)
    acc = x_ref[...] + _dot(y_ref[...], wo_ref[n_pool:, :])
    for gi, w in enumerate(POOL_WINDOWS):
        assert w <= POOL_HALO and w & (w - 1) == 0
        cols = slice(gi * gd, (gi + 1) * gd)
        ug = u[:, cols].astype(f32)
        ws = jnp.concatenate([halo[:, cols], ug], axis=0)
        k = 1
        while k < w:
            ws = ws + pltpu.roll(ws, k, axis=0)
            k *= 2
        win_sum = ws[POOL_HALO:]
        count = jnp.minimum(tseq + 1, w).astype(f32)
        pooled = win_sum / count - ug
        yg = _dot(pooled.astype(bf16), pw_ref[gi].astype(bf16)) * ps_ref[:, cols]
        acc = acc + _dot(yg.astype(bf16), wo_ref[cols, :])
    o_ref[...] = acc
    n_ref[...] = _rms(acc, g_ref[...]).astype(bf16)


def _out_proj(x2, u, y_nsa, pool_w, pool_scale, w_out, g_next, *, T, tm):
    n_rows, d = x2.shape
    n_pool = u.shape[1]
    hb = tm // POOL_HALO
    row = lambda i: (i, 0)
    const2 = lambda i: (0, 0)
    return pl.pallas_call(
        functools.partial(_out_proj_kernel, tm=tm, T=T),
        grid=(n_rows // tm,),
        in_specs=[
            pl.BlockSpec((tm, d), row),
            pl.BlockSpec((tm, n_pool), row),
            pl.BlockSpec((POOL_HALO, n_pool), lambda i: (jnp.maximum(i * hb - 1, 0), 0)),
            pl.BlockSpec((tm, y_nsa.shape[1]), row),
            pl.BlockSpec(pool_w.shape, lambda i: (0, 0, 0)),
            pl.BlockSpec((1, n_pool), const2),
            pl.BlockSpec(w_out.shape, const2, pipeline_mode=pl.Buffered(1)),
            pl.BlockSpec((1, d), const2),
        ],
        out_specs=[pl.BlockSpec((tm, d), row), pl.BlockSpec((tm, d), row)],
        out_shape=[jax.ShapeDtypeStruct((n_rows, d), f32), jax.ShapeDtypeStruct((n_rows, d), bf16)],
        compiler_params=pltpu.CompilerParams(
            dimension_semantics=("arbitrary",), vmem_limit_bytes=VMEM_LIMIT),
        name="out_proj",
    )(x2, u, u, y_nsa, pool_w, pool_scale, w_out, g_next)


def _ffn_kernel(h_hbm, n_ref, wg_ref, wu_ref, wd_ref, p_ref, pg_ref, pwg_ref, pwp_ref, fg_ref, o_hbm, acc_sc, sem, *, tm):
    i = pl.program_id(0)
    f = pl.program_id(1)
    n_i = pl.num_programs(0)
    n_f = pl.num_programs(1)

    def rows(t):
        return pl.ds(pl.multiple_of(t * tm, tm), tm)

    def load(t):
        return pltpu.make_async_copy(h_hbm.at[rows(t), :], acc_sc.at[t % 2], sem.at[0])

    def store(t):
        return pltpu.make_async_copy(acc_sc.at[t % 2], o_hbm.at[rows(t), :], sem.at[1])

    @pl.when((i == 0) & (f == 0))
    def _():
        load(i).start()

    @pl.when(f == 0)
    def _():
        load(i).wait()

    @pl.when((f == 0) & (i > 0))
    def _():
        store(i - 1).start()

    @pl.when((f == 1) & (i > 0))
    def _():
        store(i - 1).wait()

    @pl.when((f == 1) & (i + 1 < n_i))
    def _():
        load(i + 1).start()

    n = n_ref[...]
    half = wg_ref.shape[1] // 2
    acc = None
    for c in range(2):
        cols = slice(c * half, (c + 1) * half)
        gate = _dot(n, wg_ref[:, cols])
        up = _dot(n, wu_ref[:, cols])
        a = gate * jax.nn.sigmoid(gate) * up
        d = _dot(a.astype(bf16), wd_ref[cols, :])
        acc = d if acc is None else acc + d
    acc_sc[i % 2] += acc

    @pl.when(f == n_f - 1)
    def _():
        hr = tm // 4
        for part in range(4):
            r = slice(part * hr, (part + 1) * hr)
            y = acc_sc[i % 2, r, :]
            gate = jax.nn.sigmoid(_dot(_rms(y, pg_ref[...]).astype(bf16), pwg_ref[...]))
            y = y + _dot(p_ref[r, :].astype(bf16), pwp_ref[...].astype(bf16)) * gate
            acc_sc[i % 2, r, :] = _rms(y, fg_ref[...])

    @pl.when((i == n_i - 1) & (f == n_f - 1))
    def _():
        store(i).start()
        store(i).wait()


def _ffn(h, n, w_gate, w_up, w_down, p2, ple_g, w_ple_gate, w_ple_proj, final_g, *, tm, tf):
    n_rows, d = h.shape
    d_ff = w_gate.shape[1]
    assert n_rows % tm == 0 and d_ff // tf >= 2
    return pl.pallas_call(
        functools.partial(_ffn_kernel, tm=tm),
        grid=(n_rows // tm, d_ff // tf),
        in_specs=[
            pl.BlockSpec(memory_space=pl.ANY),
            pl.BlockSpec((tm, d), lambda i, f: (i, 0)),
            pl.BlockSpec((d, tf), lambda i, f: (0, f)),
            pl.BlockSpec((d, tf), lambda i, f: (0, f)),
            pl.BlockSpec((tf, d), lambda i, f: (f, 0)),
            pl.BlockSpec((tm, p2.shape[1]), lambda i, f: (i, 0)),
            pl.BlockSpec((1, d), lambda i, f: (0, 0)),
            pl.BlockSpec(w_ple_gate.shape, lambda i, f: (0, 0), pipeline_mode=pl.Buffered(1)),
            pl.BlockSpec(w_ple_proj.shape, lambda i, f: (0, 0), pipeline_mode=pl.Buffered(1)),
            pl.BlockSpec((1, d), lambda i, f: (0, 0)),
        ],
        out_specs=pl.BlockSpec(memory_space=pl.ANY),
        out_shape=jax.ShapeDtypeStruct((n_rows, d), f32),
        scratch_shapes=[pltpu.VMEM((2, tm, d), f32), pltpu.SemaphoreType.DMA((2,))],
        compiler_params=pltpu.CompilerParams(
            dimension_semantics=("arbitrary", "arbitrary"), vmem_limit_bytes=FFN_VMEM_LIMIT),
        name="ffn",
    )(h, n, w_gate, w_up, w_down, p2, ple_g, w_ple_gate, w_ple_proj, final_g)


def _rope_tables(T):
    pos = np.arange(T, dtype=np.float64)
    inv_freq = ROPE_THETA ** (-np.arange(0, ROPE_DIM, 2, dtype=np.float64) / ROPE_DIM)
    ang = pos[:, None] * inv_freq[None, :]
    cos, sin = np.cos(ang), np.sin(ang)
    rest = HEAD_DIM - ROPE_DIM
    cosf = np.concatenate([cos, cos, np.ones((T, rest))], axis=1)
    sinf = np.concatenate([-sin, sin, np.zeros((T, rest))], axis=1)
    return jnp.asarray(cosf, f32), jnp.asarray(sinf, f32)


def _block_bias_table(T):
    key_blk = np.arange(T)[:, None] // SEL_BLOCK
    return jnp.asarray(np.where(key_blk == np.arange(LANES)[None, :], NEG, 0.0).astype(bf16))


def _gate_expansion_table():
    n_col = HEADS_PER_GROUP * N_BRANCH
    e = np.zeros((3, LANES, n_col, LANES), np.float32)
    for c in range(n_col):
        e[:, c, c, :] = 1.0
    return jnp.asarray(e.reshape(3 * LANES, n_col * LANES).astype(bf16))


def _mixer_ffn(h2, in_norm_g, w_in, pool_w, pool_scale, cmp_k_pe, cmp_k_w1, cmp_k_w2,
               cmp_v_pe, cmp_v_w1, cmp_v_w2, w_out, ffn_norm_g, w_gate, w_up, w_down,
               p2, ple_norm_g, w_ple_gate, w_ple_proj, final_norm_g, *, B, T):
    n_pool = pool_scale.shape[0]
    n_heads = N_KV_GROUPS * HEADS_PER_GROUP
    nsa_w = n_heads * HEAD_DIM
    kv_w = N_KV_GROUPS * HEAD_DIM
    assert w_in.shape[1] == n_pool + nsa_w + 6 * kv_w + n_heads * N_BRANCH

    o0 = n_pool + nsa_w + 6 * kv_w
    w_p = w_in.T
    w_gt = jnp.pad(w_p[o0:].astype(bf16), ((0, LANES - n_heads * N_BRANCH), (0, 0)))

    cosf, sinf = _rope_tables(T)
    u, q, kcvc, kv, gates, w_out_b, w_gate_b, w_ple_gate_b = _in_proj(
        h2, in_norm_g[None, :], w_p, w_gt, cosf, sinf, w_out, w_gate, w_ple_gate, T=T, tm=ROW_TILE,
        n_pool=n_pool, n_q=nsa_w)

    kcmp, vcmp = _compress(
        kcvc.reshape(B, T, kcvc.shape[1]),
        cmp_k_pe.reshape(1, -1), cmp_k_w1, cmp_k_w2, cmp_v_pe.reshape(1, -1), cmp_v_w1, cmp_v_w2,
        bb=COMPRESS_BATCH if B % COMPRESS_BATCH == 0 else 1)

    eneg = _block_bias_table(T)
    y_nsa, w_up_b, w_down_b = _nsa(q, kcmp, vcmp, kv, eneg, _gate_expansion_table(), gates, w_up, w_down, B=B, T=T,
                                   tq=NSA_Q_TILE, n_sub=NSA_SUB_TILES)

    h2, n2 = _out_proj(h2, u, y_nsa, pool_w, pool_scale[None, :], w_out_b, ffn_norm_g[None, :], T=T, tm=ROW_TILE)
    return _ffn(h2, n2, w_gate_b, w_up_b, w_down_b, p2, ple_norm_g[None, :], w_ple_gate_b, w_ple_proj,
                final_norm_g[None, :], tm=FFN_ROW_TILE, tf=FFN_FF_TILE)


def kernel(x, p, in_norm_g, w_in, pool_w, pool_scale, cmp_k_pe, cmp_k_w1, cmp_k_w2, cmp_v_pe, cmp_v_w1,
           cmp_v_w2, w_out, ffn_norm_g, w_gate, w_up, w_down, ple_norm_g, w_ple_gate, w_ple_proj, final_norm_g):
    B, T, d = x.shape
    depth = w_in.shape[0]
    assert depth == 1, "the final rmsnorm is fused into the last layer's per-layer-embedding kernel"
    h2 = x.reshape(B * T, d)
    i = 0
    out = _mixer_ffn(
        h2, in_norm_g[i], w_in[i], pool_w[i], pool_scale[i], cmp_k_pe[i], cmp_k_w1[i], cmp_k_w2[i],
        cmp_v_pe[i], cmp_v_w1[i], cmp_v_w2[i], w_out[i], ffn_norm_g[i], w_gate[i], w_up[i], w_down[i],
        p[i].reshape(B * T, -1), ple_norm_g[i], w_ple_gate[i], w_ple_proj[i], final_norm_g, B=B, T=T)
    return out.reshape(B, T, d)
```

```python
import functools

import jax
import jax.numpy as jnp
import numpy as np
from jax import lax
from jax.experimental import pallas as pl
from jax.experimental.pallas import tpu as pltpu

EPS = 1e-6
NEG = -1e30
LANES = 128
POOL_WINDOWS = (2, 4, 8, 16)
POOL_HALO = 16
HEAD_DIM = 128
N_KV_GROUPS = 2
HEADS_PER_GROUP = 4
N_BRANCH = 3
CMP_BLOCK = 32
CMP_STRIDE = 16
SEL_BLOCK = 64
N_SELECT = 16
WINDOW = 512
ROPE_THETA = 500000.0
ROPE_DIM = HEAD_DIM // 4
ATTN_SCALE = HEAD_DIM ** -0.5
Q_SCALE = ATTN_SCALE * 1.4426950408889634
VMEM_LIMIT = 60 * 1024 * 1024
W_CHUNK = 64
W_SLOTS = 8
IN_SEC = 512
ROW_TILE = 512
FFN_ROW_TILE = 1024
FFN_FF_TILE = 512
NSA_Q_TILE = 256
NSA_SUB_TILES = 2
COMPRESS_BATCH = 2

f32 = jnp.float32
bf16 = jnp.bfloat16


def _dot(a, b):
    return jnp.dot(a, b, preferred_element_type=f32)


def _dot_nt(a, b):
    return lax.dot_general(a, b, (((1,), (1,)), ((), ())), preferred_element_type=f32)


def _rms(x, g):
    return x * lax.rsqrt(jnp.mean(x * x, axis=-1, keepdims=True) + EPS) * g


def _rope(h, cosf, sinf, lane):
    half = ROPE_DIM // 2
    partner = jnp.where(lane < half, pltpu.roll(h, LANES - half, axis=1), pltpu.roll(h, half, axis=1))
    return h * cosf + partner * sinf


def _w_chunk_copy(w_hbm, stage_sc, sem, c):
    slot = c % W_SLOTS
    return pltpu.make_async_copy(w_hbm.at[pl.ds(c * W_CHUNK, W_CHUNK), :], stage_sc.at[slot], sem.at[slot])


def _in_proj_kernel(x_ref, g_ref, w_hbm, wgt_ref, cos_ref, sin_ref, ca_ref, cb_ref, cc_ref,
                    u_ref, q_ref, kcvc_ref, kv_ref, gt_ref, ca_out, cb_out, cc_out,
                    w_ref, stage_sc, sem):
    @pl.when(pl.program_id(0) == 0)
    def _():
        n_chunk = w_ref.shape[0] // W_CHUNK
        for c in range(min(W_SLOTS - 1, n_chunk)):
            _w_chunk_copy(w_hbm, stage_sc, sem, c).start()
        for c in range(n_chunk):
            if c + W_SLOTS - 1 < n_chunk:
                _w_chunk_copy(w_hbm, stage_sc, sem, c + W_SLOTS - 1).start()
            _w_chunk_copy(w_hbm, stage_sc, sem, c).wait()
            w_ref[c * W_CHUNK:(c + 1) * W_CHUNK, :] = stage_sc[c % W_SLOTS].astype(bf16)

    def section(s):
        return w_ref[s * IN_SEC:(s + 1) * IN_SEC, :]

    ca_out[...] = ca_ref[...].astype(bf16)
    cb_out[...] = cb_ref[...].astype(bf16)
    cc_out[...] = cc_ref[...].astype(bf16)
    a = _rms(x_ref[...], g_ref[...]).astype(bf16)
    cosf = cos_ref[...]
    sinf = sin_ref[...]
    lane = lax.broadcasted_iota(jnp.int32, cosf.shape, 1)
    heads = IN_SEC // HEAD_DIM
    s = 0
    for c in range(0, u_ref.shape[1], IN_SEC):
        u_ref[:, c:c + IN_SEC] = _dot_nt(a, section(s)).astype(bf16)
        s += 1
    for c in range(0, q_ref.shape[1], IN_SEC):
        acc = _dot_nt(a, section(s))
        s += 1
        for j in range(heads):
            h = _rope(acc[:, j * HEAD_DIM:(j + 1) * HEAD_DIM], cosf, sinf, lane) * Q_SCALE
            q_ref[:, c + j * HEAD_DIM:c + (j + 1) * HEAD_DIM] = h.astype(bf16)
    for dst, base, dt in ((kcvc_ref, 0, f32), (kv_ref, 0, bf16), (kv_ref, IN_SEC, bf16)):
        acc = _dot_nt(a, section(s))
        s += 1
        for j in range(heads):
            h = acc[:, j * HEAD_DIM:(j + 1) * HEAD_DIM]
            if j < N_KV_GROUPS:
                h = _rope(h, cosf, sinf, lane)
            dst[:, base + j * HEAD_DIM:base + (j + 1) * HEAD_DIM] = h.astype(dt)
    gt_ref[...] = _dot_nt(a, wgt_ref[...])


def _in_proj(x2, g, w_p, w_gt, cosf, sinf, cast_a, cast_b, cast_c, *, T, tm, n_pool, n_q):
    n_rows, d = x2.shape
    assert n_pool % IN_SEC == 0 and n_q % IN_SEC == 0 and 2 * N_KV_GROUPS * HEAD_DIM == IN_SEC
    n_w = (w_p.shape[0] // W_CHUNK) * W_CHUNK
    assert w_p.dtype == f32 and n_w + w_gt.shape[0] >= w_p.shape[0] and n_w == n_pool + n_q + 3 * IN_SEC
    tiles_per_seq = T // tm
    n_steps = n_rows // tm
    row = lambda i: (i, 0)
    cast_spec = lambda w: pl.BlockSpec((w.shape[0] // n_steps, w.shape[1]), row)
    const = lambda i: (0, 0)
    tab = lambda i: (i % tiles_per_seq, 0)
    return pl.pallas_call(
        _in_proj_kernel,
        grid=(n_rows // tm,),
        in_specs=[
            pl.BlockSpec((tm, d), row),
            pl.BlockSpec((1, d), const),
            pl.BlockSpec(memory_space=pl.ANY),
            pl.BlockSpec((LANES, d), const),
            pl.BlockSpec((tm, LANES), tab),
            pl.BlockSpec((tm, LANES), tab),
            cast_spec(cast_a), cast_spec(cast_b), cast_spec(cast_c),
        ],
        out_specs=[
            pl.BlockSpec((tm, n_pool), row),
            pl.BlockSpec((tm, n_q), row),
            pl.BlockSpec((tm, IN_SEC), row),
            pl.BlockSpec((tm, 2 * IN_SEC), row),
            pl.BlockSpec((tm, LANES), row),
            cast_spec(cast_a), cast_spec(cast_b), cast_spec(cast_c),
        ],
        out_shape=[
            jax.ShapeDtypeStruct((n_rows, n_pool), bf16),
            jax.ShapeDtypeStruct((n_rows, n_q), bf16),
            jax.ShapeDtypeStruct((n_rows, IN_SEC), f32),
            jax.ShapeDtypeStruct((n_rows, 2 * IN_SEC), bf16),
            jax.ShapeDtypeStruct((n_rows, LANES), f32),
            jax.ShapeDtypeStruct(cast_a.shape, bf16),
            jax.ShapeDtypeStruct(cast_b.shape, bf16),
            jax.ShapeDtypeStruct(cast_c.shape, bf16),
        ],
        scratch_shapes=[
            pltpu.VMEM((n_w, d), bf16),
            pltpu.VMEM((W_SLOTS, W_CHUNK, d), f32),
            pltpu.SemaphoreType.DMA((W_SLOTS,)),
        ],
        compiler_params=pltpu.CompilerParams(
            dimension_semantics=("arbitrary",), vmem_limit_bytes=VMEM_LIMIT),
        name="in_proj",
    )(x2, g, w_p, w_gt, cosf, sinf, cast_a, cast_b, cast_c)


def _gelu_tanh(x):
    return 0.5 * x * (1.0 + jnp.tanh(0.7978845608028654 * (x + 0.044715 * (x * x * x))))


def _compress_kernel(kc0_ref, kc1_ref, vc0_ref, vc1_ref, pek_ref, w1k_ref, w2k_ref, pev_ref, w1v_ref, w2v_ref,
                     kc_ref, vc_ref):
    bb = kc0_ref.shape[0]
    n_rows = kc0_ref.shape[1] // CMP_STRIDE
    half_k = (CMP_BLOCK // 2) * HEAD_DIM
    n_stack = bb * N_KV_GROUPS * n_rows
    rowi = lax.broadcasted_iota(jnp.int32, (n_stack, HEAD_DIM), 0)
    for srcs, pe_ref, w1_ref, w2_ref, o_ref in (((kc0_ref, kc1_ref), pek_ref, w1k_ref, w2k_ref, kc_ref),
                                                ((vc0_ref, vc1_ref), pev_ref, w1v_ref, w2v_ref, vc_ref)):
        x = jnp.concatenate(
            [jnp.concatenate([src[b, pl.ds(l, n_rows, stride=CMP_STRIDE), :] for l in range(CMP_STRIDE)], axis=1)
             for b in range(bb) for src in srcs], axis=0)
        za = _dot((x + pe_ref[:, :half_k]).astype(bf16), w1_ref[:half_k, :].astype(bf16))
        zb = _dot((x + pe_ref[:, half_k:]).astype(bf16), w1_ref[half_k:, :].astype(bf16))
        h = za + pltpu.roll(zb, n_stack - 1, axis=0)
        o = _dot(_gelu_tanh(h).astype(bf16), w2_ref[...].astype(bf16))
        o = jnp.where((rowi & (n_rows - 1)) < n_rows - 1, o, 0.0).astype(bf16)
        for b in range(bb):
            for g in range(N_KV_GROUPS):
                r0 = (b * N_KV_GROUPS + g) * n_rows
                o_ref[b, g] = o[r0:r0 + n_rows]


def _compress(r3, pek, w1k, w2k, pev, w1v, w2v, *, bb):
    B, T, width = r3.shape
    n_rows = T // CMP_STRIDE
    assert width == 2 * N_KV_GROUPS * HEAD_DIM and N_KV_GROUPS == 2 and B % bb == 0 and n_rows & (n_rows - 1) == 0
    const = lambda b: (0, 0)
    out = jax.ShapeDtypeStruct((B, N_KV_GROUPS, n_rows, HEAD_DIM), bf16)
    ospec = pl.BlockSpec((bb, N_KV_GROUPS, n_rows, HEAD_DIM), lambda b: (b, 0, 0, 0))
    sec = lambda s: pl.BlockSpec((bb, T, HEAD_DIM), lambda b: (b, 0, s))
    return pl.pallas_call(
        _compress_kernel,
        grid=(B // bb,),
        in_specs=[
            sec(0), sec(1), sec(2), sec(3),
            pl.BlockSpec(pek.shape, const), pl.BlockSpec(w1k.shape, const), pl.BlockSpec(w2k.shape, const),
            pl.BlockSpec(pev.shape, const), pl.BlockSpec(w1v.shape, const), pl.BlockSpec(w2v.shape, const),
        ],
        out_specs=[ospec, ospec],
        out_shape=[out, out],
        compiler_params=pltpu.CompilerParams(
            dimension_semantics=("arbitrary",), vmem_limit_bytes=VMEM_LIMIT),
        name="compress",
    )(r3, r3, r3, r3, pek, w1k, w2k, pev, w1v, w2v)


def _nsa_front(qs, t0, kc, vc, kw_ref, vw_ref, *, tq, T):
    R = HEADS_PER_GROUP
    n_cmp = kc.shape[0]
    n_sel = T // SEL_BLOCK

    wk = WINDOW + tq
    w0 = pl.multiple_of(jnp.maximum(t0 - WINDOW, 0), tq)
    sw = _dot_nt(qs, kw_ref[pl.ds(w0, wk), :])
    diff = (t0 - w0) + lax.broadcasted_iota(jnp.int32, (tq, wk), 0) - lax.broadcasted_iota(jnp.int32, (tq, wk), 1)
    wbias = jnp.where((diff & -WINDOW) == 0, 0.0, NEG)
    sw = jnp.concatenate([sw[r * tq:(r + 1) * tq] + wbias for r in range(R)], axis=0)
    ew = jnp.exp2(sw - jnp.max(sw, axis=1, keepdims=True))
    o_win = _dot(ew.astype(bf16), vw_ref[pl.ds(w0, wk), :]) / jnp.sum(ew, axis=1, keepdims=True)

    s = _dot_nt(qs, kc)
    valid = (CMP_STRIDE * lax.broadcasted_iota(jnp.int32, (tq, n_cmp), 1) + (CMP_BLOCK - 1)
             <= t0 + lax.broadcasted_iota(jnp.int32, (tq, n_cmp), 0))
    s = jnp.concatenate([jnp.where(valid, s[r * tq:(r + 1) * tq], NEG) for r in range(R)], axis=0)
    e = jnp.exp2(s - jnp.max(s, axis=1, keepdims=True))
    p = e / jnp.sum(e, axis=1, keepdims=True)
    p = jnp.concatenate([jnp.where(valid, p[r * tq:(r + 1) * tq], 0.0) for r in range(R)], axis=0)
    o_cmp = _dot(p.astype(bf16), vc)

    psum = p[0:tq]
    for r in range(1, R):
        psum = psum + p[r * tq:(r + 1) * tq]
    p_hi = psum.astype(bf16)
    p_lo = (psum - p_hi.astype(f32)).astype(bf16)
    sb = lax.broadcasted_iota(jnp.int32, (n_sel, n_cmp), 0) * SEL_BLOCK
    cb = lax.broadcasted_iota(jnp.int32, (n_sel, n_cmp), 1) * CMP_STRIDE
    ov_t = jnp.where((cb < sb + SEL_BLOCK) & (cb + CMP_BLOCK > sb), 1.0, 0.0).astype(bf16)
    imp_t = _dot_nt(ov_t, p_hi) + _dot_nt(ov_t, p_lo)

    jj = lax.broadcasted_iota(jnp.int32, (n_sel, tq), 0)
    cur = jnp.right_shift(t0 + lax.broadcasted_iota(jnp.int32, (n_sel, tq), 1), SEL_BLOCK.bit_length() - 1)
    forced = (jj == 0) | (jj == cur) | (jj == cur - 1)
    v = jnp.where(jj > cur, -jnp.inf, jnp.where(forced, jnp.inf, imp_t))
    sub = 8
    slabs = [v[k * sub:(k + 1) * sub] for k in range(n_sel // sub)]
    ranks = [jnp.zeros((sub, tq), f32) for _ in slabs]
    j_in = lax.broadcasted_iota(jnp.int32, (sub, tq), 0)
    for c in range(n_sel):
        v_c = v[c:c + 1, :]
        for k, v_k in enumerate(slabs):
            if k * sub > c:
                ahead = jnp.where(v_c >= v_k, 1.0, 0.0)
            elif (k + 1) * sub <= c:
                ahead = jnp.where(v_c > v_k, 1.0, 0.0)
            else:
                ahead = jnp.where(j_in > c - k * sub, jnp.where(v_c >= v_k, 1.0, 0.0),
                                  jnp.where(v_c > v_k, 1.0, 0.0))
            ranks[k] = ranks[k] + ahead
    rank = jnp.concatenate(ranks, axis=0)
    unsel_t = jnp.where(rank < float(min(N_SELECT, n_sel)), 0.0, 1.0)
    unsel_t = jnp.concatenate([unsel_t, jnp.zeros((LANES - n_sel, tq), f32)], axis=0)
    return o_win, o_cmp, unsel_t.T.astype(bf16)


def _nsa_kernel(q_ref, kc_ref, vc_ref, ks_ref, vs_ref, kw_ref, vw_ref, eneg_ref, gexp_ref, gt_ref, ca_ref, cb_ref,
                o_ref, ca_out, cb_out, s_sc, mx_sc, ls_sc, acc_sc, *, tq, n_sub, T):
    ca_out[...] = ca_ref[...].astype(bf16)
    cb_out[...] = cb_ref[...].astype(bf16)
    R = HEADS_PER_GROUP
    M = R * tq
    tk = n_sub * tq
    i = pl.program_id(2)
    t0 = i * tk
    kc = kc_ref[0, 0]
    vc = vc_ref[0, 0]

    fronts = []
    q_rows = []
    for sb in range(n_sub):
        q = q_ref[sb * tq:(sb + 1) * tq, :]
        qs = jnp.concatenate([q[:, r * HEAD_DIM:(r + 1) * HEAD_DIM] for r in range(R)], axis=0)
        o_win, o_cmp, unsel = _nsa_front(qs, t0 + sb * tq, kc, vc, kw_ref, vw_ref, tq=tq, T=T)
        fronts.append((o_win, o_cmp))
        q_rows.append(jnp.concatenate([qs, jnp.concatenate([unsel] * R, axis=0)], axis=1))
    q_aug = jnp.concatenate(q_rows, axis=0)

    n_chunk = tk // LANES
    mx_sc[...] = jnp.full(mx_sc.shape, NEG, f32)

    def lane_max(sc):
        part = sc[:, :LANES]
        for c in range(1, sc.shape[1] // LANES):
            part = jnp.maximum(part, sc[:, c * LANES:(c + 1) * LANES])
        return part

    def score_tiles(kt, n_tiles):
        k0 = pl.multiple_of(kt * tk, tk)
        width = n_tiles * tk
        k_aug = jnp.concatenate([ks_ref[pl.ds(k0, width), :], eneg_ref[pl.ds(k0, width), :]], axis=1)
        sc = _dot_nt(q_aug, k_aug)
        for j in range(n_tiles):
            s_sc[kt + j] = sc[:, j * tk:(j + 1) * tk]
        mx_sc[...] = jnp.maximum(mx_sc[...], lane_max(sc))

    def score_pair(kp, carry):
        score_tiles(2 * kp, 2)
        return carry

    lax.fori_loop(0, i // 2, score_pair, 0)

    @pl.when(i % 2 == 1)
    def _():
        score_tiles(i - 1, 1)


    kd = pl.multiple_of(i * tk, tk)
    tri = jnp.where(lax.broadcasted_iota(jnp.int32, (tq, tq), 0) >= lax.broadcasted_iota(jnp.int32, (tq, tq), 1),
                    0.0, NEG)
    for sb in range(n_sub):
        width = (sb + 1) * tq
        rows = slice(sb * M, (sb + 1) * M)
        k_aug = jnp.concatenate([ks_ref[pl.ds(kd, width), :], eneg_ref[pl.ds(kd, width), :]], axis=1)
        sc = _dot_nt(q_aug[rows], k_aug)
        last = jnp.concatenate([sc[r * tq:(r + 1) * tq, sb * tq:] + tri for r in range(R)], axis=0)
        sc = last if sb == 0 else jnp.concatenate([sc[:, :sb * tq], last], axis=1)
        s_sc[i, rows, :width] = sc
        mx_sc[rows, :] = jnp.maximum(mx_sc[rows, :], lane_max(sc))
    mx_sc[...] = jnp.broadcast_to(jnp.max(mx_sc[...], axis=1, keepdims=True), mx_sc.shape)
    ls_sc[...] = jnp.zeros(ls_sc.shape, f32)
    acc_sc[...] = jnp.zeros(acc_sc.shape, f32)

    def pv_body(kt, carry):
        k0 = pl.multiple_of(kt * tk, tk)
        sc = s_sc[kt]
        mb = mx_sc[...]
        ps = [jnp.exp2(sc[:, c * LANES:(c + 1) * LANES] - mb) for c in range(n_chunk)]
        ls_sc[...] += functools.reduce(lambda a, b: a + b, ps)
        acc_sc[...] += _dot(jnp.concatenate(ps, axis=1).astype(bf16), vs_ref[pl.ds(k0, tk), :])
        return carry

    lax.fori_loop(0, i, pv_body, 0)
    for sb in range(n_sub):
        width = (sb + 1) * tq
        rows = slice(sb * M, (sb + 1) * M)
        sc = s_sc[i, rows, :width]
        mb = mx_sc[rows, :]
        ps = [jnp.exp2(sc[:, c * LANES:(c + 1) * LANES] - mb) for c in range(width // LANES)]
        ls_sc[rows, :] += functools.reduce(lambda a, b: a + b, ps)
        acc_sc[rows, :] += _dot(jnp.concatenate(ps, axis=1).astype(bf16), vs_ref[pl.ds(kd, width), :])
    o_sel = acc_sc[...] / jnp.sum(ls_sc[...], axis=1, keepdims=True)

    gt_all = jax.nn.sigmoid(gt_ref[...])
    gt_all = jnp.where(pl.program_id(1) == 0, gt_all, pltpu.roll(gt_all, LANES - R * N_BRANCH, axis=1))
    g_hi = gt_all.astype(bf16)
    rest = gt_all - g_hi.astype(f32)
    g_mid = rest.astype(bf16)
    g_lo = (rest - g_mid.astype(f32)).astype(bf16)
    g_exp = _dot(jnp.concatenate([g_hi, g_mid, g_lo], axis=1), gexp_ref[...])
    for sb, (o_win, o_cmp) in enumerate(fronts):
        for r in range(R):
            rows = slice(r * tq, (r + 1) * tq)
            srows = slice(sb * M + r * tq, sb * M + (r + 1) * tq)
            gates = [g_exp[sb * tq:(sb + 1) * tq, (r * N_BRANCH + br) * LANES:(r * N_BRANCH + br + 1) * LANES]
                     for br in range(N_BRANCH)]
            o = gates[0] * o_cmp[rows] + gates[1] * o_sel[srows] + gates[2] * o_win[rows]
            o_ref[sb * tq:(sb + 1) * tq, r * HEAD_DIM:(r + 1) * HEAD_DIM] = o.astype(bf16)


def _nsa(q, kcmp, vcmp, kv, eneg, gexp, gates, cast_a, cast_b, *, B, T, tq, n_sub):
    G, R = N_KV_GROUPS, HEADS_PER_GROUP
    tk = n_sub * tq
    nt = T // tk
    n_cmp = kcmp.shape[2]
    assert tq % LANES == 0 and tq & (tq - 1) == 0 and T % tk == 0 and T >= WINDOW + tq
    assert WINDOW % tq == 0 and WINDOW & (WINDOW - 1) == 0 and (R * tq) & (R * tq - 1) == 0
    rowblk = lambda b, g, i: (b * nt + i, g)
    cmp_spec = pl.BlockSpec((1, 1, n_cmp, HEAD_DIM), lambda b, g, i: (b, g, 0, 0))
    kvspec = lambda sec: pl.BlockSpec((T, HEAD_DIM), lambda b, g, i: (b, sec * G + g))
    n_steps = B * G * nt
    cast_spec = lambda w: pl.BlockSpec((w.shape[0] // n_steps, w.shape[1]), lambda b, g, i: ((b * G + g) * nt + i, 0))
    return pl.pallas_call(
        functools.partial(_nsa_kernel, tq=tq, n_sub=n_sub, T=T),
        grid=(B, G, nt),
        in_specs=[
            pl.BlockSpec((tk, R * HEAD_DIM), rowblk),
            cmp_spec, cmp_spec,
            kvspec(0), kvspec(1), kvspec(2), kvspec(3),
            pl.BlockSpec((T, LANES), lambda b, g, i: (0, 0)),
            pl.BlockSpec(gexp.shape, lambda b, g, i: (0, 0)),
            pl.BlockSpec((tk, LANES), lambda b, g, i: (b * nt + i, 0)),
            cast_spec(cast_a), cast_spec(cast_b),
        ],
        out_specs=[pl.BlockSpec((tk, R * HEAD_DIM), rowblk), cast_spec(cast_a), cast_spec(cast_b)],
        out_shape=[jax.ShapeDtypeStruct((B * T, G * R * HEAD_DIM), bf16), jax.ShapeDtypeStruct(cast_a.shape, bf16),
                   jax.ShapeDtypeStruct(cast_b.shape, bf16)],
        scratch_shapes=[
            pltpu.VMEM((T // tk, n_sub * R * tq, tk), f32),
            pltpu.VMEM((n_sub * R * tq, LANES), f32),
            pltpu.VMEM((n_sub * R * tq, LANES), f32),
            pltpu.VMEM((n_sub * R * tq, HEAD_DIM), f32),
        ],
        compiler_params=pltpu.CompilerParams(
            dimension_semantics=("arbitrary", "arbitrary", "arbitrary"), vmem_limit_bytes=VMEM_LIMIT),
        name="nsa",
    )(q, kcmp, vcmp, kv, kv, kv, kv, eneg, gexp, gates, cast_a, cast_b)


def _out_proj_kernel(x_ref, u_ref, uh_ref, y_ref, pw_ref, ps_ref, wo_ref, g_ref, o_ref, n_ref, *, tm, T):
    i = pl.program_id(0)
    tiles_per_seq = T // tm
    keep_halo = jnp.where((i % tiles_per_seq) == 0, 0.0, 1.0)
    u = u_ref[...]
    halo = uh_ref[...].astype(f32) * keep_halo
    tseq = (i % tiles_per_seq) * tm + lax.broadcasted_iota(jnp.int32, (tm, 1), 0)
    n_pool = u.shape[1]
    gd = n_pool // len(POOL_WINDOWS)
    acc = x_ref[...] + _dot(y_ref[...], wo_ref[n_pool:, :])
    for gi, w in enumerate(POOL_WINDOWS):
        assert w <= POOL_HALO and w & (w - 1) == 0
        cols = slice(gi * gd, (gi + 1) * gd)
        ug = u[:, cols].astype(f32)
        ws = jnp.concatenate([halo[:, cols], ug], axis=0)
        k = 1
        while k < w:
            ws = ws + pltpu.roll(ws, k, axis=0)
            k *= 2
        win_sum = ws[POOL_HALO:]
        count = jnp.minimum(tseq + 1, w).astype(f32)
        pooled = win_sum / count - ug
        yg = _dot(pooled.astype(bf16), pw_ref[gi].astype(bf16)) * ps_ref[:, cols]
        acc = acc + _dot(yg.astype(bf16), wo_ref[cols, :])
    o_ref[...] = acc
    n_ref[...] = _rms(acc, g_ref[...]).astype(bf16)


def _out_proj(x2, u, y_nsa, pool_w, pool_scale, w_out, g_next, *, T, tm):
    n_rows, d = x2.shape
    n_pool = u.shape[1]
    hb = tm // POOL_HALO
    row = lambda i: (i, 0)
    const2 = lambda i: (0, 0)
    return pl.pallas_call(
        functools.partial(_out_proj_kernel, tm=tm, T=T),
        grid=(n_rows // tm,),
        in_specs=[
            pl.BlockSpec((tm, d), row),
            pl.BlockSpec((tm, n_pool), row),
            pl.BlockSpec((POOL_HALO, n_pool), lambda i: (jnp.maximum(i * hb - 1, 0), 0)),
            pl.BlockSpec((tm, y_nsa.shape[1]), row),
            pl.BlockSpec(pool_w.shape, lambda i: (0, 0, 0)),
            pl.BlockSpec((1, n_pool), const2),
            pl.BlockSpec(w_out.shape, const2, pipeline_mode=pl.Buffered(1)),
            pl.BlockSpec((1, d), const2),
        ],
        out_specs=[pl.BlockSpec((tm, d), row), pl.BlockSpec((tm, d), row)],
        out_shape=[jax.ShapeDtypeStruct((n_rows, d), f32), jax.ShapeDtypeStruct((n_rows, d), bf16)],
        compiler_params=pltpu.CompilerParams(
            dimension_semantics=("arbitrary",), vmem_limit_bytes=VMEM_LIMIT),
        name="out_proj",
    )(x2, u, u, y_nsa, pool_w, pool_scale, w_out, g_next)


def _ffn_kernel(h_hbm, n_ref, wg_ref, wu_ref, wd_ref, o_hbm, acc_sc, sem, *, tm):
    i = pl.program_id(0)
    f = pl.program_id(1)
    n_i = pl.num_programs(0)
    n_f = pl.num_programs(1)

    def rows(t):
        return pl.ds(pl.multiple_of(t * tm, tm), tm)

    def load(t):
        return pltpu.make_async_copy(h_hbm.at[rows(t), :], acc_sc.at[t % 2], sem.at[0])

    def store(t):
        return pltpu.make_async_copy(acc_sc.at[t % 2], o_hbm.at[rows(t), :], sem.at[1])

    @pl.when((i == 0) & (f == 0))
    def _():
        load(i).start()

    @pl.when(f == 0)
    def _():
        load(i).wait()

    @pl.when((f == 0) & (i > 0))
    def _():
        store(i - 1).start()

    @pl.when((f == 1) & (i > 0))
    def _():
        store(i - 1).wait()

    @pl.when((f == n_f // 2) & (i + 1 < n_i))
    def _():
        load(i + 1).start()

    n = n_ref[...]
    half = wg_ref.shape[1] // 2
    acc = None
    for c in range(2):
        cols = slice(c * half, (c + 1) * half)
        gate = _dot(n, wg_ref[:, cols])
        up = _dot(n, wu_ref[:, cols])
        a = gate * jax.nn.sigmoid(gate) * up
        d = _dot(a.astype(bf16), wd_ref[cols, :])
        acc = d if acc is None else acc + d
    acc_sc[i % 2] += acc

    @pl.when((i == n_i - 1) & (f == n_f - 1))
    def _():
        store(i).start()
        store(i).wait()


def _ffn(h, n, w_gate, w_up, w_down, *, tm, tf):
    n_rows, d = h.shape
    d_ff = w_gate.shape[1]
    assert n_rows % tm == 0 and d_ff // tf >= 2
    return pl.pallas_call(
        functools.partial(_ffn_kernel, tm=tm),
        grid=(n_rows // tm, d_ff // tf),
        in_specs=[
            pl.BlockSpec(memory_space=pl.ANY),
            pl.BlockSpec((tm, d), lambda i, f: (i, 0)),
            pl.BlockSpec((d, tf), lambda i, f: (0, f)),
            pl.BlockSpec((d, tf), lambda i, f: (0, f)),
            pl.BlockSpec((tf, d), lambda i, f: (f, 0)),
        ],
        out_specs=pl.BlockSpec(memory_space=pl.ANY),
        out_shape=jax.ShapeDtypeStruct((n_rows, d), f32),
        scratch_shapes=[pltpu.VMEM((2, tm, d), f32), pltpu.SemaphoreType.DMA((2,))],
        compiler_params=pltpu.CompilerParams(
            dimension_semantics=("arbitrary", "arbitrary"), vmem_limit_bytes=VMEM_LIMIT),
        name="ffn",
    )(h, n, w_gate, w_up, w_down)


def _ple_kernel(h_ref, p_ref, g_ref, wg_ref, wp_ref, fg_ref, o_ref):
    h = h_ref[...]
    n = _rms(h, g_ref[...]).astype(bf16)
    gate = jax.nn.sigmoid(_dot(n, wg_ref[...]))
    h = h + _dot(p_ref[...].astype(bf16), wp_ref[...].astype(bf16)) * gate
    o_ref[...] = _rms(h, fg_ref[...])


def _ple(h, p2, g, w_gate, w_proj, fg, *, tm):
    n_rows, d = h.shape
    row = lambda i: (i, 0)
    const = lambda i: (0, 0)
    return pl.pallas_call(
        _ple_kernel,
        grid=(n_rows // tm,),
        in_specs=[
            pl.BlockSpec((tm, d), row),
            pl.BlockSpec((tm, p2.shape[1]), row),
            pl.BlockSpec((1, d), const),
            pl.BlockSpec(w_gate.shape, const, pipeline_mode=pl.Buffered(1)),
            pl.BlockSpec(w_proj.shape, const),
            pl.BlockSpec((1, d), const),
        ],
        out_specs=pl.BlockSpec((tm, d), row),
        out_shape=jax.ShapeDtypeStruct((n_rows, d), f32),
        compiler_params=pltpu.CompilerParams(
            dimension_semantics=("arbitrary",), vmem_limit_bytes=VMEM_LIMIT),
        name="ple",
    )(h, p2, g, w_gate, w_proj, fg)


def _rope_tables(T):
    pos = np.arange(T, dtype=np.float64)
    inv_freq = ROPE_THETA ** (-np.arange(0, ROPE_DIM, 2, dtype=np.float64) / ROPE_DIM)
    ang = pos[:, None] * inv_freq[None, :]
    cos, sin = np.cos(ang), np.sin(ang)
    rest = HEAD_DIM - ROPE_DIM
    cosf = np.concatenate([cos, cos, np.ones((T, rest))], axis=1)
    sinf = np.concatenate([-sin, sin, np.zeros((T, rest))], axis=1)
    return jnp.asarray(cosf, f32), jnp.asarray(sinf, f32)


def _block_bias_table(T):
    key_blk = np.arange(T)[:, None] // SEL_BLOCK
    return jnp.asarray(np.where(key_blk == np.arange(LANES)[None, :], NEG, 0.0).astype(bf16))


def _gate_expansion_table():
    n_col = HEADS_PER_GROUP * N_BRANCH
    e = np.zeros((3, LANES, n_col, LANES), np.float32)
    for c in range(n_col):
        e[:, c, c, :] = 1.0
    return jnp.asarray(e.reshape(3 * LANES, n_col * LANES).astype(bf16))


def _mixer_ffn(h2, in_norm_g, w_in, pool_w, pool_scale, cmp_k_pe, cmp_k_w1, cmp_k_w2,
               cmp_v_pe, cmp_v_w1, cmp_v_w2, w_out, ffn_norm_g, w_gate, w_up, w_down, w_ple_gate, *, B, T):
    n_pool = pool_scale.shape[0]
    n_heads = N_KV_GROUPS * HEADS_PER_GROUP
    nsa_w = n_heads * HEAD_DIM
    kv_w = N_KV_GROUPS * HEAD_DIM
    assert w_in.shape[1] == n_pool + nsa_w + 6 * kv_w + n_heads * N_BRANCH

    o0 = n_pool + nsa_w + 6 * kv_w
    w_p = w_in.T
    w_gt = jnp.pad(w_p[o0:].astype(bf16), ((0, LANES - n_heads * N_BRANCH), (0, 0)))

    cosf, sinf = _rope_tables(T)
    u, q, kcvc, kv, gates, w_out_b, w_gate_b, w_ple_gate_b = _in_proj(
        h2, in_norm_g[None, :], w_p, w_gt, cosf, sinf, w_out, w_gate, w_ple_gate, T=T, tm=ROW_TILE,
        n_pool=n_pool, n_q=nsa_w)

    kcmp, vcmp = _compress(
        kcvc.reshape(B, T, kcvc.shape[1]),
        cmp_k_pe.reshape(1, -1), cmp_k_w1, cmp_k_w2, cmp_v_pe.reshape(1, -1), cmp_v_w1, cmp_v_w2,
        bb=COMPRESS_BATCH if B % COMPRESS_BATCH == 0 else 1)

    eneg = _block_bias_table(T)
    y_nsa, w_up_b, w_down_b = _nsa(q, kcmp, vcmp, kv, eneg, _gate_expansion_table(), gates, w_up, w_down, B=B, T=T,
                                   tq=NSA_Q_TILE, n_sub=NSA_SUB_TILES)

    h2, n2 = _out_proj(h2, u, y_nsa, pool_w, pool_scale[None, :], w_out_b, ffn_norm_g[None, :], T=T, tm=ROW_TILE)
    return _ffn(h2, n2, w_gate_b, w_up_b, w_down_b, tm=FFN_ROW_TILE, tf=FFN_FF_TILE), w_ple_gate_b


def kernel(x, p, in_norm_g, w_in, pool_w, pool_scale, cmp_k_pe, cmp_k_w1, cmp_k_w2, cmp_v_pe, cmp_v_w1,
           cmp_v_w2, w_out, ffn_norm_g, w_gate, w_up, w_down, ple_norm_g, w_ple_gate, w_ple_proj, final_norm_g):
    B, T, d = x.shape
    depth = w_in.shape[0]
    assert depth == 1, "the final rmsnorm is fused into the last layer's per-layer-embedding kernel"
    h2 = x.reshape(B * T, d)
    i = 0
    h2, w_ple_gate_b = _mixer_ffn(
        h2, in_norm_g[i], w_in[i], pool_w[i], pool_scale[i], cmp_k_pe[i], cmp_k_w1[i], cmp_k_w2[i],
        cmp_v_pe[i], cmp_v_w1[i], cmp_v_w2[i], w_out[i], ffn_norm_g[i], w_gate[i], w_up[i], w_down[i],
        w_ple_gate[i], B=B, T=T)
    out = _ple(h2, p[i].reshape(B * T, -1), ple_norm_g[i][None, :], w_ple_gate_b,
               w_ple_proj[i], final_norm_g[None, :], tm=ROW_TILE)
    return out.reshape(B, T, d)
```
